```python
import jax
import jax.numpy as jnp
from jax import lax
import numpy as np

D_MODEL = 1024
BATCH = 4
SEQ = 8192
DEPTH = 1

CTX_LEN = 256
GRID_W = 64
HEAD_DIM = 64
ATTN_Q_HEADS = 8
ATTN_KV_HEADS = 2
ATTN_GROUP = ATTN_Q_HEADS // ATTN_KV_HEADS
WINDOW = 128
BLOCK = 128
ROPE_BASE = 10000.0
ROPE_FREQS = HEAD_DIM // 4
GM_GROUPS = 8
GM_HEAD = 64
GM_CHUNK = 128
GM_WIDTH = GM_GROUPS * GM_HEAD
ATTN_WIDTH = ATTN_Q_HEADS * HEAD_DIM
KV_WIDTH = ATTN_KV_HEADS * HEAD_DIM
N_BRANCH = 2
Q0 = 0
K0 = Q0 + ATTN_WIDTH
V0 = K0 + KV_WIDTH
U0 = V0 + KV_WIDTH
VG0 = U0 + GM_WIDTH
GATE0 = VG0 + GM_WIDTH
IN_WIDTH = GATE0 + N_BRANCH * D_MODEL
N_GROUPS = 4
EXPERTS_PER_GROUP = 8
N_EXPERTS = N_GROUPS * EXPERTS_PER_GROUP
TOP_K_IN_GROUP = 2
D_EXPERT = D_MODEL // 2
EXPERT_BLOCK = 128
LN_EPS = 1e-6
NEG_INF = -1e30
ALPHA = (2.0 * DEPTH) ** 0.25
BETA = (8.0 * DEPTH) ** -0.25

kernel_name = 'hybrid_dit_window_gqa_gmlp_hmoe'


def _layernorm(x, g=None, b=None):
    xf = x.astype(jnp.float32)
    mu = jnp.mean(xf, -1, keepdims=True)
    var = jnp.mean(jnp.square(xf - mu), -1, keepdims=True)
    y = ((xf - mu) * lax.rsqrt(var + LN_EPS)).astype(x.dtype)
    if g is not None:
        y = y * g + b
    return y


def _modulate(h, shift, scale):
    return h * (1 + scale) + shift


def _rope_tables(pos, dtype):
    inv = 1.0 / (ROPE_BASE ** (jnp.arange(ROPE_FREQS, dtype=jnp.float32) / ROPE_FREQS))
    ang = pos.astype(jnp.float32)[:, None] * inv[None, :]
    return (jnp.cos(ang).astype(dtype)[:, None, :], jnp.sin(ang).astype(dtype)[:, None, :])


def _rotate(x, cos, sin):
    xa, xb = jnp.split(x, 2, axis=-1)
    return jnp.concatenate([xa * cos - xb * sin, xb * cos + xa * sin], -1)


def _rope_2d(x, rope):
    cos_r, sin_r, cos_c, sin_c = rope
    xr, xc = jnp.split(x, 2, axis=-1)
    return jnp.concatenate([_rotate(xr, cos_r, sin_r), _rotate(xc, cos_c, sin_c)], -1)


def _window_mask(n):
    nb = n // BLOCK
    blk = jnp.arange(nb, dtype=jnp.int32)[:, None, None]
    qpos = blk * BLOCK + jnp.arange(BLOCK, dtype=jnp.int32)[None, :, None]
    kpos = (blk - 1) * BLOCK + jnp.arange(3 * BLOCK, dtype=jnp.int32)[None, None, :]
    return (jnp.abs(qpos - kpos) <= WINDOW) & (kpos >= 0) & (kpos < n)


def _split_proj(p):
    b, n, _ = p.shape
    q = p[..., Q0:K0].reshape(b, n, ATTN_Q_HEADS, HEAD_DIM)
    k = p[..., K0:V0].reshape(b, n, ATTN_KV_HEADS, HEAD_DIM)
    v = p[..., V0:U0].reshape(b, n, ATTN_KV_HEADS, HEAD_DIM)
    u = p[..., U0:VG0]
    vg = p[..., VG0:GATE0]
    ga = p[..., GATE0:GATE0 + D_MODEL]
    gb = p[..., GATE0 + D_MODEL:IN_WIDTH]
    return q, k, v, u, vg, ga, gb


def _kv_proj(h, w_in):
    b, n, _ = h.shape
    kv = h @ w_in[:, K0:U0]
    k = kv[..., :KV_WIDTH].reshape(b, n, ATTN_KV_HEADS, HEAD_DIM)
    v = kv[..., KV_WIDTH:].reshape(b, n, ATTN_KV_HEADS, HEAD_DIM)
    return k, v


def _sink_column(sink, shape):
    s = sink.astype(jnp.float32).reshape(ATTN_KV_HEADS, ATTN_GROUP, 1, 1)
    return jnp.broadcast_to(s, shape[:-1] + (1,))


def _latent_attention(q, k, v, k_ctx, v_ctx, sink, mask):
    b, n = q.shape[:2]
    nb = n // BLOCK
    qb = q.reshape(b, nb, BLOCK, ATTN_KV_HEADS, ATTN_GROUP, HEAD_DIM)

    def band(t):
        tp = jnp.pad(t, ((0, 0), (BLOCK, BLOCK), (0, 0), (0, 0)))
        tp = tp.reshape(b, nb + 2, BLOCK, ATTN_KV_HEADS, HEAD_DIM)
        return jnp.concatenate([tp[:, :-2], tp[:, 1:-1], tp[:, 2:]], axis=2)

    kw, vw = band(k), band(v)
    s_loc = jnp.einsum('bnqhgd,bnkhd->bnhgqk', qb, kw).astype(jnp.float32)
    s_loc = jnp.where(mask[None, :, None, None], s_loc, NEG_INF)
    s_ctx = jnp.einsum('bnqhgd,bkhd->bnhgqk', qb, k_ctx).astype(jnp.float32)
    logits = jnp.concatenate([_sink_column(sink, s_ctx.shape), s_ctx, s_loc], -1)
    p = jax.nn.softmax(logits, -1).astype(q.dtype)
    c_len = k_ctx.shape[1]
    o = (jnp.einsum('bnhgqk,bkhd->bnqhgd', p[..., 1:1 + c_len], v_ctx)
         + jnp.einsum('bnhgqk,bnkhd->bnqhgd', p[..., 1 + c_len:], vw))
    return o.reshape(b, n, ATTN_WIDTH)


def _context_attention(q, k, v, sink):
    b, n = q.shape[:2]
    qg = q.reshape(b, n, ATTN_KV_HEADS, ATTN_GROUP, HEAD_DIM)
    s = jnp.einsum('bqhgd,bkhd->bhgqk', qg, k).astype(jnp.float32)
    p = jax.nn.softmax(jnp.concatenate([_sink_column(sink, s.shape), s], -1), -1)[..., 1:]
    o = jnp.einsum('bhgqk,bkhd->bqhgd', p.astype(q.dtype), v)
    return o.reshape(b, n, ATTN_WIDTH)


def _chunk_gmlp(u, v, lp):
    b, n, _ = u.shape
    u = jax.nn.gelu(u)
    v = _layernorm(jax.nn.gelu(v), lp['gm_ln_g'], lp['gm_ln_b'])
    vc = v.reshape(b, n // GM_CHUNK, GM_CHUNK, GM_GROUPS, GM_HEAD)
    sp = jnp.einsum('gpq,bnqgc->bnpgc', lp['gm_ws'], vc) + lp['gm_bs'].T[None, None, :, :, None]
    return u * sp.reshape(b, n, GM_WIDTH)


def _merge(y_attn, y_gm, ga, gb, lp):
    y = jax.nn.sigmoid(ga) * (y_attn @ lp['w_pa']) + jax.nn.sigmoid(gb) * (y_gm @ lp['w_pb'])
    return y @ lp['w_o']


def _hier_moe(h, lp):
    b, n, d = h.shape
    t = h.reshape(b * n, d)
    n_tok = b * n
    g_prob = jax.nn.softmax((t @ lp['router_g_w'] + lp['router_g_b']).astype(jnp.float32), -1)
    g_w, g_idx = lax.top_k(g_prob, 1)
    e_logit = (jnp.einsum('nd,gde->nge', t, lp['router_e_w']) + lp['router_e_b']).astype(jnp.float32)
    sel = jnp.broadcast_to(g_idx[:, :, None], (n_tok, 1, EXPERTS_PER_GROUP))
    e_logit = jnp.take_along_axis(e_logit, sel, axis=1)[:, 0]
    e_val, e_idx = lax.top_k(e_logit, TOP_K_IN_GROUP)
    e_w = jax.nn.softmax(e_val, -1)
    expert = g_idx * EXPERTS_PER_GROUP + e_idx
    weight = (g_w * e_w).astype(h.dtype)
    n_assign = n_tok * TOP_K_IN_GROUP
    e_flat = expert.reshape(-1)
    w_flat = weight.reshape(-1)
    tok_flat = jnp.repeat(jnp.arange(n_tok, dtype=jnp.int32), TOP_K_IN_GROUP)
    order = jnp.argsort(e_flat)
    e_sorted = e_flat[order]
    counts = jnp.bincount(e_flat, length=N_EXPERTS)
    padded = (counts + EXPERT_BLOCK - 1) // EXPERT_BLOCK * EXPERT_BLOCK
    pad_end = jnp.cumsum(padded)
    pad_start = pad_end - padded
    start = jnp.cumsum(counts) - counts
    pos = pad_start[e_sorted] + jnp.arange(n_assign, dtype=jnp.int32) - start[e_sorted]
    cap = -(-n_assign // EXPERT_BLOCK) * EXPERT_BLOCK + N_EXPERTS * EXPERT_BLOCK
    buf_tok = jnp.zeros((cap,), jnp.int32).at[pos].set(tok_flat[order])
    buf_w = jnp.zeros((cap,), h.dtype).at[pos].set(w_flat[order])
    n_blk = cap // EXPERT_BLOCK
    blk_e = jnp.searchsorted(pad_end, jnp.arange(n_blk, dtype=jnp.int32) * EXPERT_BLOCK, side='right')
    blk_e = jnp.minimum(blk_e, N_EXPERTS - 1)
    xb = t[buf_tok].reshape(n_blk, EXPERT_BLOCK, d)
    w1, w3, w2 = lp['moe_w1'], lp['moe_w3'], lp['moe_w2']

    def run_block(args):
        xt, e = args
        hid = jax.nn.silu(xt @ w1[e]) * (xt @ w3[e])
        return hid @ w2[e]

    y = lax.map(run_block, (xb, blk_e)).reshape(cap, d)
    out = jnp.zeros_like(t).at[buf_tok].add(y * buf_w[:, None])
    return out.reshape(b, n, d)


def setup_inputs(seed: int = 0) -> dict:
    key = jax.random.key(seed)
    ks = jax.random.split(key, 32)
    L, D = DEPTH, D_MODEL

    def nrm(k, shape, scale):
        return jax.random.normal(k, shape, jnp.float32) * scale

    return {
        'x': nrm(ks[0], (BATCH, SEQ, D), 1.0),
        'c': nrm(ks[1], (BATCH, D), 1.0),
        'ctx': nrm(ks[2], (BATCH, CTX_LEN, D), 1.0),
        'c_ctx': nrm(ks[3], (D,), 1.0),
        'w_ada': nrm(ks[4], (L, D, 6 * D), 0.5 * D ** -0.5),
        'b_ada': nrm(ks[5], (L, 6 * D), 0.02),
        'w_in': nrm(ks[6], (L, D, IN_WIDTH), D ** -0.5),
        'attn_sink': nrm(ks[7], (L, ATTN_Q_HEADS), 0.5),
        'gm_ln_g': 1.0 + nrm(ks[8], (L, GM_WIDTH), 0.02),
        'gm_ln_b': nrm(ks[9], (L, GM_WIDTH), 0.02),
        'gm_ws': nrm(ks[10], (L, GM_GROUPS, GM_CHUNK, GM_CHUNK), GM_CHUNK ** -0.5),
        'gm_bs': 1.0 + nrm(ks[11], (L, GM_GROUPS, GM_CHUNK), 0.02),
        'w_pa': nrm(ks[12], (L, ATTN_WIDTH, D), BETA * ATTN_WIDTH ** -0.5),
        'w_pb': nrm(ks[13], (L, GM_WIDTH, D), BETA * GM_WIDTH ** -0.5),
        'w_o': nrm(ks[14], (L, D, D), BETA * D ** -0.5),
        'ln1_g': 1.0 + nrm(ks[15], (L, D), 0.02),
        'ln1_b': nrm(ks[16], (L, D), 0.02),
        'router_g_w': nrm(ks[17], (L, D, N_GROUPS), D ** -0.5),
        'router_g_b': nrm(ks[18], (L, N_GROUPS), 0.01),
        'router_e_w': nrm(ks[19], (L, N_GROUPS, D, EXPERTS_PER_GROUP), D ** -0.5),
        'router_e_b': nrm(ks[20], (L, N_GROUPS, EXPERTS_PER_GROUP), 0.01),
        'moe_w1': nrm(ks[21], (L, N_EXPERTS, D, D_EXPERT), D ** -0.5),
        'moe_w3': nrm(ks[22], (L, N_EXPERTS, D, D_EXPERT), D ** -0.5),
        'moe_w2': nrm(ks[23], (L, N_EXPERTS, D_EXPERT, D), BETA * D_EXPERT ** -0.5),
        'ln2_g': 1.0 + nrm(ks[24], (L, D), 0.02),
        'ln2_b': nrm(ks[25], (L, D), 0.02),
    }


def reference(x, c, ctx, c_ctx, w_ada, b_ada, w_in, attn_sink, gm_ln_g, gm_ln_b, gm_ws, gm_bs,
              w_pa, w_pb, w_o, ln1_g, ln1_b, router_g_w, router_g_b, router_e_w, router_e_b,
              moe_w1, moe_w3, moe_w2, ln2_g, ln2_b):
    n_lat = x.shape[1]
    rows = n_lat // GRID_W
    row = jnp.repeat(jnp.arange(rows, dtype=jnp.int32), GRID_W)
    col = jnp.tile(jnp.arange(GRID_W, dtype=jnp.int32), rows)
    rope = _rope_tables(row, x.dtype) + _rope_tables(col, x.dtype)
    mask = _window_mask(n_lat)
    q_scale = HEAD_DIM ** -0.5
    for i in range(DEPTH):
        lp = {'gm_ln_g': gm_ln_g[i], 'gm_ln_b': gm_ln_b[i], 'gm_ws': gm_ws[i], 'gm_bs': gm_bs[i],
              'w_pa': w_pa[i], 'w_pb': w_pb[i], 'w_o': w_o[i],
              'router_g_w': router_g_w[i], 'router_g_b': router_g_b[i],
              'router_e_w': router_e_w[i], 'router_e_b': router_e_b[i],
              'moe_w1': moe_w1[i], 'moe_w3': moe_w3[i], 'moe_w2': moe_w2[i]}
        mod = jax.nn.silu(c) @ w_ada[i] + b_ada[i]
        sh1, sc1, g1, sh2, sc2, g2 = jnp.split(mod[:, None, :], 6, axis=-1)
        cmod = jax.nn.silu(c_ctx) @ w_ada[i] + b_ada[i]
        csh1, csc1, cg1, csh2, csc2, cg2 = jnp.split(cmod, 6, axis=-1)
        hc = _modulate(_layernorm(ctx), csh1, csc1)
        if i < DEPTH - 1:
            qc, k_ctx, v_ctx, uc, vgc, gac, gbc = _split_proj(hc @ w_in[i])
            mix_c = _merge(_context_attention(qc * q_scale, k_ctx, v_ctx, attn_sink[i]),
                           _chunk_gmlp(uc, vgc, lp), gac, gbc, lp)
            ctx = _layernorm(ALPHA * ctx + cg1 * mix_c, ln1_g[i], ln1_b[i])
            hc2 = _modulate(_layernorm(ctx), csh2, csc2)
            ctx = _layernorm(ALPHA * ctx + cg2 * _hier_moe(hc2, lp), ln2_g[i], ln2_b[i])
        else:
            k_ctx, v_ctx = _kv_proj(hc, w_in[i])
        h = _modulate(_layernorm(x), sh1, sc1)
        q, k, v, u, vg, ga, gb = _split_proj(h @ w_in[i])
        q = _rope_2d(q, rope) * q_scale
        k = _rope_2d(k, rope)
        y_attn = _latent_attention(q, k, v, k_ctx, v_ctx, attn_sink[i], mask)
        y_gm = _chunk_gmlp(u, vg, lp)
        mix = _merge(y_attn, y_gm, ga, gb, lp)
        x = _layernorm(ALPHA * x + g1 * mix, ln1_g[i], ln1_b[i])
        h2 = _modulate(_layernorm(x), sh2, sc2)
        x = _layernorm(ALPHA * x + g2 * _hier_moe(h2, lp), ln2_g[i], ln2_b[i])
    return x
```

```python
import functools

import jax
import jax.numpy as jnp
from jax import lax
from jax.experimental import pallas as pl
from jax.experimental.pallas import tpu as pltpu

F32 = jnp.float32
BF16 = jnp.bfloat16

GRID_W = 64
HEAD_DIM = 64
Q_HEADS = 8
KV_HEADS = 2
GROUP = Q_HEADS // KV_HEADS
BLOCK = 128
ROPE_BASE = 10000.0
ROPE_FREQS = HEAD_DIM // 4
GM_GROUPS = 8
GM_HEAD = 64
ATTN_W = Q_HEADS * HEAD_DIM
KV_W = KV_HEADS * HEAD_DIM
GM_W = GM_GROUPS * GM_HEAD
N_GROUPS = 4
EPG = 8
N_EXPERTS = N_GROUPS * EPG
PAIRS = EPG * (EPG - 1) // 2
N_CLASSES = N_GROUPS * PAIRS
LN_EPS = 1e-6
NEG_INF = -1e30
DEPTH = 1
ALPHA = (2.0 * DEPTH) ** 0.25

LANES = 128
VMEM_LIMIT_BYTES = 56 * 1024 * 1024
ROUTE_W = LANES


def _params(*sem):
    return pltpu.CompilerParams(dimension_semantics=sem, vmem_limit_bytes=VMEM_LIMIT_BYTES)


def _ln(x):
    mu = jnp.mean(x, -1, keepdims=True)
    xc = x - mu
    var = jnp.mean(xc * xc, -1, keepdims=True)
    return xc * lax.rsqrt(var + LN_EPS)


def _dot(a, b):
    return jnp.dot(a, b, preferred_element_type=F32)


def _dot_nt(a, b):
    return lax.dot_general(a, b, (((1,), (1,)), ((), ())), preferred_element_type=F32)


def _split(a):
    hi = a.astype(BF16)
    lo = (a - hi.astype(F32)).astype(BF16)
    return hi, lo


def _dot_split(a, b_hi, b_lo):
    a_hi, a_lo = _split(a)
    return _dot(a_hi, b_hi) + _dot(a_hi, b_lo) + _dot(a_lo, b_hi)


def _mod_body(c_ref, w_ref, b_ref, o_ref):
    w_hi, w_lo = _split(w_ref[...])
    o_ref[...] = _dot_split(jax.nn.silu(c_ref[...]), w_hi, w_lo) + b_ref[...]


def _adaln_mod(c_rows, w_ada, b_ada):
    rows, d = c_rows.shape
    width = w_ada.shape[1]
    tn = 6 * LANES
    return pl.pallas_call(
        _mod_body,
        out_shape=jax.ShapeDtypeStruct((rows, width), F32),
        grid=(width // tn,),
        in_specs=[
            pl.BlockSpec((rows, d), lambda j: (0, 0)),
            pl.BlockSpec((d, tn), lambda j: (0, j)),
            pl.BlockSpec((1, tn), lambda j: (0, j)),
        ],
        out_specs=pl.BlockSpec((rows, tn), lambda j: (0, j)),
        compiler_params=_params("arbitrary"),
        name="mod",
    )(c_rows, w_ada, b_ada.reshape(1, width))


def _ctx_body(ctx_ref, sh_ref, sc_ref, w_ref, o_ref):
    h = _ln(ctx_ref[...]) * (1.0 + sc_ref[...]) + sh_ref[...]
    o_ref[...] = _dot(h.astype(BF16), w_ref[...]).astype(BF16)


def _ctx_kv(ctx, mod3, ctx_row, w_in_bf):
    b, c_len, d = ctx.shape
    kvw = 2 * KV_W
    return pl.pallas_call(
        _ctx_body,
        out_shape=jax.ShapeDtypeStruct((b, c_len, kvw), BF16),
        grid=(b,),
        in_specs=[
            pl.BlockSpec((None, c_len, d), lambda i: (i, 0, 0)),
            pl.BlockSpec((None, 1, d), lambda i: (ctx_row, 0, 0)),
            pl.BlockSpec((None, 1, d), lambda i: (ctx_row, 0, 1)),
            pl.BlockSpec((d, kvw), lambda i: (0, ATTN_W // kvw)),
        ],
        out_specs=pl.BlockSpec((None, c_len, kvw), lambda i: (i, 0, 0)),
        compiler_params=_params("arbitrary"),
        name="ctx_kv",
    )(ctx, mod3, mod3, w_in_bf)


def _rope(x, cos, s_up, s_dn):
    return x * cos + pltpu.roll(x, LANES - ROPE_FREQS, 1) * s_up + pltpu.roll(x, ROPE_FREQS, 1) * s_dn


def _project_body(x_ref, sh_ref, sc_ref, w_ref, cos_ref, sup_ref, sdn_ref, lng_ref, lnb_ref,
                  q_ref, kv_ref, u_ref, vg_ref, gate_ref):
    h = (_ln(x_ref[...]) * (1.0 + sc_ref[...]) + sh_ref[...]).astype(BF16)
    cos, s_up, s_dn = cos_ref[...], sup_ref[...], sdn_ref[...]
    q_scale = HEAD_DIM ** -0.5
    k0 = ATTN_W
    u0 = k0 + 2 * KV_W
    vg0 = u0 + GM_W
    gate0 = vg0 + GM_W
    q = _dot(h, w_ref[:, 0:k0])
    for j in range(ATTN_W // LANES):
        sl = slice(j * LANES, (j + 1) * LANES)
        q_ref[:, sl] = (_rope(q[:, sl], cos, s_up, s_dn) * q_scale).astype(BF16)
    kv = _dot(h, w_ref[:, k0:u0])
    kv_ref[:, 0:KV_W] = _rope(kv[:, 0:KV_W], cos, s_up, s_dn).astype(BF16)
    kv_ref[:, KV_W:] = kv[:, KV_W:].astype(BF16)
    u_ref[...] = jax.nn.gelu(_dot(h, w_ref[:, u0:vg0])).astype(BF16)
    vg = _ln(jax.nn.gelu(_dot(h, w_ref[:, vg0:gate0]))) * lng_ref[...] + lnb_ref[...]
    vg_ref[...] = vg.astype(BF16)
    gate_ref[...] = jax.nn.sigmoid(_dot(h, w_ref[:, gate0:])).astype(BF16)


def _project(x, mod3, w_in_bf, rope_tabs, gm_ln_g, gm_ln_b, tile):
    b, s, d = x.shape
    in_w = w_in_bf.shape[1]
    gate_w = in_w - (ATTN_W + 2 * KV_W + 2 * GM_W)
    row = lambda width: pl.BlockSpec((None, tile, width), lambda i, t: (i, t, 0))
    tab = pl.BlockSpec((tile, LANES), lambda i, t: (t, 0))
    vec = lambda width: pl.BlockSpec((1, width), lambda i, t: (0, 0))
    return pl.pallas_call(
        _project_body,
        out_shape=[
            jax.ShapeDtypeStruct((b, s, ATTN_W), BF16),
            jax.ShapeDtypeStruct((b, s, 2 * KV_W), BF16),
            jax.ShapeDtypeStruct((b, s, GM_W), BF16),
            jax.ShapeDtypeStruct((b, s, GM_W), BF16),
            jax.ShapeDtypeStruct((b, s, gate_w), BF16),
        ],
        grid=(b, s // tile),
        in_specs=[
            row(d),
            pl.BlockSpec((None, 1, d), lambda i, t: (i, 0, 0)),
            pl.BlockSpec((None, 1, d), lambda i, t: (i, 0, 1)),
            pl.BlockSpec((d, in_w), lambda i, t: (0, 0)),
            tab, tab, tab,
            vec(GM_W), vec(GM_W),
        ],
        out_specs=[row(ATTN_W), row(2 * KV_W), row(GM_W), row(GM_W), row(gate_w)],
        compiler_params=_params("arbitrary", "arbitrary"),
        name="project",
    )(x, mod3, mod3, w_in_bf, *rope_tabs, gm_ln_g.reshape(1, GM_W), gm_ln_b.reshape(1, GM_W))


def _rope_tables(s):
    pos = jnp.arange(s, dtype=jnp.int32)
    inv = 1.0 / (ROPE_BASE ** (jnp.arange(ROPE_FREQS, dtype=F32) / ROPE_FREQS))
    ang_r = (pos // GRID_W).astype(F32)[:, None] * inv[None, :]
    ang_c = (pos % GRID_W).astype(F32)[:, None] * inv[None, :]
    cr, sr, cc, sn = jnp.cos(ang_r), jnp.sin(ang_r), jnp.cos(ang_c), jnp.sin(ang_c)
    z = jnp.zeros_like(sr)
    reps = LANES // HEAD_DIM
    cos = jnp.tile(jnp.concatenate([cr, cr, cc, cc], -1), (1, reps))
    s_up = jnp.tile(jnp.concatenate([-sr, z, -sn, z], -1), (1, reps))
    s_dn = jnp.tile(jnp.concatenate([z, sr, z, sn], -1), (1, reps))
    return cos, s_up, s_dn


def _attend_body(sink_ref, q_ref, kvp_ref, kvc_ref, kvn_ref, ctx_ref, o_ref):
    i = pl.program_id(1)
    nb = pl.num_programs(1)
    rows = GROUP * BLOCK
    r = lax.broadcasted_iota(jnp.int32, (rows, 3 * BLOCK), 0) & (BLOCK - 1)
    m = lax.broadcasted_iota(jnp.int32, (rows, 3 * BLOCK), 1)
    mask = (m >= r) & (m <= r + 2 * BLOCK)
    mask &= (m >= BLOCK) | (i > 0)
    mask &= (m < 2 * BLOCK) | (i < nb - 1)
    grp = lax.broadcasted_iota(jnp.int32, (rows, 1), 0) // BLOCK
    kv_loc = jnp.concatenate([kvp_ref[...], kvc_ref[...], kvn_ref[...]], axis=0)
    kv_ctx = ctx_ref[...]
    q = q_ref[...]
    outs = []
    for h in range(KV_HEADS):
        ks = slice(h * HEAD_DIM, (h + 1) * HEAD_DIM)
        vs = slice(KV_W + h * HEAD_DIM, KV_W + (h + 1) * HEAD_DIM)
        qg = jnp.concatenate(
            [q[:, (h * GROUP + g) * HEAD_DIM:(h * GROUP + g + 1) * HEAD_DIM] for g in range(GROUP)], axis=0)
        sink = jnp.zeros((rows, 1), F32)
        for g in range(GROUP):
            sink = jnp.where(grp == g, sink_ref[h * GROUP + g], sink)
        s_ctx = _dot_nt(qg, kv_ctx[:, ks])
        s_loc = jnp.where(mask, _dot_nt(qg, kv_loc[:, ks]), NEG_INF)
        top = jnp.maximum(jnp.maximum(jnp.max(s_ctx, -1, keepdims=True), jnp.max(s_loc, -1, keepdims=True)), sink)
        e_ctx = jnp.exp(s_ctx - top)
        e_loc = jnp.exp(s_loc - top)
        den = jnp.sum(e_ctx, -1, keepdims=True) + jnp.sum(e_loc, -1, keepdims=True) + jnp.exp(sink - top)
        o = (_dot(e_ctx.astype(BF16), kv_ctx[:, vs]) + _dot(e_loc.astype(BF16), kv_loc[:, vs])) * (1.0 / den)
        outs += [o[g * BLOCK:(g + 1) * BLOCK] for g in range(GROUP)]
    o_ref[...] = jnp.concatenate(outs, axis=1).astype(BF16)


def _attend(q, kv, kv_ctx, sink):
    b, s, _ = q.shape
    nb = s // BLOCK
    c_len = kv_ctx.shape[1]
    kvw = 2 * KV_W
    return pl.pallas_call(
        _attend_body,
        out_shape=jax.ShapeDtypeStruct((b, s, ATTN_W), BF16),
        grid=(b, nb),
        in_specs=[
            pl.BlockSpec(memory_space=pltpu.SMEM),
            pl.BlockSpec((None, BLOCK, ATTN_W), lambda i, j: (i, j, 0)),
            pl.BlockSpec((None, BLOCK, kvw), lambda i, j: (i, jnp.maximum(j - 1, 0), 0)),
            pl.BlockSpec((None, BLOCK, kvw), lambda i, j: (i, j, 0)),
            pl.BlockSpec((None, BLOCK, kvw), lambda i, j: (i, jnp.minimum(j + 1, nb - 1), 0)),
            pl.BlockSpec((None, c_len, kvw), lambda i, j: (i, 0, 0)),
        ],
        out_specs=pl.BlockSpec((None, BLOCK, ATTN_W), lambda i, j: (i, j, 0)),
        compiler_params=_params("arbitrary", "arbitrary"),
        name="attend",
    )(sink, q, kv, kv, kv, kv_ctx)


def _route(logits):
    lane = lax.broadcasted_iota(jnp.int32, logits.shape, 1)
    far = jnp.int32(2 * ROUTE_W)

    def first_max(vals):
        top = jnp.max(vals, -1, keepdims=True)
        return top, jnp.min(jnp.where(vals == top, lane, far), -1, keepdims=True)

    is_g = lane < N_GROUPS
    g_top, g_idx = first_max(jnp.where(is_g, logits, NEG_INF))
    g_w = 1.0 / jnp.sum(jnp.where(is_g, jnp.exp(logits - g_top), 0.0), -1, keepdims=True)
    lo = N_GROUPS + EPG * g_idx
    e_vals = jnp.where((lane >= lo) & (lane < lo + EPG), logits, NEG_INF)
    v1, i1 = first_max(e_vals)
    v2, i2 = first_max(jnp.where(lane == i1, NEG_INF, e_vals))
    t = jnp.exp(v2 - v1)
    w1 = g_w * (1.0 / (1.0 + t))
    w2 = g_w * (t / (1.0 + t))
    rec = jnp.where(lane == 0, (i1 - N_GROUPS).astype(F32), 0.0)
    rec = jnp.where(lane == 1, (i2 - N_GROUPS).astype(F32), rec)
    rec = jnp.where(lane == 2, w1, rec)
    return jnp.where(lane == 3, w2, rec)


def _merge_body(x_ref, ya_ref, u_ref, vg_ref, gate_ref, ws_ref, bs_ref, wpa_ref, wpb_ref, wo_ref,
                lng_ref, lnb_ref, g1_ref, sh2_ref, sc2_ref, wrh_ref, wrl_ref, br_ref, x1_ref, rec_ref):
    d = x_ref.shape[-1]
    vg = vg_ref[...]
    lane = lax.broadcasted_iota(jnp.int32, (BLOCK, LANES), 1)
    pieces = []
    for j in range(GM_W // LANES):
        v = vg[:, j * LANES:(j + 1) * LANES]
        per = LANES // GM_HEAD
        sp = _dot(ws_ref[per * j], v)
        for k in range(1, per):
            sp = jnp.where(lane < k * GM_HEAD, sp, _dot(ws_ref[per * j + k], v))
        pieces.append(sp)
    sp = jnp.concatenate(pieces, axis=1) + bs_ref[...]
    y_gm = (u_ref[...].astype(F32) * sp).astype(BF16)
    gate = gate_ref[...].astype(F32)
    y = gate[:, :d] * _dot(ya_ref[...], wpa_ref[...]) + gate[:, d:] * _dot(y_gm, wpb_ref[...])
    mix = _dot(y.astype(BF16), wo_ref[...])
    x1 = _ln(ALPHA * x_ref[...] + g1_ref[...] * mix) * lng_ref[...] + lnb_ref[...]
    x1_ref[...] = x1
    h2 = _ln(x1) * (1.0 + sc2_ref[...]) + sh2_ref[...]
    rec_ref[...] = _route(_dot_split(h2, wrh_ref[...], wrl_ref[...]) + br_ref[...])


def _merge(x, y_attn, u, vg, gates, ws_bf, bs_rows, wpa, wpb, wo, ln_g, ln_b, mod3, wr_hi, wr_lo, br):
    b, s, d = x.shape
    nb = s // BLOCK
    row = lambda width: pl.BlockSpec((None, BLOCK, width), lambda i, j: (i, j, 0))
    full = lambda a: pl.BlockSpec(a.shape, lambda i, j: (0,) * a.ndim)
    modv = lambda k: pl.BlockSpec((None, 1, d), lambda i, j: (i, 0, k))
    ln_g = ln_g.reshape(1, d)
    ln_b = ln_b.reshape(1, d)
    return pl.pallas_call(
        _merge_body,
        out_shape=[jax.ShapeDtypeStruct((b, s, d), F32), jax.ShapeDtypeStruct((b, s, ROUTE_W), F32)],
        grid=(b, nb),
        in_specs=[
            row(d), row(ATTN_W), row(GM_W), row(GM_W), row(2 * d),
            full(ws_bf), full(bs_rows), full(wpa), full(wpb), full(wo), full(ln_g), full(ln_b),
            modv(2), modv(3), modv(4),
            full(wr_hi), full(wr_lo), full(br),
        ],
        out_specs=[row(d), row(ROUTE_W)],
        compiler_params=_params("arbitrary", "arbitrary"),
        name="merge",
    )(x, y_attn, u, vg, gates, ws_bf, bs_rows, wpa, wpb, wo, ln_g, ln_b, mod3, mod3, mod3, wr_hi, wr_lo, br)


def _expert(h, w1_ref, w3_ref, w2_ref):
    hid = jax.nn.silu(_dot(h, w1_ref[...])) * _dot(h, w3_ref[...])
    return _dot(hid.astype(BF16), w2_ref[...])


def _moe_body(ublk_ref, ucls_ref, ulo_ref, uhi_ref, ufirst_ref, ulast_ref, nunits_ref,
              tok_ref, tokn_ref, meta_ref, mod_ref, lng_ref, lnb_ref,
              w1a_ref, w3a_ref, w2a_ref, w1b_ref, w3b_ref, w2b_ref, x1_hbm, out_hbm,
              xbuf, obuf, h2_ref, acc_ref, gsem, ssem, *, n_batch, seq, n_blocks):
    u = pl.program_id(0)
    n_units = nunits_ref[0]
    blk = ublk_ref[u]
    slot = blk % 2
    d = acc_ref.shape[-1]

    def gather(tok_smem, dst_slot, start):
        for r in range(BLOCK):
            cp = pltpu.make_async_copy(x1_hbm.at[pl.ds(tok_smem[0, r], 1)], xbuf.at[dst_slot, pl.ds(r, 1)],
                                       gsem.at[dst_slot])
            if start:
                cp.start()
            else:
                cp.wait()

    def scatter(tok_smem, src_slot, start):
        for r in range(BLOCK):
            cp = pltpu.make_async_copy(obuf.at[src_slot, pl.ds(r, 1)], out_hbm.at[pl.ds(tok_smem[0, r], 1)],
                                       ssem.at[src_slot])
            if start:
                cp.start()
            else:
                cp.wait()

    live = u < n_units
    first = live & (ufirst_ref[u] == 1)
    last = live & (ulast_ref[u] == 1)

    @pl.when(u == 0)
    def _():
        gather(tok_ref, 0, True)

    @pl.when(first)
    def _():
        gather(tok_ref, slot, False)

        @pl.when(blk + 1 < n_blocks)
        def _():
            gather(tokn_ref, 1 - slot, True)

        meta = meta_ref[...]
        brow = meta[:, 3:4].astype(jnp.int32)
        sh = jnp.zeros((BLOCK, d), F32)
        sc = jnp.zeros((BLOCK, d), F32)
        for b in range(n_batch):
            sh = jnp.where(brow == b, mod_ref[b:b + 1, 3 * d:4 * d], sh)
            sc = jnp.where(brow == b, mod_ref[b:b + 1, 4 * d:5 * d], sc)
        h2_ref[...] = (_ln(xbuf[slot]) * (1.0 + sc) + sh).astype(BF16)
        acc_ref[...] = jnp.zeros_like(acc_ref)

    @pl.when(live)
    def _():
        meta = meta_ref[...]
        mine = meta[:, 0:1].astype(jnp.int32) == ucls_ref[u]
        w_lo = jnp.where(mine, meta[:, 1:2], 0.0)
        w_hi = jnp.where(mine, meta[:, 2:3], 0.0)
        h = h2_ref[...]
        acc_ref[...] += w_lo * _expert(h, w1a_ref, w3a_ref, w2a_ref) + w_hi * _expert(h, w1b_ref, w3b_ref, w2b_ref)

    @pl.when(last)
    def _():
        meta = meta_ref[...]
        brow = meta[:, 3:4].astype(jnp.int32)
        g2 = jnp.zeros((BLOCK, d), F32)
        for b in range(n_batch):
            g2 = jnp.where(brow == b, mod_ref[b:b + 1, 5 * d:6 * d], g2)

        @pl.when(blk >= 2)
        def _():
            scatter(tok_ref, slot, False)

        obuf[slot] = _ln(ALPHA * xbuf[slot] + g2 * acc_ref[...]) * lng_ref[...] + lnb_ref[...]
        scatter(tok_ref, slot, True)

    @pl.when(u == n_units - 1)
    def _():
        scatter(tok_ref, slot, False)

        @pl.when(n_blocks >= 2)
        def _():
            scatter(tok_ref, 1 - slot, False)


def _moe(x1, units, tok_sorted, meta_sorted, mod, ln_g, ln_b, w1, w3, w2, n_batch, seq):
    n, d = x1.shape
    n_blocks = n // BLOCK
    ublk, ucls, ulo, uhi, ufirst, ulast, nunits = units
    max_units = ublk.shape[0]
    de = w1.shape[-1]
    tok3 = tok_sorted.reshape(n_blocks, 1, BLOCK)

    def tok_spec(off):
        return pl.BlockSpec((None, 1, BLOCK), lambda u, ublk, *_: (jnp.minimum(ublk[u] + off, n_blocks - 1), 0, 0),
                            memory_space=pltpu.SMEM)

    def w_spec(shape, which):
        return pl.BlockSpec((None,) + shape, lambda u, ublk, ucls, ulo, uhi, *_: ((ulo, uhi)[which][u], 0, 0))

    full = lambda a: pl.BlockSpec(a.shape, lambda u, *_: (0,) * a.ndim)
    ln_g = ln_g.reshape(1, d)
    ln_b = ln_b.reshape(1, d)
    grid_spec = pltpu.PrefetchScalarGridSpec(
        num_scalar_prefetch=7,
        grid=(max_units,),
        in_specs=[
            tok_spec(0), tok_spec(1),
            pl.BlockSpec((BLOCK, ROUTE_W), lambda u, ublk, *_: (ublk[u], 0)),
            full(mod), full(ln_g), full(ln_b),
            w_spec((d, de), 0), w_spec((d, de), 0), w_spec((de, d), 0),
            w_spec((d, de), 1), w_spec((d, de), 1), w_spec((de, d), 1),
            pl.BlockSpec(memory_space=pl.ANY),
        ],
        out_specs=pl.BlockSpec(memory_space=pl.ANY),
        scratch_shapes=[
            pltpu.VMEM((2, BLOCK, d), F32),
            pltpu.VMEM((2, BLOCK, d), F32),
            pltpu.VMEM((BLOCK, d), BF16),
            pltpu.VMEM((BLOCK, d), F32),
            pltpu.SemaphoreType.DMA((2,)),
            pltpu.SemaphoreType.DMA((2,)),
        ],
    )
    return pl.pallas_call(
        functools.partial(_moe_body, n_batch=n_batch, seq=seq, n_blocks=n_blocks),
        out_shape=jax.ShapeDtypeStruct((n, d), F32),
        grid_spec=grid_spec,
        compiler_params=pltpu.CompilerParams(dimension_semantics=("arbitrary",), vmem_limit_bytes=VMEM_LIMIT_BYTES,
                                             has_side_effects=True),
        name="moe",
    )(ublk, ucls, ulo, uhi, ufirst, ulast, nunits, tok3, tok3, meta_sorted, mod, ln_g, ln_b,
      w1, w3, w2, w1, w3, w2, x1)


def _moe_plan(rec, n_batch, seq):
    n = rec.shape[0]
    n_blocks = n // BLOCK
    e1 = rec[:, 0].astype(jnp.int32)
    e2 = rec[:, 1].astype(jnp.int32)
    swap = e2 < e1
    lo = jnp.where(swap, e2, e1)
    hi = jnp.where(swap, e1, e2)
    w_lo = jnp.where(swap, rec[:, 3], rec[:, 2])
    w_hi = jnp.where(swap, rec[:, 2], rec[:, 3])
    grp = lo // EPG
    a = lo % EPG
    b = hi % EPG
    cls = grp * PAIRS + (a * (2 * EPG - 1 - a)) // 2 + (b - a - 1)
    order = jnp.argsort(cls, stable=True).astype(jnp.int32)
    cls_s = cls[order]
    batch_s = (order // seq).astype(F32)
    meta = jnp.zeros((n, ROUTE_W), F32)
    meta = meta.at[:, 0].set(cls_s.astype(F32)).at[:, 1].set(w_lo[order]).at[:, 2].set(w_hi[order]).at[:, 3].set(batch_s)
    pos = jnp.arange(n, dtype=jnp.int32)
    prev = jnp.concatenate([cls_s[:1] - 1, cls_s[:-1]])
    is_start = (pos % BLOCK == 0) | (cls_s != prev)
    max_units = n_blocks + N_CLASSES - 1
    n_units = jnp.sum(is_start).astype(jnp.int32)
    starts = jnp.nonzero(is_start, size=max_units, fill_value=0)[0].astype(jnp.int32)
    uidx = jnp.arange(max_units, dtype=jnp.int32)
    starts = jnp.where(uidx < n_units, starts, starts[n_units - 1])
    nxt = jnp.concatenate([starts[1:], jnp.full((1,), n, jnp.int32)])
    nxt = jnp.where(uidx + 1 < n_units, nxt, n)
    ublk = starts // BLOCK
    ufirst = (starts % BLOCK == 0).astype(jnp.int32)
    ulast = (nxt % BLOCK == 0).astype(jnp.int32)
    ucls = cls_s[starts]
    ulo = lo[order][starts]
    uhi = hi[order][starts]
    return (ublk, ucls, ulo, uhi, ufirst, ulast, n_units.reshape(1)), order, meta


def kernel(x, c, ctx, c_ctx, w_ada, b_ada, w_in, attn_sink, gm_ln_g, gm_ln_b, gm_ws, gm_bs, w_pa, w_pb, w_o,
           ln1_g, ln1_b, router_g_w, router_g_b, router_e_w, router_e_b, moe_w1, moe_w3, moe_w2, ln2_g, ln2_b):
    b, s, d = x.shape
    assert w_ada.shape[0] == DEPTH and s % BLOCK == 0 and s % GRID_W == 0
    mod_rows = 16
    assert b + 1 <= mod_rows
    c_rows = jnp.zeros((mod_rows, d), F32).at[:b].set(c).at[b].set(c_ctx)
    mod = _adaln_mod(c_rows, w_ada[0], b_ada[0])
    mod3 = mod.reshape(mod_rows, 1, 6 * d)
    w_in_bf = w_in[0].astype(BF16)
    kv_ctx = _ctx_kv(ctx, mod3, b, w_in_bf)
    tile = 512 if s % 512 == 0 else BLOCK
    q, kv, u, vg, gates = _project(x, mod3, w_in_bf, _rope_tables(s), gm_ln_g[0], gm_ln_b[0], tile)
    y_attn = _attend(q, kv, kv_ctx, attn_sink[0])
    bs_rows = jnp.repeat(gm_bs[0].T, GM_HEAD, axis=1)
    w_route = jnp.concatenate([router_g_w[0], jnp.transpose(router_e_w[0], (1, 0, 2)).reshape(d, N_EXPERTS)], axis=1)
    w_route = jnp.pad(w_route, ((0, 0), (0, ROUTE_W - w_route.shape[1])))
    b_route = jnp.pad(jnp.concatenate([router_g_b[0], router_e_b[0].reshape(-1)]), (0, ROUTE_W - N_GROUPS - N_EXPERTS))
    wr_hi = w_route.astype(BF16)
    wr_lo = (w_route - wr_hi.astype(F32)).astype(BF16)
    x1, rec = _merge(x, y_attn, u, vg, gates, gm_ws[0].astype(BF16), bs_rows, w_pa[0].astype(BF16),
                     w_pb[0].astype(BF16), w_o[0].astype(BF16), ln1_g[0], ln1_b[0], mod3, wr_hi, wr_lo,
                     b_route.reshape(1, ROUTE_W))
    n = b * s
    units, order, meta = _moe_plan(rec.reshape(n, ROUTE_W), b, s)
    out = _moe(x1.reshape(n, d), units, order, meta, mod, ln2_g[0], ln2_b[0],
               moe_w1[0].astype(BF16), moe_w3[0].astype(BF16), moe_w2[0].astype(BF16), b, s)
    return out.reshape(b, s, d)
```

```python
import functools

import jax
import jax.numpy as jnp
from jax import lax
from jax.experimental import pallas as pl
from jax.experimental.pallas import tpu as pltpu

F32 = jnp.float32
BF16 = jnp.bfloat16

GRID_W = 64
HEAD_DIM = 64
Q_HEADS = 8
KV_HEADS = 2
GROUP = Q_HEADS // KV_HEADS
BLOCK = 128
ROPE_BASE = 10000.0
ROPE_FREQS = HEAD_DIM // 4
GM_GROUPS = 8
GM_HEAD = 64
ATTN_W = Q_HEADS * HEAD_DIM
KV_W = KV_HEADS * HEAD_DIM
GM_W = GM_GROUPS * GM_HEAD
N_GROUPS = 4
EPG = 8
N_EXPERTS = N_GROUPS * EPG
PAIRS = EPG * (EPG - 1) // 2
N_CLASSES = N_GROUPS * PAIRS
LN_EPS = 1e-6
NEG_INF = -1e30
DEPTH = 1
ALPHA = (2.0 * DEPTH) ** 0.25

LANES = 128
VMEM_LIMIT_BYTES = 56 * 1024 * 1024
MOE_VMEM_LIMIT_BYTES = 60 * 1024 * 1024
ROUTE_W = LANES
META_W = 8


def _params(*sem):
    return pltpu.CompilerParams(dimension_semantics=sem, vmem_limit_bytes=VMEM_LIMIT_BYTES)


def _ln(x):
    mu = jnp.mean(x, -1, keepdims=True)
    xc = x - mu
    var = jnp.mean(xc * xc, -1, keepdims=True)
    return xc * lax.rsqrt(var + LN_EPS)


def _dot(a, b):
    return jnp.dot(a, b, preferred_element_type=F32)


def _dot_nt(a, b):
    return lax.dot_general(a, b, (((1,), (1,)), ((), ())), preferred_element_type=F32)


def _split(a):
    hi = a.astype(BF16)
    lo = (a - hi.astype(F32)).astype(BF16)
    return hi, lo


def _dot_split(a, b_hi, b_lo):
    a_hi, a_lo = _split(a)
    return _dot(a_hi, b_hi) + _dot(a_hi, b_lo) + _dot(a_lo, b_hi)


def _mod_body(c_ref, w_ref, b_ref, o_ref):
    w_hi, w_lo = _split(w_ref[...])
    o_ref[...] = _dot_split(jax.nn.silu(c_ref[...]), w_hi, w_lo) + b_ref[...]


def _adaln_mod(c_rows, w_ada, b_ada):
    rows, d = c_rows.shape
    width = w_ada.shape[1]
    tn = 6 * LANES
    return pl.pallas_call(
        _mod_body,
        out_shape=jax.ShapeDtypeStruct((rows, width), F32),
        grid=(width // tn,),
        in_specs=[
            pl.BlockSpec((rows, d), lambda j: (0, 0)),
            pl.BlockSpec((d, tn), lambda j: (0, j)),
            pl.BlockSpec((1, tn), lambda j: (0, j)),
        ],
        out_specs=pl.BlockSpec((rows, tn), lambda j: (0, j)),
        compiler_params=_params("arbitrary"),
        name="mod",
    )(c_rows, w_ada, b_ada.reshape(1, width))


def _ctx_body(ctx_ref, sh_ref, sc_ref, w_ref, o_ref):
    h = _ln(ctx_ref[...]) * (1.0 + sc_ref[...]) + sh_ref[...]
    o_ref[...] = _dot(h.astype(BF16), w_ref[...]).astype(BF16)


def _ctx_kv(ctx, mod3, ctx_row, w_in_bf):
    b, c_len, d = ctx.shape
    kvw = 2 * KV_W
    return pl.pallas_call(
        _ctx_body,
        out_shape=jax.ShapeDtypeStruct((b, c_len, kvw), BF16),
        grid=(b,),
        in_specs=[
            pl.BlockSpec((None, c_len, d), lambda i: (i, 0, 0)),
            pl.BlockSpec((None, 1, d), lambda i: (ctx_row, 0, 0)),
            pl.BlockSpec((None, 1, d), lambda i: (ctx_row, 0, 1)),
            pl.BlockSpec((d, kvw), lambda i: (0, ATTN_W // kvw)),
        ],
        out_specs=pl.BlockSpec((None, c_len, kvw), lambda i: (i, 0, 0)),
        compiler_params=_params("arbitrary"),
        name="ctx_kv",
    )(ctx, mod3, mod3, w_in_bf)


def _rope(x, cos, s_up, s_dn):
    return x * cos + pltpu.roll(x, LANES - ROPE_FREQS, 1) * s_up + pltpu.roll(x, ROPE_FREQS, 1) * s_dn


def _project_body(x_ref, sh_ref, sc_ref, w_ref, cos_ref, sup_ref, sdn_ref, lng_ref, lnb_ref,
                  q_ref, kv_ref, u_ref, vg_ref, gate_ref):
    h = (_ln(x_ref[...]) * (1.0 + sc_ref[...]) + sh_ref[...]).astype(BF16)
    cos, s_up, s_dn = cos_ref[...], sup_ref[...], sdn_ref[...]
    q_scale = HEAD_DIM ** -0.5
    k0 = ATTN_W
    u0 = k0 + 2 * KV_W
    vg0 = u0 + GM_W
    gate0 = vg0 + GM_W
    q = _dot(h, w_ref[:, 0:k0])
    for j in range(ATTN_W // LANES):
        sl = slice(j * LANES, (j + 1) * LANES)
        q_ref[:, sl] = (_rope(q[:, sl], cos, s_up, s_dn) * q_scale).astype(BF16)
    kv = _dot(h, w_ref[:, k0:u0])
    kv_ref[:, 0:KV_W] = _rope(kv[:, 0:KV_W], cos, s_up, s_dn).astype(BF16)
    kv_ref[:, KV_W:] = kv[:, KV_W:].astype(BF16)
    u_ref[...] = jax.nn.gelu(_dot(h, w_ref[:, u0:vg0])).astype(BF16)
    vg = _ln(jax.nn.gelu(_dot(h, w_ref[:, vg0:gate0]))) * lng_ref[...] + lnb_ref[...]
    vg_ref[...] = vg.astype(BF16)
    gate_ref[...] = jax.nn.sigmoid(_dot(h, w_ref[:, gate0:])).astype(BF16)


def _project(x, mod3, w_in_bf, rope_tabs, gm_ln_g, gm_ln_b, tile):
    b, s, d = x.shape
    in_w = w_in_bf.shape[1]
    gate_w = in_w - (ATTN_W + 2 * KV_W + 2 * GM_W)
    row = lambda width: pl.BlockSpec((None, tile, width), lambda i, t: (i, t, 0))
    tab = pl.BlockSpec((tile, LANES), lambda i, t: (t, 0))
    vec = lambda width: pl.BlockSpec((1, width), lambda i, t: (0, 0))
    return pl.pallas_call(
        _project_body,
        out_shape=[
            jax.ShapeDtypeStruct((b, s, ATTN_W), BF16),
            jax.ShapeDtypeStruct((b, s, 2 * KV_W), BF16),
            jax.ShapeDtypeStruct((b, s, GM_W), BF16),
            jax.ShapeDtypeStruct((b, s, GM_W), BF16),
            jax.ShapeDtypeStruct((b, s, gate_w), BF16),
        ],
        grid=(b, s // tile),
        in_specs=[
            row(d),
            pl.BlockSpec((None, 1, d), lambda i, t: (i, 0, 0)),
            pl.BlockSpec((None, 1, d), lambda i, t: (i, 0, 1)),
            pl.BlockSpec((d, in_w), lambda i, t: (0, 0)),
            tab, tab, tab,
            vec(GM_W), vec(GM_W),
        ],
        out_specs=[row(ATTN_W), row(2 * KV_W), row(GM_W), row(GM_W), row(gate_w)],
        compiler_params=_params("arbitrary", "arbitrary"),
        name="project",
    )(x, mod3, mod3, w_in_bf, *rope_tabs, gm_ln_g.reshape(1, GM_W), gm_ln_b.reshape(1, GM_W))


def _rope_tables(s):
    pos = jnp.arange(s, dtype=jnp.int32)
    inv = 1.0 / (ROPE_BASE ** (jnp.arange(ROPE_FREQS, dtype=F32) / ROPE_FREQS))
    ang_r = (pos // GRID_W).astype(F32)[:, None] * inv[None, :]
    ang_c = (pos % GRID_W).astype(F32)[:, None] * inv[None, :]
    cr, sr, cc, sn = jnp.cos(ang_r), jnp.sin(ang_r), jnp.cos(ang_c), jnp.sin(ang_c)
    z = jnp.zeros_like(sr)
    reps = LANES // HEAD_DIM
    cos = jnp.tile(jnp.concatenate([cr, cr, cc, cc], -1), (1, reps))
    s_up = jnp.tile(jnp.concatenate([-sr, z, -sn, z], -1), (1, reps))
    s_dn = jnp.tile(jnp.concatenate([z, sr, z, sn], -1), (1, reps))
    return cos, s_up, s_dn


def _attend_body(sink_ref, q_ref, kvp_ref, kvc_ref, kvn_ref, ctx_ref, o_ref):
    i = pl.program_id(1)
    nb = pl.num_programs(1)
    rows = GROUP * BLOCK
    r = lax.broadcasted_iota(jnp.int32, (rows, 3 * BLOCK), 0) & (BLOCK - 1)
    m = lax.broadcasted_iota(jnp.int32, (rows, 3 * BLOCK), 1)
    mask = (m >= r) & (m <= r + 2 * BLOCK)
    mask &= (m >= BLOCK) | (i > 0)
    mask &= (m < 2 * BLOCK) | (i < nb - 1)
    grp = lax.broadcasted_iota(jnp.int32, (rows, 1), 0) // BLOCK
    kv_loc = jnp.concatenate([kvp_ref[...], kvc_ref[...], kvn_ref[...]], axis=0)
    kv_ctx = ctx_ref[...]
    q = q_ref[...]
    outs = []
    for h in range(KV_HEADS):
        ks = slice(h * HEAD_DIM, (h + 1) * HEAD_DIM)
        vs = slice(KV_W + h * HEAD_DIM, KV_W + (h + 1) * HEAD_DIM)
        qg = jnp.concatenate(
            [q[:, (h * GROUP + g) * HEAD_DIM:(h * GROUP + g + 1) * HEAD_DIM] for g in range(GROUP)], axis=0)
        sink = jnp.zeros((rows, 1), F32)
        for g in range(GROUP):
            sink = jnp.where(grp == g, sink_ref[h * GROUP + g], sink)
        s_ctx = _dot_nt(qg, kv_ctx[:, ks])
        s_loc = jnp.where(mask, _dot_nt(qg, kv_loc[:, ks]), NEG_INF)
        top = jnp.maximum(jnp.maximum(jnp.max(s_ctx, -1, keepdims=True), jnp.max(s_loc, -1, keepdims=True)), sink)
        e_ctx = jnp.exp(s_ctx - top)
        e_loc = jnp.exp(s_loc - top)
        den = jnp.sum(e_ctx, -1, keepdims=True) + jnp.sum(e_loc, -1, keepdims=True) + jnp.exp(sink - top)
        o = (_dot(e_ctx.astype(BF16), kv_ctx[:, vs]) + _dot(e_loc.astype(BF16), kv_loc[:, vs])) * (1.0 / den)
        outs += [o[g * BLOCK:(g + 1) * BLOCK] for g in range(GROUP)]
    o_ref[...] = jnp.concatenate(outs, axis=1).astype(BF16)


def _attend(q, kv, kv_ctx, sink):
    b, s, _ = q.shape
    nb = s // BLOCK
    c_len = kv_ctx.shape[1]
    kvw = 2 * KV_W
    return pl.pallas_call(
        _attend_body,
        out_shape=jax.ShapeDtypeStruct((b, s, ATTN_W), BF16),
        grid=(b, nb),
        in_specs=[
            pl.BlockSpec(memory_space=pltpu.SMEM),
            pl.BlockSpec((None, BLOCK, ATTN_W), lambda i, j: (i, j, 0)),
            pl.BlockSpec((None, BLOCK, kvw), lambda i, j: (i, jnp.maximum(j - 1, 0), 0)),
            pl.BlockSpec((None, BLOCK, kvw), lambda i, j: (i, j, 0)),
            pl.BlockSpec((None, BLOCK, kvw), lambda i, j: (i, jnp.minimum(j + 1, nb - 1), 0)),
            pl.BlockSpec((None, c_len, kvw), lambda i, j: (i, 0, 0)),
        ],
        out_specs=pl.BlockSpec((None, BLOCK, ATTN_W), lambda i, j: (i, j, 0)),
        compiler_params=_params("arbitrary", "arbitrary"),
        name="attend",
    )(sink, q, kv, kv, kv, kv_ctx)


def _route(logits):
    lane = lax.broadcasted_iota(jnp.int32, logits.shape, 1)
    far = jnp.int32(2 * ROUTE_W)

    def first_max(vals):
        top = jnp.max(vals, -1, keepdims=True)
        return top, jnp.min(jnp.where(vals == top, lane, far), -1, keepdims=True)

    is_g = lane < N_GROUPS
    g_top, g_idx = first_max(jnp.where(is_g, logits, NEG_INF))
    g_w = 1.0 / jnp.sum(jnp.where(is_g, jnp.exp(logits - g_top), 0.0), -1, keepdims=True)
    lo = N_GROUPS + EPG * g_idx
    e_vals = jnp.where((lane >= lo) & (lane < lo + EPG), logits, NEG_INF)
    v1, i1 = first_max(e_vals)
    v2, i2 = first_max(jnp.where(lane == i1, NEG_INF, e_vals))
    t = jnp.exp(v2 - v1)
    w1 = g_w * (1.0 / (1.0 + t))
    w2 = g_w * (t / (1.0 + t))
    rec = jnp.where(lane == 0, (i1 - N_GROUPS).astype(F32), 0.0)
    rec = jnp.where(lane == 1, (i2 - N_GROUPS).astype(F32), rec)
    rec = jnp.where(lane == 2, w1, rec)
    return jnp.where(lane == 3, w2, rec)


def _merge_body(x_ref, ya_ref, u_ref, vg_ref, gate_ref, ws_ref, bs_ref, wpa_ref, wpb_ref, wo_ref,
                lng_ref, lnb_ref, g1_ref, sh2_ref, sc2_ref, wrh_ref, wrl_ref, br_ref, x1_ref, rec_ref):
    d = x_ref.shape[-1]
    vg = vg_ref[...]
    lane = lax.broadcasted_iota(jnp.int32, (BLOCK, LANES), 1)
    pieces = []
    for j in range(GM_W // LANES):
        v = vg[:, j * LANES:(j + 1) * LANES]
        per = LANES // GM_HEAD
        sp = _dot(ws_ref[per * j], v)
        for k in range(1, per):
            sp = jnp.where(lane < k * GM_HEAD, sp, _dot(ws_ref[per * j + k], v))
        pieces.append(sp)
    sp = jnp.concatenate(pieces, axis=1) + bs_ref[...]
    y_gm = (u_ref[...].astype(F32) * sp).astype(BF16)
    gate = gate_ref[...].astype(F32)
    y = gate[:, :d] * _dot(ya_ref[...], wpa_ref[...]) + gate[:, d:] * _dot(y_gm, wpb_ref[...])
    mix = _dot(y.astype(BF16), wo_ref[...])
    x1 = _ln(ALPHA * x_ref[...] + g1_ref[...] * mix) * lng_ref[...] + lnb_ref[...]
    x1_ref[...] = x1
    h2 = _ln(x1) * (1.0 + sc2_ref[...]) + sh2_ref[...]
    rec_ref[...] = _route(_dot_split(h2, wrh_ref[...], wrl_ref[...]) + br_ref[...])


def _merge(x, y_attn, u, vg, gates, ws_bf, bs_rows, wpa, wpb, wo, ln_g, ln_b, mod3, wr_hi, wr_lo, br):
    b, s, d = x.shape
    nb = s // BLOCK
    row = lambda width: pl.BlockSpec((None, BLOCK, width), lambda i, j: (i, j, 0))
    full = lambda a: pl.BlockSpec(a.shape, lambda i, j: (0,) * a.ndim)
    modv = lambda k: pl.BlockSpec((None, 1, d), lambda i, j: (i, 0, k))
    ln_g = ln_g.reshape(1, d)
    ln_b = ln_b.reshape(1, d)
    return pl.pallas_call(
        _merge_body,
        out_shape=[jax.ShapeDtypeStruct((b, s, d), F32), jax.ShapeDtypeStruct((b, s, ROUTE_W), F32)],
        grid=(b, nb),
        in_specs=[
            row(d), row(ATTN_W), row(GM_W), row(GM_W), row(2 * d),
            full(ws_bf), full(bs_rows), full(wpa), full(wpb), full(wo), full(ln_g), full(ln_b),
            modv(2), modv(3), modv(4),
            full(wr_hi), full(wr_lo), full(br),
        ],
        out_specs=[row(d), row(ROUTE_W)],
        compiler_params=_params("arbitrary", "arbitrary"),
        name="merge",
    )(x, y_attn, u, vg, gates, ws_bf, bs_rows, wpa, wpb, wo, ln_g, ln_b, mod3, mod3, mod3, wr_hi, wr_lo, br)


def _expert(h, w1, w3, w2):
    hid = jax.nn.silu(_dot(h, w1)) * _dot(h, w3)
    return _dot(hid, w2)


def _row_copies(src_hbm, idx_smem, buf, slot, sem):
    return [pltpu.make_async_copy(src_hbm.at[pl.ds(idx_smem[0, r], 1)], buf.at[slot, pl.ds(r, 1)], sem.at[slot])
            for r in range(BLOCK)]


def _moe_body(bgrp_ref, blo_ref, bhi_ref, nreal_ref, tok_ref, tokn_ref, meta_ref, mod_ref,
              w1_ref, w3_ref, w2_ref, x1_hbm, o_ref, xbuf, gsem, *, n_batch):
    u = pl.program_id(0)
    n_real = nreal_ref[0]
    slot = u % 2
    d = o_ref.shape[-1]

    @pl.when(u == 0)
    def _():
        for cp in _row_copies(x1_hbm, tok_ref, xbuf, 0, gsem):
            cp.start()

    @pl.when(u < n_real)
    def _():
        for cp in _row_copies(x1_hbm, tok_ref, xbuf, slot, gsem):
            cp.wait()
        for cp in _row_copies(x1_hbm, tokn_ref, xbuf, 1 - slot, gsem):
            cp.start()
        meta = meta_ref[...]
        w_lo = meta[:, 0:1]
        w_hi = meta[:, 1:2]
        brow = meta[:, 2:3].astype(jnp.int32)
        sh = jnp.zeros((BLOCK, d), F32)
        sc = jnp.zeros((BLOCK, d), F32)
        for b in range(n_batch):
            sh = jnp.where(brow == b, mod_ref[b:b + 1, 3 * d:4 * d], sh)
            sc = jnp.where(brow == b, mod_ref[b:b + 1, 4 * d:5 * d], sc)
        h = _ln(xbuf[slot]) * (1.0 + sc) + sh
        a = blo_ref[u]
        b = bhi_ref[u]
        o_ref[...] = (w_lo * _expert(h, w1_ref[a], w3_ref[a], w2_ref[a])
                      + w_hi * _expert(h, w1_ref[b], w3_ref[b], w2_ref[b]))

    @pl.when(u == n_real - 1)
    def _():
        for cp in _row_copies(x1_hbm, tokn_ref, xbuf, 1 - slot, gsem):
            cp.wait()

    @pl.when(u >= n_real)
    def _():
        o_ref[...] = jnp.zeros_like(o_ref)


def _moe(x1, plan, mod, w1, w3, w2, n_batch):
    n, d = x1.shape
    bgrp, blo, bhi, n_real, tok_pad, meta_pad = plan
    max_blocks = bgrp.shape[0]
    de = w1.shape[-1]
    tok3 = tok_pad.reshape(max_blocks, 1, BLOCK)
    w1g = w1.reshape(N_GROUPS, EPG, d, de)
    w3g = w3.reshape(N_GROUPS, EPG, d, de)
    w2g = w2.reshape(N_GROUPS, EPG, de, d)

    def clamp(u, n_real, off):
        return jnp.minimum(u + off, n_real[0] - 1)

    def tok_spec(off):
        return pl.BlockSpec((None, 1, BLOCK), lambda u, g, lo, hi, n_real: (clamp(u, n_real, off), 0, 0),
                            memory_space=pltpu.SMEM)

    def w_spec(a):
        return pl.BlockSpec((None,) + a.shape[1:], lambda u, g, lo, hi, n_real: (g[u], 0, 0, 0),
                            pipeline_mode=pl.Buffered(1))

    grid_spec = pltpu.PrefetchScalarGridSpec(
        num_scalar_prefetch=4,
        grid=(max_blocks,),
        in_specs=[
            tok_spec(0), tok_spec(1),
            pl.BlockSpec((BLOCK, meta_pad.shape[1]), lambda u, g, lo, hi, n_real: (clamp(u, n_real, 0), 0)),
            pl.BlockSpec(mod.shape, lambda u, *_: (0, 0)),
            w_spec(w1g), w_spec(w3g), w_spec(w2g),
            pl.BlockSpec(memory_space=pl.ANY),
        ],
        out_specs=pl.BlockSpec((BLOCK, d), lambda u, *_: (u, 0)),
        scratch_shapes=[pltpu.VMEM((2, BLOCK, d), F32), pltpu.SemaphoreType.DMA((2,))],
    )
    return pl.pallas_call(
        functools.partial(_moe_body, n_batch=n_batch),
        out_shape=jax.ShapeDtypeStruct((max_blocks * BLOCK, d), F32),
        grid_spec=grid_spec,
        compiler_params=pltpu.CompilerParams(dimension_semantics=("arbitrary",), vmem_limit_bytes=MOE_VMEM_LIMIT_BYTES),
        name="moe",
    )(bgrp, blo, bhi, n_real, tok3, tok3, meta_pad, mod, w1g, w3g, w2g, x1)


def _final_body(pos_ref, posn_ref, x1_ref, g2_ref, lng_ref, lnb_ref, ms_hbm, o_ref, gbuf, gsem):
    u = pl.program_id(0)
    slot = u % 2

    @pl.when(u == 0)
    def _():
        for cp in _row_copies(ms_hbm, pos_ref, gbuf, 0, gsem):
            cp.start()

    for cp in _row_copies(ms_hbm, pos_ref, gbuf, slot, gsem):
        cp.wait()
    for cp in _row_copies(ms_hbm, posn_ref, gbuf, 1 - slot, gsem):
        cp.start()
    o_ref[...] = _ln(ALPHA * x1_ref[...] + g2_ref[...] * gbuf[slot]) * lng_ref[...] + lnb_ref[...]

    @pl.when(u == pl.num_programs(0) - 1)
    def _():
        for cp in _row_copies(ms_hbm, posn_ref, gbuf, 1 - slot, gsem):
            cp.wait()


def _final(x1, ms, pos_tok, mod3, ln_g, ln_b, seq):
    n, d = x1.shape
    n_blocks = n // BLOCK
    per_batch = seq // BLOCK
    pos3 = pos_tok.reshape(n_blocks, 1, BLOCK)
    ln_g = ln_g.reshape(1, d)
    ln_b = ln_b.reshape(1, d)
    vec = pl.BlockSpec((1, d), lambda u: (0, 0))

    def pos_spec(off):
        return pl.BlockSpec((None, 1, BLOCK), lambda u: (jnp.minimum(u + off, n_blocks - 1), 0, 0),
                            memory_space=pltpu.SMEM)

    return pl.pallas_call(
        _final_body,
        out_shape=jax.ShapeDtypeStruct((n, d), F32),
        grid=(n_blocks,),
        in_specs=[
            pos_spec(0), pos_spec(1),
            pl.BlockSpec((BLOCK, d), lambda u: (u, 0)),
            pl.BlockSpec((None, 1, d), lambda u: (u // per_batch, 0, 5)),
            vec, vec,
            pl.BlockSpec(memory_space=pl.ANY),
        ],
        out_specs=pl.BlockSpec((BLOCK, d), lambda u: (u, 0)),
        scratch_shapes=[pltpu.VMEM((2, BLOCK, d), F32), pltpu.SemaphoreType.DMA((2,))],
        compiler_params=_params("arbitrary"),
        name="final",
    )(pos3, pos3, x1, mod3, ln_g, ln_b, ms)


def _moe_plan(rec, seq):
    n = rec.shape[0]
    max_blocks = n // BLOCK + N_CLASSES - 1
    e1 = rec[:, 0].astype(jnp.int32)
    e2 = rec[:, 1].astype(jnp.int32)
    swap = e2 < e1
    lo = jnp.where(swap, e2, e1)
    hi = jnp.where(swap, e1, e2)
    w_lo = jnp.where(swap, rec[:, 3], rec[:, 2])
    w_hi = jnp.where(swap, rec[:, 2], rec[:, 3])
    a = lo % EPG
    b = hi % EPG
    cls = (lo // EPG) * PAIRS + (a * (2 * EPG - 1 - a)) // 2 + (b - a - 1)
    counts = jnp.sum((cls[:, None] == jnp.arange(N_CLASSES, dtype=jnp.int32)[None, :]).astype(jnp.int32), axis=0)
    padded = (counts + BLOCK - 1) // BLOCK * BLOCK
    pad_end = jnp.cumsum(padded)
    pad_start = pad_end - padded
    start = jnp.cumsum(counts) - counts
    order = jnp.argsort(cls, stable=True).astype(jnp.int32)
    cls_s = cls[order]
    pos_s = pad_start[cls_s] + jnp.arange(n, dtype=jnp.int32) - start[cls_s]
    rows = jnp.stack([order.astype(F32), w_lo[order], w_hi[order], (order // seq).astype(F32)], axis=1)
    padded_rows = jnp.zeros((max_blocks * BLOCK, 4), F32).at[pos_s].set(rows)
    tok_pad = padded_rows[:, 0].astype(jnp.int32)
    meta_pad = jnp.pad(padded_rows[:, 1:], ((0, 0), (0, META_W - 3)))
    pos_tok = jnp.zeros((n,), jnp.int32).at[order].set(pos_s)
    n_real = (pad_end[-1] // BLOCK).astype(jnp.int32)
    blk = jnp.minimum(jnp.arange(max_blocks, dtype=jnp.int32), n_real - 1)
    blk_cls = jnp.sum((pad_end[None, :] <= (blk * BLOCK)[:, None]).astype(jnp.int32), axis=1)
    blk_cls = jnp.minimum(blk_cls, N_CLASSES - 1)
    pair = blk_cls % PAIRS
    pa = jnp.array([i for i in range(EPG) for _ in range(i + 1, EPG)], jnp.int32)
    pb = jnp.array([j for i in range(EPG) for j in range(i + 1, EPG)], jnp.int32)
    return (blk_cls // PAIRS, pa[pair], pb[pair], n_real.reshape(1), tok_pad, meta_pad), pos_tok


def kernel(x, c, ctx, c_ctx, w_ada, b_ada, w_in, attn_sink, gm_ln_g, gm_ln_b, gm_ws, gm_bs, w_pa, w_pb, w_o,
           ln1_g, ln1_b, router_g_w, router_g_b, router_e_w, router_e_b, moe_w1, moe_w3, moe_w2, ln2_g, ln2_b):
    b, s, d = x.shape
    assert w_ada.shape[0] == DEPTH and s % BLOCK == 0 and s % GRID_W == 0
    mod_rows = 16
    assert b + 1 <= mod_rows
    c_rows = jnp.zeros((mod_rows, d), F32).at[:b].set(c).at[b].set(c_ctx)
    mod = _adaln_mod(c_rows, w_ada[0], b_ada[0])
    mod3 = mod.reshape(mod_rows, 1, 6 * d)
    w_in_bf = w_in[0].astype(BF16)
    kv_ctx = _ctx_kv(ctx, mod3, b, w_in_bf)
    tile = 512 if s % 512 == 0 else BLOCK
    q, kv, u, vg, gates = _project(x, mod3, w_in_bf, _rope_tables(s), gm_ln_g[0], gm_ln_b[0], tile)
    y_attn = _attend(q, kv, kv_ctx, attn_sink[0])
    bs_rows = jnp.repeat(gm_bs[0].T, GM_HEAD, axis=1)
    w_route = jnp.concatenate([router_g_w[0], jnp.transpose(router_e_w[0], (1, 0, 2)).reshape(d, N_EXPERTS)], axis=1)
    w_route = jnp.pad(w_route, ((0, 0), (0, ROUTE_W - w_route.shape[1])))
    b_route = jnp.pad(jnp.concatenate([router_g_b[0], router_e_b[0].reshape(-1)]), (0, ROUTE_W - N_GROUPS - N_EXPERTS))
    wr_hi = w_route.astype(BF16)
    wr_lo = (w_route - wr_hi.astype(F32)).astype(BF16)
    x1, rec = _merge(x, y_attn, u, vg, gates, gm_ws[0].astype(BF16), bs_rows, w_pa[0].astype(BF16),
                     w_pb[0].astype(BF16), w_o[0].astype(BF16), ln1_g[0], ln1_b[0], mod3, wr_hi, wr_lo,
                     b_route.reshape(1, ROUTE_W))
    n = b * s
    plan, pos_tok = _moe_plan(rec.reshape(n, ROUTE_W), s)
    x1 = x1.reshape(n, d)
    ms = _moe(x1, plan, mod, moe_w1[0], moe_w3[0], moe_w2[0], b)
    out = _final(x1, ms, pos_tok, mod3, ln2_g[0], ln2_b[0], s)
    return out.reshape(b, s, d)
```

```python
import functools

import jax
import jax.numpy as jnp
from jax import lax
from jax.experimental import pallas as pl
from jax.experimental.pallas import tpu as pltpu

F32 = jnp.float32
BF16 = jnp.bfloat16
I32 = jnp.int32

GRID_W = 64
HEAD_DIM = 64
Q_HEADS = 8
KV_HEADS = 2
GROUP = Q_HEADS // KV_HEADS
BLOCK = 128
ROPE_BASE = 10000.0
ROPE_FREQS = HEAD_DIM // 4
GM_GROUPS = 8
GM_HEAD = 64
ATTN_W = Q_HEADS * HEAD_DIM
KV_W = KV_HEADS * HEAD_DIM
GM_W = GM_GROUPS * GM_HEAD
N_GROUPS = 4
EPG = 8
N_EXPERTS = N_GROUPS * EPG
PAIRS = EPG * (EPG - 1) // 2
N_CLASSES = N_GROUPS * PAIRS
LN_EPS = 1e-6
NEG_INF = -1e30
DEPTH = 1
ALPHA = (2.0 * DEPTH) ** 0.25

LANES = 128
SUBLANES = 8
VMEM_LIMIT_BYTES = 56 * 1024 * 1024
MOE_VMEM_LIMIT_BYTES = 60 * 1024 * 1024
ROUTE_W = LANES
REC_WLO, REC_WHI, REC_CLS, REC_RANK = 0, 1, 2, 3


def _params(*sem):
    return pltpu.CompilerParams(dimension_semantics=sem, vmem_limit_bytes=VMEM_LIMIT_BYTES)


def _ln(x):
    mu = jnp.mean(x, -1, keepdims=True)
    xc = x - mu
    var = jnp.mean(xc * xc, -1, keepdims=True)
    return xc * lax.rsqrt(var + LN_EPS)


def _dot(a, b):
    return jnp.dot(a, b, preferred_element_type=F32)


def _dot_nt(a, b):
    return lax.dot_general(a, b, (((1,), (1,)), ((), ())), preferred_element_type=F32)


def _split(a):
    hi = a.astype(BF16)
    lo = (a - hi.astype(F32)).astype(BF16)
    return hi, lo


def _dot_split(a, b_hi, b_lo):
    a_hi, a_lo = _split(a)
    return _dot(a_hi, b_hi) + _dot(a_hi, b_lo) + _dot(a_lo, b_hi)


def _iota(shape, axis):
    return lax.broadcasted_iota(I32, shape, axis)


def _mod_body(c_ref, w_ref, b_ref, o_ref):
    w_hi, w_lo = _split(w_ref[...])
    o_ref[...] = _dot_split(jax.nn.silu(c_ref[...]), w_hi, w_lo) + b_ref[...]


def _adaln_mod(c_rows, w_ada, b_ada):
    rows, d = c_rows.shape
    width = w_ada.shape[1]
    tn = 6 * LANES
    return pl.pallas_call(
        _mod_body,
        out_shape=jax.ShapeDtypeStruct((rows, width), F32),
        grid=(width // tn,),
        in_specs=[
            pl.BlockSpec((rows, d), lambda j: (0, 0)),
            pl.BlockSpec((d, tn), lambda j: (0, j)),
            pl.BlockSpec((1, tn), lambda j: (0, j)),
        ],
        out_specs=pl.BlockSpec((rows, tn), lambda j: (0, j)),
        compiler_params=_params("arbitrary"),
        name="mod",
    )(c_rows, w_ada, b_ada.reshape(1, width))


def _ctx_body(ctx_ref, sh_ref, sc_ref, w_ref, o_ref):
    h = _ln(ctx_ref[...]) * (1.0 + sc_ref[...]) + sh_ref[...]
    o_ref[...] = _dot(h.astype(BF16), w_ref[...]).astype(BF16)


def _ctx_kv(ctx, mod3, ctx_row, w_in_bf):
    b, c_len, d = ctx.shape
    kvw = 2 * KV_W
    return pl.pallas_call(
        _ctx_body,
        out_shape=jax.ShapeDtypeStruct((b, c_len, kvw), BF16),
        grid=(b,),
        in_specs=[
            pl.BlockSpec((None, c_len, d), lambda i: (i, 0, 0)),
            pl.BlockSpec((None, 1, d), lambda i: (ctx_row, 0, 0)),
            pl.BlockSpec((None, 1, d), lambda i: (ctx_row, 0, 1)),
            pl.BlockSpec((d, kvw), lambda i: (0, ATTN_W // kvw)),
        ],
        out_specs=pl.BlockSpec((None, c_len, kvw), lambda i: (i, 0, 0)),
        compiler_params=_params("arbitrary"),
        name="ctx_kv",
    )(ctx, mod3, mod3, w_in_bf)


def _rope(x, cos, s_up, s_dn):
    return x * cos + pltpu.roll(x, LANES - ROPE_FREQS, 1) * s_up + pltpu.roll(x, ROPE_FREQS, 1) * s_dn


def _project_body(x_ref, sh_ref, sc_ref, w_ref, cos_ref, sup_ref, sdn_ref, lng_ref, lnb_ref,
                  q_ref, kv_ref, u_ref, vg_ref, gate_ref):
    h = (_ln(x_ref[...]) * (1.0 + sc_ref[...]) + sh_ref[...]).astype(BF16)
    cos, s_up, s_dn = cos_ref[...], sup_ref[...], sdn_ref[...]
    q_scale = HEAD_DIM ** -0.5
    k0 = ATTN_W
    u0 = k0 + 2 * KV_W
    vg0 = u0 + GM_W
    gate0 = vg0 + GM_W
    q = _dot(h, w_ref[:, 0:k0])
    for j in range(ATTN_W // LANES):
        sl = slice(j * LANES, (j + 1) * LANES)
        q_ref[:, sl] = (_rope(q[:, sl], cos, s_up, s_dn) * q_scale).astype(BF16)
    kv = _dot(h, w_ref[:, k0:u0])
    kv_ref[:, 0:KV_W] = _rope(kv[:, 0:KV_W], cos, s_up, s_dn).astype(BF16)
    kv_ref[:, KV_W:] = kv[:, KV_W:].astype(BF16)
    u_ref[...] = jax.nn.gelu(_dot(h, w_ref[:, u0:vg0])).astype(BF16)
    vg = _ln(jax.nn.gelu(_dot(h, w_ref[:, vg0:gate0]))) * lng_ref[...] + lnb_ref[...]
    vg_ref[...] = vg.astype(BF16)
    gate_ref[...] = jax.nn.sigmoid(_dot(h, w_ref[:, gate0:])).astype(BF16)


def _project(x, mod3, w_in_bf, rope_tabs, gm_ln_g, gm_ln_b, tile):
    b, s, d = x.shape
    in_w = w_in_bf.shape[1]
    gate_w = in_w - (ATTN_W + 2 * KV_W + 2 * GM_W)
    row = lambda width: pl.BlockSpec((None, tile, width), lambda i, t: (i, t, 0))
    tab = pl.BlockSpec((tile, LANES), lambda i, t: (t, 0))
    vec = lambda width: pl.BlockSpec((1, width), lambda i, t: (0, 0))
    return pl.pallas_call(
        _project_body,
        out_shape=[
            jax.ShapeDtypeStruct((b, s, ATTN_W), BF16),
            jax.ShapeDtypeStruct((b, s, 2 * KV_W), BF16),
            jax.ShapeDtypeStruct((b, s, GM_W), BF16),
            jax.ShapeDtypeStruct((b, s, GM_W), BF16),
            jax.ShapeDtypeStruct((b, s, gate_w), BF16),
        ],
        grid=(b, s // tile),
        in_specs=[
            row(d),
            pl.BlockSpec((None, 1, d), lambda i, t: (i, 0, 0)),
            pl.BlockSpec((None, 1, d), lambda i, t: (i, 0, 1)),
            pl.BlockSpec((d, in_w), lambda i, t: (0, 0)),
            tab, tab, tab,
            vec(GM_W), vec(GM_W),
        ],
        out_specs=[row(ATTN_W), row(2 * KV_W), row(GM_W), row(GM_W), row(gate_w)],
        compiler_params=_params("arbitrary", "arbitrary"),
        name="project",
    )(x, mod3, mod3, w_in_bf, *rope_tabs, gm_ln_g.reshape(1, GM_W), gm_ln_b.reshape(1, GM_W))


def _rope_tables(s):
    pos = jnp.arange(s, dtype=I32)
    inv = 1.0 / (ROPE_BASE ** (jnp.arange(ROPE_FREQS, dtype=F32) / ROPE_FREQS))
    ang_r = (pos // GRID_W).astype(F32)[:, None] * inv[None, :]
    ang_c = (pos % GRID_W).astype(F32)[:, None] * inv[None, :]
    cr, sr, cc, sn = jnp.cos(ang_r), jnp.sin(ang_r), jnp.cos(ang_c), jnp.sin(ang_c)
    z = jnp.zeros_like(sr)
    reps = LANES // HEAD_DIM
    cos = jnp.tile(jnp.concatenate([cr, cr, cc, cc], -1), (1, reps))
    s_up = jnp.tile(jnp.concatenate([-sr, z, -sn, z], -1), (1, reps))
    s_dn = jnp.tile(jnp.concatenate([z, sr, z, sn], -1), (1, reps))
    return cos, s_up, s_dn


def _attend_body(sink_ref, q_ref, kvp_ref, kvc_ref, kvn_ref, ctx_ref, o_ref):
    i = pl.program_id(1)
    nb = pl.num_programs(1)
    rows = GROUP * BLOCK
    r = _iota((rows, 3 * BLOCK), 0) & (BLOCK - 1)
    m = _iota((rows, 3 * BLOCK), 1)
    mask = (m >= r) & (m <= r + 2 * BLOCK)
    mask &= (m >= BLOCK) | (i > 0)
    mask &= (m < 2 * BLOCK) | (i < nb - 1)
    grp = _iota((rows, 1), 0) // BLOCK
    kv_loc = jnp.concatenate([kvp_ref[...], kvc_ref[...], kvn_ref[...]], axis=0)
    kv_ctx = ctx_ref[...]
    q = q_ref[...]
    outs = []
    for h in range(KV_HEADS):
        ks = slice(h * HEAD_DIM, (h + 1) * HEAD_DIM)
        vs = slice(KV_W + h * HEAD_DIM, KV_W + (h + 1) * HEAD_DIM)
        qg = jnp.concatenate(
            [q[:, (h * GROUP + g) * HEAD_DIM:(h * GROUP + g + 1) * HEAD_DIM] for g in range(GROUP)], axis=0)
        sink = jnp.zeros((rows, 1), F32)
        for g in range(GROUP):
            sink = jnp.where(grp == g, sink_ref[h * GROUP + g], sink)
        s_ctx = _dot_nt(qg, kv_ctx[:, ks])
        s_loc = jnp.where(mask, _dot_nt(qg, kv_loc[:, ks]), NEG_INF)
        top = jnp.maximum(jnp.maximum(jnp.max(s_ctx, -1, keepdims=True), jnp.max(s_loc, -1, keepdims=True)), sink)
        e_ctx = jnp.exp(s_ctx - top)
        e_loc = jnp.exp(s_loc - top)
        den = jnp.sum(e_ctx, -1, keepdims=True) + jnp.sum(e_loc, -1, keepdims=True) + jnp.exp(sink - top)
        o = (_dot(e_ctx.astype(BF16), kv_ctx[:, vs]) + _dot(e_loc.astype(BF16), kv_loc[:, vs])) * (1.0 / den)
        outs += [o[g * BLOCK:(g + 1) * BLOCK] for g in range(GROUP)]
    o_ref[...] = jnp.concatenate(outs, axis=1).astype(BF16)


def _attend(q, kv, kv_ctx, sink):
    b, s, _ = q.shape
    nb = s // BLOCK
    c_len = kv_ctx.shape[1]
    kvw = 2 * KV_W
    return pl.pallas_call(
        _attend_body,
        out_shape=jax.ShapeDtypeStruct((b, s, ATTN_W), BF16),
        grid=(b, nb),
        in_specs=[
            pl.BlockSpec(memory_space=pltpu.SMEM),
            pl.BlockSpec((None, BLOCK, ATTN_W), lambda i, j: (i, j, 0)),
            pl.BlockSpec((None, BLOCK, kvw), lambda i, j: (i, jnp.maximum(j - 1, 0), 0)),
            pl.BlockSpec((None, BLOCK, kvw), lambda i, j: (i, j, 0)),
            pl.BlockSpec((None, BLOCK, kvw), lambda i, j: (i, jnp.minimum(j + 1, nb - 1), 0)),
            pl.BlockSpec((None, c_len, kvw), lambda i, j: (i, 0, 0)),
        ],
        out_specs=pl.BlockSpec((None, BLOCK, ATTN_W), lambda i, j: (i, j, 0)),
        compiler_params=_params("arbitrary", "arbitrary"),
        name="attend",
    )(sink, q, kv, kv, kv, kv_ctx)


def _route(logits):
    lane = _iota(logits.shape, 1)
    far = jnp.int32(2 * ROUTE_W)

    def first_max(vals):
        top = jnp.max(vals, -1, keepdims=True)
        return top, jnp.min(jnp.where(vals == top, lane, far), -1, keepdims=True)

    is_g = lane < N_GROUPS
    g_top, g_idx = first_max(jnp.where(is_g, logits, NEG_INF))
    g_w = 1.0 / jnp.sum(jnp.where(is_g, jnp.exp(logits - g_top), 0.0), -1, keepdims=True)
    lo = N_GROUPS + EPG * g_idx
    e_vals = jnp.where((lane >= lo) & (lane < lo + EPG), logits, NEG_INF)
    v1, i1 = first_max(e_vals)
    v2, i2 = first_max(jnp.where(lane == i1, NEG_INF, e_vals))
    t = jnp.exp(v2 - v1)
    w1 = g_w * (1.0 / (1.0 + t))
    w2 = g_w * (t / (1.0 + t))
    swap = i2 < i1
    a = jnp.where(swap, i2, i1) - lo
    b = jnp.where(swap, i1, i2) - lo
    cls = g_idx * PAIRS + ((a * (2 * EPG - 1 - a)) >> 1) + (b - a - 1)
    return jnp.where(swap, w2, w1), jnp.where(swap, w1, w2), cls


def _merge_body(x_ref, ya_ref, u_ref, vg_ref, gate_ref, ws_ref, bs_ref, wpa_ref, wpb_ref, wo_ref,
                lng_ref, lnb_ref, g1_ref, sh2_ref, sc2_ref, wrh_ref, wrl_ref, br_ref, x1_ref, rec_ref, cnt_ref):
    tile, d = x_ref.shape
    lane = _iota((BLOCK, LANES), 1)
    below = (lane < _iota((BLOCK, LANES), 0)).astype(BF16)
    chunks = []
    for c in range(tile // BLOCK):
        rows = slice(c * BLOCK, (c + 1) * BLOCK)
        pieces = []
        for j in range(GM_W // LANES):
            v = vg_ref[rows, j * LANES:(j + 1) * LANES]
            per = LANES // GM_HEAD
            sp = _dot(ws_ref[per * j], v)
            for k in range(1, per):
                sp = jnp.where(lane < k * GM_HEAD, sp, _dot(ws_ref[per * j + k], v))
            pieces.append(sp)
        chunks.append(jnp.concatenate(pieces, axis=1) + bs_ref[...])
    sp = jnp.concatenate(chunks, axis=0)
    y_gm = (u_ref[...].astype(F32) * sp).astype(BF16)
    gate = gate_ref[...].astype(F32)
    y = gate[:, :d] * _dot(ya_ref[...], wpa_ref[...]) + gate[:, d:] * _dot(y_gm, wpb_ref[...])
    mix = _dot(y.astype(BF16), wo_ref[...])
    x1 = _ln(ALPHA * x_ref[...] + g1_ref[...] * mix) * lng_ref[...] + lnb_ref[...]
    x1_ref[...] = x1
    h2 = _ln(x1) * (1.0 + sc2_ref[...]) + sh2_ref[...]
    w_lo, w_hi, cls = _route(_dot_split(h2, wrh_ref[...], wrl_ref[...]) + br_ref[...])
    for c in range(tile // BLOCK):
        rows = slice(c * BLOCK, (c + 1) * BLOCK)
        onehot = lane == cls[rows]
        hot = onehot.astype(BF16)
        rank = jnp.sum(jnp.where(onehot, _dot(below, hot), 0.0), -1, keepdims=True)
        cnt_ref[c] = jnp.sum(hot.astype(F32), 0, keepdims=True)
        rec = jnp.where(lane == REC_WLO, w_lo[rows], 0.0)
        rec = jnp.where(lane == REC_WHI, w_hi[rows], rec)
        rec = jnp.where(lane == REC_CLS, cls[rows].astype(F32), rec)
        rec_ref[rows, :] = jnp.where(lane == REC_RANK, rank, rec)


def _merge(x, y_attn, u, vg, gates, ws_bf, bs_rows, wpa, wpb, wo, ln_g, ln_b, mod3, wr_hi, wr_lo, br, tile):
    b, s, d = x.shape
    nt = s // tile
    per = tile // BLOCK
    row = lambda width: pl.BlockSpec((None, tile, width), lambda i, j: (i, j, 0))
    full = lambda a: pl.BlockSpec(a.shape, lambda i, j: (0,) * a.ndim)
    modv = lambda k: pl.BlockSpec((None, 1, d), lambda i, j: (i, 0, k))
    ln_g = ln_g.reshape(1, d)
    ln_b = ln_b.reshape(1, d)
    return pl.pallas_call(
        _merge_body,
        out_shape=[jax.ShapeDtypeStruct((b, s, d), F32), jax.ShapeDtypeStruct((b, s, ROUTE_W), F32),
                   jax.ShapeDtypeStruct((b * s // BLOCK, 1, LANES), F32)],
        grid=(b, nt),
        in_specs=[
            row(d), row(ATTN_W), row(GM_W), row(GM_W), row(2 * d),
            full(ws_bf), full(bs_rows), full(wpa), full(wpb), full(wo), full(ln_g), full(ln_b),
            modv(2), modv(3), modv(4),
            full(wr_hi), full(wr_lo), full(br),
        ],
        out_specs=[row(d), row(ROUTE_W), pl.BlockSpec((per, 1, LANES), lambda i, j: (i * nt + j, 0, 0))],
        compiler_params=_params("arbitrary", "arbitrary"),
        name="merge",
    )(x, y_attn, u, vg, gates, ws_bf, bs_rows, wpa, wpb, wo, ln_g, ln_b, mod3, mod3, mod3, wr_hi, wr_lo, br)


def _plan_body(cnt_ref, rec_ref, pos_ref, tab_ref, cblk_ref, base_ref, *, chunks):
    step = pl.program_id(0)
    n_chunks = cnt_ref.shape[0]
    max_blocks = tab_ref.shape[0]

    @pl.when(step == 0)
    def _():
        cnt = cnt_ref[...]
        total = jnp.sum(cnt, 0, keepdims=True)
        n_blk = jnp.floor((total + (BLOCK - 1)) * (1.0 / BLOCK))
        before = (_iota((LANES, LANES), 0) < _iota((LANES, LANES), 1)).astype(BF16)
        nb_hi, nb_lo = _split(jnp.broadcast_to(n_blk, (SUBLANES, LANES)))
        blk_start = (_dot(nb_hi, before) + _dot(nb_lo, before))[0:1]
        earlier = (_iota((n_chunks, n_chunks), 1) < _iota((n_chunks, n_chunks), 0)).astype(BF16)
        base_ref[...] = blk_start * float(BLOCK) + _dot(earlier, cnt.astype(BF16))
        blk_end = blk_start + n_blk
        sub = _iota((SUBLANES, LANES), 0)
        cblk_ref[...] = jnp.where(sub == 0, blk_start, jnp.where(sub == 1, blk_end, 0.0)).astype(I32)
        lane = _iota((max_blocks, LANES), 1)
        blk = _iota((max_blocks, LANES), 0).astype(F32)
        is_cls = lane < N_CLASSES
        n_real = jnp.max(jnp.where(is_cls, blk_end, 0.0), -1, keepdims=True)
        u = jnp.minimum(blk, n_real - 1.0)
        cls_of = jnp.sum(jnp.where(is_cls & (blk_end <= u), 1.0, 0.0), -1, keepdims=True)
        tab = jnp.where(lane == 0, cls_of, jnp.where(lane == 1, n_real, 0.0))
        tab_ref[...] = tab.astype(I32)

    lane = _iota((BLOCK, LANES), 1)
    for c in range(chunks):
        rec = rec_ref[c * BLOCK:(c + 1) * BLOCK, :]
        cls = rec[:, REC_CLS:REC_CLS + 1].astype(I32)
        base = base_ref[pl.ds(step * chunks + c, 1), :]
        pos = jnp.sum(jnp.where(lane == cls, base, 0.0), -1, keepdims=True) + rec[:, REC_RANK:REC_RANK + 1]
        pos_ref[c] = jnp.transpose(jnp.broadcast_to(pos, (BLOCK, LANES)))[0:1].astype(I32)


def _plan(counts, rec, max_blocks):
    n = rec.shape[0]
    n_chunks = n // BLOCK
    chunks = SUBLANES
    while n_chunks % chunks:
        chunks //= 2
    return pl.pallas_call(
        functools.partial(_plan_body, chunks=chunks),
        out_shape=[jax.ShapeDtypeStruct((n_chunks, 1, LANES), I32), jax.ShapeDtypeStruct((max_blocks, LANES), I32),
                   jax.ShapeDtypeStruct((SUBLANES, LANES), I32)],
        grid=(n_chunks // chunks,),
        in_specs=[
            pl.BlockSpec((n_chunks, LANES), lambda i: (0, 0)),
            pl.BlockSpec((chunks * BLOCK, ROUTE_W), lambda i: (i, 0)),
        ],
        out_specs=[
            pl.BlockSpec((chunks, 1, LANES), lambda i: (i, 0, 0)),
            pl.BlockSpec((max_blocks, LANES), lambda i: (0, 0)),
            pl.BlockSpec((SUBLANES, LANES), lambda i: (0, 0)),
        ],
        scratch_shapes=[pltpu.VMEM((n_chunks, LANES), F32)],
        compiler_params=_params("arbitrary"),
        name="plan",
    )(counts, rec)


def _tile_rows(base, j, pitch):
    return pl.ds(base + j, BLOCK, stride=pitch)


def _load_tiles(ref, base, pitch, width):
    return jnp.concatenate([ref[_tile_rows(base, j, pitch), :] for j in range(width // LANES)], axis=1)


def _store_tiles(ref, base, pitch, val):
    for j in range(val.shape[1] // LANES):
        ref[_tile_rows(base, j, pitch), :] = val[:, j * LANES:(j + 1) * LANES]


def _scatter_body(cls_blk_ref, pos_ref, x1_ref, sh_ref, sc_ref, xs_hbm, hbuf, zbuf, ssem, zsem):
    u = pl.program_id(0)
    slot = u % 2
    d = x1_ref.shape[-1]
    tile_rows = BLOCK * SUBLANES

    def row_copies(s):
        return [pltpu.make_async_copy(hbuf.at[pl.ds(s * tile_rows + r * SUBLANES, SUBLANES)],
                                      xs_hbm.at[pl.ds(pl.multiple_of(pos_ref[0, r] * SUBLANES, SUBLANES), SUBLANES)],
                                      ssem.at[s]) for r in range(BLOCK)]

    @pl.when(u == 0)
    def _():
        zbuf[...] = jnp.zeros_like(zbuf)

        def zero_copy(c):
            start = pl.multiple_of((cls_blk_ref[1, c] - 1) * tile_rows, tile_rows)
            return pltpu.make_async_copy(zbuf, xs_hbm.at[pl.ds(start, tile_rows)], zsem)

        for c in range(N_CLASSES):
            @pl.when(cls_blk_ref[1, c] > cls_blk_ref[0, c])
            def _():
                zero_copy(c).start()
        for c in range(N_CLASSES):
            @pl.when(cls_blk_ref[1, c] > cls_blk_ref[0, c])
            def _():
                zero_copy(c).wait()

        def tail_copy(blk):
            return pltpu.make_async_copy(zbuf, xs_hbm.at[pl.ds(pl.multiple_of(blk * tile_rows, tile_rows), tile_rows)],
                                         zsem)

        n_real = cls_blk_ref[1, N_CLASSES - 1]
        n_total = xs_hbm.shape[0] // tile_rows
        lax.fori_loop(n_real, n_total, lambda blk, carry: (tail_copy(blk).start(), carry)[1], 0)
        lax.fori_loop(n_real, n_total, lambda blk, carry: (tail_copy(blk).wait(), carry)[1], 0)

    @pl.when(u >= 2)
    def _():
        for cp in row_copies(slot):
            cp.wait()

    _store_tiles(hbuf, slot * tile_rows, SUBLANES, _ln(x1_ref[...]) * (1.0 + sc_ref[...]) + sh_ref[...])
    for cp in row_copies(slot):
        cp.start()

    @pl.when(u == pl.num_programs(0) - 1)
    def _():
        for cp in row_copies(slot):
            cp.wait()

        @pl.when(u >= 1)
        def _():
            for cp in row_copies(1 - slot):
                cp.wait()


def _scatter(x1, mod3, pos, cls_blk, max_blocks, seq):
    n, d = x1.shape
    n_chunks = n // BLOCK
    per_batch = seq // BLOCK
    tile_rows = BLOCK * SUBLANES
    return pl.pallas_call(
        _scatter_body,
        out_shape=jax.ShapeDtypeStruct((max_blocks * tile_rows, LANES), F32),
        grid=(n_chunks,),
        in_specs=[
            pl.BlockSpec(memory_space=pltpu.SMEM),
            pl.BlockSpec((None, 1, LANES), lambda u: (u, 0, 0), memory_space=pltpu.SMEM),
            pl.BlockSpec((BLOCK, d), lambda u: (u, 0)),
            pl.BlockSpec((None, 1, d), lambda u: (u // per_batch, 0, 3)),
            pl.BlockSpec((None, 1, d), lambda u: (u // per_batch, 0, 4)),
        ],
        out_specs=pl.BlockSpec(memory_space=pl.ANY),
        scratch_shapes=[
            pltpu.VMEM((2 * tile_rows, LANES), F32),
            pltpu.VMEM((tile_rows, LANES), F32),
            pltpu.SemaphoreType.DMA((2,)),
            pltpu.SemaphoreType.DMA,
        ],
        compiler_params=_params("arbitrary"),
        name="scatter",
    )(cls_blk, pos, x1, mod3, mod3)


def _expert(h, w1, w3, w2):
    hid = jax.nn.silu(_dot(h, w1)) * _dot(h, w3)
    return _dot(hid, w2)


def _moe_body(bgrp_ref, blo_ref, bhi_ref, nreal_ref, xs_ref, w1_ref, w3_ref, w2_ref, o_ref):
    u = pl.program_id(0)
    d = w1_ref.shape[1]

    @pl.when(u < nreal_ref[0])
    def _():
        h = _load_tiles(xs_ref, 0, SUBLANES, d)
        for k, e in enumerate((blo_ref[u], bhi_ref[u])):
            _store_tiles(o_ref, k * SUBLANES, 2 * SUBLANES, _expert(h, w1_ref[e], w3_ref[e], w2_ref[e]))

    @pl.when(u >= nreal_ref[0])
    def _():
        o_ref[...] = jnp.zeros_like(o_ref)


def _moe(xs, tables, w1, w3, w2):
    bgrp, blo, bhi, n_real = tables
    max_blocks = bgrp.shape[0]
    d, de = w1.shape[1:]
    tile_rows = BLOCK * SUBLANES
    w1g = w1.reshape(N_GROUPS, EPG, d, de)
    w3g = w3.reshape(N_GROUPS, EPG, d, de)
    w2g = w2.reshape(N_GROUPS, EPG, de, d)

    def w_spec(a):
        return pl.BlockSpec((None,) + a.shape[1:], lambda u, g, lo, hi, n_real: (g[u], 0, 0, 0),
                            pipeline_mode=pl.Buffered(1))

    grid_spec = pltpu.PrefetchScalarGridSpec(
        num_scalar_prefetch=4,
        grid=(max_blocks,),
        in_specs=[
            pl.BlockSpec((tile_rows, LANES), lambda u, g, lo, hi, n_real: (jnp.minimum(u, n_real[0] - 1), 0)),
            w_spec(w1g), w_spec(w3g), w_spec(w2g),
        ],
        out_specs=pl.BlockSpec((2 * tile_rows, LANES), lambda u, *_: (u, 0)),
    )
    return pl.pallas_call(
        _moe_body,
        out_shape=jax.ShapeDtypeStruct((max_blocks * 2 * tile_rows, LANES), F32),
        grid_spec=grid_spec,
        compiler_params=pltpu.CompilerParams(dimension_semantics=("arbitrary",), vmem_limit_bytes=MOE_VMEM_LIMIT_BYTES),
        name="moe",
    )(bgrp, blo, bhi, n_real, xs, w1g, w3g, w2g)


def _final_body(pos_ref, posn_ref, x1_ref, rec_ref, g2_ref, lng_ref, lnb_ref, ms_hbm, o_ref, gbuf, gsem):
    u = pl.program_id(0)
    slot = u % 2
    d = x1_ref.shape[-1]
    pitch = 2 * SUBLANES
    tile_rows = BLOCK * pitch

    def row_copies(idx_ref, s):
        return [pltpu.make_async_copy(ms_hbm.at[pl.ds(pl.multiple_of(idx_ref[0, r] * pitch, pitch), pitch)],
                                      gbuf.at[pl.ds(s * tile_rows + r * pitch, pitch)], gsem.at[s])
                for r in range(BLOCK)]

    @pl.when(u == 0)
    def _():
        for cp in row_copies(pos_ref, 0):
            cp.start()

    for cp in row_copies(pos_ref, slot):
        cp.wait()
    for cp in row_copies(posn_ref, 1 - slot):
        cp.start()
    rec = rec_ref[...]
    y_lo = _load_tiles(gbuf, slot * tile_rows, pitch, d)
    y_hi = _load_tiles(gbuf, slot * tile_rows + SUBLANES, pitch, d)
    moe = rec[:, REC_WLO:REC_WLO + 1] * y_lo + rec[:, REC_WHI:REC_WHI + 1] * y_hi
    o_ref[...] = _ln(ALPHA * x1_ref[...] + g2_ref[...] * moe) * lng_ref[...] + lnb_ref[...]

    @pl.when(u == pl.num_programs(0) - 1)
    def _():
        for cp in row_copies(posn_ref, 1 - slot):
            cp.wait()


def _final(x1, rec, ms, pos, mod3, ln_g, ln_b, seq):
    n, d = x1.shape
    n_chunks = n // BLOCK
    per_batch = seq // BLOCK
    vec = pl.BlockSpec((1, d), lambda u: (0, 0))

    def pos_spec(off):
        return pl.BlockSpec((None, 1, LANES), lambda u: (jnp.minimum(u + off, n_chunks - 1), 0, 0),
                            memory_space=pltpu.SMEM)

    return pl.pallas_call(
        _final_body,
        out_shape=jax.ShapeDtypeStruct((n, d), F32),
        grid=(n_chunks,),
        in_specs=[
            pos_spec(0), pos_spec(1),
            pl.BlockSpec((BLOCK, d), lambda u: (u, 0)),
            pl.BlockSpec((BLOCK, ROUTE_W), lambda u: (u, 0)),
            pl.BlockSpec((None, 1, d), lambda u: (u // per_batch, 0, 5)),
            vec, vec,
            pl.BlockSpec(memory_space=pl.ANY),
        ],
        out_specs=pl.BlockSpec((BLOCK, d), lambda u: (u, 0)),
        scratch_shapes=[pltpu.VMEM((2 * BLOCK * 2 * SUBLANES, LANES), F32), pltpu.SemaphoreType.DMA((2,))],
        compiler_params=_params("arbitrary"),
        name="final",
    )(pos, pos, x1, rec, mod3, ln_g.reshape(1, d), ln_b.reshape(1, d), ms)


def _block_tables(tab):
    cls = tab[:, 0]
    pair = cls % PAIRS
    first = jnp.array([i for i in range(EPG) for _ in range(i + 1, EPG)], I32)
    second = jnp.array([j for i in range(EPG) for j in range(i + 1, EPG)], I32)
    return cls // PAIRS, first[pair], second[pair], tab[0, 1].reshape(1)


def kernel(x, c, ctx, c_ctx, w_ada, b_ada, w_in, attn_sink, gm_ln_g, gm_ln_b, gm_ws, gm_bs, w_pa, w_pb, w_o,
           ln1_g, ln1_b, router_g_w, router_g_b, router_e_w, router_e_b, moe_w1, moe_w3, moe_w2, ln2_g, ln2_b):
    b, s, d = x.shape
    assert w_ada.shape[0] == DEPTH and s % BLOCK == 0 and s % GRID_W == 0
    mod_rows = 2 * SUBLANES
    assert b + 1 <= mod_rows
    c_rows = jnp.zeros((mod_rows, d), F32).at[:b].set(c).at[b].set(c_ctx)
    mod = _adaln_mod(c_rows, w_ada[0], b_ada[0])
    mod3 = mod.reshape(mod_rows, 1, 6 * d)
    w_in_bf = w_in[0].astype(BF16)
    kv_ctx = _ctx_kv(ctx, mod3, b, w_in_bf)
    tile = 512 if s % 512 == 0 else BLOCK
    q, kv, u, vg, gates = _project(x, mod3, w_in_bf, _rope_tables(s), gm_ln_g[0], gm_ln_b[0], tile)
    y_attn = _attend(q, kv, kv_ctx, attn_sink[0])
    bs_rows = jnp.repeat(gm_bs[0].T, GM_HEAD, axis=1)
    w_route = jnp.concatenate([router_g_w[0], jnp.transpose(router_e_w[0], (1, 0, 2)).reshape(d, N_EXPERTS)], axis=1)
    w_route = jnp.pad(w_route, ((0, 0), (0, ROUTE_W - w_route.shape[1])))
    b_route = jnp.pad(jnp.concatenate([router_g_b[0], router_e_b[0].reshape(-1)]), (0, ROUTE_W - N_GROUPS - N_EXPERTS))
    wr_hi = w_route.astype(BF16)
    wr_lo = (w_route - wr_hi.astype(F32)).astype(BF16)
    merge_tile = 2 * BLOCK if s % (2 * BLOCK) == 0 else BLOCK
    x1, rec, counts = _merge(x, y_attn, u, vg, gates, gm_ws[0].astype(BF16), bs_rows, w_pa[0].astype(BF16),
                             w_pb[0].astype(BF16), w_o[0].astype(BF16), ln1_g[0], ln1_b[0], mod3, wr_hi, wr_lo,
                             b_route.reshape(1, ROUTE_W), merge_tile)
    n = b * s
    x1 = x1.reshape(n, d)
    rec = rec.reshape(n, ROUTE_W)
    max_blocks = n // BLOCK + N_CLASSES - 1
    pos, tab, cls_blk = _plan(counts.reshape(n // BLOCK, LANES), rec, max_blocks)
    tables = _block_tables(tab)
    xs = _scatter(x1, mod3, pos, cls_blk, max_blocks, s)
    ms = _moe(xs, tables, moe_w1[0], moe_w3[0], moe_w2[0])
    out = _final(x1, rec, ms, pos, mod3, ln2_g[0], ln2_b[0], s)
    return out.reshape(b, s, d)
```

```python
import functools

import jax
import jax.numpy as jnp
from jax import lax
from jax.experimental import pallas as pl
from jax.experimental.pallas import tpu as pltpu

F32 = jnp.float32
BF16 = jnp.bfloat16
I32 = jnp.int32

GRID_W = 64
HEAD_DIM = 64
Q_HEADS = 8
KV_HEADS = 2
GROUP = Q_HEADS // KV_HEADS
BLOCK = 128
ROPE_BASE = 10000.0
ROPE_FREQS = HEAD_DIM // 4
GM_GROUPS = 8
GM_HEAD = 64
ATTN_W = Q_HEADS * HEAD_DIM
KV_W = KV_HEADS * HEAD_DIM
GM_W = GM_GROUPS * GM_HEAD
N_GROUPS = 4
EPG = 8
N_EXPERTS = N_GROUPS * EPG
PAIRS = EPG * (EPG - 1) // 2
N_CLASSES = N_GROUPS * PAIRS
LN_EPS = 1e-6
NEG_INF = -1e30
DEPTH = 1
ALPHA = (2.0 * DEPTH) ** 0.25

LANES = 128
SUBLANES = 8
VMEM_LIMIT_BYTES = 56 * 1024 * 1024
MOE_VMEM_LIMIT_BYTES = 60 * 1024 * 1024
ROUTE_W = LANES
REC_WLO, REC_WHI, REC_CLS, REC_RANK = 0, 1, 2, 3


def _params(*sem):
    return pltpu.CompilerParams(dimension_semantics=sem, vmem_limit_bytes=VMEM_LIMIT_BYTES)


def _ln(x):
    mu = jnp.mean(x, -1, keepdims=True)
    xc = x - mu
    var = jnp.mean(xc * xc, -1, keepdims=True)
    return xc * lax.rsqrt(var + LN_EPS)


def _dot(a, b):
    return jnp.dot(a, b, preferred_element_type=F32)


def _dot_nt(a, b):
    return lax.dot_general(a, b, (((1,), (1,)), ((), ())), preferred_element_type=F32)


def _split(a):
    hi = a.astype(BF16)
    lo = (a - hi.astype(F32)).astype(BF16)
    return hi, lo


def _dot_split(a, b_hi, b_lo):
    a_hi, a_lo = _split(a)
    return _dot(a_hi, b_hi) + _dot(a_hi, b_lo) + _dot(a_lo, b_hi)


def _iota(shape, axis):
    return lax.broadcasted_iota(I32, shape, axis)


def _mod_body(c_ref, w_ref, b_ref, o_ref):
    w_hi, w_lo = _split(w_ref[...])
    o_ref[...] = _dot_split(jax.nn.silu(c_ref[...]), w_hi, w_lo) + b_ref[...]


def _adaln_mod(c_rows, w_ada, b_ada):
    rows, d = c_rows.shape
    width = w_ada.shape[1]
    tn = 6 * LANES
    return pl.pallas_call(
        _mod_body,
        out_shape=jax.ShapeDtypeStruct((rows, width), F32),
        grid=(width // tn,),
        in_specs=[
            pl.BlockSpec((rows, d), lambda j: (0, 0)),
            pl.BlockSpec((d, tn), lambda j: (0, j)),
            pl.BlockSpec((1, tn), lambda j: (0, j)),
        ],
        out_specs=pl.BlockSpec((rows, tn), lambda j: (0, j)),
        compiler_params=_params("arbitrary"),
        name="mod",
    )(c_rows, w_ada, b_ada.reshape(1, width))


def _ctx_body(ctx_ref, sh_ref, sc_ref, w_ref, o_ref):
    h = _ln(ctx_ref[...]) * (1.0 + sc_ref[...]) + sh_ref[...]
    o_ref[...] = _dot(h.astype(BF16), w_ref[...]).astype(BF16)


def _ctx_kv(ctx, mod3, ctx_row, w_in_bf):
    b, c_len, d = ctx.shape
    kvw = 2 * KV_W
    return pl.pallas_call(
        _ctx_body,
        out_shape=jax.ShapeDtypeStruct((b, c_len, kvw), BF16),
        grid=(b,),
        in_specs=[
            pl.BlockSpec((None, c_len, d), lambda i: (i, 0, 0)),
            pl.BlockSpec((None, 1, d), lambda i: (ctx_row, 0, 0)),
            pl.BlockSpec((None, 1, d), lambda i: (ctx_row, 0, 1)),
            pl.BlockSpec((d, kvw), lambda i: (0, ATTN_W // kvw)),
        ],
        out_specs=pl.BlockSpec((None, c_len, kvw), lambda i: (i, 0, 0)),
        compiler_params=_params("arbitrary"),
        name="ctx_kv",
    )(ctx, mod3, mod3, w_in_bf)


def _rope(x, cos, s_up, s_dn):
    return x * cos + pltpu.roll(x, LANES - ROPE_FREQS, 1) * s_up + pltpu.roll(x, ROPE_FREQS, 1) * s_dn


def _project_body(x_ref, sh_ref, sc_ref, w_ref, cos_ref, sup_ref, sdn_ref, lng_ref, lnb_ref,
                  q_ref, kv_ref, u_ref, vg_ref, gate_ref):
    h = (_ln(x_ref[...]) * (1.0 + sc_ref[...]) + sh_ref[...]).astype(BF16)
    cos, s_up, s_dn = cos_ref[...], sup_ref[...], sdn_ref[...]
    q_scale = HEAD_DIM ** -0.5
    k0 = ATTN_W
    u0 = k0 + 2 * KV_W
    vg0 = u0 + GM_W
    gate0 = vg0 + GM_W
    q = _dot(h, w_ref[:, 0:k0])
    for j in range(ATTN_W // LANES):
        sl = slice(j * LANES, (j + 1) * LANES)
        q_ref[:, sl] = (_rope(q[:, sl], cos, s_up, s_dn) * q_scale).astype(BF16)
    kv = _dot(h, w_ref[:, k0:u0])
    kv_ref[:, 0:KV_W] = _rope(kv[:, 0:KV_W], cos, s_up, s_dn).astype(BF16)
    kv_ref[:, KV_W:] = kv[:, KV_W:].astype(BF16)
    u_ref[...] = jax.nn.gelu(_dot(h, w_ref[:, u0:vg0])).astype(BF16)
    vg = _ln(jax.nn.gelu(_dot(h, w_ref[:, vg0:gate0]))) * lng_ref[...] + lnb_ref[...]
    vg_ref[...] = vg.astype(BF16)
    gate_ref[...] = jax.nn.sigmoid(_dot(h, w_ref[:, gate0:])).astype(BF16)


def _project(x, mod3, w_in_bf, rope_tabs, gm_ln_g, gm_ln_b, tile):
    b, s, d = x.shape
    in_w = w_in_bf.shape[1]
    gate_w = in_w - (ATTN_W + 2 * KV_W + 2 * GM_W)
    row = lambda width: pl.BlockSpec((None, tile, width), lambda i, t: (i, t, 0))
    tab = pl.BlockSpec((tile, LANES), lambda i, t: (t, 0))
    vec = lambda width: pl.BlockSpec((1, width), lambda i, t: (0, 0))
    return pl.pallas_call(
        _project_body,
        out_shape=[
            jax.ShapeDtypeStruct((b, s, ATTN_W), BF16),
            jax.ShapeDtypeStruct((b, s, 2 * KV_W), BF16),
            jax.ShapeDtypeStruct((b, s, GM_W), BF16),
            jax.ShapeDtypeStruct((b, s, GM_W), BF16),
            jax.ShapeDtypeStruct((b, s, gate_w), BF16),
        ],
        grid=(b, s // tile),
        in_specs=[
            row(d),
            pl.BlockSpec((None, 1, d), lambda i, t: (i, 0, 0)),
            pl.BlockSpec((None, 1, d), lambda i, t: (i, 0, 1)),
            pl.BlockSpec((d, in_w), lambda i, t: (0, 0)),
            tab, tab, tab,
            vec(GM_W), vec(GM_W),
        ],
        out_specs=[row(ATTN_W), row(2 * KV_W), row(GM_W), row(GM_W), row(gate_w)],
        compiler_params=_params("arbitrary", "arbitrary"),
        name="project",
    )(x, mod3, mod3, w_in_bf, *rope_tabs, gm_ln_g.reshape(1, GM_W), gm_ln_b.reshape(1, GM_W))


def _rope_tables(s):
    pos = jnp.arange(s, dtype=I32)
    inv = 1.0 / (ROPE_BASE ** (jnp.arange(ROPE_FREQS, dtype=F32) / ROPE_FREQS))
    ang_r = (pos // GRID_W).astype(F32)[:, None] * inv[None, :]
    ang_c = (pos % GRID_W).astype(F32)[:, None] * inv[None, :]
    cr, sr, cc, sn = jnp.cos(ang_r), jnp.sin(ang_r), jnp.cos(ang_c), jnp.sin(ang_c)
    z = jnp.zeros_like(sr)
    reps = LANES // HEAD_DIM
    cos = jnp.tile(jnp.concatenate([cr, cr, cc, cc], -1), (1, reps))
    s_up = jnp.tile(jnp.concatenate([-sr, z, -sn, z], -1), (1, reps))
    s_dn = jnp.tile(jnp.concatenate([z, sr, z, sn], -1), (1, reps))
    return cos, s_up, s_dn


def _attend_body(sink_ref, q_ref, kvp_ref, kvc_ref, kvn_ref, ctx_ref, o_ref):
    i = pl.program_id(1)
    nb = pl.num_programs(1)
    rows = GROUP * BLOCK
    r = _iota((rows, 3 * BLOCK), 0) & (BLOCK - 1)
    m = _iota((rows, 3 * BLOCK), 1)
    mask = (m >= r) & (m <= r + 2 * BLOCK)
    mask &= (m >= BLOCK) | (i > 0)
    mask &= (m < 2 * BLOCK) | (i < nb - 1)
    grp = _iota((rows, 1), 0) // BLOCK
    kv_loc = jnp.concatenate([kvp_ref[...], kvc_ref[...], kvn_ref[...]], axis=0)
    kv_ctx = ctx_ref[...]
    q = q_ref[...]
    outs = []
    for h in range(KV_HEADS):
        ks = slice(h * HEAD_DIM, (h + 1) * HEAD_DIM)
        vs = slice(KV_W + h * HEAD_DIM, KV_W + (h + 1) * HEAD_DIM)
        qg = jnp.concatenate(
            [q[:, (h * GROUP + g) * HEAD_DIM:(h * GROUP + g + 1) * HEAD_DIM] for g in range(GROUP)], axis=0)
        sink = jnp.zeros((rows, 1), F32)
        for g in range(GROUP):
            sink = jnp.where(grp == g, sink_ref[h * GROUP + g], sink)
        s_ctx = _dot_nt(qg, kv_ctx[:, ks])
        s_loc = jnp.where(mask, _dot_nt(qg, kv_loc[:, ks]), NEG_INF)
        top = jnp.maximum(jnp.maximum(jnp.max(s_ctx, -1, keepdims=True), jnp.max(s_loc, -1, keepdims=True)), sink)
        e_ctx = jnp.exp(s_ctx - top)
        e_loc = jnp.exp(s_loc - top)
        den = jnp.sum(e_ctx, -1, keepdims=True) + jnp.sum(e_loc, -1, keepdims=True) + jnp.exp(sink - top)
        o = (_dot(e_ctx.astype(BF16), kv_ctx[:, vs]) + _dot(e_loc.astype(BF16), kv_loc[:, vs])) * (1.0 / den)
        outs += [o[g * BLOCK:(g + 1) * BLOCK] for g in range(GROUP)]
    o_ref[...] = jnp.concatenate(outs, axis=1).astype(BF16)


def _attend(q, kv, kv_ctx, sink):
    b, s, _ = q.shape
    nb = s // BLOCK
    c_len = kv_ctx.shape[1]
    kvw = 2 * KV_W
    return pl.pallas_call(
        _attend_body,
        out_shape=jax.ShapeDtypeStruct((b, s, ATTN_W), BF16),
        grid=(b, nb),
        in_specs=[
            pl.BlockSpec(memory_space=pltpu.SMEM),
            pl.BlockSpec((None, BLOCK, ATTN_W), lambda i, j: (i, j, 0)),
            pl.BlockSpec((None, BLOCK, kvw), lambda i, j: (i, jnp.maximum(j - 1, 0), 0)),
            pl.BlockSpec((None, BLOCK, kvw), lambda i, j: (i, j, 0)),
            pl.BlockSpec((None, BLOCK, kvw), lambda i, j: (i, jnp.minimum(j + 1, nb - 1), 0)),
            pl.BlockSpec((None, c_len, kvw), lambda i, j: (i, 0, 0)),
        ],
        out_specs=pl.BlockSpec((None, BLOCK, ATTN_W), lambda i, j: (i, j, 0)),
        compiler_params=_params("arbitrary", "arbitrary"),
        name="attend",
    )(sink, q, kv, kv, kv, kv_ctx)


def _route(logits):
    lane = _iota(logits.shape, 1)
    far = jnp.int32(2 * ROUTE_W)

    def first_max(vals):
        top = jnp.max(vals, -1, keepdims=True)
        return top, jnp.min(jnp.where(vals == top, lane, far), -1, keepdims=True)

    is_g = lane < N_GROUPS
    g_top, g_idx = first_max(jnp.where(is_g, logits, NEG_INF))
    g_w = 1.0 / jnp.sum(jnp.where(is_g, jnp.exp(logits - g_top), 0.0), -1, keepdims=True)
    lo = N_GROUPS + EPG * g_idx
    e_vals = jnp.where((lane >= lo) & (lane < lo + EPG), logits, NEG_INF)
    v1, i1 = first_max(e_vals)
    v2, i2 = first_max(jnp.where(lane == i1, NEG_INF, e_vals))
    t = jnp.exp(v2 - v1)
    w1 = g_w * (1.0 / (1.0 + t))
    w2 = g_w * (t / (1.0 + t))
    swap = i2 < i1
    a = jnp.where(swap, i2, i1) - lo
    b = jnp.where(swap, i1, i2) - lo
    cls = g_idx * PAIRS + ((a * (2 * EPG - 1 - a)) >> 1) + (b - a - 1)
    return jnp.where(swap, w2, w1), jnp.where(swap, w1, w2), cls


def _merge_body(x_ref, ya_ref, u_ref, vg_ref, gate_ref, ws_ref, bs_ref, wpa_ref, wpb_ref, wo_ref,
                lng_ref, lnb_ref, g1_ref, sh2_ref, sc2_ref, wrh_ref, wrl_ref, br_ref, x1_ref, rec_ref, cnt_ref):
    tile, d = x_ref.shape
    lane = _iota((BLOCK, LANES), 1)
    below = (lane < _iota((BLOCK, LANES), 0)).astype(BF16)
    chunks = []
    for c in range(tile // BLOCK):
        rows = slice(c * BLOCK, (c + 1) * BLOCK)
        pieces = []
        for j in range(GM_W // LANES):
            v = vg_ref[rows, j * LANES:(j + 1) * LANES]
            per = LANES // GM_HEAD
            sp = _dot(ws_ref[per * j], v)
            for k in range(1, per):
                sp = jnp.where(lane < k * GM_HEAD, sp, _dot(ws_ref[per * j + k], v))
            pieces.append(sp)
        chunks.append(jnp.concatenate(pieces, axis=1) + bs_ref[...])
    sp = jnp.concatenate(chunks, axis=0)
    y_gm = (u_ref[...].astype(F32) * sp).astype(BF16)
    gate = gate_ref[...].astype(F32)
    y = gate[:, :d] * _dot(ya_ref[...], wpa_ref[...]) + gate[:, d:] * _dot(y_gm, wpb_ref[...])
    mix = _dot(y.astype(BF16), wo_ref[...])
    x1 = _ln(ALPHA * x_ref[...] + g1_ref[...] * mix) * lng_ref[...] + lnb_ref[...]
    x1_ref[...] = x1
    h2 = _ln(x1) * (1.0 + sc2_ref[...]) + sh2_ref[...]
    w_lo, w_hi, cls = _route(_dot_split(h2, wrh_ref[...], wrl_ref[...]) + br_ref[...])
    for c in range(tile // BLOCK):
        rows = slice(c * BLOCK, (c + 1) * BLOCK)
        onehot = lane == cls[rows]
        hot = onehot.astype(BF16)
        rank = jnp.sum(jnp.where(onehot, _dot(below, hot), 0.0), -1, keepdims=True)
        cnt_ref[c] = jnp.sum(hot.astype(F32), 0, keepdims=True)
        rec = jnp.where(lane == REC_WLO, w_lo[rows], 0.0)
        rec = jnp.where(lane == REC_WHI, w_hi[rows], rec)
        rec = jnp.where(lane == REC_CLS, cls[rows].astype(F32), rec)
        rec_ref[rows, :] = jnp.where(lane == REC_RANK, rank, rec)


def _merge(x, y_attn, u, vg, gates, ws_bf, bs_rows, wpa, wpb, wo, ln_g, ln_b, mod3, wr_hi, wr_lo, br, tile):
    b, s, d = x.shape
    nt = s // tile
    per = tile // BLOCK
    row = lambda width: pl.BlockSpec((None, tile, width), lambda i, j: (i, j, 0))
    full = lambda a: pl.BlockSpec(a.shape, lambda i, j: (0,) * a.ndim)
    modv = lambda k: pl.BlockSpec((None, 1, d), lambda i, j: (i, 0, k))
    ln_g = ln_g.reshape(1, d)
    ln_b = ln_b.reshape(1, d)
    return pl.pallas_call(
        _merge_body,
        out_shape=[jax.ShapeDtypeStruct((b, s, d), F32), jax.ShapeDtypeStruct((b, s, ROUTE_W), F32),
                   jax.ShapeDtypeStruct((b * s // BLOCK, 1, LANES), F32)],
        grid=(b, nt),
        in_specs=[
            row(d), row(ATTN_W), row(GM_W), row(GM_W), row(2 * d),
            full(ws_bf), full(bs_rows), full(wpa), full(wpb), full(wo), full(ln_g), full(ln_b),
            modv(2), modv(3), modv(4),
            full(wr_hi), full(wr_lo), full(br),
        ],
        out_specs=[row(d), row(ROUTE_W), pl.BlockSpec((per, 1, LANES), lambda i, j: (i * nt + j, 0, 0))],
        compiler_params=_params("arbitrary", "arbitrary"),
        name="merge",
    )(x, y_attn, u, vg, gates, ws_bf, bs_rows, wpa, wpb, wo, ln_g, ln_b, mod3, mod3, mod3, wr_hi, wr_lo, br)


def _plan_body(cnt_ref, rec_ref, pos_ref, tab_ref, cblk_ref, base_ref, *, chunks):
    step = pl.program_id(0)
    n_chunks = cnt_ref.shape[0]
    max_blocks = tab_ref.shape[0]

    @pl.when(step == 0)
    def _():
        cnt = cnt_ref[...]
        total = jnp.sum(cnt, 0, keepdims=True)
        n_blk = jnp.floor((total + (BLOCK - 1)) * (1.0 / BLOCK))
        before = (_iota((LANES, LANES), 0) < _iota((LANES, LANES), 1)).astype(BF16)
        nb_hi, nb_lo = _split(jnp.broadcast_to(n_blk, (SUBLANES, LANES)))
        blk_start = (_dot(nb_hi, before) + _dot(nb_lo, before))[0:1]
        earlier = (_iota((n_chunks, n_chunks), 1) < _iota((n_chunks, n_chunks), 0)).astype(BF16)
        base_ref[...] = blk_start * float(BLOCK) + _dot(earlier, cnt.astype(BF16))
        blk_end = blk_start + n_blk
        sub = _iota((SUBLANES, LANES), 0)
        cblk_ref[...] = jnp.where(sub == 0, blk_start, jnp.where(sub == 1, blk_end, 0.0)).astype(I32)
        lane = _iota((max_blocks, LANES), 1)
        blk = _iota((max_blocks, LANES), 0).astype(F32)
        is_cls = lane < N_CLASSES
        n_real = jnp.max(jnp.where(is_cls, blk_end, 0.0), -1, keepdims=True)
        u = jnp.minimum(blk, n_real - 1.0)
        cls_of = jnp.sum(jnp.where(is_cls & (blk_end <= u), 1.0, 0.0), -1, keepdims=True)
        tab = jnp.where(lane == 0, cls_of, jnp.where(lane == 1, n_real, 0.0))
        tab_ref[...] = tab.astype(I32)

    lane = _iota((BLOCK, LANES), 1)
    for c in range(chunks):
        rec = rec_ref[c * BLOCK:(c + 1) * BLOCK, :]
        cls = rec[:, REC_CLS:REC_CLS + 1].astype(I32)
        base = base_ref[pl.ds(step * chunks + c, 1), :]
        pos = jnp.sum(jnp.where(lane == cls, base, 0.0), -1, keepdims=True) + rec[:, REC_RANK:REC_RANK + 1]
        pos_ref[c] = jnp.transpose(jnp.broadcast_to(pos, (BLOCK, LANES)))[0:1].astype(I32)


def _plan(counts, rec, max_blocks):
    n = rec.shape[0]
    n_chunks = n // BLOCK
    chunks = SUBLANES
    while n_chunks % chunks:
        chunks //= 2
    return pl.pallas_call(
        functools.partial(_plan_body, chunks=chunks),
        out_shape=[jax.ShapeDtypeStruct((n_chunks, 1, LANES), I32), jax.ShapeDtypeStruct((max_blocks, LANES), I32),
                   jax.ShapeDtypeStruct((SUBLANES, LANES), I32)],
        grid=(n_chunks // chunks,),
        in_specs=[
            pl.BlockSpec((n_chunks, LANES), lambda i: (0, 0)),
            pl.BlockSpec((chunks * BLOCK, ROUTE_W), lambda i: (i, 0)),
        ],
        out_specs=[
            pl.BlockSpec((chunks, 1, LANES), lambda i: (i, 0, 0)),
            pl.BlockSpec((max_blocks, LANES), lambda i: (0, 0)),
            pl.BlockSpec((SUBLANES, LANES), lambda i: (0, 0)),
        ],
        scratch_shapes=[pltpu.VMEM((n_chunks, LANES), F32)],
        compiler_params=_params("arbitrary"),
        name="plan",
    )(counts, rec)


def _tile_rows(base, j, pitch):
    return pl.ds(base + j, BLOCK, stride=pitch)


def _start_all(copies):
    for r, cp in enumerate(copies):
        cp.start(priority=r % 2)


def _load_tiles(ref, base, pitch, width):
    return jnp.concatenate([ref[_tile_rows(base, j, pitch), :] for j in range(width // LANES)], axis=1)


def _store_tiles(ref, base, pitch, val):
    for j in range(val.shape[1] // LANES):
        ref[_tile_rows(base, j, pitch), :] = val[:, j * LANES:(j + 1) * LANES]


def _scatter_body(cls_blk_ref, pos_ref, x1_ref, sh_ref, sc_ref, xs_hbm, hbuf, zbuf, ssem, zsem):
    u = pl.program_id(0)
    slot = u % 2
    d = x1_ref.shape[-1]
    tile_rows = BLOCK * SUBLANES

    def row_copies(s):
        return [pltpu.make_async_copy(hbuf.at[pl.ds(s * tile_rows + r * SUBLANES, SUBLANES)],
                                      xs_hbm.at[pl.ds(pl.multiple_of(pos_ref[0, r] * SUBLANES, SUBLANES), SUBLANES)],
                                      ssem.at[s]) for r in range(BLOCK)]

    @pl.when(u == 0)
    def _():
        zbuf[...] = jnp.zeros_like(zbuf)

        def zero_copy(c):
            start = pl.multiple_of((cls_blk_ref[1, c] - 1) * tile_rows, tile_rows)
            return pltpu.make_async_copy(zbuf, xs_hbm.at[pl.ds(start, tile_rows)], zsem)

        for c in range(N_CLASSES):
            @pl.when(cls_blk_ref[1, c] > cls_blk_ref[0, c])
            def _():
                zero_copy(c).start()
        for c in range(N_CLASSES):
            @pl.when(cls_blk_ref[1, c] > cls_blk_ref[0, c])
            def _():
                zero_copy(c).wait()

        def tail_copy(blk):
            return pltpu.make_async_copy(zbuf, xs_hbm.at[pl.ds(pl.multiple_of(blk * tile_rows, tile_rows), tile_rows)],
                                         zsem)

        n_real = cls_blk_ref[1, N_CLASSES - 1]
        n_total = xs_hbm.shape[0] // tile_rows
        lax.fori_loop(n_real, n_total, lambda blk, carry: (tail_copy(blk).start(), carry)[1], 0)
        lax.fori_loop(n_real, n_total, lambda blk, carry: (tail_copy(blk).wait(), carry)[1], 0)

    @pl.when(u >= 2)
    def _():
        for cp in row_copies(slot):
            cp.wait()

    _store_tiles(hbuf, slot * tile_rows, SUBLANES, _ln(x1_ref[...]) * (1.0 + sc_ref[...]) + sh_ref[...])
    _start_all(row_copies(slot))

    @pl.when(u == pl.num_programs(0) - 1)
    def _():
        for cp in row_copies(slot):
            cp.wait()

        @pl.when(u >= 1)
        def _():
            for cp in row_copies(1 - slot):
                cp.wait()


def _scatter(x1, mod3, pos, cls_blk, max_blocks, seq):
    n, d = x1.shape
    n_chunks = n // BLOCK
    per_batch = seq // BLOCK
    tile_rows = BLOCK * SUBLANES
    return pl.pallas_call(
        _scatter_body,
        out_shape=jax.ShapeDtypeStruct((max_blocks * tile_rows, LANES), F32),
        grid=(n_chunks,),
        in_specs=[
            pl.BlockSpec(memory_space=pltpu.SMEM),
            pl.BlockSpec((None, 1, LANES), lambda u: (u, 0, 0), memory_space=pltpu.SMEM),
            pl.BlockSpec((BLOCK, d), lambda u: (u, 0)),
            pl.BlockSpec((None, 1, d), lambda u: (u // per_batch, 0, 3)),
            pl.BlockSpec((None, 1, d), lambda u: (u // per_batch, 0, 4)),
        ],
        out_specs=pl.BlockSpec(memory_space=pl.ANY),
        scratch_shapes=[
            pltpu.VMEM((2 * tile_rows, LANES), F32),
            pltpu.VMEM((tile_rows, LANES), F32),
            pltpu.SemaphoreType.DMA((2,)),
            pltpu.SemaphoreType.DMA,
        ],
        compiler_params=_params("arbitrary"),
        name="scatter",
    )(cls_blk, pos, x1, mod3, mod3)


def _expert(h, w1, w3, w2):
    hid = jax.nn.silu(_dot(h, w1)) * _dot(h, w3)
    return _dot(hid, w2)


def _moe_body(bgrp_ref, blo_ref, bhi_ref, nreal_ref, xs_ref, w1_ref, w3_ref, w2_ref, o_ref):
    u = pl.program_id(0)
    d = w1_ref.shape[1]

    @pl.when(u < nreal_ref[0])
    def _():
        h = _load_tiles(xs_ref, 0, SUBLANES, d)
        for k, e in enumerate((blo_ref[u], bhi_ref[u])):
            _store_tiles(o_ref, k * SUBLANES, 2 * SUBLANES, _expert(h, w1_ref[e], w3_ref[e], w2_ref[e]))

    @pl.when(u >= nreal_ref[0])
    def _():
        o_ref[...] = jnp.zeros_like(o_ref)


def _moe(xs, tables, w1, w3, w2):
    bgrp, blo, bhi, n_real = tables
    max_blocks = bgrp.shape[0]
    d, de = w1.shape[1:]
    tile_rows = BLOCK * SUBLANES
    w1g = w1.reshape(N_GROUPS, EPG, d, de)
    w3g = w3.reshape(N_GROUPS, EPG, d, de)
    w2g = w2.reshape(N_GROUPS, EPG, de, d)

    def w_spec(a):
        return pl.BlockSpec((None,) + a.shape[1:], lambda u, g, lo, hi, n_real: (g[u], 0, 0, 0),
                            pipeline_mode=pl.Buffered(1))

    grid_spec = pltpu.PrefetchScalarGridSpec(
        num_scalar_prefetch=4,
        grid=(max_blocks,),
        in_specs=[
            pl.BlockSpec((tile_rows, LANES), lambda u, g, lo, hi, n_real: (jnp.minimum(u, n_real[0] - 1), 0)),
            w_spec(w1g), w_spec(w3g), w_spec(w2g),
        ],
        out_specs=pl.BlockSpec((2 * tile_rows, LANES), lambda u, *_: (u, 0)),
    )
    return pl.pallas_call(
        _moe_body,
        out_shape=jax.ShapeDtypeStruct((max_blocks * 2 * tile_rows, LANES), F32),
        grid_spec=grid_spec,
        compiler_params=pltpu.CompilerParams(dimension_semantics=("arbitrary",), vmem_limit_bytes=MOE_VMEM_LIMIT_BYTES),
        name="moe",
    )(bgrp, blo, bhi, n_real, xs, w1g, w3g, w2g)


def _final_body(pos_ref, posn_ref, x1_ref, rec_ref, g2_ref, lng_ref, lnb_ref, ms_hbm, o_ref, gbuf, gsem):
    u = pl.program_id(0)
    slot = u % 2
    d = x1_ref.shape[-1]
    pitch = 2 * SUBLANES
    tile_rows = BLOCK * pitch

    def row_copies(idx_ref, s):
        return [pltpu.make_async_copy(ms_hbm.at[pl.ds(pl.multiple_of(idx_ref[0, r] * pitch, pitch), pitch)],
                                      gbuf.at[pl.ds(s * tile_rows + r * pitch, pitch)], gsem.at[s])
                for r in range(BLOCK)]

    @pl.when(u == 0)
    def _():
        _start_all(row_copies(pos_ref, 0))

    for cp in row_copies(pos_ref, slot):
        cp.wait()
    _start_all(row_copies(posn_ref, 1 - slot))
    rec = rec_ref[...]
    y_lo = _load_tiles(gbuf, slot * tile_rows, pitch, d)
    y_hi = _load_tiles(gbuf, slot * tile_rows + SUBLANES, pitch, d)
    moe = rec[:, REC_WLO:REC_WLO + 1] * y_lo + rec[:, REC_WHI:REC_WHI + 1] * y_hi
    o_ref[...] = _ln(ALPHA * x1_ref[...] + g2_ref[...] * moe) * lng_ref[...] + lnb_ref[...]

    @pl.when(u == pl.num_programs(0) - 1)
    def _():
        for cp in row_copies(posn_ref, 1 - slot):
            cp.wait()


def _final(x1, rec, ms, pos, mod3, ln_g, ln_b, seq):
    n, d = x1.shape
    n_chunks = n // BLOCK
    per_batch = seq // BLOCK
    vec = pl.BlockSpec((1, d), lambda u: (0, 0))

    def pos_spec(off):
        return pl.BlockSpec((None, 1, LANES), lambda u: (jnp.minimum(u + off, n_chunks - 1), 0, 0),
                            memory_space=pltpu.SMEM)

    return pl.pallas_call(
        _final_body,
        out_shape=jax.ShapeDtypeStruct((n, d), F32),
        grid=(n_chunks,),
        in_specs=[
            pos_spec(0), pos_spec(1),
            pl.BlockSpec((BLOCK, d), lambda u: (u, 0)),
            pl.BlockSpec((BLOCK, ROUTE_W), lambda u: (u, 0)),
            pl.BlockSpec((None, 1, d), lambda u: (u // per_batch, 0, 5)),
            vec, vec,
            pl.BlockSpec(memory_space=pl.ANY),
        ],
        out_specs=pl.BlockSpec((BLOCK, d), lambda u: (u, 0)),
        scratch_shapes=[pltpu.VMEM((2 * BLOCK * 2 * SUBLANES, LANES), F32), pltpu.SemaphoreType.DMA((2,))],
        compiler_params=_params("arbitrary"),
        name="final",
    )(pos, pos, x1, rec, mod3, ln_g.reshape(1, d), ln_b.reshape(1, d), ms)


def _block_tables(tab):
    cls = tab[:, 0]
    pair = cls % PAIRS
    first = jnp.array([i for i in range(EPG) for _ in range(i + 1, EPG)], I32)
    second = jnp.array([j for i in range(EPG) for j in range(i + 1, EPG)], I32)
    return cls // PAIRS, first[pair], second[pair], tab[0, 1].reshape(1)


def kernel(x, c, ctx, c_ctx, w_ada, b_ada, w_in, attn_sink, gm_ln_g, gm_ln_b, gm_ws, gm_bs, w_pa, w_pb, w_o,
           ln1_g, ln1_b, router_g_w, router_g_b, router_e_w, router_e_b, moe_w1, moe_w3, moe_w2, ln2_g, ln2_b):
    b, s, d = x.shape
    assert w_ada.shape[0] == DEPTH and s % BLOCK == 0 and s % GRID_W == 0
    mod_rows = 2 * SUBLANES
    assert b + 1 <= mod_rows
    c_rows = jnp.zeros((mod_rows, d), F32).at[:b].set(c).at[b].set(c_ctx)
    mod = _adaln_mod(c_rows, w_ada[0], b_ada[0])
    mod3 = mod.reshape(mod_rows, 1, 6 * d)
    w_in_bf = w_in[0].astype(BF16)
    kv_ctx = _ctx_kv(ctx, mod3, b, w_in_bf)
    tile = 512 if s % 512 == 0 else BLOCK
    q, kv, u, vg, gates = _project(x, mod3, w_in_bf, _rope_tables(s), gm_ln_g[0], gm_ln_b[0], tile)
    y_attn = _attend(q, kv, kv_ctx, attn_sink[0])
    bs_rows = jnp.repeat(gm_bs[0].T, GM_HEAD, axis=1)
    w_route = jnp.concatenate([router_g_w[0], jnp.transpose(router_e_w[0], (1, 0, 2)).reshape(d, N_EXPERTS)], axis=1)
    w_route = jnp.pad(w_route, ((0, 0), (0, ROUTE_W - w_route.shape[1])))
    b_route = jnp.pad(jnp.concatenate([router_g_b[0], router_e_b[0].reshape(-1)]), (0, ROUTE_W - N_GROUPS - N_EXPERTS))
    wr_hi = w_route.astype(BF16)
    wr_lo = (w_route - wr_hi.astype(F32)).astype(BF16)
    merge_tile = 2 * BLOCK if s % (2 * BLOCK) == 0 else BLOCK
    x1, rec, counts = _merge(x, y_attn, u, vg, gates, gm_ws[0].astype(BF16), bs_rows, w_pa[0].astype(BF16),
                             w_pb[0].astype(BF16), w_o[0].astype(BF16), ln1_g[0], ln1_b[0], mod3, wr_hi, wr_lo,
                             b_route.reshape(1, ROUTE_W), merge_tile)
    n = b * s
    x1 = x1.reshape(n, d)
    rec = rec.reshape(n, ROUTE_W)
    max_blocks = n // BLOCK + N_CLASSES - 1
    pos, tab, cls_blk = _plan(counts.reshape(n // BLOCK, LANES), rec, max_blocks)
    tables = _block_tables(tab)
    xs = _scatter(x1, mod3, pos, cls_blk, max_blocks, s)
    ms = _moe(xs, tables, moe_w1[0], moe_w3[0], moe_w2[0])
    out = _final(x1, rec, ms, pos, mod3, ln2_g[0], ln2_b[0], s)
    return out.reshape(b, s, d)
```

```python
import functools

import jax
import jax.numpy as jnp
from jax import lax
from jax.experimental import pallas as pl
from jax.experimental.pallas import tpu as pltpu

F32 = jnp.float32
BF16 = jnp.bfloat16
I32 = jnp.int32

GRID_W = 64
HEAD_DIM = 64
Q_HEADS = 8
KV_HEADS = 2
GROUP = Q_HEADS // KV_HEADS
BLOCK = 128
ROPE_BASE = 10000.0
ROPE_FREQS = HEAD_DIM // 4
GM_GROUPS = 8
GM_HEAD = 64
ATTN_W = Q_HEADS * HEAD_DIM
KV_W = KV_HEADS * HEAD_DIM
GM_W = GM_GROUPS * GM_HEAD
N_GROUPS = 4
EPG = 8
N_EXPERTS = N_GROUPS * EPG
PAIRS = EPG * (EPG - 1) // 2
N_CLASSES = N_GROUPS * PAIRS
LN_EPS = 1e-6
NEG_INF = -1e30
DEPTH = 1
ALPHA = (2.0 * DEPTH) ** 0.25

LANES = 128
SUBLANES = 8
VMEM_LIMIT_BYTES = 56 * 1024 * 1024
MOE_VMEM_LIMIT_BYTES = 60 * 1024 * 1024
MOE_PACK = 2
ROUTE_W = LANES
REC_WLO, REC_WHI, REC_CLS, REC_RANK = 0, 1, 2, 3


def _params(*sem):
    return pltpu.CompilerParams(dimension_semantics=sem, vmem_limit_bytes=VMEM_LIMIT_BYTES)


def _ln(x):
    mu = jnp.mean(x, -1, keepdims=True)
    xc = x - mu
    var = jnp.mean(xc * xc, -1, keepdims=True)
    return xc * lax.rsqrt(var + LN_EPS)


def _dot(a, b):
    return jnp.dot(a, b, preferred_element_type=F32)


def _dot_nt(a, b):
    return lax.dot_general(a, b, (((1,), (1,)), ((), ())), preferred_element_type=F32)


def _split(a):
    hi = a.astype(BF16)
    lo = (a - hi.astype(F32)).astype(BF16)
    return hi, lo


def _dot_split(a, b_hi, b_lo):
    a_hi, a_lo = _split(a)
    return _dot(a_hi, b_hi) + _dot(a_hi, b_lo) + _dot(a_lo, b_hi)


def _iota(shape, axis):
    return lax.broadcasted_iota(I32, shape, axis)


def _mod_body(c_ref, w_ref, b_ref, o_ref):
    w_hi, w_lo = _split(w_ref[...])
    o_ref[...] = _dot_split(jax.nn.silu(c_ref[...]), w_hi, w_lo) + b_ref[...]


def _adaln_mod(c_rows, w_ada, b_ada):
    rows, d = c_rows.shape
    width = w_ada.shape[1]
    tn = 6 * LANES
    return pl.pallas_call(
        _mod_body,
        out_shape=jax.ShapeDtypeStruct((rows, width), F32),
        grid=(width // tn,),
        in_specs=[
            pl.BlockSpec((rows, d), lambda j: (0, 0)),
            pl.BlockSpec((d, tn), lambda j: (0, j)),
            pl.BlockSpec((1, tn), lambda j: (0, j)),
        ],
        out_specs=pl.BlockSpec((rows, tn), lambda j: (0, j)),
        compiler_params=_params("arbitrary"),
        name="mod",
    )(c_rows, w_ada, b_ada.reshape(1, width))


def _ctx_body(ctx_ref, sh_ref, sc_ref, w_ref, o_ref):
    h = _ln(ctx_ref[...]) * (1.0 + sc_ref[...]) + sh_ref[...]
    o_ref[...] = _dot(h.astype(BF16), w_ref[...]).astype(BF16)


def _ctx_kv(ctx, mod3, ctx_row, w_in_bf):
    b, c_len, d = ctx.shape
    kvw = 2 * KV_W
    return pl.pallas_call(
        _ctx_body,
        out_shape=jax.ShapeDtypeStruct((b, c_len, kvw), BF16),
        grid=(b,),
        in_specs=[
            pl.BlockSpec((None, c_len, d), lambda i: (i, 0, 0)),
            pl.BlockSpec((None, 1, d), lambda i: (ctx_row, 0, 0)),
            pl.BlockSpec((None, 1, d), lambda i: (ctx_row, 0, 1)),
            pl.BlockSpec((d, kvw), lambda i: (0, ATTN_W // kvw)),
        ],
        out_specs=pl.BlockSpec((None, c_len, kvw), lambda i: (i, 0, 0)),
        compiler_params=_params("arbitrary"),
        name="ctx_kv",
    )(ctx, mod3, mod3, w_in_bf)


def _rope(x, cos, s_up, s_dn):
    return x * cos + pltpu.roll(x, LANES - ROPE_FREQS, 1) * s_up + pltpu.roll(x, ROPE_FREQS, 1) * s_dn


def _project_body(x_ref, sh_ref, sc_ref, w_ref, cos_ref, sup_ref, sdn_ref, lng_ref, lnb_ref,
                  q_ref, kv_ref, u_ref, vg_ref, gate_ref):
    h = (_ln(x_ref[...]) * (1.0 + sc_ref[...]) + sh_ref[...]).astype(BF16)
    cos, s_up, s_dn = cos_ref[...], sup_ref[...], sdn_ref[...]
    q_scale = HEAD_DIM ** -0.5
    k0 = ATTN_W
    u0 = k0 + 2 * KV_W
    vg0 = u0 + GM_W
    gate0 = vg0 + GM_W
    q = _dot(h, w_ref[:, 0:k0])
    for j in range(ATTN_W // LANES):
        sl = slice(j * LANES, (j + 1) * LANES)
        q_ref[:, sl] = (_rope(q[:, sl], cos, s_up, s_dn) * q_scale).astype(BF16)
    kv = _dot(h, w_ref[:, k0:u0])
    kv_ref[:, 0:KV_W] = _rope(kv[:, 0:KV_W], cos, s_up, s_dn).astype(BF16)
    kv_ref[:, KV_W:] = kv[:, KV_W:].astype(BF16)
    u_ref[...] = jax.nn.gelu(_dot(h, w_ref[:, u0:vg0])).astype(BF16)
    vg = _ln(jax.nn.gelu(_dot(h, w_ref[:, vg0:gate0]))) * lng_ref[...] + lnb_ref[...]
    vg_ref[...] = vg.astype(BF16)
    gate_ref[...] = jax.nn.sigmoid(_dot(h, w_ref[:, gate0:])).astype(BF16)


def _project(x, mod3, w_in_bf, rope_tabs, gm_ln_g, gm_ln_b, tile):
    b, s, d = x.shape
    in_w = w_in_bf.shape[1]
    gate_w = in_w - (ATTN_W + 2 * KV_W + 2 * GM_W)
    row = lambda width: pl.BlockSpec((None, tile, width), lambda i, t: (i, t, 0))
    tab = pl.BlockSpec((tile, LANES), lambda i, t: (t, 0))
    vec = lambda width: pl.BlockSpec((1, width), lambda i, t: (0, 0))
    return pl.pallas_call(
        _project_body,
        out_shape=[
            jax.ShapeDtypeStruct((b, s, ATTN_W), BF16),
            jax.ShapeDtypeStruct((b, s, 2 * KV_W), BF16),
            jax.ShapeDtypeStruct((b, s, GM_W), BF16),
            jax.ShapeDtypeStruct((b, s, GM_W), BF16),
            jax.ShapeDtypeStruct((b, s, gate_w), BF16),
        ],
        grid=(b, s // tile),
        in_specs=[
            row(d),
            pl.BlockSpec((None, 1, d), lambda i, t: (i, 0, 0)),
            pl.BlockSpec((None, 1, d), lambda i, t: (i, 0, 1)),
            pl.BlockSpec((d, in_w), lambda i, t: (0, 0)),
            tab, tab, tab,
            vec(GM_W), vec(GM_W),
        ],
        out_specs=[row(ATTN_W), row(2 * KV_W), row(GM_W), row(GM_W), row(gate_w)],
        compiler_params=_params("arbitrary", "arbitrary"),
        name="project",
    )(x, mod3, mod3, w_in_bf, *rope_tabs, gm_ln_g.reshape(1, GM_W), gm_ln_b.reshape(1, GM_W))


def _rope_tables(s):
    pos = jnp.arange(s, dtype=I32)
    inv = 1.0 / (ROPE_BASE ** (jnp.arange(ROPE_FREQS, dtype=F32) / ROPE_FREQS))
    ang_r = (pos // GRID_W).astype(F32)[:, None] * inv[None, :]
    ang_c = (pos % GRID_W).astype(F32)[:, None] * inv[None, :]
    cr, sr, cc, sn = jnp.cos(ang_r), jnp.sin(ang_r), jnp.cos(ang_c), jnp.sin(ang_c)
    z = jnp.zeros_like(sr)
    reps = LANES // HEAD_DIM
    cos = jnp.tile(jnp.concatenate([cr, cr, cc, cc], -1), (1, reps))
    s_up = jnp.tile(jnp.concatenate([-sr, z, -sn, z], -1), (1, reps))
    s_dn = jnp.tile(jnp.concatenate([z, sr, z, sn], -1), (1, reps))
    return cos, s_up, s_dn


def _attend_body(sink_ref, q_ref, kvp_ref, kvc_ref, kvn_ref, ctx_ref, o_ref):
    i = pl.program_id(1)
    n_steps = pl.num_programs(1)
    per = q_ref.shape[0] // BLOCK
    rows = GROUP * BLOCK
    r = _iota((rows, 3 * BLOCK), 0) & (BLOCK - 1)
    m = _iota((rows, 3 * BLOCK), 1)
    band = (m >= r) & (m <= r + 2 * BLOCK)
    grp = _iota((rows, 1), 0) // BLOCK
    kv_all = jnp.concatenate([kvp_ref[...], kvc_ref[...], kvn_ref[...]], axis=0)
    kv_ctx = ctx_ref[...]
    for t in range(per):
        mask = band
        if t == 0:
            mask = mask & ((m >= BLOCK) | (i > 0))
        if t == per - 1:
            mask = mask & ((m < 2 * BLOCK) | (i < n_steps - 1))
        q = q_ref[t * BLOCK:(t + 1) * BLOCK, :]
        kv_loc = kv_all[t * BLOCK:(t + 3) * BLOCK]
        outs = []
        for h in range(KV_HEADS):
            ks = slice(h * HEAD_DIM, (h + 1) * HEAD_DIM)
            vs = slice(KV_W + h * HEAD_DIM, KV_W + (h + 1) * HEAD_DIM)
            qg = jnp.concatenate(
                [q[:, (h * GROUP + g) * HEAD_DIM:(h * GROUP + g + 1) * HEAD_DIM] for g in range(GROUP)], axis=0)
            sink = jnp.zeros((rows, 1), F32)
            for g in range(GROUP):
                sink = jnp.where(grp == g, sink_ref[h * GROUP + g], sink)
            s_ctx = _dot_nt(qg, kv_ctx[:, ks])
            s_loc = jnp.where(mask, _dot_nt(qg, kv_loc[:, ks]), NEG_INF)
            top = jnp.maximum(jnp.maximum(jnp.max(s_ctx, -1, keepdims=True), jnp.max(s_loc, -1, keepdims=True)), sink)
            e_ctx = jnp.exp(s_ctx - top)
            e_loc = jnp.exp(s_loc - top)
            den = jnp.sum(e_ctx, -1, keepdims=True) + jnp.sum(e_loc, -1, keepdims=True) + jnp.exp(sink - top)
            o = (_dot(e_ctx.astype(BF16), kv_ctx[:, vs]) + _dot(e_loc.astype(BF16), kv_loc[:, vs])) * (1.0 / den)
            outs += [o[g * BLOCK:(g + 1) * BLOCK] for g in range(GROUP)]
        o_ref[t * BLOCK:(t + 1) * BLOCK, :] = jnp.concatenate(outs, axis=1).astype(BF16)


def _attend(q, kv, kv_ctx, sink, per):
    b, s, _ = q.shape
    nb = s // BLOCK
    c_len = kv_ctx.shape[1]
    kvw = 2 * KV_W
    return pl.pallas_call(
        _attend_body,
        out_shape=jax.ShapeDtypeStruct((b, s, ATTN_W), BF16),
        grid=(b, nb // per),
        in_specs=[
            pl.BlockSpec(memory_space=pltpu.SMEM),
            pl.BlockSpec((None, per * BLOCK, ATTN_W), lambda i, j: (i, j, 0)),
            pl.BlockSpec((None, BLOCK, kvw), lambda i, j: (i, jnp.maximum(j * per - 1, 0), 0)),
            pl.BlockSpec((None, per * BLOCK, kvw), lambda i, j: (i, j, 0)),
            pl.BlockSpec((None, BLOCK, kvw), lambda i, j: (i, jnp.minimum((j + 1) * per, nb - 1), 0)),
            pl.BlockSpec((None, c_len, kvw), lambda i, j: (i, 0, 0)),
        ],
        out_specs=pl.BlockSpec((None, per * BLOCK, ATTN_W), lambda i, j: (i, j, 0)),
        compiler_params=_params("arbitrary", "arbitrary"),
        name="attend",
    )(sink, q, kv, kv, kv, kv_ctx)


def _route(logits):
    lane = _iota(logits.shape, 1)
    far = jnp.int32(2 * ROUTE_W)

    def first_max(vals):
        top = jnp.max(vals, -1, keepdims=True)
        return top, jnp.min(jnp.where(vals == top, lane, far), -1, keepdims=True)

    is_g = lane < N_GROUPS
    g_top, g_idx = first_max(jnp.where(is_g, logits, NEG_INF))
    g_w = 1.0 / jnp.sum(jnp.where(is_g, jnp.exp(logits - g_top), 0.0), -1, keepdims=True)
    lo = N_GROUPS + EPG * g_idx
    e_vals = jnp.where((lane >= lo) & (lane < lo + EPG), logits, NEG_INF)
    v1, i1 = first_max(e_vals)
    v2, i2 = first_max(jnp.where(lane == i1, NEG_INF, e_vals))
    t = jnp.exp(v2 - v1)
    w1 = g_w * (1.0 / (1.0 + t))
    w2 = g_w * (t / (1.0 + t))
    swap = i2 < i1
    a = jnp.where(swap, i2, i1) - lo
    b = jnp.where(swap, i1, i2) - lo
    cls = g_idx * PAIRS + ((a * (2 * EPG - 1 - a)) >> 1) + (b - a - 1)
    return jnp.where(swap, w2, w1), jnp.where(swap, w1, w2), cls


def _merge_body(x_ref, ya_ref, u_ref, vg_ref, gate_ref, ws_ref, bs_ref, wpa_ref, wpb_ref, wo_ref,
                lng_ref, lnb_ref, g1_ref, sh2_ref, sc2_ref, wrh_ref, wrl_ref, br_ref, x1_ref, rec_ref, cnt_ref):
    tile, d = x_ref.shape
    lane = _iota((BLOCK, LANES), 1)
    below = (lane < _iota((BLOCK, LANES), 0)).astype(BF16)
    chunks = []
    for c in range(tile // BLOCK):
        rows = slice(c * BLOCK, (c + 1) * BLOCK)
        pieces = []
        for j in range(GM_W // LANES):
            v = vg_ref[rows, j * LANES:(j + 1) * LANES]
            per = LANES // GM_HEAD
            sp = _dot(ws_ref[per * j], v)
            for k in range(1, per):
                sp = jnp.where(lane < k * GM_HEAD, sp, _dot(ws_ref[per * j + k], v))
            pieces.append(sp)
        chunks.append(jnp.concatenate(pieces, axis=1) + bs_ref[...])
    sp = jnp.concatenate(chunks, axis=0)
    y_gm = (u_ref[...].astype(F32) * sp).astype(BF16)
    gate = gate_ref[...].astype(F32)
    y = gate[:, :d] * _dot(ya_ref[...], wpa_ref[...]) + gate[:, d:] * _dot(y_gm, wpb_ref[...])
    mix = _dot(y.astype(BF16), wo_ref[...])
    x1 = _ln(ALPHA * x_ref[...] + g1_ref[...] * mix) * lng_ref[...] + lnb_ref[...]
    x1_ref[...] = x1
    h2 = _ln(x1) * (1.0 + sc2_ref[...]) + sh2_ref[...]
    w_lo, w_hi, cls = _route(_dot_split(h2, wrh_ref[...], wrl_ref[...]) + br_ref[...])
    for c in range(tile // BLOCK):
        rows = slice(c * BLOCK, (c + 1) * BLOCK)
        onehot = lane == cls[rows]
        hot = onehot.astype(BF16)
        rank = jnp.sum(jnp.where(onehot, _dot(below, hot), 0.0), -1, keepdims=True)
        cnt_ref[c] = jnp.sum(hot.astype(F32), 0, keepdims=True)
        rec = jnp.where(lane == REC_WLO, w_lo[rows], 0.0)
        rec = jnp.where(lane == REC_WHI, w_hi[rows], rec)
        rec = jnp.where(lane == REC_CLS, cls[rows].astype(F32), rec)
        rec_ref[rows, :] = jnp.where(lane == REC_RANK, rank, rec)


def _merge(x, y_attn, u, vg, gates, ws_bf, bs_rows, wpa, wpb, wo, ln_g, ln_b, mod3, wr_hi, wr_lo, br, tile):
    b, s, d = x.shape
    nt = s // tile
    per = tile // BLOCK
    row = lambda width: pl.BlockSpec((None, tile, width), lambda i, j: (i, j, 0))
    full = lambda a: pl.BlockSpec(a.shape, lambda i, j: (0,) * a.ndim)
    modv = lambda k: pl.BlockSpec((None, 1, d), lambda i, j: (i, 0, k))
    ln_g = ln_g.reshape(1, d)
    ln_b = ln_b.reshape(1, d)
    return pl.pallas_call(
        _merge_body,
        out_shape=[jax.ShapeDtypeStruct((b, s, d), F32), jax.ShapeDtypeStruct((b, s, ROUTE_W), F32),
                   jax.ShapeDtypeStruct((b * s // BLOCK, 1, LANES), F32)],
        grid=(b, nt),
        in_specs=[
            row(d), row(ATTN_W), row(GM_W), row(GM_W), row(2 * d),
            full(ws_bf), full(bs_rows), full(wpa), full(wpb), full(wo), full(ln_g), full(ln_b),
            modv(2), modv(3), modv(4),
            full(wr_hi), full(wr_lo), full(br),
        ],
        out_specs=[row(d), row(ROUTE_W), pl.BlockSpec((per, 1, LANES), lambda i, j: (i * nt + j, 0, 0))],
        compiler_params=_params("arbitrary", "arbitrary"),
        name="merge",
    )(x, y_attn, u, vg, gates, ws_bf, bs_rows, wpa, wpb, wo, ln_g, ln_b, mod3, mod3, mod3, wr_hi, wr_lo, br)


def _plan_body(cnt_ref, rec_ref, pos_ref, tab_ref, cblk_ref, base_ref, *, chunks):
    step = pl.program_id(0)
    n_chunks = cnt_ref.shape[0]
    max_blocks = tab_ref.shape[0]

    @pl.when(step == 0)
    def _():
        cnt = cnt_ref[...]
        total = jnp.sum(cnt, 0, keepdims=True)
        n_blk = jnp.floor((total + (BLOCK - 1)) * (1.0 / BLOCK))
        before = (_iota((LANES, LANES), 0) < _iota((LANES, LANES), 1)).astype(BF16)
        nb_hi, nb_lo = _split(jnp.broadcast_to(n_blk, (SUBLANES, LANES)))
        blk_start = (_dot(nb_hi, before) + _dot(nb_lo, before))[0:1]
        lane1 = _iota((1, LANES), 1)
        packed = blk_start
        shift = jnp.zeros((1, 1), F32)
        for g in range(1, N_GROUPS):
            first = jnp.sum(jnp.where(lane1 == g * PAIRS, blk_start, 0.0), -1, keepdims=True) + shift
            over = first - MOE_PACK * jnp.floor(first * (1.0 / MOE_PACK))
            shift = shift + jnp.where(over > 0.0, MOE_PACK - over, 0.0)
            packed = jnp.where(lane1 >= g * PAIRS, blk_start + shift, packed)
        blk_start = packed
        earlier = (_iota((n_chunks, n_chunks), 1) < _iota((n_chunks, n_chunks), 0)).astype(BF16)
        base_ref[...] = blk_start * float(BLOCK) + _dot(earlier, cnt.astype(BF16))
        blk_end = blk_start + n_blk
        sub = _iota((SUBLANES, LANES), 0)
        cblk_ref[...] = jnp.where(sub == 0, blk_start, jnp.where(sub == 1, blk_end, 0.0)).astype(I32)
        lane = _iota((max_blocks, LANES), 1)
        blk = _iota((max_blocks, LANES), 0).astype(F32)
        is_cls = lane < N_CLASSES
        n_real = jnp.max(jnp.where(is_cls, blk_end, 0.0), -1, keepdims=True)
        u = jnp.minimum(blk, n_real - 1.0)
        cls_of = jnp.sum(jnp.where(is_cls & (blk_end <= u), 1.0, 0.0), -1, keepdims=True)
        n_steps = jnp.floor((n_real + (MOE_PACK - 1)) * (1.0 / MOE_PACK))
        tab = jnp.where(lane == 0, cls_of, jnp.where(lane == 1, n_steps, 0.0))
        tab_ref[...] = tab.astype(I32)

    lane = _iota((BLOCK, LANES), 1)
    for c in range(chunks):
        rec = rec_ref[c * BLOCK:(c + 1) * BLOCK, :]
        cls = rec[:, REC_CLS:REC_CLS + 1].astype(I32)
        base = base_ref[pl.ds(step * chunks + c, 1), :]
        pos = jnp.sum(jnp.where(lane == cls, base, 0.0), -1, keepdims=True) + rec[:, REC_RANK:REC_RANK + 1]
        pos_ref[c] = jnp.transpose(jnp.broadcast_to(pos, (BLOCK, LANES)))[0:1].astype(I32)


def _plan(counts, rec, max_blocks):
    n = rec.shape[0]
    n_chunks = n // BLOCK
    chunks = SUBLANES
    while n_chunks % chunks:
        chunks //= 2
    return pl.pallas_call(
        functools.partial(_plan_body, chunks=chunks),
        out_shape=[jax.ShapeDtypeStruct((n_chunks, 1, LANES), I32), jax.ShapeDtypeStruct((max_blocks, LANES), I32),
                   jax.ShapeDtypeStruct((SUBLANES, LANES), I32)],
        grid=(n_chunks // chunks,),
        in_specs=[
            pl.BlockSpec((n_chunks, LANES), lambda i: (0, 0)),
            pl.BlockSpec((chunks * BLOCK, ROUTE_W), lambda i: (i, 0)),
        ],
        out_specs=[
            pl.BlockSpec((chunks, 1, LANES), lambda i: (i, 0, 0)),
            pl.BlockSpec((max_blocks, LANES), lambda i: (0, 0)),
            pl.BlockSpec((SUBLANES, LANES), lambda i: (0, 0)),
        ],
        scratch_shapes=[pltpu.VMEM((n_chunks, LANES), F32)],
        compiler_params=_params("arbitrary"),
        name="plan",
    )(counts, rec)


def _tile_rows(base, j, pitch):
    return pl.ds(base + j, BLOCK, stride=pitch)


def _start_all(copies):
    for r, cp in enumerate(copies):
        cp.start(priority=r % 2)


def _load_tiles(ref, base, pitch, width):
    return jnp.concatenate([ref[_tile_rows(base, j, pitch), :] for j in range(width // LANES)], axis=1)


def _store_tiles(ref, base, pitch, val):
    for j in range(val.shape[1] // LANES):
        ref[_tile_rows(base, j, pitch), :] = val[:, j * LANES:(j + 1) * LANES]


def _scatter_body(cls_blk_ref, pos_ref, x1_ref, sh_ref, sc_ref, xs_hbm, hbuf, zbuf, ssem, zsem):
    u = pl.program_id(0)
    slot = u % 2
    d = x1_ref.shape[-1]
    tile_rows = BLOCK * SUBLANES

    def row_copies(s):
        return [pltpu.make_async_copy(hbuf.at[pl.ds(s * tile_rows + r * SUBLANES, SUBLANES)],
                                      xs_hbm.at[pl.ds(pl.multiple_of(pos_ref[0, r] * SUBLANES, SUBLANES), SUBLANES)],
                                      ssem.at[s]) for r in range(BLOCK)]

    @pl.when(u == 0)
    def _():
        zbuf[...] = jnp.zeros_like(zbuf)

        def zero_copy(c):
            start = pl.multiple_of((cls_blk_ref[1, c] - 1) * tile_rows, tile_rows)
            return pltpu.make_async_copy(zbuf, xs_hbm.at[pl.ds(start, tile_rows)], zsem)

        for c in range(N_CLASSES):
            @pl.when(cls_blk_ref[1, c] > cls_blk_ref[0, c])
            def _():
                zero_copy(c).start()
        for c in range(N_CLASSES):
            @pl.when(cls_blk_ref[1, c] > cls_blk_ref[0, c])
            def _():
                zero_copy(c).wait()

        def gap_copy(blk):
            return pltpu.make_async_copy(zbuf, xs_hbm.at[pl.ds(pl.multiple_of(blk * tile_rows, tile_rows), tile_rows)],
                                         zsem)

        gaps = [(cls_blk_ref[1, g * PAIRS - 1], cls_blk_ref[0, g * PAIRS]) for g in range(1, N_GROUPS)]
        gaps.append((cls_blk_ref[1, N_CLASSES - 1], xs_hbm.shape[0] // tile_rows))
        for lo, hi in gaps:
            lax.fori_loop(lo, hi, lambda blk, carry: (gap_copy(blk).start(), carry)[1], 0)
        for lo, hi in gaps:
            lax.fori_loop(lo, hi, lambda blk, carry: (gap_copy(blk).wait(), carry)[1], 0)

    @pl.when(u >= 2)
    def _():
        for cp in row_copies(slot):
            cp.wait()

    _store_tiles(hbuf, slot * tile_rows, SUBLANES, _ln(x1_ref[...]) * (1.0 + sc_ref[...]) + sh_ref[...])
    _start_all(row_copies(slot))

    @pl.when(u == pl.num_programs(0) - 1)
    def _():
        for cp in row_copies(slot):
            cp.wait()

        @pl.when(u >= 1)
        def _():
            for cp in row_copies(1 - slot):
                cp.wait()


def _scatter(x1, mod3, pos, cls_blk, max_blocks, seq):
    n, d = x1.shape
    n_chunks = n // BLOCK
    per_batch = seq // BLOCK
    tile_rows = BLOCK * SUBLANES
    return pl.pallas_call(
        _scatter_body,
        out_shape=jax.ShapeDtypeStruct((max_blocks * tile_rows, LANES), F32),
        grid=(n_chunks,),
        in_specs=[
            pl.BlockSpec(memory_space=pltpu.SMEM),
            pl.BlockSpec((None, 1, LANES), lambda u: (u, 0, 0), memory_space=pltpu.SMEM),
            pl.BlockSpec((BLOCK, d), lambda u: (u, 0)),
            pl.BlockSpec((None, 1, d), lambda u: (u // per_batch, 0, 3)),
            pl.BlockSpec((None, 1, d), lambda u: (u // per_batch, 0, 4)),
        ],
        out_specs=pl.BlockSpec(memory_space=pl.ANY),
        scratch_shapes=[
            pltpu.VMEM((2 * tile_rows, LANES), F32),
            pltpu.VMEM((tile_rows, LANES), F32),
            pltpu.SemaphoreType.DMA((2,)),
            pltpu.SemaphoreType.DMA,
        ],
        compiler_params=_params("arbitrary"),
        name="scatter",
    )(cls_blk, pos, x1, mod3, mod3)


def _expert(h, w1, w3, w2):
    hid = jax.nn.silu(_dot(h, w1)) * _dot(h, w3)
    return _dot(hid, w2)


def _moe_body(sgrp_ref, blo_ref, bhi_ref, nsteps_ref, xs_ref, w1_ref, w3_ref, w2_ref, o_ref):
    u = pl.program_id(0)
    d = w1_ref.shape[1]
    in_rows = BLOCK * SUBLANES
    out_rows = 2 * in_rows

    @pl.when(u < nsteps_ref[0])
    def _():
        for k in range(MOE_PACK):
            h = _load_tiles(xs_ref, k * in_rows, SUBLANES, d)
            for side, e in enumerate((blo_ref[MOE_PACK * u + k], bhi_ref[MOE_PACK * u + k])):
                _store_tiles(o_ref, k * out_rows + side * SUBLANES, 2 * SUBLANES,
                             _expert(h, w1_ref[e], w3_ref[e], w2_ref[e]))

    @pl.when(u >= nsteps_ref[0])
    def _():
        o_ref[...] = jnp.zeros_like(o_ref)


def _moe(xs, tables, w1, w3, w2):
    sgrp, blo, bhi, n_steps = tables
    max_steps = sgrp.shape[0]
    d, de = w1.shape[1:]
    in_rows = MOE_PACK * BLOCK * SUBLANES
    w1g = w1.reshape(N_GROUPS, EPG, d, de)
    w3g = w3.reshape(N_GROUPS, EPG, d, de)
    w2g = w2.reshape(N_GROUPS, EPG, de, d)

    def w_spec(a):
        return pl.BlockSpec((None,) + a.shape[1:], lambda u, g, lo, hi, n_steps: (g[u], 0, 0, 0),
                            pipeline_mode=pl.Buffered(1))

    grid_spec = pltpu.PrefetchScalarGridSpec(
        num_scalar_prefetch=4,
        grid=(max_steps,),
        in_specs=[
            pl.BlockSpec((in_rows, LANES), lambda u, g, lo, hi, n_steps: (jnp.minimum(u, n_steps[0] - 1), 0)),
            w_spec(w1g), w_spec(w3g), w_spec(w2g),
        ],
        out_specs=pl.BlockSpec((2 * in_rows, LANES), lambda u, *_: (u, 0)),
    )
    return pl.pallas_call(
        _moe_body,
        out_shape=jax.ShapeDtypeStruct((max_steps * 2 * in_rows, LANES), F32),
        grid_spec=grid_spec,
        compiler_params=pltpu.CompilerParams(dimension_semantics=("arbitrary",), vmem_limit_bytes=MOE_VMEM_LIMIT_BYTES),
        name="moe",
    )(sgrp, blo, bhi, n_steps, xs, w1g, w3g, w2g)


def _final_body(pos_ref, posn_ref, x1_ref, rec_ref, g2_ref, lng_ref, lnb_ref, ms_hbm, o_ref, gbuf, gsem):
    u = pl.program_id(0)
    slot = u % 2
    d = x1_ref.shape[-1]
    pitch = 2 * SUBLANES
    tile_rows = BLOCK * pitch

    def row_copies(idx_ref, s):
        return [pltpu.make_async_copy(ms_hbm.at[pl.ds(pl.multiple_of(idx_ref[0, r] * pitch, pitch), pitch)],
                                      gbuf.at[pl.ds(s * tile_rows + r * pitch, pitch)], gsem.at[s])
                for r in range(BLOCK)]

    @pl.when(u == 0)
    def _():
        _start_all(row_copies(pos_ref, 0))

    for cp in row_copies(pos_ref, slot):
        cp.wait()
    _start_all(row_copies(posn_ref, 1 - slot))
    rec = rec_ref[...]
    y_lo = _load_tiles(gbuf, slot * tile_rows, pitch, d)
    y_hi = _load_tiles(gbuf, slot * tile_rows + SUBLANES, pitch, d)
    moe = rec[:, REC_WLO:REC_WLO + 1] * y_lo + rec[:, REC_WHI:REC_WHI + 1] * y_hi
    o_ref[...] = _ln(ALPHA * x1_ref[...] + g2_ref[...] * moe) * lng_ref[...] + lnb_ref[...]

    @pl.when(u == pl.num_programs(0) - 1)
    def _():
        for cp in row_copies(posn_ref, 1 - slot):
            cp.wait()


def _final(x1, rec, ms, pos, mod3, ln_g, ln_b, seq):
    n, d = x1.shape
    n_chunks = n // BLOCK
    per_batch = seq // BLOCK
    vec = pl.BlockSpec((1, d), lambda u: (0, 0))

    def pos_spec(off):
        return pl.BlockSpec((None, 1, LANES), lambda u: (jnp.minimum(u + off, n_chunks - 1), 0, 0),
                            memory_space=pltpu.SMEM)

    return pl.pallas_call(
        _final_body,
        out_shape=jax.ShapeDtypeStruct((n, d), F32),
        grid=(n_chunks,),
        in_specs=[
            pos_spec(0), pos_spec(1),
            pl.BlockSpec((BLOCK, d), lambda u: (u, 0)),
            pl.BlockSpec((BLOCK, ROUTE_W), lambda u: (u, 0)),
            pl.BlockSpec((None, 1, d), lambda u: (u // per_batch, 0, 5)),
            vec, vec,
            pl.BlockSpec(memory_space=pl.ANY),
        ],
        out_specs=pl.BlockSpec((BLOCK, d), lambda u: (u, 0)),
        scratch_shapes=[pltpu.VMEM((2 * BLOCK * 2 * SUBLANES, LANES), F32), pltpu.SemaphoreType.DMA((2,))],
        compiler_params=_params("arbitrary"),
        name="final",
    )(pos, pos, x1, rec, mod3, ln_g.reshape(1, d), ln_b.reshape(1, d), ms)


def _block_tables(tab):
    cls = tab[:, 0]
    pair = cls % PAIRS
    first = jnp.array([i for i in range(EPG) for _ in range(i + 1, EPG)], I32)
    second = jnp.array([j for i in range(EPG) for j in range(i + 1, EPG)], I32)
    return (cls // PAIRS)[::MOE_PACK], first[pair], second[pair], tab[0, 1].reshape(1)


def kernel(x, c, ctx, c_ctx, w_ada, b_ada, w_in, attn_sink, gm_ln_g, gm_ln_b, gm_ws, gm_bs, w_pa, w_pb, w_o,
           ln1_g, ln1_b, router_g_w, router_g_b, router_e_w, router_e_b, moe_w1, moe_w3, moe_w2, ln2_g, ln2_b):
    b, s, d = x.shape
    assert w_ada.shape[0] == DEPTH and s % BLOCK == 0 and s % GRID_W == 0
    mod_rows = 2 * SUBLANES
    assert b + 1 <= mod_rows
    c_rows = jnp.zeros((mod_rows, d), F32).at[:b].set(c).at[b].set(c_ctx)
    mod = _adaln_mod(c_rows, w_ada[0], b_ada[0])
    mod3 = mod.reshape(mod_rows, 1, 6 * d)
    w_in_bf = w_in[0].astype(BF16)
    kv_ctx = _ctx_kv(ctx, mod3, b, w_in_bf)
    tile = 512 if s % 512 == 0 else BLOCK
    q, kv, u, vg, gates = _project(x, mod3, w_in_bf, _rope_tables(s), gm_ln_g[0], gm_ln_b[0], tile)
    nb = s // BLOCK
    y_attn = _attend(q, kv, kv_ctx, attn_sink[0], next(p for p in (4, 2, 1) if nb % p == 0))
    bs_rows = jnp.repeat(gm_bs[0].T, GM_HEAD, axis=1)
    w_route = jnp.concatenate([router_g_w[0], jnp.transpose(router_e_w[0], (1, 0, 2)).reshape(d, N_EXPERTS)], axis=1)
    w_route = jnp.pad(w_route, ((0, 0), (0, ROUTE_W - w_route.shape[1])))
    b_route = jnp.pad(jnp.concatenate([router_g_b[0], router_e_b[0].reshape(-1)]), (0, ROUTE_W - N_GROUPS - N_EXPERTS))
    wr_hi = w_route.astype(BF16)
    wr_lo = (w_route - wr_hi.astype(F32)).astype(BF16)
    merge_tile = BLOCK * next(p for p in (4, 2, 1) if nb % p == 0)
    x1, rec, counts = _merge(x, y_attn, u, vg, gates, gm_ws[0].astype(BF16), bs_rows, w_pa[0].astype(BF16),
                             w_pb[0].astype(BF16), w_o[0].astype(BF16), ln1_g[0], ln1_b[0], mod3, wr_hi, wr_lo,
                             b_route.reshape(1, ROUTE_W), merge_tile)
    n = b * s
    x1 = x1.reshape(n, d)
    rec = rec.reshape(n, ROUTE_W)
    max_blocks = n // BLOCK + N_CLASSES - 1 + (N_GROUPS - 1) * (MOE_PACK - 1)
    max_blocks = -(-max_blocks // MOE_PACK) * MOE_PACK
    pos, tab, cls_blk = _plan(counts.reshape(n // BLOCK, LANES), rec, max_blocks)
    tables = _block_tables(tab)
    xs = _scatter(x1, mod3, pos, cls_blk, max_blocks, s)
    ms = _moe(xs, tables, moe_w1[0], moe_w3[0], moe_w2[0])
    out = _final(x1, rec, ms, pos, mod3, ln2_g[0], ln2_b[0], s)
    return out.reshape(b, s, d)
```

```python
import functools

import jax
import jax.numpy as jnp
from jax import lax
from jax.experimental import pallas as pl
from jax.experimental.pallas import tpu as pltpu

F32 = jnp.float32
BF16 = jnp.bfloat16
I32 = jnp.int32

GRID_W = 64
HEAD_DIM = 64
Q_HEADS = 8
KV_HEADS = 2
GROUP = Q_HEADS // KV_HEADS
BLOCK = 128
ROPE_BASE = 10000.0
ROPE_FREQS = HEAD_DIM // 4
GM_GROUPS = 8
GM_HEAD = 64
ATTN_W = Q_HEADS * HEAD_DIM
KV_W = KV_HEADS * HEAD_DIM
GM_W = GM_GROUPS * GM_HEAD
N_GROUPS = 4
EPG = 8
N_EXPERTS = N_GROUPS * EPG
PAIRS = EPG * (EPG - 1) // 2
N_CLASSES = N_GROUPS * PAIRS
LN_EPS = 1e-6
NEG_INF = -1e30
DEPTH = 1
ALPHA = (2.0 * DEPTH) ** 0.25

LANES = 128
SUBLANES = 8
VMEM_LIMIT_BYTES = 56 * 1024 * 1024
MOE_VMEM_LIMIT_BYTES = 60 * 1024 * 1024
MOE_PACK = 2
ROUTE_W = LANES
REC_WLO, REC_WHI, REC_CLS, REC_RANK = 0, 1, 2, 3


def _params(*sem):
    return pltpu.CompilerParams(dimension_semantics=sem, vmem_limit_bytes=VMEM_LIMIT_BYTES)


def _ln(x):
    mu = jnp.mean(x, -1, keepdims=True)
    xc = x - mu
    var = jnp.mean(xc * xc, -1, keepdims=True)
    return xc * lax.rsqrt(var + LN_EPS)


def _dot(a, b):
    return jnp.dot(a, b, preferred_element_type=F32)


def _dot_nt(a, b):
    return lax.dot_general(a, b, (((1,), (1,)), ((), ())), preferred_element_type=F32)


def _split(a):
    hi = a.astype(BF16)
    lo = (a - hi.astype(F32)).astype(BF16)
    return hi, lo


def _dot_split(a, b_hi, b_lo):
    a_hi, a_lo = _split(a)
    return _dot(a_hi, b_hi) + _dot(a_hi, b_lo) + _dot(a_lo, b_hi)


def _iota(shape, axis):
    return lax.broadcasted_iota(I32, shape, axis)


def _mod_body(c_ref, w_ref, b_ref, o_ref):
    w_hi, w_lo = _split(w_ref[...])
    o_ref[...] = _dot_split(jax.nn.silu(c_ref[...]), w_hi, w_lo) + b_ref[...]


def _adaln_mod(c_rows, w_ada, b_ada):
    rows, d = c_rows.shape
    width = w_ada.shape[1]
    tn = 6 * LANES
    return pl.pallas_call(
        _mod_body,
        out_shape=jax.ShapeDtypeStruct((rows, width), F32),
        grid=(width // tn,),
        in_specs=[
            pl.BlockSpec((rows, d), lambda j: (0, 0)),
            pl.BlockSpec((d, tn), lambda j: (0, j)),
            pl.BlockSpec((1, tn), lambda j: (0, j)),
        ],
        out_specs=pl.BlockSpec((rows, tn), lambda j: (0, j)),
        compiler_params=_params("arbitrary"),
        name="mod",
    )(c_rows, w_ada, b_ada.reshape(1, width))


def _ctx_body(ctx_ref, sh_ref, sc_ref, w_ref, o_ref):
    h = _ln(ctx_ref[...]) * (1.0 + sc_ref[...]) + sh_ref[...]
    o_ref[...] = _dot(h.astype(BF16), w_ref[...]).astype(BF16)


def _ctx_kv(ctx, mod3, ctx_row, w_in_bf):
    b, c_len, d = ctx.shape
    kvw = 2 * KV_W
    return pl.pallas_call(
        _ctx_body,
        out_shape=jax.ShapeDtypeStruct((b, c_len, kvw), BF16),
        grid=(b,),
        in_specs=[
            pl.BlockSpec((None, c_len, d), lambda i: (i, 0, 0)),
            pl.BlockSpec((None, 1, d), lambda i: (ctx_row, 0, 0)),
            pl.BlockSpec((None, 1, d), lambda i: (ctx_row, 0, 1)),
            pl.BlockSpec((d, kvw), lambda i: (0, ATTN_W // kvw)),
        ],
        out_specs=pl.BlockSpec((None, c_len, kvw), lambda i: (i, 0, 0)),
        compiler_params=_params("arbitrary"),
        name="ctx_kv",
    )(ctx, mod3, mod3, w_in_bf)


def _rope(x, cos, s_up, s_dn):
    return x * cos + pltpu.roll(x, LANES - ROPE_FREQS, 1) * s_up + pltpu.roll(x, ROPE_FREQS, 1) * s_dn


def _project_body(x_ref, sh_ref, sc_ref, w_ref, cos_ref, sup_ref, sdn_ref, lng_ref, lnb_ref,
                  q_ref, kv_ref, u_ref, vg_ref, gate_ref):
    h = (_ln(x_ref[...]) * (1.0 + sc_ref[...]) + sh_ref[...]).astype(BF16)
    cos, s_up, s_dn = cos_ref[...], sup_ref[...], sdn_ref[...]
    q_scale = HEAD_DIM ** -0.5
    k0 = ATTN_W
    u0 = k0 + 2 * KV_W
    vg0 = u0 + GM_W
    gate0 = vg0 + GM_W
    q = _dot(h, w_ref[:, 0:k0])
    for j in range(ATTN_W // LANES):
        sl = slice(j * LANES, (j + 1) * LANES)
        q_ref[:, sl] = (_rope(q[:, sl], cos, s_up, s_dn) * q_scale).astype(BF16)
    kv = _dot(h, w_ref[:, k0:u0])
    kv_ref[:, 0:KV_W] = _rope(kv[:, 0:KV_W], cos, s_up, s_dn).astype(BF16)
    kv_ref[:, KV_W:] = kv[:, KV_W:].astype(BF16)
    u_ref[...] = jax.nn.gelu(_dot(h, w_ref[:, u0:vg0])).astype(BF16)
    vg = _ln(jax.nn.gelu(_dot(h, w_ref[:, vg0:gate0]))) * lng_ref[...] + lnb_ref[...]
    vg_ref[...] = vg.astype(BF16)
    gate_ref[...] = jax.nn.sigmoid(_dot(h, w_ref[:, gate0:])).astype(BF16)


def _project(x, mod3, w_in_bf, rope_tabs, gm_ln_g, gm_ln_b, tile):
    b, s, d = x.shape
    in_w = w_in_bf.shape[1]
    gate_w = in_w - (ATTN_W + 2 * KV_W + 2 * GM_W)
    row = lambda width: pl.BlockSpec((None, tile, width), lambda i, t: (i, t, 0))
    tab = pl.BlockSpec((tile, LANES), lambda i, t: (t, 0))
    vec = lambda width: pl.BlockSpec((1, width), lambda i, t: (0, 0))
    return pl.pallas_call(
        _project_body,
        out_shape=[
            jax.ShapeDtypeStruct((b, s, ATTN_W), BF16),
            jax.ShapeDtypeStruct((b, s, 2 * KV_W), BF16),
            jax.ShapeDtypeStruct((b, s, GM_W), BF16),
            jax.ShapeDtypeStruct((b, s, GM_W), BF16),
            jax.ShapeDtypeStruct((b, s, gate_w), BF16),
        ],
        grid=(b, s // tile),
        in_specs=[
            row(d),
            pl.BlockSpec((None, 1, d), lambda i, t: (i, 0, 0)),
            pl.BlockSpec((None, 1, d), lambda i, t: (i, 0, 1)),
            pl.BlockSpec((d, in_w), lambda i, t: (0, 0)),
            tab, tab, tab,
            vec(GM_W), vec(GM_W),
        ],
        out_specs=[row(ATTN_W), row(2 * KV_W), row(GM_W), row(GM_W), row(gate_w)],
        compiler_params=_params("arbitrary", "arbitrary"),
        name="project",
    )(x, mod3, mod3, w_in_bf, *rope_tabs, gm_ln_g.reshape(1, GM_W), gm_ln_b.reshape(1, GM_W))


def _rope_tables(s):
    pos = jnp.arange(s, dtype=I32)
    inv = 1.0 / (ROPE_BASE ** (jnp.arange(ROPE_FREQS, dtype=F32) / ROPE_FREQS))
    ang_r = (pos // GRID_W).astype(F32)[:, None] * inv[None, :]
    ang_c = (pos % GRID_W).astype(F32)[:, None] * inv[None, :]
    cr, sr, cc, sn = jnp.cos(ang_r), jnp.sin(ang_r), jnp.cos(ang_c), jnp.sin(ang_c)
    z = jnp.zeros_like(sr)
    reps = LANES // HEAD_DIM
    cos = jnp.tile(jnp.concatenate([cr, cr, cc, cc], -1), (1, reps))
    s_up = jnp.tile(jnp.concatenate([-sr, z, -sn, z], -1), (1, reps))
    s_dn = jnp.tile(jnp.concatenate([z, sr, z, sn], -1), (1, reps))
    return cos, s_up, s_dn


def _attend_body(sink_ref, q_ref, kvp_ref, kvc_ref, kvn_ref, ctx_ref, o_ref):
    i = pl.program_id(1)
    n_steps = pl.num_programs(1)
    per = q_ref.shape[0] // BLOCK
    rows = GROUP * BLOCK
    r = _iota((rows, 3 * BLOCK), 0) & (BLOCK - 1)
    m = _iota((rows, 3 * BLOCK), 1)
    band = (m >= r) & (m <= r + 2 * BLOCK)
    grp = _iota((rows, 1), 0) // BLOCK
    kv_all = jnp.concatenate([kvp_ref[...], kvc_ref[...], kvn_ref[...]], axis=0)
    kv_ctx = ctx_ref[...]
    for t in range(per):
        mask = band
        if t == 0:
            mask = mask & ((m >= BLOCK) | (i > 0))
        if t == per - 1:
            mask = mask & ((m < 2 * BLOCK) | (i < n_steps - 1))
        q = q_ref[t * BLOCK:(t + 1) * BLOCK, :]
        kv_loc = kv_all[t * BLOCK:(t + 3) * BLOCK]
        outs = []
        for h in range(KV_HEADS):
            ks = slice(h * HEAD_DIM, (h + 1) * HEAD_DIM)
            vs = slice(KV_W + h * HEAD_DIM, KV_W + (h + 1) * HEAD_DIM)
            qg = jnp.concatenate(
                [q[:, (h * GROUP + g) * HEAD_DIM:(h * GROUP + g + 1) * HEAD_DIM] for g in range(GROUP)], axis=0)
            sink = jnp.zeros((rows, 1), F32)
            for g in range(GROUP):
                sink = jnp.where(grp == g, sink_ref[h * GROUP + g], sink)
            s_ctx = _dot_nt(qg, kv_ctx[:, ks])
            s_loc = jnp.where(mask, _dot_nt(qg, kv_loc[:, ks]), NEG_INF)
            top = jnp.maximum(jnp.maximum(jnp.max(s_ctx, -1, keepdims=True), jnp.max(s_loc, -1, keepdims=True)), sink)
            e_ctx = jnp.exp(s_ctx - top)
            e_loc = jnp.exp(s_loc - top)
            den = jnp.sum(e_ctx, -1, keepdims=True) + jnp.sum(e_loc, -1, keepdims=True) + jnp.exp(sink - top)
            o = (_dot(e_ctx.astype(BF16), kv_ctx[:, vs]) + _dot(e_loc.astype(BF16), kv_loc[:, vs])) * (1.0 / den)
            outs += [o[g * BLOCK:(g + 1) * BLOCK] for g in range(GROUP)]
        o_ref[t * BLOCK:(t + 1) * BLOCK, :] = jnp.concatenate(outs, axis=1).astype(BF16)


def _attend(q, kv, kv_ctx, sink, per):
    b, s, _ = q.shape
    nb = s // BLOCK
    c_len = kv_ctx.shape[1]
    kvw = 2 * KV_W
    return pl.pallas_call(
        _attend_body,
        out_shape=jax.ShapeDtypeStruct((b, s, ATTN_W), BF16),
        grid=(b, nb // per),
        in_specs=[
            pl.BlockSpec(memory_space=pltpu.SMEM),
            pl.BlockSpec((None, per * BLOCK, ATTN_W), lambda i, j: (i, j, 0)),
            pl.BlockSpec((None, BLOCK, kvw), lambda i, j: (i, jnp.maximum(j * per - 1, 0), 0)),
            pl.BlockSpec((None, per * BLOCK, kvw), lambda i, j: (i, j, 0)),
            pl.BlockSpec((None, BLOCK, kvw), lambda i, j: (i, jnp.minimum((j + 1) * per, nb - 1), 0)),
            pl.BlockSpec((None, c_len, kvw), lambda i, j: (i, 0, 0)),
        ],
        out_specs=pl.BlockSpec((None, per * BLOCK, ATTN_W), lambda i, j: (i, j, 0)),
        compiler_params=_params("arbitrary", "arbitrary"),
        name="attend",
    )(sink, q, kv, kv, kv, kv_ctx)


def _route(logits):
    lane = _iota(logits.shape, 1)
    far = jnp.int32(2 * ROUTE_W)

    def first_max(vals):
        top = jnp.max(vals, -1, keepdims=True)
        return top, jnp.min(jnp.where(vals == top, lane, far), -1, keepdims=True)

    is_g = lane < N_GROUPS
    g_top, g_idx = first_max(jnp.where(is_g, logits, NEG_INF))
    g_w = 1.0 / jnp.sum(jnp.where(is_g, jnp.exp(logits - g_top), 0.0), -1, keepdims=True)
    lo = N_GROUPS + EPG * g_idx
    e_vals = jnp.where((lane >= lo) & (lane < lo + EPG), logits, NEG_INF)
    v1, i1 = first_max(e_vals)
    v2, i2 = first_max(jnp.where(lane == i1, NEG_INF, e_vals))
    t = jnp.exp(v2 - v1)
    w1 = g_w * (1.0 / (1.0 + t))
    w2 = g_w * (t / (1.0 + t))
    swap = i2 < i1
    a = jnp.where(swap, i2, i1) - lo
    b = jnp.where(swap, i1, i2) - lo
    cls = g_idx * PAIRS + ((a * (2 * EPG - 1 - a)) >> 1) + (b - a - 1)
    return jnp.where(swap, w2, w1), jnp.where(swap, w1, w2), cls


def _merge_body(x_ref, ya_ref, u_ref, vg_ref, gate_ref, ws_ref, bs_ref, wpa_ref, wpb_ref, wo_ref,
                lng_ref, lnb_ref, g1_ref, sh2_ref, sc2_ref, wrh_ref, wrl_ref, br_ref, x1_ref, rec_ref, cnt_ref):
    tile, d = x_ref.shape
    lane = _iota((BLOCK, LANES), 1)
    below = (lane < _iota((BLOCK, LANES), 0)).astype(BF16)
    chunks = []
    for c in range(tile // BLOCK):
        rows = slice(c * BLOCK, (c + 1) * BLOCK)
        pieces = []
        for j in range(GM_W // LANES):
            v = vg_ref[rows, j * LANES:(j + 1) * LANES]
            per = LANES // GM_HEAD
            sp = _dot(ws_ref[per * j], v)
            for k in range(1, per):
                sp = jnp.where(lane < k * GM_HEAD, sp, _dot(ws_ref[per * j + k], v))
            pieces.append(sp)
        chunks.append(jnp.concatenate(pieces, axis=1) + bs_ref[...])
    sp = jnp.concatenate(chunks, axis=0)
    y_gm = (u_ref[...].astype(F32) * sp).astype(BF16)
    gate = gate_ref[...].astype(F32)
    y = gate[:, :d] * _dot(ya_ref[...], wpa_ref[...]) + gate[:, d:] * _dot(y_gm, wpb_ref[...])
    mix = _dot(y.astype(BF16), wo_ref[...])
    x1 = _ln(ALPHA * x_ref[...] + g1_ref[...] * mix) * lng_ref[...] + lnb_ref[...]
    x1_ref[...] = x1
    h2 = _ln(x1) * (1.0 + sc2_ref[...]) + sh2_ref[...]
    w_lo, w_hi, cls = _route(_dot_split(h2, wrh_ref[...], wrl_ref[...]) + br_ref[...])
    for c in range(tile // BLOCK):
        rows = slice(c * BLOCK, (c + 1) * BLOCK)
        onehot = lane == cls[rows]
        hot = onehot.astype(BF16)
        rank = jnp.sum(jnp.where(onehot, _dot(below, hot), 0.0), -1, keepdims=True)
        cnt_ref[c] = jnp.sum(hot.astype(F32), 0, keepdims=True)
        rec = jnp.where(lane == REC_WLO, w_lo[rows], 0.0)
        rec = jnp.where(lane == REC_WHI, w_hi[rows], rec)
        rec = jnp.where(lane == REC_CLS, cls[rows].astype(F32), rec)
        rec_ref[rows, :] = jnp.where(lane == REC_RANK, rank, rec)


def _merge(x, y_attn, u, vg, gates, ws_bf, bs_rows, wpa, wpb, wo, ln_g, ln_b, mod3, wr_hi, wr_lo, br, tile):
    b, s, d = x.shape
    nt = s // tile
    per = tile // BLOCK
    row = lambda width: pl.BlockSpec((None, tile, width), lambda i, j: (i, j, 0))
    full = lambda a: pl.BlockSpec(a.shape, lambda i, j: (0,) * a.ndim)
    modv = lambda k: pl.BlockSpec((None, 1, d), lambda i, j: (i, 0, k))
    ln_g = ln_g.reshape(1, d)
    ln_b = ln_b.reshape(1, d)
    return pl.pallas_call(
        _merge_body,
        out_shape=[jax.ShapeDtypeStruct((b, s, d), F32), jax.ShapeDtypeStruct((b, s, ROUTE_W), F32),
                   jax.ShapeDtypeStruct((b * s // BLOCK, 1, LANES), F32)],
        grid=(b, nt),
        in_specs=[
            row(d), row(ATTN_W), row(GM_W), row(GM_W), row(2 * d),
            full(ws_bf), full(bs_rows), full(wpa), full(wpb), full(wo), full(ln_g), full(ln_b),
            modv(2), modv(3), modv(4),
            full(wr_hi), full(wr_lo), full(br),
        ],
        out_specs=[row(d), row(ROUTE_W), pl.BlockSpec((per, 1, LANES), lambda i, j: (i * nt + j, 0, 0))],
        compiler_params=_params("arbitrary", "arbitrary"),
        name="merge",
    )(x, y_attn, u, vg, gates, ws_bf, bs_rows, wpa, wpb, wo, ln_g, ln_b, mod3, mod3, mod3, wr_hi, wr_lo, br)


def _plan_body(cnt_ref, rec_ref, lpos_ref, tab_ref, cblk_ref, run_ref, cntt_ref, lst_ref, base_ref, lstf_ref, *, chunks):
    step = pl.program_id(0)
    n_chunks = cnt_ref.shape[0]
    n_tiles = run_ref.shape[0]
    max_blocks = tab_ref.shape[0]

    @pl.when(step == 0)
    def _():
        cnt = cnt_ref[...]
        total = jnp.sum(cnt, 0, keepdims=True)
        n_blk = jnp.floor((total + (BLOCK - 1)) * (1.0 / BLOCK))
        before = (_iota((LANES, LANES), 0) < _iota((LANES, LANES), 1)).astype(BF16)
        nb_hi, nb_lo = _split(jnp.broadcast_to(n_blk, (SUBLANES, LANES)))
        blk_start = (_dot(nb_hi, before) + _dot(nb_lo, before))[0:1]
        lane1 = _iota((1, LANES), 1)
        packed = blk_start
        shift = jnp.zeros((1, 1), F32)
        for g in range(1, N_GROUPS):
            first = jnp.sum(jnp.where(lane1 == g * PAIRS, blk_start, 0.0), -1, keepdims=True) + shift
            over = first - MOE_PACK * jnp.floor(first * (1.0 / MOE_PACK))
            shift = shift + jnp.where(over > 0.0, MOE_PACK - over, 0.0)
            packed = jnp.where(lane1 >= g * PAIRS, blk_start + shift, packed)
        blk_start = packed
        earlier = (_iota((n_chunks, n_chunks), 1) < _iota((n_chunks, n_chunks), 0)).astype(BF16)
        base_ref[...] = blk_start * float(BLOCK) + _dot(earlier, cnt.astype(BF16))
        blk_end = blk_start + n_blk
        sub = _iota((SUBLANES, LANES), 0)
        cblk_ref[...] = jnp.where(sub == 0, blk_start, jnp.where(sub == 1, blk_end, 0.0)).astype(I32)
        lane = _iota((max_blocks, LANES), 1)
        blk = _iota((max_blocks, LANES), 0).astype(F32)
        is_cls = lane < N_CLASSES
        n_real = jnp.max(jnp.where(is_cls, blk_end, 0.0), -1, keepdims=True)
        u = jnp.minimum(blk, n_real - 1.0)
        cls_of = jnp.sum(jnp.where(is_cls & (blk_end <= u), 1.0, 0.0), -1, keepdims=True)
        n_steps = jnp.floor((n_real + (MOE_PACK - 1)) * (1.0 / MOE_PACK))
        tab = jnp.where(lane == 0, cls_of, jnp.where(lane == 1, n_steps, 0.0))
        tab_ref[...] = tab.astype(I32)
        tile = _iota((n_tiles, n_chunks), 0)
        chunk = _iota((n_tiles, n_chunks), 1)
        owner = ((chunk >= tile * chunks) & (chunk < (tile + 1) * chunks)).astype(BF16)
        cnt_tile = _dot(owner, cnt.astype(BF16))
        ct_hi, ct_lo = _split(cnt_tile)
        local_start = _dot(ct_hi, before) + _dot(ct_lo, before)
        run_ref[...] = base_ref[pl.ds(0, n_tiles, stride=chunks), :].astype(I32)
        cntt_ref[...] = cnt_tile.astype(I32)
        lst_ref[...] = local_start.astype(I32)
        lstf_ref[...] = local_start

    lane = _iota((BLOCK, LANES), 1)
    tile_off = lstf_ref[pl.ds(step, 1), :] - base_ref[pl.ds(step * chunks, 1), :]
    for c in range(chunks):
        rec = rec_ref[c * BLOCK:(c + 1) * BLOCK, :]
        cls = rec[:, REC_CLS:REC_CLS + 1].astype(I32)
        base = base_ref[pl.ds(step * chunks + c, 1), :] + tile_off
        lpos = jnp.sum(jnp.where(lane == cls, base, 0.0), -1, keepdims=True) + rec[:, REC_RANK:REC_RANK + 1]
        lpos_ref[c] = jnp.transpose(jnp.broadcast_to(lpos, (BLOCK, LANES)))[0:1].astype(I32)


def _plan(counts, rec, max_blocks, chunks):
    n = rec.shape[0]
    n_chunks = n // BLOCK
    n_tiles = n_chunks // chunks
    tile_tab = jax.ShapeDtypeStruct((n_tiles, LANES), I32)
    whole = lambda rows: pl.BlockSpec((rows, LANES), lambda i: (0, 0))
    return pl.pallas_call(
        functools.partial(_plan_body, chunks=chunks),
        out_shape=[jax.ShapeDtypeStruct((n_chunks, 1, LANES), I32), jax.ShapeDtypeStruct((max_blocks, LANES), I32),
                   jax.ShapeDtypeStruct((SUBLANES, LANES), I32), tile_tab, tile_tab, tile_tab],
        grid=(n_tiles,),
        in_specs=[whole(n_chunks), pl.BlockSpec((chunks * BLOCK, ROUTE_W), lambda i: (i, 0))],
        out_specs=[pl.BlockSpec((chunks, 1, LANES), lambda i: (i, 0, 0)), whole(max_blocks), whole(SUBLANES),
                   whole(n_tiles), whole(n_tiles), whole(n_tiles)],
        scratch_shapes=[pltpu.VMEM((n_chunks, LANES), F32), pltpu.VMEM((n_tiles, LANES), F32)],
        compiler_params=_params("arbitrary"),
        name="plan",
    )(counts, rec)


def _tile_rows(base, j, pitch):
    return pl.ds(base + j, BLOCK, stride=pitch)


def _load_tiles(ref, base, pitch, width):
    return jnp.concatenate([ref[_tile_rows(base, j, pitch), :] for j in range(width // LANES)], axis=1)


def _store_tiles(ref, base, pitch, val):
    for j in range(val.shape[1] // LANES):
        ref[_tile_rows(base, j, pitch), :] = val[:, j * LANES:(j + 1) * LANES]


def _start_runs(cnt_ref, lst_ref, run_ref, buf, buf_base, hbm, sem, to_hbm):
    for c in range(N_CLASSES):
        n = cnt_ref[0, c]

        @pl.when(n > 0)
        def _():
            rows = n * SUBLANES
            v = buf.at[pl.ds(pl.multiple_of(buf_base + lst_ref[0, c] * SUBLANES, SUBLANES), rows)]
            h = hbm.at[pl.ds(pl.multiple_of(run_ref[0, c] * SUBLANES, SUBLANES), rows)]
            (pltpu.make_async_copy(v, h, sem) if to_hbm else pltpu.make_async_copy(h, v, sem)).start(priority=c % 2)


def _wait_runs(buf, buf_base, rows, hbm, sem):
    pltpu.make_async_copy(hbm.at[pl.ds(0, rows)], buf.at[pl.ds(buf_base, rows)], sem).wait()


def _scatter_body(cls_blk_ref, run_ref, cnt_ref, lst_ref, lpos_ref, x1_ref, sh_ref, sc_ref, xs_hbm,
                  abuf, bbuf, zbuf, ssem, zsem):
    u = pl.program_id(0)
    slot = u % 2
    chunks = x1_ref.shape[0] // BLOCK
    tile_rows = BLOCK * SUBLANES
    sort_rows = chunks * tile_rows

    @pl.when(u == 0)
    def _():
        zbuf[...] = jnp.zeros_like(zbuf)

        def zero_copy(c):
            start = pl.multiple_of((cls_blk_ref[1, c] - 1) * tile_rows, tile_rows)
            return pltpu.make_async_copy(zbuf, xs_hbm.at[pl.ds(start, tile_rows)], zsem)

        for c in range(N_CLASSES):
            @pl.when(cls_blk_ref[1, c] > cls_blk_ref[0, c])
            def _():
                zero_copy(c).start()
        for c in range(N_CLASSES):
            @pl.when(cls_blk_ref[1, c] > cls_blk_ref[0, c])
            def _():
                zero_copy(c).wait()

        def gap_copy(blk):
            return pltpu.make_async_copy(zbuf, xs_hbm.at[pl.ds(pl.multiple_of(blk * tile_rows, tile_rows), tile_rows)],
                                         zsem)

        gaps = [(cls_blk_ref[1, g * PAIRS - 1], cls_blk_ref[0, g * PAIRS]) for g in range(1, N_GROUPS)]
        gaps.append((cls_blk_ref[1, N_CLASSES - 1], xs_hbm.shape[0] // tile_rows))
        for lo, hi in gaps:
            lax.fori_loop(lo, hi, lambda blk, carry: (gap_copy(blk).start(), carry)[1], 0)
        for lo, hi in gaps:
            lax.fori_loop(lo, hi, lambda blk, carry: (gap_copy(blk).wait(), carry)[1], 0)

    @pl.when(u >= 2)
    def _():
        _wait_runs(bbuf, slot * sort_rows, sort_rows, xs_hbm, ssem.at[slot])

    scale = 1.0 + sc_ref[...]
    shift = sh_ref[...]

    def sort_chunk(c, carry):
        rows = pl.ds(pl.multiple_of(c * BLOCK, BLOCK), BLOCK)
        _store_tiles(abuf, 0, SUBLANES, _ln(x1_ref[rows, :]) * scale + shift)
        for r in range(BLOCK):
            dst = pl.multiple_of(slot * sort_rows + lpos_ref[c, 0, r] * SUBLANES, SUBLANES)
            bbuf[pl.ds(dst, SUBLANES), :] = abuf[r * SUBLANES:(r + 1) * SUBLANES, :]
        return carry

    lax.fori_loop(0, chunks, sort_chunk, 0)
    _start_runs(cnt_ref, lst_ref, run_ref, bbuf, slot * sort_rows, xs_hbm, ssem.at[slot], True)

    @pl.when(u == pl.num_programs(0) - 1)
    def _():
        _wait_runs(bbuf, slot * sort_rows, sort_rows, xs_hbm, ssem.at[slot])

        @pl.when(u >= 1)
        def _():
            _wait_runs(bbuf, (1 - slot) * sort_rows, sort_rows, xs_hbm, ssem.at[1 - slot])


def _tile_tab_spec(off, n_tiles):
    return pl.BlockSpec((None, 1, LANES), lambda u: (jnp.minimum(u + off, n_tiles - 1), 0, 0),
                        memory_space=pltpu.SMEM)


def _scatter(x1, mod3, lpos, tile_tabs, cls_blk, max_blocks, seq, chunks):
    n, d = x1.shape
    n_tiles = n // (chunks * BLOCK)
    per_batch = seq // (chunks * BLOCK)
    tile_rows = BLOCK * SUBLANES
    return pl.pallas_call(
        _scatter_body,
        out_shape=jax.ShapeDtypeStruct((max_blocks * tile_rows, LANES), F32),
        grid=(n_tiles,),
        in_specs=[
            pl.BlockSpec(memory_space=pltpu.SMEM),
            _tile_tab_spec(0, n_tiles), _tile_tab_spec(0, n_tiles), _tile_tab_spec(0, n_tiles),
            pl.BlockSpec((chunks, 1, LANES), lambda u: (u, 0, 0), memory_space=pltpu.SMEM),
            pl.BlockSpec((chunks * BLOCK, d), lambda u: (u, 0)),
            pl.BlockSpec((None, 1, d), lambda u: (u // per_batch, 0, 3)),
            pl.BlockSpec((None, 1, d), lambda u: (u // per_batch, 0, 4)),
        ],
        out_specs=pl.BlockSpec(memory_space=pl.ANY),
        scratch_shapes=[
            pltpu.VMEM((tile_rows, LANES), F32),
            pltpu.VMEM((2 * chunks * tile_rows, LANES), F32),
            pltpu.VMEM((tile_rows, LANES), F32),
            pltpu.SemaphoreType.DMA((2,)),
            pltpu.SemaphoreType.DMA,
        ],
        compiler_params=_params("arbitrary"),
        name="scatter",
    )(cls_blk, *tile_tabs, lpos, x1, mod3, mod3)


def _expert(h, w1, w3, w2):
    hid = jax.nn.silu(_dot(h, w1)) * _dot(h, w3)
    return _dot(hid, w2)


def _moe_body(sgrp_ref, blo_ref, bhi_ref, nsteps_ref, xs_ref, w1_ref, w3_ref, w2_ref, olo_ref, ohi_ref):
    u = pl.program_id(0)
    d = w1_ref.shape[1]
    tile_rows = BLOCK * SUBLANES

    @pl.when(u < nsteps_ref[0])
    def _():
        for k in range(MOE_PACK):
            h = _load_tiles(xs_ref, k * tile_rows, SUBLANES, d)
            for o_ref, e in ((olo_ref, blo_ref[MOE_PACK * u + k]), (ohi_ref, bhi_ref[MOE_PACK * u + k])):
                _store_tiles(o_ref, k * tile_rows, SUBLANES, _expert(h, w1_ref[e], w3_ref[e], w2_ref[e]))

    @pl.when(u >= nsteps_ref[0])
    def _():
        olo_ref[...] = jnp.zeros_like(olo_ref)
        ohi_ref[...] = jnp.zeros_like(ohi_ref)


def _moe(xs, tables, w1, w3, w2):
    sgrp, blo, bhi, n_steps = tables
    max_steps = sgrp.shape[0]
    d, de = w1.shape[1:]
    in_rows = MOE_PACK * BLOCK * SUBLANES
    w1g = w1.reshape(N_GROUPS, EPG, d, de)
    w3g = w3.reshape(N_GROUPS, EPG, d, de)
    w2g = w2.reshape(N_GROUPS, EPG, de, d)

    def w_spec(a):
        return pl.BlockSpec((None,) + a.shape[1:], lambda u, g, lo, hi, n_steps: (g[u], 0, 0, 0),
                            pipeline_mode=pl.Buffered(1))

    grid_spec = pltpu.PrefetchScalarGridSpec(
        num_scalar_prefetch=4,
        grid=(max_steps,),
        in_specs=[
            pl.BlockSpec((in_rows, LANES), lambda u, g, lo, hi, n_steps: (jnp.minimum(u, n_steps[0] - 1), 0)),
            w_spec(w1g), w_spec(w3g), w_spec(w2g),
        ],
        out_specs=[pl.BlockSpec((in_rows, LANES), lambda u, *_: (u, 0))] * 2,
    )
    return pl.pallas_call(
        _moe_body,
        out_shape=[jax.ShapeDtypeStruct((max_steps * in_rows, LANES), F32)] * 2,
        grid_spec=grid_spec,
        compiler_params=pltpu.CompilerParams(dimension_semantics=("arbitrary",), vmem_limit_bytes=MOE_VMEM_LIMIT_BYTES),
        name="moe",
    )(sgrp, blo, bhi, n_steps, xs, w1g, w3g, w2g)


def _final_body(run_ref, cnt_ref, lst_ref, runn_ref, cntn_ref, lstn_ref, lpos_ref, x1_ref, rec_ref, g2_ref,
                lng_ref, lnb_ref, mlo_hbm, mhi_hbm, o_ref, glo, ghi, tlo, thi, gsem):
    u = pl.program_id(0)
    slot = u % 2
    d = x1_ref.shape[-1]
    chunks = x1_ref.shape[0] // BLOCK
    sort_rows = chunks * BLOCK * SUBLANES
    sides = ((mlo_hbm, glo), (mhi_hbm, ghi))

    def start_runs(tabs, s):
        for hbm, buf in sides:
            _start_runs(tabs[1], tabs[2], tabs[0], buf, s * sort_rows, hbm, gsem.at[s], False)

    def wait_runs(s):
        for hbm, buf in sides:
            _wait_runs(buf, s * sort_rows, sort_rows, hbm, gsem.at[s])

    @pl.when(u == 0)
    def _():
        start_runs((run_ref, cnt_ref, lst_ref), 0)

    wait_runs(slot)
    start_runs((runn_ref, cntn_ref, lstn_ref), 1 - slot)
    gain = g2_ref[...]
    ln_g = lng_ref[...]
    ln_b = lnb_ref[...]

    def finish_chunk(c, carry):
        for r in range(BLOCK):
            src = pl.ds(pl.multiple_of(slot * sort_rows + lpos_ref[c, 0, r] * SUBLANES, SUBLANES), SUBLANES)
            tlo[r * SUBLANES:(r + 1) * SUBLANES, :] = glo[src, :]
            thi[r * SUBLANES:(r + 1) * SUBLANES, :] = ghi[src, :]
        rows = pl.ds(pl.multiple_of(c * BLOCK, BLOCK), BLOCK)
        rec = rec_ref[rows, :]
        moe = (rec[:, REC_WLO:REC_WLO + 1] * _load_tiles(tlo, 0, SUBLANES, d)
               + rec[:, REC_WHI:REC_WHI + 1] * _load_tiles(thi, 0, SUBLANES, d))
        o_ref[rows, :] = _ln(ALPHA * x1_ref[rows, :] + gain * moe) * ln_g + ln_b
        return carry

    lax.fori_loop(0, chunks, finish_chunk, 0)

    @pl.when(u == pl.num_programs(0) - 1)
    def _():
        wait_runs(1 - slot)


def _final(x1, rec, ms_lo, ms_hi, lpos, tile_tabs, mod3, ln_g, ln_b, seq, chunks):
    n, d = x1.shape
    n_tiles = n // (chunks * BLOCK)
    per_batch = seq // (chunks * BLOCK)
    tile_rows = BLOCK * SUBLANES
    vec = pl.BlockSpec((1, d), lambda u: (0, 0))
    hbm = pl.BlockSpec(memory_space=pl.ANY)
    return pl.pallas_call(
        _final_body,
        out_shape=jax.ShapeDtypeStruct((n, d), F32),
        grid=(n_tiles,),
        in_specs=[
            _tile_tab_spec(0, n_tiles), _tile_tab_spec(0, n_tiles), _tile_tab_spec(0, n_tiles),
            _tile_tab_spec(1, n_tiles), _tile_tab_spec(1, n_tiles), _tile_tab_spec(1, n_tiles),
            pl.BlockSpec((chunks, 1, LANES), lambda u: (u, 0, 0), memory_space=pltpu.SMEM),
            pl.BlockSpec((chunks * BLOCK, d), lambda u: (u, 0)),
            pl.BlockSpec((chunks * BLOCK, ROUTE_W), lambda u: (u, 0)),
            pl.BlockSpec((None, 1, d), lambda u: (u // per_batch, 0, 5)),
            vec, vec, hbm, hbm,
        ],
        out_specs=pl.BlockSpec((chunks * BLOCK, d), lambda u: (u, 0)),
        scratch_shapes=[
            pltpu.VMEM((2 * chunks * tile_rows, LANES), F32),
            pltpu.VMEM((2 * chunks * tile_rows, LANES), F32),
            pltpu.VMEM((tile_rows, LANES), F32),
            pltpu.VMEM((tile_rows, LANES), F32),
            pltpu.SemaphoreType.DMA((2,)),
        ],
        compiler_params=_params("arbitrary"),
        name="final",
    )(*tile_tabs, *tile_tabs, lpos, x1, rec, mod3, ln_g.reshape(1, d), ln_b.reshape(1, d), ms_lo, ms_hi)


def _block_tables(tab):
    cls = tab[:, 0]
    pair = cls % PAIRS
    first = jnp.array([i for i in range(EPG) for _ in range(i + 1, EPG)], I32)
    second = jnp.array([j for i in range(EPG) for j in range(i + 1, EPG)], I32)
    return (cls // PAIRS)[::MOE_PACK], first[pair], second[pair], tab[0, 1].reshape(1)


def kernel(x, c, ctx, c_ctx, w_ada, b_ada, w_in, attn_sink, gm_ln_g, gm_ln_b, gm_ws, gm_bs, w_pa, w_pb, w_o,
           ln1_g, ln1_b, router_g_w, router_g_b, router_e_w, router_e_b, moe_w1, moe_w3, moe_w2, ln2_g, ln2_b):
    b, s, d = x.shape
    assert w_ada.shape[0] == DEPTH and s % BLOCK == 0 and s % GRID_W == 0
    mod_rows = 2 * SUBLANES
    assert b + 1 <= mod_rows
    c_rows = jnp.zeros((mod_rows, d), F32).at[:b].set(c).at[b].set(c_ctx)
    mod = _adaln_mod(c_rows, w_ada[0], b_ada[0])
    mod3 = mod.reshape(mod_rows, 1, 6 * d)
    w_in_bf = w_in[0].astype(BF16)
    kv_ctx = _ctx_kv(ctx, mod3, b, w_in_bf)
    tile = 512 if s % 512 == 0 else BLOCK
    q, kv, u, vg, gates = _project(x, mod3, w_in_bf, _rope_tables(s), gm_ln_g[0], gm_ln_b[0], tile)
    nb = s // BLOCK
    y_attn = _attend(q, kv, kv_ctx, attn_sink[0], next(p for p in (4, 2, 1) if nb % p == 0))
    bs_rows = jnp.repeat(gm_bs[0].T, GM_HEAD, axis=1)
    w_route = jnp.concatenate([router_g_w[0], jnp.transpose(router_e_w[0], (1, 0, 2)).reshape(d, N_EXPERTS)], axis=1)
    w_route = jnp.pad(w_route, ((0, 0), (0, ROUTE_W - w_route.shape[1])))
    b_route = jnp.pad(jnp.concatenate([router_g_b[0], router_e_b[0].reshape(-1)]), (0, ROUTE_W - N_GROUPS - N_EXPERTS))
    wr_hi = w_route.astype(BF16)
    wr_lo = (w_route - wr_hi.astype(F32)).astype(BF16)
    merge_tile = BLOCK * next(p for p in (4, 2, 1) if nb % p == 0)
    x1, rec, counts = _merge(x, y_attn, u, vg, gates, gm_ws[0].astype(BF16), bs_rows, w_pa[0].astype(BF16),
                             w_pb[0].astype(BF16), w_o[0].astype(BF16), ln1_g[0], ln1_b[0], mod3, wr_hi, wr_lo,
                             b_route.reshape(1, ROUTE_W), merge_tile)
    n = b * s
    x1 = x1.reshape(n, d)
    rec = rec.reshape(n, ROUTE_W)
    max_blocks = n // BLOCK + N_CLASSES - 1 + (N_GROUPS - 1) * (MOE_PACK - 1)
    max_blocks = -(-max_blocks // MOE_PACK) * MOE_PACK
    sort_chunks = next(p for p in (SUBLANES, 4, 2, 1) if nb % p == 0)
    lpos, tab, cls_blk, *tile_tabs = _plan(counts.reshape(n // BLOCK, LANES), rec, max_blocks, sort_chunks)
    tile_tabs = [t.reshape(t.shape[0], 1, LANES) for t in tile_tabs]
    xs = _scatter(x1, mod3, lpos, tile_tabs, cls_blk, max_blocks, s, sort_chunks)
    ms_lo, ms_hi = _moe(xs, _block_tables(tab), moe_w1[0], moe_w3[0], moe_w2[0])
    out = _final(x1, rec, ms_lo, ms_hi, lpos, tile_tabs, mod3, ln2_g[0], ln2_b[0], s, sort_chunks)
    return out.reshape(b, s, d)
```

```python
import functools

import jax
import jax.numpy as jnp
from jax import lax
from jax.experimental import pallas as pl
from jax.experimental.pallas import tpu as pltpu

F32 = jnp.float32
BF16 = jnp.bfloat16
I32 = jnp.int32

GRID_W = 64
HEAD_DIM = 64
Q_HEADS = 8
KV_HEADS = 2
GROUP = Q_HEADS // KV_HEADS
BLOCK = 128
ROPE_BASE = 10000.0
ROPE_FREQS = HEAD_DIM // 4
GM_GROUPS = 8
GM_HEAD = 64
ATTN_W = Q_HEADS * HEAD_DIM
KV_W = KV_HEADS * HEAD_DIM
GM_W = GM_GROUPS * GM_HEAD
N_GROUPS = 4
EPG = 8
N_EXPERTS = N_GROUPS * EPG
PAIRS = EPG * (EPG - 1) // 2
N_CLASSES = N_GROUPS * PAIRS
LN_EPS = 1e-6
NEG_INF = -1e30
DEPTH = 1
ALPHA = (2.0 * DEPTH) ** 0.25

LANES = 128
SUBLANES = 8
VMEM_LIMIT_BYTES = 56 * 1024 * 1024
MOE_VMEM_LIMIT_BYTES = 60 * 1024 * 1024
MOE_PACK = 2
ROUTE_W = LANES
REC_WLO, REC_WHI, REC_CLS, REC_RANK = 0, 1, 2, 3


def _params(*sem):
    return pltpu.CompilerParams(dimension_semantics=sem, vmem_limit_bytes=VMEM_LIMIT_BYTES)


def _ln(x):
    mu = jnp.mean(x, -1, keepdims=True)
    xc = x - mu
    var = jnp.mean(xc * xc, -1, keepdims=True)
    return xc * lax.rsqrt(var + LN_EPS)


def _dot(a, b):
    return jnp.dot(a, b, preferred_element_type=F32)


def _dot_nt(a, b):
    return lax.dot_general(a, b, (((1,), (1,)), ((), ())), preferred_element_type=F32)


def _split(a):
    hi = a.astype(BF16)
    lo = (a - hi.astype(F32)).astype(BF16)
    return hi, lo


def _dot_split(a, b_hi, b_lo):
    a_hi, a_lo = _split(a)
    return _dot(a_hi, b_hi) + _dot(a_hi, b_lo) + _dot(a_lo, b_hi)


def _iota(shape, axis):
    return lax.broadcasted_iota(I32, shape, axis)


def _mod_body(c_ref, w_ref, b_ref, o_ref):
    w_hi, w_lo = _split(w_ref[...])
    o_ref[...] = _dot_split(jax.nn.silu(c_ref[...]), w_hi, w_lo) + b_ref[...]


def _adaln_mod(c_rows, w_ada, b_ada):
    rows, d = c_rows.shape
    width = w_ada.shape[1]
    tn = 6 * LANES
    return pl.pallas_call(
        _mod_body,
        out_shape=jax.ShapeDtypeStruct((rows, width), F32),
        grid=(width // tn,),
        in_specs=[
            pl.BlockSpec((rows, d), lambda j: (0, 0)),
            pl.BlockSpec((d, tn), lambda j: (0, j)),
            pl.BlockSpec((1, tn), lambda j: (0, j)),
        ],
        out_specs=pl.BlockSpec((rows, tn), lambda j: (0, j)),
        compiler_params=_params("arbitrary"),
        name="mod",
    )(c_rows, w_ada, b_ada.reshape(1, width))


def _ctx_body(ctx_ref, sh_ref, sc_ref, w_ref, o_ref):
    h = _ln(ctx_ref[...]) * (1.0 + sc_ref[...]) + sh_ref[...]
    o_ref[...] = _dot(h.astype(BF16), w_ref[...]).astype(BF16)


def _ctx_kv(ctx, mod3, ctx_row, w_in_bf):
    b, c_len, d = ctx.shape
    kvw = 2 * KV_W
    return pl.pallas_call(
        _ctx_body,
        out_shape=jax.ShapeDtypeStruct((b, c_len, kvw), BF16),
        grid=(b,),
        in_specs=[
            pl.BlockSpec((None, c_len, d), lambda i: (i, 0, 0)),
            pl.BlockSpec((None, 1, d), lambda i: (ctx_row, 0, 0)),
            pl.BlockSpec((None, 1, d), lambda i: (ctx_row, 0, 1)),
            pl.BlockSpec((d, kvw), lambda i: (0, ATTN_W // kvw)),
        ],
        out_specs=pl.BlockSpec((None, c_len, kvw), lambda i: (i, 0, 0)),
        compiler_params=_params("arbitrary"),
        name="ctx_kv",
    )(ctx, mod3, mod3, w_in_bf)


def _rope(x, cos, s_up, s_dn):
    return x * cos + pltpu.roll(x, LANES - ROPE_FREQS, 1) * s_up + pltpu.roll(x, ROPE_FREQS, 1) * s_dn


def _project_body(x_ref, sh_ref, sc_ref, w_ref, cos_ref, sup_ref, sdn_ref, lng_ref, lnb_ref,
                  q_ref, kv_ref, u_ref, vg_ref, gate_ref):
    h = (_ln(x_ref[...]) * (1.0 + sc_ref[...]) + sh_ref[...]).astype(BF16)
    cos, s_up, s_dn = cos_ref[...], sup_ref[...], sdn_ref[...]
    q_scale = HEAD_DIM ** -0.5
    k0 = ATTN_W
    u0 = k0 + 2 * KV_W
    vg0 = u0 + GM_W
    gate0 = vg0 + GM_W
    q = _dot(h, w_ref[:, 0:k0])
    for j in range(ATTN_W // LANES):
        sl = slice(j * LANES, (j + 1) * LANES)
        q_ref[:, sl] = (_rope(q[:, sl], cos, s_up, s_dn) * q_scale).astype(BF16)
    kv = _dot(h, w_ref[:, k0:u0])
    kv_ref[:, 0:KV_W] = _rope(kv[:, 0:KV_W], cos, s_up, s_dn).astype(BF16)
    kv_ref[:, KV_W:] = kv[:, KV_W:].astype(BF16)
    u_ref[...] = jax.nn.gelu(_dot(h, w_ref[:, u0:vg0])).astype(BF16)
    vg = _ln(jax.nn.gelu(_dot(h, w_ref[:, vg0:gate0]))) * lng_ref[...] + lnb_ref[...]
    vg_ref[...] = vg.astype(BF16)
    gate_ref[...] = jax.nn.sigmoid(_dot(h, w_ref[:, gate0:])).astype(BF16)


def _project(x, mod3, w_in_bf, rope_tabs, gm_ln_g, gm_ln_b, tile):
    b, s, d = x.shape
    in_w = w_in_bf.shape[1]
    gate_w = in_w - (ATTN_W + 2 * KV_W + 2 * GM_W)
    row = lambda width: pl.BlockSpec((None, tile, width), lambda i, t: (i, t, 0))
    tab = pl.BlockSpec((tile, LANES), lambda i, t: (t, 0))
    vec = lambda width: pl.BlockSpec((1, width), lambda i, t: (0, 0))
    return pl.pallas_call(
        _project_body,
        out_shape=[
            jax.ShapeDtypeStruct((b, s, ATTN_W), BF16),
            jax.ShapeDtypeStruct((b, s, 2 * KV_W), BF16),
            jax.ShapeDtypeStruct((b, s, GM_W), BF16),
            jax.ShapeDtypeStruct((b, s, GM_W), BF16),
            jax.ShapeDtypeStruct((b, s, gate_w), BF16),
        ],
        grid=(b, s // tile),
        in_specs=[
            row(d),
            pl.BlockSpec((None, 1, d), lambda i, t: (i, 0, 0)),
            pl.BlockSpec((None, 1, d), lambda i, t: (i, 0, 1)),
            pl.BlockSpec((d, in_w), lambda i, t: (0, 0)),
            tab, tab, tab,
            vec(GM_W), vec(GM_W),
        ],
        out_specs=[row(ATTN_W), row(2 * KV_W), row(GM_W), row(GM_W), row(gate_w)],
        compiler_params=_params("arbitrary", "arbitrary"),
        name="project",
    )(x, mod3, mod3, w_in_bf, *rope_tabs, gm_ln_g.reshape(1, GM_W), gm_ln_b.reshape(1, GM_W))


def _rope_tables(s):
    pos = jnp.arange(s, dtype=I32)
    inv = 1.0 / (ROPE_BASE ** (jnp.arange(ROPE_FREQS, dtype=F32) / ROPE_FREQS))
    ang_r = (pos // GRID_W).astype(F32)[:, None] * inv[None, :]
    ang_c = (pos % GRID_W).astype(F32)[:, None] * inv[None, :]
    cr, sr, cc, sn = jnp.cos(ang_r), jnp.sin(ang_r), jnp.cos(ang_c), jnp.sin(ang_c)
    z = jnp.zeros_like(sr)
    reps = LANES // HEAD_DIM
    cos = jnp.tile(jnp.concatenate([cr, cr, cc, cc], -1), (1, reps))
    s_up = jnp.tile(jnp.concatenate([-sr, z, -sn, z], -1), (1, reps))
    s_dn = jnp.tile(jnp.concatenate([z, sr, z, sn], -1), (1, reps))
    return cos, s_up, s_dn


def _attend_body(sink_ref, q_ref, kvp_ref, kvc_ref, kvn_ref, ctx_ref, o_ref):
    i = pl.program_id(1)
    n_steps = pl.num_programs(1)
    per = q_ref.shape[0] // BLOCK
    rows = GROUP * BLOCK
    r = _iota((rows, 3 * BLOCK), 0) & (BLOCK - 1)
    m = _iota((rows, 3 * BLOCK), 1)
    band = (m >= r) & (m <= r + 2 * BLOCK)
    grp = _iota((rows, 1), 0) // BLOCK
    kv_all = jnp.concatenate([kvp_ref[...], kvc_ref[...], kvn_ref[...]], axis=0)
    kv_ctx = ctx_ref[...]
    for t in range(per):
        mask = band
        if t == 0:
            mask = mask & ((m >= BLOCK) | (i > 0))
        if t == per - 1:
            mask = mask & ((m < 2 * BLOCK) | (i < n_steps - 1))
        q = q_ref[t * BLOCK:(t + 1) * BLOCK, :]
        kv_loc = kv_all[t * BLOCK:(t + 3) * BLOCK]
        outs = []
        for h in range(KV_HEADS):
            ks = slice(h * HEAD_DIM, (h + 1) * HEAD_DIM)
            vs = slice(KV_W + h * HEAD_DIM, KV_W + (h + 1) * HEAD_DIM)
            qg = jnp.concatenate(
                [q[:, (h * GROUP + g) * HEAD_DIM:(h * GROUP + g + 1) * HEAD_DIM] for g in range(GROUP)], axis=0)
            sink = jnp.zeros((rows, 1), F32)
            for g in range(GROUP):
                sink = jnp.where(grp == g, sink_ref[h * GROUP + g], sink)
            s_ctx = _dot_nt(qg, kv_ctx[:, ks])
            s_loc = jnp.where(mask, _dot_nt(qg, kv_loc[:, ks]), NEG_INF)
            top = jnp.maximum(jnp.maximum(jnp.max(s_ctx, -1, keepdims=True), jnp.max(s_loc, -1, keepdims=True)), sink)
            e_ctx = jnp.exp(s_ctx - top)
            e_loc = jnp.exp(s_loc - top)
            den = jnp.sum(e_ctx, -1, keepdims=True) + jnp.sum(e_loc, -1, keepdims=True) + jnp.exp(sink - top)
            o = (_dot(e_ctx.astype(BF16), kv_ctx[:, vs]) + _dot(e_loc.astype(BF16), kv_loc[:, vs])) * (1.0 / den)
            outs += [o[g * BLOCK:(g + 1) * BLOCK] for g in range(GROUP)]
        o_ref[t * BLOCK:(t + 1) * BLOCK, :] = jnp.concatenate(outs, axis=1).astype(BF16)


def _attend(q, kv, kv_ctx, sink, per):
    b, s, _ = q.shape
    nb = s // BLOCK
    c_len = kv_ctx.shape[1]
    kvw = 2 * KV_W
    return pl.pallas_call(
        _attend_body,
        out_shape=jax.ShapeDtypeStruct((b, s, ATTN_W), BF16),
        grid=(b, nb // per),
        in_specs=[
            pl.BlockSpec(memory_space=pltpu.SMEM),
            pl.BlockSpec((None, per * BLOCK, ATTN_W), lambda i, j: (i, j, 0)),
            pl.BlockSpec((None, BLOCK, kvw), lambda i, j: (i, jnp.maximum(j * per - 1, 0), 0)),
            pl.BlockSpec((None, per * BLOCK, kvw), lambda i, j: (i, j, 0)),
            pl.BlockSpec((None, BLOCK, kvw), lambda i, j: (i, jnp.minimum((j + 1) * per, nb - 1), 0)),
            pl.BlockSpec((None, c_len, kvw), lambda i, j: (i, 0, 0)),
        ],
        out_specs=pl.BlockSpec((None, per * BLOCK, ATTN_W), lambda i, j: (i, j, 0)),
        compiler_params=_params("arbitrary", "arbitrary"),
        name="attend",
    )(sink, q, kv, kv, kv, kv_ctx)


def _route(logits):
    lane = _iota(logits.shape, 1)
    far = jnp.int32(2 * ROUTE_W)

    def first_max(vals):
        top = jnp.max(vals, -1, keepdims=True)
        return top, jnp.min(jnp.where(vals == top, lane, far), -1, keepdims=True)

    is_g = lane < N_GROUPS
    g_top, g_idx = first_max(jnp.where(is_g, logits, NEG_INF))
    g_w = 1.0 / jnp.sum(jnp.where(is_g, jnp.exp(logits - g_top), 0.0), -1, keepdims=True)
    lo = N_GROUPS + EPG * g_idx
    e_vals = jnp.where((lane >= lo) & (lane < lo + EPG), logits, NEG_INF)
    v1, i1 = first_max(e_vals)
    v2, i2 = first_max(jnp.where(lane == i1, NEG_INF, e_vals))
    t = jnp.exp(v2 - v1)
    w1 = g_w * (1.0 / (1.0 + t))
    w2 = g_w * (t / (1.0 + t))
    swap = i2 < i1
    a = jnp.where(swap, i2, i1) - lo
    b = jnp.where(swap, i1, i2) - lo
    cls = g_idx * PAIRS + ((a * (2 * EPG - 1 - a)) >> 1) + (b - a - 1)
    return jnp.where(swap, w2, w1), jnp.where(swap, w1, w2), cls


def _merge_body(x_ref, ya_ref, u_ref, vg_ref, gate_ref, ws_ref, bs_ref, wpa_ref, wpb_ref, wo_ref,
                lng_ref, lnb_ref, g1_ref, sh2_ref, sc2_ref, wrh_ref, wrl_ref, br_ref, x1_ref, rec_ref, cnt_ref):
    tile, d = x_ref.shape
    lane = _iota((BLOCK, LANES), 1)
    below = (lane < _iota((BLOCK, LANES), 0)).astype(BF16)
    chunks = []
    for c in range(tile // BLOCK):
        rows = slice(c * BLOCK, (c + 1) * BLOCK)
        pieces = []
        for j in range(GM_W // LANES):
            v = vg_ref[rows, j * LANES:(j + 1) * LANES]
            per = LANES // GM_HEAD
            sp = _dot(ws_ref[per * j], v)
            for k in range(1, per):
                sp = jnp.where(lane < k * GM_HEAD, sp, _dot(ws_ref[per * j + k], v))
            pieces.append(sp)
        chunks.append(jnp.concatenate(pieces, axis=1) + bs_ref[...])
    sp = jnp.concatenate(chunks, axis=0)
    y_gm = (u_ref[...].astype(F32) * sp).astype(BF16)
    gate = gate_ref[...].astype(F32)
    y = gate[:, :d] * _dot(ya_ref[...], wpa_ref[...]) + gate[:, d:] * _dot(y_gm, wpb_ref[...])
    mix = _dot(y.astype(BF16), wo_ref[...])
    x1 = _ln(ALPHA * x_ref[...] + g1_ref[...] * mix) * lng_ref[...] + lnb_ref[...]
    x1_ref[...] = x1
    h2 = _ln(x1) * (1.0 + sc2_ref[...]) + sh2_ref[...]
    w_lo, w_hi, cls = _route(_dot_split(h2, wrh_ref[...], wrl_ref[...]) + br_ref[...])
    for c in range(tile // BLOCK):
        rows = slice(c * BLOCK, (c + 1) * BLOCK)
        onehot = lane == cls[rows]
        hot = onehot.astype(BF16)
        rank = jnp.sum(jnp.where(onehot, _dot(below, hot), 0.0), -1, keepdims=True)
        cnt_ref[c] = jnp.sum(hot.astype(F32), 0, keepdims=True)
        rec = jnp.where(lane == REC_WLO, w_lo[rows], 0.0)
        rec = jnp.where(lane == REC_WHI, w_hi[rows], rec)
        rec = jnp.where(lane == REC_CLS, cls[rows].astype(F32), rec)
        rec_ref[rows, :] = jnp.where(lane == REC_RANK, rank, rec)


def _merge(x, y_attn, u, vg, gates, ws_bf, bs_rows, wpa, wpb, wo, ln_g, ln_b, mod3, wr_hi, wr_lo, br, tile):
    b, s, d = x.shape
    nt = s // tile
    per = tile // BLOCK
    row = lambda width: pl.BlockSpec((None, tile, width), lambda i, j: (i, j, 0))
    full = lambda a: pl.BlockSpec(a.shape, lambda i, j: (0,) * a.ndim)
    modv = lambda k: pl.BlockSpec((None, 1, d), lambda i, j: (i, 0, k))
    ln_g = ln_g.reshape(1, d)
    ln_b = ln_b.reshape(1, d)
    return pl.pallas_call(
        _merge_body,
        out_shape=[jax.ShapeDtypeStruct((b, s, d), F32), jax.ShapeDtypeStruct((b, s, ROUTE_W), F32),
                   jax.ShapeDtypeStruct((b * s // BLOCK, 1, LANES), F32)],
        grid=(b, nt),
        in_specs=[
            row(d), row(ATTN_W), row(GM_W), row(GM_W), row(2 * d),
            full(ws_bf), full(bs_rows), full(wpa), full(wpb), full(wo), full(ln_g), full(ln_b),
            modv(2), modv(3), modv(4),
            full(wr_hi), full(wr_lo), full(br),
        ],
        out_specs=[row(d), row(ROUTE_W), pl.BlockSpec((per, 1, LANES), lambda i, j: (i * nt + j, 0, 0))],
        compiler_params=_params("arbitrary", "arbitrary"),
        name="merge",
    )(x, y_attn, u, vg, gates, ws_bf, bs_rows, wpa, wpb, wo, ln_g, ln_b, mod3, mod3, mod3, wr_hi, wr_lo, br)


def _plan_body(cnt_ref, rec_ref, lpos_ref, tab_ref, cblk_ref, run_ref, cntt_ref, lst_ref, base_ref, lstf_ref, *, chunks):
    step = pl.program_id(0)
    n_chunks = cnt_ref.shape[0]
    n_tiles = run_ref.shape[0]
    max_blocks = tab_ref.shape[0]

    @pl.when(step == 0)
    def _():
        cnt = cnt_ref[...]
        total = jnp.sum(cnt, 0, keepdims=True)
        n_blk = jnp.floor((total + (BLOCK - 1)) * (1.0 / BLOCK))
        before = (_iota((LANES, LANES), 0) < _iota((LANES, LANES), 1)).astype(BF16)
        nb_hi, nb_lo = _split(jnp.broadcast_to(n_blk, (SUBLANES, LANES)))
        blk_start = (_dot(nb_hi, before) + _dot(nb_lo, before))[0:1]
        lane1 = _iota((1, LANES), 1)
        packed = blk_start
        shift = jnp.zeros((1, 1), F32)
        for g in range(1, N_GROUPS):
            first = jnp.sum(jnp.where(lane1 == g * PAIRS, blk_start, 0.0), -1, keepdims=True) + shift
            over = first - MOE_PACK * jnp.floor(first * (1.0 / MOE_PACK))
            shift = shift + jnp.where(over > 0.0, MOE_PACK - over, 0.0)
            packed = jnp.where(lane1 >= g * PAIRS, blk_start + shift, packed)
        blk_start = packed
        earlier = (_iota((n_chunks, n_chunks), 1) < _iota((n_chunks, n_chunks), 0)).astype(BF16)
        base_ref[...] = blk_start * float(BLOCK) + _dot(earlier, cnt.astype(BF16))
        blk_end = blk_start + n_blk
        sub = _iota((SUBLANES, LANES), 0)
        cblk_ref[...] = jnp.where(sub == 0, blk_start, jnp.where(sub == 1, blk_end, 0.0)).astype(I32)
        lane = _iota((max_blocks, LANES), 1)
        blk = _iota((max_blocks, LANES), 0).astype(F32)
        is_cls = lane < N_CLASSES
        n_real = jnp.max(jnp.where(is_cls, blk_end, 0.0), -1, keepdims=True)
        u = jnp.minimum(blk, n_real - 1.0)
        cls_of = jnp.sum(jnp.where(is_cls & (blk_end <= u), 1.0, 0.0), -1, keepdims=True)
        n_steps = jnp.floor((n_real + (MOE_PACK - 1)) * (1.0 / MOE_PACK))
        tab = jnp.where(lane == 0, cls_of, jnp.where(lane == 1, n_steps, 0.0))
        tab_ref[...] = tab.astype(I32)
        tile = _iota((n_tiles, n_chunks), 0)
        chunk = _iota((n_tiles, n_chunks), 1)
        owner = ((chunk >= tile * chunks) & (chunk < (tile + 1) * chunks)).astype(BF16)
        cnt_tile = _dot(owner, cnt.astype(BF16))
        ct_hi, ct_lo = _split(cnt_tile)
        local_start = _dot(ct_hi, before) + _dot(ct_lo, before)
        run_ref[...] = base_ref[pl.ds(0, n_tiles, stride=chunks), :].astype(I32)
        cntt_ref[...] = cnt_tile.astype(I32)
        lst_ref[...] = local_start.astype(I32)
        lstf_ref[...] = local_start

    lane = _iota((BLOCK, LANES), 1)
    tile_off = lstf_ref[pl.ds(step, 1), :] - base_ref[pl.ds(step * chunks, 1), :]
    for c in range(chunks):
        rec = rec_ref[c * BLOCK:(c + 1) * BLOCK, :]
        cls = rec[:, REC_CLS:REC_CLS + 1].astype(I32)
        base = base_ref[pl.ds(step * chunks + c, 1), :] + tile_off
        lpos = jnp.sum(jnp.where(lane == cls, base, 0.0), -1, keepdims=True) + rec[:, REC_RANK:REC_RANK + 1]
        lpos_ref[c] = jnp.transpose(jnp.broadcast_to(lpos, (BLOCK, LANES)))[0:1].astype(I32)


def _plan(counts, rec, max_blocks, chunks):
    n = rec.shape[0]
    n_chunks = n // BLOCK
    n_tiles = n_chunks // chunks
    tile_tab = jax.ShapeDtypeStruct((n_tiles, LANES), I32)
    whole = lambda rows: pl.BlockSpec((rows, LANES), lambda i: (0, 0))
    return pl.pallas_call(
        functools.partial(_plan_body, chunks=chunks),
        out_shape=[jax.ShapeDtypeStruct((n_chunks, 1, LANES), I32), jax.ShapeDtypeStruct((max_blocks, LANES), I32),
                   jax.ShapeDtypeStruct((SUBLANES, LANES), I32), tile_tab, tile_tab, tile_tab],
        grid=(n_tiles,),
        in_specs=[whole(n_chunks), pl.BlockSpec((chunks * BLOCK, ROUTE_W), lambda i: (i, 0))],
        out_specs=[pl.BlockSpec((chunks, 1, LANES), lambda i: (i, 0, 0)), whole(max_blocks), whole(SUBLANES),
                   whole(n_tiles), whole(n_tiles), whole(n_tiles)],
        scratch_shapes=[pltpu.VMEM((n_chunks, LANES), F32), pltpu.VMEM((n_tiles, LANES), F32)],
        compiler_params=_params("arbitrary"),
        name="plan",
    )(counts, rec)


def _tile_rows(base, j, pitch):
    return pl.ds(base + j, BLOCK, stride=pitch)


def _load_tiles(ref, base, pitch, width):
    return jnp.concatenate([ref[_tile_rows(base, j, pitch), :] for j in range(width // LANES)], axis=1)


def _store_tiles(ref, base, pitch, val):
    for j in range(val.shape[1] // LANES):
        ref[_tile_rows(base, j, pitch), :] = val[:, j * LANES:(j + 1) * LANES]


def _start_runs(cnt_ref, lst_ref, run_ref, buf, buf_base, hbm, sem, to_hbm):
    for c in range(N_CLASSES):
        n = cnt_ref[0, c]

        @pl.when(n > 0)
        def _():
            rows = n * SUBLANES
            v = buf.at[pl.ds(pl.multiple_of(buf_base + lst_ref[0, c] * SUBLANES, SUBLANES), rows)]
            h = hbm.at[pl.ds(pl.multiple_of(run_ref[0, c] * SUBLANES, SUBLANES), rows)]
            (pltpu.make_async_copy(v, h, sem) if to_hbm else pltpu.make_async_copy(h, v, sem)).start(priority=c % 2)


def _wait_runs(buf, buf_base, rows, hbm, sem):
    pltpu.make_async_copy(hbm.at[pl.ds(0, rows)], buf.at[pl.ds(buf_base, rows)], sem).wait()


def _scatter_body(cls_blk_ref, run_ref, cnt_ref, lst_ref, lpos_ref, x1_ref, sh_ref, sc_ref, xs_hbm,
                  abuf, bbuf, zbuf, ssem, zsem):
    u = pl.program_id(0)
    slot = u % 2
    chunks = x1_ref.shape[0] // BLOCK
    tile_rows = BLOCK * SUBLANES
    sort_rows = chunks * tile_rows

    @pl.when(u == 0)
    def _():
        zbuf[...] = jnp.zeros_like(zbuf)

        def zero_copy(c):
            start = pl.multiple_of((cls_blk_ref[1, c] - 1) * tile_rows, tile_rows)
            return pltpu.make_async_copy(zbuf, xs_hbm.at[pl.ds(start, tile_rows)], zsem)

        for c in range(N_CLASSES):
            @pl.when(cls_blk_ref[1, c] > cls_blk_ref[0, c])
            def _():
                zero_copy(c).start()
        for c in range(N_CLASSES):
            @pl.when(cls_blk_ref[1, c] > cls_blk_ref[0, c])
            def _():
                zero_copy(c).wait()

        def gap_copy(blk):
            return pltpu.make_async_copy(zbuf, xs_hbm.at[pl.ds(pl.multiple_of(blk * tile_rows, tile_rows), tile_rows)],
                                         zsem)

        gaps = [(cls_blk_ref[1, g * PAIRS - 1], cls_blk_ref[0, g * PAIRS]) for g in range(1, N_GROUPS)]
        gaps.append((cls_blk_ref[1, N_CLASSES - 1], xs_hbm.shape[0] // tile_rows))
        for lo, hi in gaps:
            lax.fori_loop(lo, hi, lambda blk, carry: (gap_copy(blk).start(), carry)[1], 0)
        for lo, hi in gaps:
            lax.fori_loop(lo, hi, lambda blk, carry: (gap_copy(blk).wait(), carry)[1], 0)

    @pl.when(u >= 2)
    def _():
        _wait_runs(bbuf, slot * sort_rows, sort_rows, xs_hbm, ssem.at[slot])

    scale = 1.0 + sc_ref[...]
    shift = sh_ref[...]

    unroll = abuf.shape[0] // tile_rows

    def sort_chunks(i, carry):
        for k in range(unroll):
            c = i * unroll + k
            stage = k * tile_rows
            rows = pl.ds(pl.multiple_of(c * BLOCK, BLOCK), BLOCK)
            _store_tiles(abuf, stage, SUBLANES, _ln(x1_ref[rows, :]) * scale + shift)
            for r in range(BLOCK):
                dst = pl.multiple_of(slot * sort_rows + lpos_ref[c, 0, r] * SUBLANES, SUBLANES)
                bbuf[pl.ds(dst, SUBLANES), :] = abuf[stage + r * SUBLANES:stage + (r + 1) * SUBLANES, :]
        return carry

    lax.fori_loop(0, chunks // unroll, sort_chunks, 0)
    _start_runs(cnt_ref, lst_ref, run_ref, bbuf, slot * sort_rows, xs_hbm, ssem.at[slot], True)

    @pl.when(u == pl.num_programs(0) - 1)
    def _():
        _wait_runs(bbuf, slot * sort_rows, sort_rows, xs_hbm, ssem.at[slot])

        @pl.when(u >= 1)
        def _():
            _wait_runs(bbuf, (1 - slot) * sort_rows, sort_rows, xs_hbm, ssem.at[1 - slot])


def _stage_chunks(chunks):
    return 2 if chunks % 2 == 0 else 1


def _tile_tab_spec(off, n_tiles):
    return pl.BlockSpec((None, 1, LANES), lambda u: (jnp.minimum(u + off, n_tiles - 1), 0, 0),
                        memory_space=pltpu.SMEM)


def _scatter(x1, mod3, lpos, tile_tabs, cls_blk, max_blocks, seq, chunks):
    n, d = x1.shape
    n_tiles = n // (chunks * BLOCK)
    per_batch = seq // (chunks * BLOCK)
    tile_rows = BLOCK * SUBLANES
    return pl.pallas_call(
        _scatter_body,
        out_shape=jax.ShapeDtypeStruct((max_blocks * tile_rows, LANES), F32),
        grid=(n_tiles,),
        in_specs=[
            pl.BlockSpec(memory_space=pltpu.SMEM),
            _tile_tab_spec(0, n_tiles), _tile_tab_spec(0, n_tiles), _tile_tab_spec(0, n_tiles),
            pl.BlockSpec((chunks, 1, LANES), lambda u: (u, 0, 0), memory_space=pltpu.SMEM),
            pl.BlockSpec((chunks * BLOCK, d), lambda u: (u, 0)),
            pl.BlockSpec((None, 1, d), lambda u: (u // per_batch, 0, 3)),
            pl.BlockSpec((None, 1, d), lambda u: (u // per_batch, 0, 4)),
        ],
        out_specs=pl.BlockSpec(memory_space=pl.ANY),
        scratch_shapes=[
            pltpu.VMEM((_stage_chunks(chunks) * tile_rows, LANES), F32),
            pltpu.VMEM((2 * chunks * tile_rows, LANES), F32),
            pltpu.VMEM((tile_rows, LANES), F32),
            pltpu.SemaphoreType.DMA((2,)),
            pltpu.SemaphoreType.DMA,
        ],
        compiler_params=_params("arbitrary"),
        name="scatter",
    )(cls_blk, *tile_tabs, lpos, x1, mod3, mod3)


def _expert(h, w1, w3, w2):
    hid = jax.nn.silu(_dot(h, w1)) * _dot(h, w3)
    return _dot(hid, w2)


def _moe_body(sgrp_ref, blo_ref, bhi_ref, nsteps_ref, xs_ref, w1_ref, w3_ref, w2_ref, olo_ref, ohi_ref):
    u = pl.program_id(0)
    d = w1_ref.shape[1]
    tile_rows = BLOCK * SUBLANES

    @pl.when(u < nsteps_ref[0])
    def _():
        jobs = [(k, o_ref, e) for k in range(MOE_PACK)
                for o_ref, e in ((olo_ref, blo_ref[MOE_PACK * u + k]), (ohi_ref, bhi_ref[MOE_PACK * u + k]))]
        h = [_load_tiles(xs_ref, k * tile_rows, SUBLANES, d) for k in range(MOE_PACK)]
        up = [(_dot(h[k], w1_ref[e]), _dot(h[k], w3_ref[e])) for k, _, e in jobs]
        hid = [jax.nn.silu(a) * b for a, b in up]
        for (k, o_ref, e), g in zip(jobs, hid):
            _store_tiles(o_ref, k * tile_rows, SUBLANES, _dot(g, w2_ref[e]))

    @pl.when(u >= nsteps_ref[0])
    def _():
        olo_ref[...] = jnp.zeros_like(olo_ref)
        ohi_ref[...] = jnp.zeros_like(ohi_ref)


def _moe(xs, tables, w1, w3, w2):
    sgrp, blo, bhi, n_steps = tables
    max_steps = sgrp.shape[0]
    d, de = w1.shape[1:]
    in_rows = MOE_PACK * BLOCK * SUBLANES
    w1g = w1.reshape(N_GROUPS, EPG, d, de)
    w3g = w3.reshape(N_GROUPS, EPG, d, de)
    w2g = w2.reshape(N_GROUPS, EPG, de, d)

    def w_spec(a):
        return pl.BlockSpec((None,) + a.shape[1:], lambda u, g, lo, hi, n_steps: (g[u], 0, 0, 0),
                            pipeline_mode=pl.Buffered(1))

    grid_spec = pltpu.PrefetchScalarGridSpec(
        num_scalar_prefetch=4,
        grid=(max_steps,),
        in_specs=[
            pl.BlockSpec((in_rows, LANES), lambda u, g, lo, hi, n_steps: (jnp.minimum(u, n_steps[0] - 1), 0)),
            w_spec(w1g), w_spec(w3g), w_spec(w2g),
        ],
        out_specs=[pl.BlockSpec((in_rows, LANES), lambda u, *_: (u, 0))] * 2,
    )
    return pl.pallas_call(
        _moe_body,
        out_shape=[jax.ShapeDtypeStruct((max_steps * in_rows, LANES), F32)] * 2,
        grid_spec=grid_spec,
        compiler_params=pltpu.CompilerParams(dimension_semantics=("arbitrary",), vmem_limit_bytes=MOE_VMEM_LIMIT_BYTES),
        name="moe",
    )(sgrp, blo, bhi, n_steps, xs, w1g, w3g, w2g)


def _final_body(run_ref, cnt_ref, lst_ref, runn_ref, cntn_ref, lstn_ref, lpos_ref, x1_ref, rec_ref, g2_ref,
                lng_ref, lnb_ref, mlo_hbm, mhi_hbm, o_ref, glo, ghi, tlo, thi, gsem):
    u = pl.program_id(0)
    slot = u % 2
    d = x1_ref.shape[-1]
    chunks = x1_ref.shape[0] // BLOCK
    sort_rows = chunks * BLOCK * SUBLANES
    sides = ((mlo_hbm, glo), (mhi_hbm, ghi))

    def start_runs(tabs, s):
        for hbm, buf in sides:
            _start_runs(tabs[1], tabs[2], tabs[0], buf, s * sort_rows, hbm, gsem.at[s], False)

    def wait_runs(s):
        for hbm, buf in sides:
            _wait_runs(buf, s * sort_rows, sort_rows, hbm, gsem.at[s])

    @pl.when(u == 0)
    def _():
        start_runs((run_ref, cnt_ref, lst_ref), 0)

    wait_runs(slot)
    start_runs((runn_ref, cntn_ref, lstn_ref), 1 - slot)
    gain = g2_ref[...]
    ln_g = lng_ref[...]
    ln_b = lnb_ref[...]

    lanes_rows = BLOCK * SUBLANES
    unroll = tlo.shape[0] // lanes_rows

    def finish_chunks(i, carry):
        for k in range(unroll):
            c = i * unroll + k
            stage = k * lanes_rows
            for r in range(BLOCK):
                src = pl.ds(pl.multiple_of(slot * sort_rows + lpos_ref[c, 0, r] * SUBLANES, SUBLANES), SUBLANES)
                tlo[stage + r * SUBLANES:stage + (r + 1) * SUBLANES, :] = glo[src, :]
                thi[stage + r * SUBLANES:stage + (r + 1) * SUBLANES, :] = ghi[src, :]
            rows = pl.ds(pl.multiple_of(c * BLOCK, BLOCK), BLOCK)
            rec = rec_ref[rows, :]
            moe = (rec[:, REC_WLO:REC_WLO + 1] * _load_tiles(tlo, stage, SUBLANES, d)
                   + rec[:, REC_WHI:REC_WHI + 1] * _load_tiles(thi, stage, SUBLANES, d))
            o_ref[rows, :] = _ln(ALPHA * x1_ref[rows, :] + gain * moe) * ln_g + ln_b
        return carry

    lax.fori_loop(0, chunks // unroll, finish_chunks, 0)

    @pl.when(u == pl.num_programs(0) - 1)
    def _():
        wait_runs(1 - slot)


def _final(x1, rec, ms_lo, ms_hi, lpos, tile_tabs, mod3, ln_g, ln_b, seq, chunks):
    n, d = x1.shape
    n_tiles = n // (chunks * BLOCK)
    per_batch = seq // (chunks * BLOCK)
    tile_rows = BLOCK * SUBLANES
    vec = pl.BlockSpec((1, d), lambda u: (0, 0))
    hbm = pl.BlockSpec(memory_space=pl.ANY)
    return pl.pallas_call(
        _final_body,
        out_shape=jax.ShapeDtypeStruct((n, d), F32),
        grid=(n_tiles,),
        in_specs=[
            _tile_tab_spec(0, n_tiles), _tile_tab_spec(0, n_tiles), _tile_tab_spec(0, n_tiles),
            _tile_tab_spec(1, n_tiles), _tile_tab_spec(1, n_tiles), _tile_tab_spec(1, n_tiles),
            pl.BlockSpec((chunks, 1, LANES), lambda u: (u, 0, 0), memory_space=pltpu.SMEM),
            pl.BlockSpec((chunks * BLOCK, d), lambda u: (u, 0)),
            pl.BlockSpec((chunks * BLOCK, ROUTE_W), lambda u: (u, 0)),
            pl.BlockSpec((None, 1, d), lambda u: (u // per_batch, 0, 5)),
            vec, vec, hbm, hbm,
        ],
        out_specs=pl.BlockSpec((chunks * BLOCK, d), lambda u: (u, 0)),
        scratch_shapes=[
            pltpu.VMEM((2 * chunks * tile_rows, LANES), F32),
            pltpu.VMEM((2 * chunks * tile_rows, LANES), F32),
            pltpu.VMEM((_stage_chunks(chunks) * tile_rows, LANES), F32),
            pltpu.VMEM((_stage_chunks(chunks) * tile_rows, LANES), F32),
            pltpu.SemaphoreType.DMA((2,)),
        ],
        compiler_params=_params("arbitrary"),
        name="final",
    )(*tile_tabs, *tile_tabs, lpos, x1, rec, mod3, ln_g.reshape(1, d), ln_b.reshape(1, d), ms_lo, ms_hi)


def _block_tables(tab):
    cls = tab[:, 0]
    pair = cls % PAIRS
    first = jnp.array([i for i in range(EPG) for _ in range(i + 1, EPG)], I32)
    second = jnp.array([j for i in range(EPG) for j in range(i + 1, EPG)], I32)
    return (cls // PAIRS)[::MOE_PACK], first[pair], second[pair], tab[0, 1].reshape(1)


def kernel(x, c, ctx, c_ctx, w_ada, b_ada, w_in, attn_sink, gm_ln_g, gm_ln_b, gm_ws, gm_bs, w_pa, w_pb, w_o,
           ln1_g, ln1_b, router_g_w, router_g_b, router_e_w, router_e_b, moe_w1, moe_w3, moe_w2, ln2_g, ln2_b):
    b, s, d = x.shape
    assert w_ada.shape[0] == DEPTH and s % BLOCK == 0 and s % GRID_W == 0
    mod_rows = 2 * SUBLANES
    assert b + 1 <= mod_rows
    c_rows = jnp.zeros((mod_rows, d), F32).at[:b].set(c).at[b].set(c_ctx)
    mod = _adaln_mod(c_rows, w_ada[0], b_ada[0])
    mod3 = mod.reshape(mod_rows, 1, 6 * d)
    w_in_bf = w_in[0].astype(BF16)
    kv_ctx = _ctx_kv(ctx, mod3, b, w_in_bf)
    tile = 512 if s % 512 == 0 else BLOCK
    q, kv, u, vg, gates = _project(x, mod3, w_in_bf, _rope_tables(s), gm_ln_g[0], gm_ln_b[0], tile)
    nb = s // BLOCK
    y_attn = _attend(q, kv, kv_ctx, attn_sink[0], next(p for p in (4, 2, 1) if nb % p == 0))
    bs_rows = jnp.repeat(gm_bs[0].T, GM_HEAD, axis=1)
    w_route = jnp.concatenate([router_g_w[0], jnp.transpose(router_e_w[0], (1, 0, 2)).reshape(d, N_EXPERTS)], axis=1)
    w_route = jnp.pad(w_route, ((0, 0), (0, ROUTE_W - w_route.shape[1])))
    b_route = jnp.pad(jnp.concatenate([router_g_b[0], router_e_b[0].reshape(-1)]), (0, ROUTE_W - N_GROUPS - N_EXPERTS))
    wr_hi = w_route.astype(BF16)
    wr_lo = (w_route - wr_hi.astype(F32)).astype(BF16)
    merge_tile = BLOCK * next(p for p in (4, 2, 1) if nb % p == 0)
    x1, rec, counts = _merge(x, y_attn, u, vg, gates, gm_ws[0].astype(BF16), bs_rows, w_pa[0].astype(BF16),
                             w_pb[0].astype(BF16), w_o[0].astype(BF16), ln1_g[0], ln1_b[0], mod3, wr_hi, wr_lo,
                             b_route.reshape(1, ROUTE_W), merge_tile)
    n = b * s
    x1 = x1.reshape(n, d)
    rec = rec.reshape(n, ROUTE_W)
    max_blocks = n // BLOCK + N_CLASSES - 1 + (N_GROUPS - 1) * (MOE_PACK - 1)
    max_blocks = -(-max_blocks // MOE_PACK) * MOE_PACK
    sort_chunks = next(p for p in (SUBLANES, 4, 2, 1) if nb % p == 0)
    lpos, tab, cls_blk, *tile_tabs = _plan(counts.reshape(n // BLOCK, LANES), rec, max_blocks, sort_chunks)
    tile_tabs = [t.reshape(t.shape[0], 1, LANES) for t in tile_tabs]
    xs = _scatter(x1, mod3, lpos, tile_tabs, cls_blk, max_blocks, s, sort_chunks)
    ms_lo, ms_hi = _moe(xs, _block_tables(tab), moe_w1[0], moe_w3[0], moe_w2[0])
    out = _final(x1, rec, ms_lo, ms_hi, lpos, tile_tabs, mod3, ln2_g[0], ln2_b[0], s, sort_chunks)
    return out.reshape(b, s, d)
```

```python
import functools

import jax
import jax.numpy as jnp
from jax import lax
from jax.experimental import pallas as pl
from jax.experimental.pallas import tpu as pltpu

F32 = jnp.float32
BF16 = jnp.bfloat16
I32 = jnp.int32

GRID_W = 64
HEAD_DIM = 64
Q_HEADS = 8
KV_HEADS = 2
GROUP = Q_HEADS // KV_HEADS
BLOCK = 128
ROPE_BASE = 10000.0
ROPE_FREQS = HEAD_DIM // 4
GM_GROUPS = 8
GM_HEAD = 64
ATTN_W = Q_HEADS * HEAD_DIM
KV_W = KV_HEADS * HEAD_DIM
GM_W = GM_GROUPS * GM_HEAD
N_GROUPS = 4
EPG = 8
N_EXPERTS = N_GROUPS * EPG
PAIRS = EPG * (EPG - 1) // 2
N_CLASSES = N_GROUPS * PAIRS
LN_EPS = 1e-6
NEG_INF = -1e30
DEPTH = 1
ALPHA = (2.0 * DEPTH) ** 0.25

LANES = 128
SUBLANES = 8
VMEM_LIMIT_BYTES = 56 * 1024 * 1024
MOE_VMEM_LIMIT_BYTES = 60 * 1024 * 1024
MOE_PACK = 2
ROUTE_W = LANES
REC_WLO, REC_WHI, REC_CLS, REC_RANK = 0, 1, 2, 3


def _params(*sem):
    return pltpu.CompilerParams(dimension_semantics=sem, vmem_limit_bytes=VMEM_LIMIT_BYTES)


def _ln(x):
    mu = jnp.mean(x, -1, keepdims=True)
    xc = x - mu
    var = jnp.mean(xc * xc, -1, keepdims=True)
    return xc * lax.rsqrt(var + LN_EPS)


def _dot(a, b):
    return jnp.dot(a, b, preferred_element_type=F32)


def _dot_nt(a, b):
    return lax.dot_general(a, b, (((1,), (1,)), ((), ())), preferred_element_type=F32)


def _dot_tn(a, b):
    return lax.dot_general(a, b, (((0,), (0,)), ((), ())), preferred_element_type=F32)


def _split(a):
    hi = a.astype(BF16)
    lo = (a - hi.astype(F32)).astype(BF16)
    return hi, lo


def _dot_split(a, b_hi, b_lo):
    a_hi, a_lo = _split(a)
    return _dot(a_hi, b_hi) + _dot(a_hi, b_lo) + _dot(a_lo, b_hi)


def _iota(shape, axis):
    return lax.broadcasted_iota(I32, shape, axis)


def _mod_body(c_ref, w_ref, b_ref, o_ref):
    w_hi, w_lo = _split(w_ref[...])
    o_ref[...] = _dot_split(jax.nn.silu(c_ref[...]), w_hi, w_lo) + b_ref[...]


def _adaln_mod(c_rows, w_ada, b_ada):
    rows, d = c_rows.shape
    width = w_ada.shape[1]
    tn = 6 * LANES
    return pl.pallas_call(
        _mod_body,
        out_shape=jax.ShapeDtypeStruct((rows, width), F32),
        grid=(width // tn,),
        in_specs=[
            pl.BlockSpec((rows, d), lambda j: (0, 0)),
            pl.BlockSpec((d, tn), lambda j: (0, j)),
            pl.BlockSpec((1, tn), lambda j: (0, j)),
        ],
        out_specs=pl.BlockSpec((rows, tn), lambda j: (0, j)),
        compiler_params=_params("arbitrary"),
        name="mod",
    )(c_rows, w_ada, b_ada.reshape(1, width))


def _ctx_body(ctx_ref, sh_ref, sc_ref, w_ref, o_ref):
    h = _ln(ctx_ref[...]) * (1.0 + sc_ref[...]) + sh_ref[...]
    o_ref[...] = _dot(h.astype(BF16), w_ref[...]).astype(BF16)


def _ctx_kv(ctx, mod3, ctx_row, w_in_bf):
    b, c_len, d = ctx.shape
    kvw = 2 * KV_W
    return pl.pallas_call(
        _ctx_body,
        out_shape=jax.ShapeDtypeStruct((b, c_len, kvw), BF16),
        grid=(b,),
        in_specs=[
            pl.BlockSpec((None, c_len, d), lambda i: (i, 0, 0)),
            pl.BlockSpec((None, 1, d), lambda i: (ctx_row, 0, 0)),
            pl.BlockSpec((None, 1, d), lambda i: (ctx_row, 0, 1)),
            pl.BlockSpec((d, kvw), lambda i: (0, ATTN_W // kvw)),
        ],
        out_specs=pl.BlockSpec((None, c_len, kvw), lambda i: (i, 0, 0)),
        compiler_params=_params("arbitrary"),
        name="ctx_kv",
    )(ctx, mod3, mod3, w_in_bf)


def _rope(x, cos, s_up, s_dn):
    return x * cos + pltpu.roll(x, LANES - ROPE_FREQS, 1) * s_up + pltpu.roll(x, ROPE_FREQS, 1) * s_dn


def _project_body(x_ref, sh_ref, sc_ref, w_ref, cos_ref, sup_ref, sdn_ref, lng_ref, lnb_ref,
                  q_ref, kv_ref, u_ref, vg_ref, gate_ref):
    h = (_ln(x_ref[...]) * (1.0 + sc_ref[...]) + sh_ref[...]).astype(BF16)
    cos, s_up, s_dn = cos_ref[...], sup_ref[...], sdn_ref[...]
    q_scale = HEAD_DIM ** -0.5
    k0 = ATTN_W
    u0 = k0 + 2 * KV_W
    vg0 = u0 + GM_W
    gate0 = vg0 + GM_W
    q = _dot(h, w_ref[:, 0:k0])
    for j in range(ATTN_W // LANES):
        sl = slice(j * LANES, (j + 1) * LANES)
        q_ref[:, sl] = (_rope(q[:, sl], cos, s_up, s_dn) * q_scale).astype(BF16)
    kv = _dot(h, w_ref[:, k0:u0])
    kv_ref[:, 0:KV_W] = _rope(kv[:, 0:KV_W], cos, s_up, s_dn).astype(BF16)
    kv_ref[:, KV_W:] = kv[:, KV_W:].astype(BF16)
    u_ref[...] = jax.nn.gelu(_dot(h, w_ref[:, u0:vg0])).astype(BF16)
    vg = _ln(jax.nn.gelu(_dot(h, w_ref[:, vg0:gate0]))) * lng_ref[...] + lnb_ref[...]
    vg_ref[...] = vg.astype(BF16)
    gate_ref[...] = jax.nn.sigmoid(_dot(h, w_ref[:, gate0:])).astype(BF16)


def _project(x, mod3, w_in_bf, rope_tabs, gm_ln_g, gm_ln_b, tile):
    b, s, d = x.shape
    in_w = w_in_bf.shape[1]
    gate_w = in_w - (ATTN_W + 2 * KV_W + 2 * GM_W)
    row = lambda width: pl.BlockSpec((None, tile, width), lambda i, t: (i, t, 0))
    tab = pl.BlockSpec((tile, LANES), lambda i, t: (t, 0))
    vec = lambda width: pl.BlockSpec((1, width), lambda i, t: (0, 0))
    return pl.pallas_call(
        _project_body,
        out_shape=[
            jax.ShapeDtypeStruct((b, s, ATTN_W), BF16),
            jax.ShapeDtypeStruct((b, s, 2 * KV_W), BF16),
            jax.ShapeDtypeStruct((b, s, GM_W), BF16),
            jax.ShapeDtypeStruct((b, s, GM_W), BF16),
            jax.ShapeDtypeStruct((b, s, gate_w), BF16),
        ],
        grid=(b, s // tile),
        in_specs=[
            row(d),
            pl.BlockSpec((None, 1, d), lambda i, t: (i, 0, 0)),
            pl.BlockSpec((None, 1, d), lambda i, t: (i, 0, 1)),
            pl.BlockSpec((d, in_w), lambda i, t: (0, 0)),
            tab, tab, tab,
            vec(GM_W), vec(GM_W),
        ],
        out_specs=[row(ATTN_W), row(2 * KV_W), row(GM_W), row(GM_W), row(gate_w)],
        compiler_params=_params("arbitrary", "arbitrary"),
        name="project",
    )(x, mod3, mod3, w_in_bf, *rope_tabs, gm_ln_g.reshape(1, GM_W), gm_ln_b.reshape(1, GM_W))


def _rope_tables(s):
    pos = jnp.arange(s, dtype=I32)
    inv = 1.0 / (ROPE_BASE ** (jnp.arange(ROPE_FREQS, dtype=F32) / ROPE_FREQS))
    ang_r = (pos // GRID_W).astype(F32)[:, None] * inv[None, :]
    ang_c = (pos % GRID_W).astype(F32)[:, None] * inv[None, :]
    cr, sr, cc, sn = jnp.cos(ang_r), jnp.sin(ang_r), jnp.cos(ang_c), jnp.sin(ang_c)
    z = jnp.zeros_like(sr)
    reps = LANES // HEAD_DIM
    cos = jnp.tile(jnp.concatenate([cr, cr, cc, cc], -1), (1, reps))
    s_up = jnp.tile(jnp.concatenate([-sr, z, -sn, z], -1), (1, reps))
    s_dn = jnp.tile(jnp.concatenate([z, sr, z, sn], -1), (1, reps))
    return cos, s_up, s_dn


def _attend_body(sink_ref, q_ref, kvp_ref, kvc_ref, kvn_ref, ctx_ref, o_ref):
    i = pl.program_id(1)
    n_steps = pl.num_programs(1)
    per = q_ref.shape[0] // BLOCK
    cols = GROUP * BLOCK
    r = _iota((3 * BLOCK, cols), 1) & (BLOCK - 1)
    m = _iota((3 * BLOCK, cols), 0)
    band = (m >= r) & (m <= r + 2 * BLOCK)
    grp = _iota((1, cols), 1) // BLOCK
    kv_all = jnp.concatenate([kvp_ref[...], kvc_ref[...], kvn_ref[...]], axis=0)
    kv_ctx = ctx_ref[...]
    for t in range(per):
        mask = band
        if t == 0:
            mask = mask & ((m >= BLOCK) | (i > 0))
        if t == per - 1:
            mask = mask & ((m < 2 * BLOCK) | (i < n_steps - 1))
        q = q_ref[t * BLOCK:(t + 1) * BLOCK, :]
        kv_loc = kv_all[t * BLOCK:(t + 3) * BLOCK]
        outs = []
        for h in range(KV_HEADS):
            ks = slice(h * HEAD_DIM, (h + 1) * HEAD_DIM)
            vs = slice(KV_W + h * HEAD_DIM, KV_W + (h + 1) * HEAD_DIM)
            qg = jnp.concatenate(
                [q[:, (h * GROUP + g) * HEAD_DIM:(h * GROUP + g + 1) * HEAD_DIM] for g in range(GROUP)], axis=0)
            sink = jnp.zeros((1, cols), F32)
            for g in range(GROUP):
                sink = jnp.where(grp == g, sink_ref[h * GROUP + g], sink)
            s_ctx = _dot_nt(kv_ctx[:, ks], qg)
            s_loc = jnp.where(mask, _dot_nt(kv_loc[:, ks], qg), NEG_INF)
            top = jnp.maximum(jnp.maximum(jnp.max(s_ctx, 0, keepdims=True), jnp.max(s_loc, 0, keepdims=True)), sink)
            e_ctx = jnp.exp(s_ctx - top)
            e_loc = jnp.exp(s_loc - top)
            den = jnp.sum(e_ctx, 0, keepdims=True) + jnp.sum(e_loc, 0, keepdims=True) + jnp.exp(sink - top)
            o = (_dot_tn(kv_ctx[:, vs], e_ctx.astype(BF16)) + _dot_tn(kv_loc[:, vs], e_loc.astype(BF16))) * (1.0 / den)
            o = jnp.transpose(o)
            outs += [o[g * BLOCK:(g + 1) * BLOCK] for g in range(GROUP)]
        o_ref[t * BLOCK:(t + 1) * BLOCK, :] = jnp.concatenate(outs, axis=1).astype(BF16)


def _attend(q, kv, kv_ctx, sink, per):
    b, s, _ = q.shape
    nb = s // BLOCK
    c_len = kv_ctx.shape[1]
    kvw = 2 * KV_W
    return pl.pallas_call(
        _attend_body,
        out_shape=jax.ShapeDtypeStruct((b, s, ATTN_W), BF16),
        grid=(b, nb // per),
        in_specs=[
            pl.BlockSpec(memory_space=pltpu.SMEM),
            pl.BlockSpec((None, per * BLOCK, ATTN_W), lambda i, j: (i, j, 0)),
            pl.BlockSpec((None, BLOCK, kvw), lambda i, j: (i, jnp.maximum(j * per - 1, 0), 0)),
            pl.BlockSpec((None, per * BLOCK, kvw), lambda i, j: (i, j, 0)),
            pl.BlockSpec((None, BLOCK, kvw), lambda i, j: (i, jnp.minimum((j + 1) * per, nb - 1), 0)),
            pl.BlockSpec((None, c_len, kvw), lambda i, j: (i, 0, 0)),
        ],
        out_specs=pl.BlockSpec((None, per * BLOCK, ATTN_W), lambda i, j: (i, j, 0)),
        compiler_params=_params("arbitrary", "arbitrary"),
        name="attend",
    )(sink, q, kv, kv, kv, kv_ctx)


def _route(logits):
    lane = _iota(logits.shape, 1)
    far = jnp.int32(2 * ROUTE_W)

    def first_max(vals):
        top = jnp.max(vals, -1, keepdims=True)
        return top, jnp.min(jnp.where(vals == top, lane, far), -1, keepdims=True)

    is_g = lane < N_GROUPS
    g_top, g_idx = first_max(jnp.where(is_g, logits, NEG_INF))
    g_w = 1.0 / jnp.sum(jnp.where(is_g, jnp.exp(logits - g_top), 0.0), -1, keepdims=True)
    lo = N_GROUPS + EPG * g_idx
    e_vals = jnp.where((lane >= lo) & (lane < lo + EPG), logits, NEG_INF)
    v1, i1 = first_max(e_vals)
    v2, i2 = first_max(jnp.where(lane == i1, NEG_INF, e_vals))
    t = jnp.exp(v2 - v1)
    w1 = g_w * (1.0 / (1.0 + t))
    w2 = g_w * (t / (1.0 + t))
    swap = i2 < i1
    a = jnp.where(swap, i2, i1) - lo
    b = jnp.where(swap, i1, i2) - lo
    cls = g_idx * PAIRS + ((a * (2 * EPG - 1 - a)) >> 1) + (b - a - 1)
    return jnp.where(swap, w2, w1), jnp.where(swap, w1, w2), cls


def _merge_body(x_ref, ya_ref, u_ref, vg_ref, gate_ref, ws_ref, bs_ref, wpa_ref, wpb_ref, wo_ref,
                lng_ref, lnb_ref, g1_ref, sh2_ref, sc2_ref, wrh_ref, wrl_ref, br_ref, x1_ref, rec_ref, cnt_ref):
    tile, d = x_ref.shape
    lane = _iota((BLOCK, LANES), 1)
    below = (lane < _iota((BLOCK, LANES), 0)).astype(BF16)
    chunks = []
    for c in range(tile // BLOCK):
        rows = slice(c * BLOCK, (c + 1) * BLOCK)
        pieces = []
        for j in range(GM_W // LANES):
            v = vg_ref[rows, j * LANES:(j + 1) * LANES]
            per = LANES // GM_HEAD
            sp = _dot(ws_ref[per * j], v)
            for k in range(1, per):
                sp = jnp.where(lane < k * GM_HEAD, sp, _dot(ws_ref[per * j + k], v))
            pieces.append(sp)
        chunks.append(jnp.concatenate(pieces, axis=1) + bs_ref[...])
    sp = jnp.concatenate(chunks, axis=0)
    y_gm = (u_ref[...].astype(F32) * sp).astype(BF16)
    gate = gate_ref[...].astype(F32)
    y = gate[:, :d] * _dot(ya_ref[...], wpa_ref[...]) + gate[:, d:] * _dot(y_gm, wpb_ref[...])
    mix = _dot(y.astype(BF16), wo_ref[...])
    x1 = _ln(ALPHA * x_ref[...] + g1_ref[...] * mix) * lng_ref[...] + lnb_ref[...]
    x1_ref[...] = x1
    h2 = _ln(x1) * (1.0 + sc2_ref[...]) + sh2_ref[...]
    w_lo, w_hi, cls = _route(_dot_split(h2, wrh_ref[...], wrl_ref[...]) + br_ref[...])
    for c in range(tile // BLOCK):
        rows = slice(c * BLOCK, (c + 1) * BLOCK)
        onehot = lane == cls[rows]
        hot = onehot.astype(BF16)
        rank = jnp.sum(jnp.where(onehot, _dot(below, hot), 0.0), -1, keepdims=True)
        cnt_ref[c] = jnp.sum(hot.astype(F32), 0, keepdims=True)
        rec = jnp.where(lane == REC_WLO, w_lo[rows], 0.0)
        rec = jnp.where(lane == REC_WHI, w_hi[rows], rec)
        rec = jnp.where(lane == REC_CLS, cls[rows].astype(F32), rec)
        rec_ref[rows, :] = jnp.where(lane == REC_RANK, rank, rec)


def _merge(x, y_attn, u, vg, gates, ws_bf, bs_rows, wpa, wpb, wo, ln_g, ln_b, mod3, wr_hi, wr_lo, br, tile):
    b, s, d = x.shape
    nt = s // tile
    per = tile // BLOCK
    row = lambda width: pl.BlockSpec((None, tile, width), lambda i, j: (i, j, 0))
    full = lambda a: pl.BlockSpec(a.shape, lambda i, j: (0,) * a.ndim)
    modv = lambda k: pl.BlockSpec((None, 1, d), lambda i, j: (i, 0, k))
    ln_g = ln_g.reshape(1, d)
    ln_b = ln_b.reshape(1, d)
    return pl.pallas_call(
        _merge_body,
        out_shape=[jax.ShapeDtypeStruct((b, s, d), F32), jax.ShapeDtypeStruct((b, s, ROUTE_W), F32),
                   jax.ShapeDtypeStruct((b * s // BLOCK, 1, LANES), F32)],
        grid=(b, nt),
        in_specs=[
            row(d), row(ATTN_W), row(GM_W), row(GM_W), row(2 * d),
            full(ws_bf), full(bs_rows), full(wpa), full(wpb), full(wo), full(ln_g), full(ln_b),
            modv(2), modv(3), modv(4),
            full(wr_hi), full(wr_lo), full(br),
        ],
        out_specs=[row(d), row(ROUTE_W), pl.BlockSpec((per, 1, LANES), lambda i, j: (i * nt + j, 0, 0))],
        compiler_params=_params("arbitrary", "arbitrary"),
        name="merge",
    )(x, y_attn, u, vg, gates, ws_bf, bs_rows, wpa, wpb, wo, ln_g, ln_b, mod3, mod3, mod3, wr_hi, wr_lo, br)


def _plan_body(cnt_ref, rec_ref, lpos_ref, tab_ref, cblk_ref, run_ref, cntt_ref, lst_ref, base_ref, lstf_ref, *, chunks):
    step = pl.program_id(0)
    n_chunks = cnt_ref.shape[0]
    n_tiles = run_ref.shape[0]
    max_blocks = tab_ref.shape[0]

    @pl.when(step == 0)
    def _():
        cnt = cnt_ref[...]
        total = jnp.sum(cnt, 0, keepdims=True)
        n_blk = jnp.floor((total + (BLOCK - 1)) * (1.0 / BLOCK))
        before = (_iota((LANES, LANES), 0) < _iota((LANES, LANES), 1)).astype(BF16)
        nb_hi, nb_lo = _split(jnp.broadcast_to(n_blk, (SUBLANES, LANES)))
        blk_start = (_dot(nb_hi, before) + _dot(nb_lo, before))[0:1]
        lane1 = _iota((1, LANES), 1)
        packed = blk_start
        shift = jnp.zeros((1, 1), F32)
        for g in range(1, N_GROUPS):
            first = jnp.sum(jnp.where(lane1 == g * PAIRS, blk_start, 0.0), -1, keepdims=True) + shift
            over = first - MOE_PACK * jnp.floor(first * (1.0 / MOE_PACK))
            shift = shift + jnp.where(over > 0.0, MOE_PACK - over, 0.0)
            packed = jnp.where(lane1 >= g * PAIRS, blk_start + shift, packed)
        blk_start = packed
        earlier = (_iota((n_chunks, n_chunks), 1) < _iota((n_chunks, n_chunks), 0)).astype(BF16)
        base_ref[...] = blk_start * float(BLOCK) + _dot(earlier, cnt.astype(BF16))
        blk_end = blk_start + n_blk
        sub = _iota((SUBLANES, LANES), 0)
        cblk_ref[...] = jnp.where(sub == 0, blk_start, jnp.where(sub == 1, blk_end, 0.0)).astype(I32)
        lane = _iota((max_blocks, LANES), 1)
        blk = _iota((max_blocks, LANES), 0).astype(F32)
        is_cls = lane < N_CLASSES
        n_real = jnp.max(jnp.where(is_cls, blk_end, 0.0), -1, keepdims=True)
        u = jnp.minimum(blk, n_real - 1.0)
        cls_of = jnp.sum(jnp.where(is_cls & (blk_end <= u), 1.0, 0.0), -1, keepdims=True)
        n_steps = jnp.floor((n_real + (MOE_PACK - 1)) * (1.0 / MOE_PACK))
        tab = jnp.where(lane == 0, cls_of, jnp.where(lane == 1, n_steps, 0.0))
        tab_ref[...] = tab.astype(I32)
        tile = _iota((n_tiles, n_chunks), 0)
        chunk = _iota((n_tiles, n_chunks), 1)
        owner = ((chunk >= tile * chunks) & (chunk < (tile + 1) * chunks)).astype(BF16)
        cnt_tile = _dot(owner, cnt.astype(BF16))
        ct_hi, ct_lo = _split(cnt_tile)
        local_start = _dot(ct_hi, before) + _dot(ct_lo, before)
        run_ref[...] = base_ref[pl.ds(0, n_tiles, stride=chunks), :].astype(I32)
        cntt_ref[...] = cnt_tile.astype(I32)
        lst_ref[...] = local_start.astype(I32)
        lstf_ref[...] = local_start

    lane = _iota((BLOCK, LANES), 1)
    tile_off = lstf_ref[pl.ds(step, 1), :] - base_ref[pl.ds(step * chunks, 1), :]
    pick_rank = (_iota((2 * SUBLANES, LANES), 1) == REC_RANK).astype(BF16)
    for c in range(chunks):
        rec = rec_ref[c * BLOCK:(c + 1) * BLOCK, :]
        hot = (lane == rec[:, REC_CLS:REC_CLS + 1].astype(I32)).astype(BF16)
        base = base_ref[pl.ds(step * chunks + c, 1), :] + tile_off
        b_hi, b_lo = _split(jnp.broadcast_to(base, (2 * SUBLANES, LANES)))
        lpos = _dot_nt(b_hi, hot) + _dot_nt(b_lo, hot) + _dot_nt(pick_rank, rec.astype(BF16))
        lpos_ref[c] = lpos[0:1].astype(I32)


def _plan(counts, rec, max_blocks, chunks):
    n = rec.shape[0]
    n_chunks = n // BLOCK
    n_tiles = n_chunks // chunks
    tile_tab = jax.ShapeDtypeStruct((n_tiles, LANES), I32)
    whole = lambda rows: pl.BlockSpec((rows, LANES), lambda i: (0, 0))
    return pl.pallas_call(
        functools.partial(_plan_body, chunks=chunks),
        out_shape=[jax.ShapeDtypeStruct((n_chunks, 1, LANES), I32), jax.ShapeDtypeStruct((max_blocks, LANES), I32),
                   jax.ShapeDtypeStruct((SUBLANES, LANES), I32), tile_tab, tile_tab, tile_tab],
        grid=(n_tiles,),
        in_specs=[whole(n_chunks), pl.BlockSpec((chunks * BLOCK, ROUTE_W), lambda i: (i, 0))],
        out_specs=[pl.BlockSpec((chunks, 1, LANES), lambda i: (i, 0, 0)), whole(max_blocks), whole(SUBLANES),
                   whole(n_tiles), whole(n_tiles), whole(n_tiles)],
        scratch_shapes=[pltpu.VMEM((n_chunks, LANES), F32), pltpu.VMEM((n_tiles, LANES), F32)],
        compiler_params=_params("arbitrary"),
        name="plan",
    )(counts, rec)


def _tile_rows(base, j, pitch):
    return pl.ds(base + j, BLOCK, stride=pitch)


def _load_tiles(ref, base, pitch, width):
    return jnp.concatenate([ref[_tile_rows(base, j, pitch), :] for j in range(width // LANES)], axis=1)


def _store_tiles(ref, base, pitch, val):
    for j in range(val.shape[1] // LANES):
        ref[_tile_rows(base, j, pitch), :] = val[:, j * LANES:(j + 1) * LANES]


def _start_runs(cnt_ref, lst_ref, run_ref, buf, buf_base, hbm, sem, to_hbm):
    for c in range(N_CLASSES):
        n = cnt_ref[0, c]

        @pl.when(n > 0)
        def _():
            rows = n * SUBLANES
            v = buf.at[pl.ds(pl.multiple_of(buf_base + lst_ref[0, c] * SUBLANES, SUBLANES), rows)]
            h = hbm.at[pl.ds(pl.multiple_of(run_ref[0, c] * SUBLANES, SUBLANES), rows)]
            (pltpu.make_async_copy(v, h, sem) if to_hbm else pltpu.make_async_copy(h, v, sem)).start(priority=c % 2)


def _wait_runs(buf, buf_base, rows, hbm, sem):
    pltpu.make_async_copy(hbm.at[pl.ds(0, rows)], buf.at[pl.ds(buf_base, rows)], sem).wait()


def _scatter_body(cls_blk_ref, run_ref, cnt_ref, lst_ref, lpos_ref, x1_ref, sh_ref, sc_ref, xs_hbm,
                  abuf, bbuf, zbuf, ssem, zsem):
    u = pl.program_id(0)
    slot = u % 2
    chunks = x1_ref.shape[0] // BLOCK
    tile_rows = BLOCK * SUBLANES
    sort_rows = chunks * tile_rows

    @pl.when(u == 0)
    def _():
        zbuf[...] = jnp.zeros_like(zbuf)

        def zero_copy(c):
            start = pl.multiple_of((cls_blk_ref[1, c] - 1) * tile_rows, tile_rows)
            return pltpu.make_async_copy(zbuf, xs_hbm.at[pl.ds(start, tile_rows)], zsem)

        for c in range(N_CLASSES):
            @pl.when(cls_blk_ref[1, c] > cls_blk_ref[0, c])
            def _():
                zero_copy(c).start()
        for c in range(N_CLASSES):
            @pl.when(cls_blk_ref[1, c] > cls_blk_ref[0, c])
            def _():
                zero_copy(c).wait()

        def gap_copy(blk):
            return pltpu.make_async_copy(zbuf, xs_hbm.at[pl.ds(pl.multiple_of(blk * tile_rows, tile_rows), tile_rows)],
                                         zsem)

        gaps = [(cls_blk_ref[1, g * PAIRS - 1], cls_blk_ref[0, g * PAIRS]) for g in range(1, N_GROUPS)]
        gaps.append((cls_blk_ref[1, N_CLASSES - 1], xs_hbm.shape[0] // tile_rows))
        for lo, hi in gaps:
            lax.fori_loop(lo, hi, lambda blk, carry: (gap_copy(blk).start(), carry)[1], 0)
        for lo, hi in gaps:
            lax.fori_loop(lo, hi, lambda blk, carry: (gap_copy(blk).wait(), carry)[1], 0)

    @pl.when(u >= 2)
    def _():
        _wait_runs(bbuf, slot * sort_rows, sort_rows, xs_hbm, ssem.at[slot])

    scale = 1.0 + sc_ref[...]
    shift = sh_ref[...]

    unroll = abuf.shape[0] // tile_rows

    def sort_chunks(i, carry):
        for k in range(unroll):
            c = i * unroll + k
            stage = k * tile_rows
            rows = pl.ds(pl.multiple_of(c * BLOCK, BLOCK), BLOCK)
            _store_tiles(abuf, stage, SUBLANES, _ln(x1_ref[rows, :]) * scale + shift)
            for r in range(BLOCK):
                dst = pl.multiple_of(slot * sort_rows + lpos_ref[c, 0, r] * SUBLANES, SUBLANES)
                bbuf[pl.ds(dst, SUBLANES), :] = abuf[stage + r * SUBLANES:stage + (r + 1) * SUBLANES, :]
        return carry

    lax.fori_loop(0, chunks // unroll, sort_chunks, 0)
    _start_runs(cnt_ref, lst_ref, run_ref, bbuf, slot * sort_rows, xs_hbm, ssem.at[slot], True)

    @pl.when(u == pl.num_programs(0) - 1)
    def _():
        _wait_runs(bbuf, slot * sort_rows, sort_rows, xs_hbm, ssem.at[slot])

        @pl.when(u >= 1)
        def _():
            _wait_runs(bbuf, (1 - slot) * sort_rows, sort_rows, xs_hbm, ssem.at[1 - slot])


def _stage_chunks(chunks):
    return 2 if chunks % 2 == 0 else 1


def _tile_tab_spec(off, n_tiles):
    return pl.BlockSpec((None, 1, LANES), lambda u: (jnp.minimum(u + off, n_tiles - 1), 0, 0),
                        memory_space=pltpu.SMEM)


def _scatter(x1, mod3, lpos, tile_tabs, cls_blk, max_blocks, seq, chunks):
    n, d = x1.shape
    n_tiles = n // (chunks * BLOCK)
    per_batch = seq // (chunks * BLOCK)
    tile_rows = BLOCK * SUBLANES
    return pl.pallas_call(
        _scatter_body,
        out_shape=jax.ShapeDtypeStruct((max_blocks * tile_rows, LANES), F32),
        grid=(n_tiles,),
        in_specs=[
            pl.BlockSpec(memory_space=pltpu.SMEM),
            _tile_tab_spec(0, n_tiles), _tile_tab_spec(0, n_tiles), _tile_tab_spec(0, n_tiles),
            pl.BlockSpec((chunks, 1, LANES), lambda u: (u, 0, 0), memory_space=pltpu.SMEM),
            pl.BlockSpec((chunks * BLOCK, d), lambda u: (u, 0)),
            pl.BlockSpec((None, 1, d), lambda u: (u // per_batch, 0, 3)),
            pl.BlockSpec((None, 1, d), lambda u: (u // per_batch, 0, 4)),
        ],
        out_specs=pl.BlockSpec(memory_space=pl.ANY),
        scratch_shapes=[
            pltpu.VMEM((_stage_chunks(chunks) * tile_rows, LANES), F32),
            pltpu.VMEM((2 * chunks * tile_rows, LANES), F32),
            pltpu.VMEM((tile_rows, LANES), F32),
            pltpu.SemaphoreType.DMA((2,)),
            pltpu.SemaphoreType.DMA,
        ],
        compiler_params=_params("arbitrary"),
        name="scatter",
    )(cls_blk, *tile_tabs, lpos, x1, mod3, mod3)


def _expert(h, w1, w3, w2):
    hid = jax.nn.silu(_dot(h, w1)) * _dot(h, w3)
    return _dot(hid, w2)


def _moe_body(sgrp_ref, blo_ref, bhi_ref, nsteps_ref, xs_ref, w1_ref, w3_ref, w2_ref, olo_ref, ohi_ref):
    u = pl.program_id(0)
    d = w1_ref.shape[1]
    tile_rows = BLOCK * SUBLANES

    @pl.when(u < nsteps_ref[0])
    def _():
        jobs = [(k, o_ref, e) for k in range(MOE_PACK)
                for o_ref, e in ((olo_ref, blo_ref[MOE_PACK * u + k]), (ohi_ref, bhi_ref[MOE_PACK * u + k]))]
        h = [_load_tiles(xs_ref, k * tile_rows, SUBLANES, d) for k in range(MOE_PACK)]
        up = [(_dot(h[k], w1_ref[e]), _dot(h[k], w3_ref[e])) for k, _, e in jobs]
        hid = [jax.nn.silu(a) * b for a, b in up]
        for (k, o_ref, e), g in zip(jobs, hid):
            _store_tiles(o_ref, k * tile_rows, SUBLANES, _dot(g, w2_ref[e]))

    @pl.when(u >= nsteps_ref[0])
    def _():
        olo_ref[...] = jnp.zeros_like(olo_ref)
        ohi_ref[...] = jnp.zeros_like(ohi_ref)


def _moe(xs, tables, w1, w3, w2):
    sgrp, blo, bhi, n_steps = tables
    max_steps = sgrp.shape[0]
    d, de = w1.shape[1:]
    in_rows = MOE_PACK * BLOCK * SUBLANES
    w1g = w1.reshape(N_GROUPS, EPG, d, de)
    w3g = w3.reshape(N_GROUPS, EPG, d, de)
    w2g = w2.reshape(N_GROUPS, EPG, de, d)

    def w_spec(a):
        return pl.BlockSpec((None,) + a.shape[1:], lambda u, g, lo, hi, n_steps: (g[u], 0, 0, 0),
                            pipeline_mode=pl.Buffered(1))

    grid_spec = pltpu.PrefetchScalarGridSpec(
        num_scalar_prefetch=4,
        grid=(max_steps,),
        in_specs=[
            pl.BlockSpec((in_rows, LANES), lambda u, g, lo, hi, n_steps: (jnp.minimum(u, n_steps[0] - 1), 0)),
            w_spec(w1g), w_spec(w3g), w_spec(w2g),
        ],
        out_specs=[pl.BlockSpec((in_rows, LANES), lambda u, *_: (u, 0))] * 2,
    )
    return pl.pallas_call(
        _moe_body,
        out_shape=[jax.ShapeDtypeStruct((max_steps * in_rows, LANES), F32)] * 2,
        grid_spec=grid_spec,
        compiler_params=pltpu.CompilerParams(dimension_semantics=("arbitrary",), vmem_limit_bytes=MOE_VMEM_LIMIT_BYTES),
        name="moe",
    )(sgrp, blo, bhi, n_steps, xs, w1g, w3g, w2g)


def _final_body(run_ref, cnt_ref, lst_ref, runn_ref, cntn_ref, lstn_ref, lpos_ref, x1_ref, rec_ref, g2_ref,
                lng_ref, lnb_ref, mlo_hbm, mhi_hbm, o_ref, glo, ghi, tlo, thi, gsem):
    u = pl.program_id(0)
    slot = u % 2
    d = x1_ref.shape[-1]
    chunks = x1_ref.shape[0] // BLOCK
    sort_rows = chunks * BLOCK * SUBLANES
    sides = ((mlo_hbm, glo), (mhi_hbm, ghi))

    def start_runs(tabs, s):
        for hbm, buf in sides:
            _start_runs(tabs[1], tabs[2], tabs[0], buf, s * sort_rows, hbm, gsem.at[s], False)

    def wait_runs(s):
        for hbm, buf in sides:
            _wait_runs(buf, s * sort_rows, sort_rows, hbm, gsem.at[s])

    @pl.when(u == 0)
    def _():
        start_runs((run_ref, cnt_ref, lst_ref), 0)

    wait_runs(slot)
    start_runs((runn_ref, cntn_ref, lstn_ref), 1 - slot)
    gain = g2_ref[...]
    ln_g = lng_ref[...]
    ln_b = lnb_ref[...]

    lanes_rows = BLOCK * SUBLANES
    unroll = tlo.shape[0] // lanes_rows

    def finish_chunks(i, carry):
        for k in range(unroll):
            c = i * unroll + k
            stage = k * lanes_rows
            for r in range(BLOCK):
                src = pl.ds(pl.multiple_of(slot * sort_rows + lpos_ref[c, 0, r] * SUBLANES, SUBLANES), SUBLANES)
                tlo[stage + r * SUBLANES:stage + (r + 1) * SUBLANES, :] = glo[src, :]
                thi[stage + r * SUBLANES:stage + (r + 1) * SUBLANES, :] = ghi[src, :]
            rows = pl.ds(pl.multiple_of(c * BLOCK, BLOCK), BLOCK)
            rec = rec_ref[rows, :]
            moe = (rec[:, REC_WLO:REC_WLO + 1] * _load_tiles(tlo, stage, SUBLANES, d)
                   + rec[:, REC_WHI:REC_WHI + 1] * _load_tiles(thi, stage, SUBLANES, d))
            o_ref[rows, :] = _ln(ALPHA * x1_ref[rows, :] + gain * moe) * ln_g + ln_b
        return carry

    lax.fori_loop(0, chunks // unroll, finish_chunks, 0)

    @pl.when(u == pl.num_programs(0) - 1)
    def _():
        wait_runs(1 - slot)


def _final(x1, rec, ms_lo, ms_hi, lpos, tile_tabs, mod3, ln_g, ln_b, seq, chunks):
    n, d = x1.shape
    n_tiles = n // (chunks * BLOCK)
    per_batch = seq // (chunks * BLOCK)
    tile_rows = BLOCK * SUBLANES
    vec = pl.BlockSpec((1, d), lambda u: (0, 0))
    hbm = pl.BlockSpec(memory_space=pl.ANY)
    return pl.pallas_call(
        _final_body,
        out_shape=jax.ShapeDtypeStruct((n, d), F32),
        grid=(n_tiles,),
        in_specs=[
            _tile_tab_spec(0, n_tiles), _tile_tab_spec(0, n_tiles), _tile_tab_spec(0, n_tiles),
            _tile_tab_spec(1, n_tiles), _tile_tab_spec(1, n_tiles), _tile_tab_spec(1, n_tiles),
            pl.BlockSpec((chunks, 1, LANES), lambda u: (u, 0, 0), memory_space=pltpu.SMEM),
            pl.BlockSpec((chunks * BLOCK, d), lambda u: (u, 0)),
            pl.BlockSpec((chunks * BLOCK, ROUTE_W), lambda u: (u, 0)),
            pl.BlockSpec((None, 1, d), lambda u: (u // per_batch, 0, 5)),
            vec, vec, hbm, hbm,
        ],
        out_specs=pl.BlockSpec((chunks * BLOCK, d), lambda u: (u, 0)),
        scratch_shapes=[
            pltpu.VMEM((2 * chunks * tile_rows, LANES), F32),
            pltpu.VMEM((2 * chunks * tile_rows, LANES), F32),
            pltpu.VMEM((_stage_chunks(chunks) * tile_rows, LANES), F32),
            pltpu.VMEM((_stage_chunks(chunks) * tile_rows, LANES), F32),
            pltpu.SemaphoreType.DMA((2,)),
        ],
        compiler_params=_params("arbitrary"),
        name="final",
    )(*tile_tabs, *tile_tabs, lpos, x1, rec, mod3, ln_g.reshape(1, d), ln_b.reshape(1, d), ms_lo, ms_hi)


def _block_tables(tab):
    cls = tab[:, 0]
    pair = cls % PAIRS
    first = jnp.array([i for i in range(EPG) for _ in range(i + 1, EPG)], I32)
    second = jnp.array([j for i in range(EPG) for j in range(i + 1, EPG)], I32)
    return (cls // PAIRS)[::MOE_PACK], first[pair], second[pair], tab[0, 1].reshape(1)


def kernel(x, c, ctx, c_ctx, w_ada, b_ada, w_in, attn_sink, gm_ln_g, gm_ln_b, gm_ws, gm_bs, w_pa, w_pb, w_o,
           ln1_g, ln1_b, router_g_w, router_g_b, router_e_w, router_e_b, moe_w1, moe_w3, moe_w2, ln2_g, ln2_b):
    b, s, d = x.shape
    assert w_ada.shape[0] == DEPTH and s % BLOCK == 0 and s % GRID_W == 0
    mod_rows = 2 * SUBLANES
    assert b + 1 <= mod_rows
    c_rows = jnp.zeros((mod_rows, d), F32).at[:b].set(c).at[b].set(c_ctx)
    mod = _adaln_mod(c_rows, w_ada[0], b_ada[0])
    mod3 = mod.reshape(mod_rows, 1, 6 * d)
    w_in_bf = w_in[0].astype(BF16)
    kv_ctx = _ctx_kv(ctx, mod3, b, w_in_bf)
    tile = next(t for t in (1024, 512, BLOCK) if s % t == 0)
    q, kv, u, vg, gates = _project(x, mod3, w_in_bf, _rope_tables(s), gm_ln_g[0], gm_ln_b[0], tile)
    nb = s // BLOCK
    y_attn = _attend(q, kv, kv_ctx, attn_sink[0], next(p for p in (4, 2, 1) if nb % p == 0))
    bs_rows = jnp.repeat(gm_bs[0].T, GM_HEAD, axis=1)
    w_route = jnp.concatenate([router_g_w[0], jnp.transpose(router_e_w[0], (1, 0, 2)).reshape(d, N_EXPERTS)], axis=1)
    w_route = jnp.pad(w_route, ((0, 0), (0, ROUTE_W - w_route.shape[1])))
    b_route = jnp.pad(jnp.concatenate([router_g_b[0], router_e_b[0].reshape(-1)]), (0, ROUTE_W - N_GROUPS - N_EXPERTS))
    wr_hi = w_route.astype(BF16)
    wr_lo = (w_route - wr_hi.astype(F32)).astype(BF16)
    merge_tile = BLOCK * next(p for p in (4, 2, 1) if nb % p == 0)
    x1, rec, counts = _merge(x, y_attn, u, vg, gates, gm_ws[0].astype(BF16), bs_rows, w_pa[0].astype(BF16),
                             w_pb[0].astype(BF16), w_o[0].astype(BF16), ln1_g[0], ln1_b[0], mod3, wr_hi, wr_lo,
                             b_route.reshape(1, ROUTE_W), merge_tile)
    n = b * s
    x1 = x1.reshape(n, d)
    rec = rec.reshape(n, ROUTE_W)
    max_blocks = n // BLOCK + N_CLASSES - 1 + (N_GROUPS - 1) * (MOE_PACK - 1)
    max_blocks = -(-max_blocks // MOE_PACK) * MOE_PACK
    sort_chunks = next(p for p in (SUBLANES, 4, 2, 1) if nb % p == 0)
    lpos, tab, cls_blk, *tile_tabs = _plan(counts.reshape(n // BLOCK, LANES), rec, max_blocks, sort_chunks)
    tile_tabs = [t.reshape(t.shape[0], 1, LANES) for t in tile_tabs]
    xs = _scatter(x1, mod3, lpos, tile_tabs, cls_blk, max_blocks, s, sort_chunks)
    ms_lo, ms_hi = _moe(xs, _block_tables(tab), moe_w1[0], moe_w3[0], moe_w2[0])
    out = _final(x1, rec, ms_lo, ms_hi, lpos, tile_tabs, mod3, ln2_g[0], ln2_b[0], s, sort_chunks)
    return out.reshape(b, s, d)
```

```python
import functools

import jax
import jax.numpy as jnp
from jax import lax
from jax.experimental import pallas as pl
from jax.experimental.pallas import tpu as pltpu

F32 = jnp.float32
BF16 = jnp.bfloat16
I32 = jnp.int32

GRID_W = 64
HEAD_DIM = 64
Q_HEADS = 8
KV_HEADS = 2
GROUP = Q_HEADS // KV_HEADS
BLOCK = 128
ROPE_BASE = 10000.0
ROPE_FREQS = HEAD_DIM // 4
GM_GROUPS = 8
GM_HEAD = 64
ATTN_W = Q_HEADS * HEAD_DIM
KV_W = KV_HEADS * HEAD_DIM
GM_W = GM_GROUPS * GM_HEAD
N_GROUPS = 4
EPG = 8
N_EXPERTS = N_GROUPS * EPG
PAIRS = EPG * (EPG - 1) // 2
N_CLASSES = N_GROUPS * PAIRS
LN_EPS = 1e-6
NEG_INF = -1e30
DEPTH = 1
ALPHA = (2.0 * DEPTH) ** 0.25

LANES = 128
SUBLANES = 8
VMEM_LIMIT_BYTES = 56 * 1024 * 1024
MOE_VMEM_LIMIT_BYTES = 60 * 1024 * 1024
MOE_PACK = 2
ROUTE_W = LANES
REC_WLO, REC_WHI, REC_CLS, REC_RANK = 0, 1, 2, 3


def _params(*sem):
    return pltpu.CompilerParams(dimension_semantics=sem, vmem_limit_bytes=VMEM_LIMIT_BYTES)


def _ln(x):
    mu = jnp.mean(x, -1, keepdims=True)
    xc = x - mu
    var = jnp.mean(xc * xc, -1, keepdims=True)
    return xc * lax.rsqrt(var + LN_EPS)


def _dot(a, b):
    return jnp.dot(a, b, preferred_element_type=F32)


def _dot_nt(a, b):
    return lax.dot_general(a, b, (((1,), (1,)), ((), ())), preferred_element_type=F32)


def _split(a):
    hi = a.astype(BF16)
    lo = (a - hi.astype(F32)).astype(BF16)
    return hi, lo


def _dot_split(a, b_hi, b_lo):
    a_hi, a_lo = _split(a)
    return _dot(a_hi, b_hi) + _dot(a_hi, b_lo) + _dot(a_lo, b_hi)


def _iota(shape, axis):
    return lax.broadcasted_iota(I32, shape, axis)


def _mod_body(c_ref, w_ref, b_ref, o_ref):
    w_hi, w_lo = _split(w_ref[...])
    o_ref[...] = _dot_split(jax.nn.silu(c_ref[...]), w_hi, w_lo) + b_ref[...]


def _adaln_mod(c_rows, w_ada, b_ada):
    rows, d = c_rows.shape
    width = w_ada.shape[1]
    tn = 6 * LANES
    return pl.pallas_call(
        _mod_body,
        out_shape=jax.ShapeDtypeStruct((rows, width), F32),
        grid=(width // tn,),
        in_specs=[
            pl.BlockSpec((rows, d), lambda j: (0, 0)),
            pl.BlockSpec((d, tn), lambda j: (0, j)),
            pl.BlockSpec((1, tn), lambda j: (0, j)),
        ],
        out_specs=pl.BlockSpec((rows, tn), lambda j: (0, j)),
        compiler_params=_params("arbitrary"),
        name="mod",
    )(c_rows, w_ada, b_ada.reshape(1, width))


def _ctx_body(ctx_ref, sh_ref, sc_ref, w_ref, o_ref):
    h = _ln(ctx_ref[...]) * (1.0 + sc_ref[...]) + sh_ref[...]
    o_ref[...] = _dot(h.astype(BF16), w_ref[...]).astype(BF16)


def _ctx_kv(ctx, mod3, ctx_row, w_in_bf):
    b, c_len, d = ctx.shape
    kvw = 2 * KV_W
    return pl.pallas_call(
        _ctx_body,
        out_shape=jax.ShapeDtypeStruct((b, c_len, kvw), BF16),
        grid=(b,),
        in_specs=[
            pl.BlockSpec((None, c_len, d), lambda i: (i, 0, 0)),
            pl.BlockSpec((None, 1, d), lambda i: (ctx_row, 0, 0)),
            pl.BlockSpec((None, 1, d), lambda i: (ctx_row, 0, 1)),
            pl.BlockSpec((d, kvw), lambda i: (0, ATTN_W // kvw)),
        ],
        out_specs=pl.BlockSpec((None, c_len, kvw), lambda i: (i, 0, 0)),
        compiler_params=_params("arbitrary"),
        name="ctx_kv",
    )(ctx, mod3, mod3, w_in_bf)


def _rope(x, cos, s_up, s_dn):
    return x * cos + pltpu.roll(x, LANES - ROPE_FREQS, 1) * s_up + pltpu.roll(x, ROPE_FREQS, 1) * s_dn


def _project_body(x_ref, sh_ref, sc_ref, w_ref, cos_ref, sup_ref, sdn_ref, lng_ref, lnb_ref, *rest):
    n_side = (len(rest) - 5) // 2
    side_in, (q_ref, kv_ref, u_ref, vg_ref, gate_ref), side_out = rest[:n_side], rest[n_side:n_side + 5], rest[n_side + 5:]
    for src, dst in zip(side_in, side_out):
        dst[...] = src[...].astype(BF16)
    h = (_ln(x_ref[...]) * (1.0 + sc_ref[...]) + sh_ref[...]).astype(BF16)
    cos, s_up, s_dn = cos_ref[...], sup_ref[...], sdn_ref[...]
    q_scale = HEAD_DIM ** -0.5
    k0 = ATTN_W
    u0 = k0 + 2 * KV_W
    vg0 = u0 + GM_W
    gate0 = vg0 + GM_W
    q = _dot(h, w_ref[:, 0:k0])
    for j in range(ATTN_W // LANES):
        sl = slice(j * LANES, (j + 1) * LANES)
        q_ref[:, sl] = (_rope(q[:, sl], cos, s_up, s_dn) * q_scale).astype(BF16)
    kv = _dot(h, w_ref[:, k0:u0])
    kv_ref[:, 0:KV_W] = _rope(kv[:, 0:KV_W], cos, s_up, s_dn).astype(BF16)
    kv_ref[:, KV_W:] = kv[:, KV_W:].astype(BF16)
    u_ref[...] = jax.nn.gelu(_dot(h, w_ref[:, u0:vg0])).astype(BF16)
    vg = _ln(jax.nn.gelu(_dot(h, w_ref[:, vg0:gate0]))) * lng_ref[...] + lnb_ref[...]
    vg_ref[...] = vg.astype(BF16)
    gate_ref[...] = jax.nn.sigmoid(_dot(h, w_ref[:, gate0:])).astype(BF16)


def _project(x, mod3, w_in_bf, rope_tabs, gm_ln_g, gm_ln_b, tile, side):
    b, s, d = x.shape
    in_w = w_in_bf.shape[1]
    gate_w = in_w - (ATTN_W + 2 * KV_W + 2 * GM_W)
    nt = s // tile
    steps = b * nt
    row = lambda width: pl.BlockSpec((None, tile, width), lambda i, t: (i, t, 0))
    tab = pl.BlockSpec((tile, LANES), lambda i, t: (t, 0))
    vec = lambda width: pl.BlockSpec((1, width), lambda i, t: (0, 0))
    side_specs = [pl.BlockSpec((a.shape[0] // steps, a.shape[1]), lambda i, t: (i * nt + t, 0)) for a in side]
    outs = pl.pallas_call(
        _project_body,
        out_shape=[
            jax.ShapeDtypeStruct((b, s, ATTN_W), BF16),
            jax.ShapeDtypeStruct((b, s, 2 * KV_W), BF16),
            jax.ShapeDtypeStruct((b, s, GM_W), BF16),
            jax.ShapeDtypeStruct((b, s, GM_W), BF16),
            jax.ShapeDtypeStruct((b, s, gate_w), BF16),
        ] + [jax.ShapeDtypeStruct(a.shape, BF16) for a in side],
        grid=(b, nt),
        in_specs=[
            row(d),
            pl.BlockSpec((None, 1, d), lambda i, t: (i, 0, 0)),
            pl.BlockSpec((None, 1, d), lambda i, t: (i, 0, 1)),
            pl.BlockSpec((d, in_w), lambda i, t: (0, 0)),
            tab, tab, tab,
            vec(GM_W), vec(GM_W),
        ] + side_specs,
        out_specs=[row(ATTN_W), row(2 * KV_W), row(GM_W), row(GM_W), row(gate_w)] + side_specs,
        compiler_params=_params("arbitrary", "arbitrary"),
        name="project",
    )(x, mod3, mod3, w_in_bf, *rope_tabs, gm_ln_g.reshape(1, GM_W), gm_ln_b.reshape(1, GM_W), *side)
    return outs[:5], outs[5:]


def _rope_tables(s):
    pos = jnp.arange(s, dtype=I32)
    inv = 1.0 / (ROPE_BASE ** (jnp.arange(ROPE_FREQS, dtype=F32) / ROPE_FREQS))
    ang_r = (pos // GRID_W).astype(F32)[:, None] * inv[None, :]
    ang_c = (pos % GRID_W).astype(F32)[:, None] * inv[None, :]
    cr, sr, cc, sn = jnp.cos(ang_r), jnp.sin(ang_r), jnp.cos(ang_c), jnp.sin(ang_c)
    z = jnp.zeros_like(sr)
    reps = LANES // HEAD_DIM
    cos = jnp.tile(jnp.concatenate([cr, cr, cc, cc], -1), (1, reps))
    s_up = jnp.tile(jnp.concatenate([-sr, z, -sn, z], -1), (1, reps))
    s_dn = jnp.tile(jnp.concatenate([z, sr, z, sn], -1), (1, reps))
    return cos, s_up, s_dn


def _attend_body(sink_ref, q_ref, kvp_ref, kvc_ref, kvn_ref, ctx_ref, o_ref):
    i = pl.program_id(1)
    n_steps = pl.num_programs(1)
    per = q_ref.shape[0] // BLOCK
    rows = GROUP * BLOCK
    r = _iota((rows, 3 * BLOCK), 0) & (BLOCK - 1)
    m = _iota((rows, 3 * BLOCK), 1)
    band = (m >= r) & (m <= r + 2 * BLOCK)
    grp = _iota((rows, 1), 0) // BLOCK
    kv_all = jnp.concatenate([kvp_ref[...], kvc_ref[...], kvn_ref[...]], axis=0)
    kv_ctx = ctx_ref[...]
    for t in range(per):
        mask = band
        if t == 0:
            mask = mask & ((m >= BLOCK) | (i > 0))
        if t == per - 1:
            mask = mask & ((m < 2 * BLOCK) | (i < n_steps - 1))
        q = q_ref[t * BLOCK:(t + 1) * BLOCK, :]
        kv_loc = kv_all[t * BLOCK:(t + 3) * BLOCK]
        outs = []
        for h in range(KV_HEADS):
            ks = slice(h * HEAD_DIM, (h + 1) * HEAD_DIM)
            vs = slice(KV_W + h * HEAD_DIM, KV_W + (h + 1) * HEAD_DIM)
            qg = jnp.concatenate(
                [q[:, (h * GROUP + g) * HEAD_DIM:(h * GROUP + g + 1) * HEAD_DIM] for g in range(GROUP)], axis=0)
            sink = jnp.zeros((rows, 1), F32)
            for g in range(GROUP):
                sink = jnp.where(grp == g, sink_ref[h * GROUP + g], sink)
            s_ctx = _dot_nt(qg, kv_ctx[:, ks])
            s_loc = jnp.where(mask, _dot_nt(qg, kv_loc[:, ks]), NEG_INF)
            top = jnp.maximum(jnp.maximum(jnp.max(s_ctx, -1, keepdims=True), jnp.max(s_loc, -1, keepdims=True)), sink)
            e_ctx = jnp.exp(s_ctx - top)
            e_loc = jnp.exp(s_loc - top)
            den = jnp.sum(e_ctx, -1, keepdims=True) + jnp.sum(e_loc, -1, keepdims=True) + jnp.exp(sink - top)
            o = (_dot(e_ctx.astype(BF16), kv_ctx[:, vs]) + _dot(e_loc.astype(BF16), kv_loc[:, vs])) * (1.0 / den)
            outs += [o[g * BLOCK:(g + 1) * BLOCK] for g in range(GROUP)]
        o_ref[t * BLOCK:(t + 1) * BLOCK, :] = jnp.concatenate(outs, axis=1).astype(BF16)


def _attend(q, kv, kv_ctx, sink, per):
    b, s, _ = q.shape
    nb = s // BLOCK
    c_len = kv_ctx.shape[1]
    kvw = 2 * KV_W
    return pl.pallas_call(
        _attend_body,
        out_shape=jax.ShapeDtypeStruct((b, s, ATTN_W), BF16),
        grid=(b, nb // per),
        in_specs=[
            pl.BlockSpec(memory_space=pltpu.SMEM),
            pl.BlockSpec((None, per * BLOCK, ATTN_W), lambda i, j: (i, j, 0)),
            pl.BlockSpec((None, BLOCK, kvw), lambda i, j: (i, jnp.maximum(j * per - 1, 0), 0)),
            pl.BlockSpec((None, per * BLOCK, kvw), lambda i, j: (i, j, 0)),
            pl.BlockSpec((None, BLOCK, kvw), lambda i, j: (i, jnp.minimum((j + 1) * per, nb - 1), 0)),
            pl.BlockSpec((None, c_len, kvw), lambda i, j: (i, 0, 0)),
        ],
        out_specs=pl.BlockSpec((None, per * BLOCK, ATTN_W), lambda i, j: (i, j, 0)),
        compiler_params=_params("arbitrary", "arbitrary"),
        name="attend",
    )(sink, q, kv, kv, kv, kv_ctx)


def _route(logits):
    lane = _iota(logits.shape, 1)
    far = jnp.int32(2 * ROUTE_W)

    def first_max(vals):
        top = jnp.max(vals, -1, keepdims=True)
        return top, jnp.min(jnp.where(vals == top, lane, far), -1, keepdims=True)

    is_g = lane < N_GROUPS
    g_top, g_idx = first_max(jnp.where(is_g, logits, NEG_INF))
    g_w = 1.0 / jnp.sum(jnp.where(is_g, jnp.exp(logits - g_top), 0.0), -1, keepdims=True)
    lo = N_GROUPS + EPG * g_idx
    e_vals = jnp.where((lane >= lo) & (lane < lo + EPG), logits, NEG_INF)
    v1, i1 = first_max(e_vals)
    v2, i2 = first_max(jnp.where(lane == i1, NEG_INF, e_vals))
    t = jnp.exp(v2 - v1)
    w1 = g_w * (1.0 / (1.0 + t))
    w2 = g_w * (t / (1.0 + t))
    swap = i2 < i1
    a = jnp.where(swap, i2, i1) - lo
    b = jnp.where(swap, i1, i2) - lo
    cls = g_idx * PAIRS + ((a * (2 * EPG - 1 - a)) >> 1) + (b - a - 1)
    return jnp.where(swap, w2, w1), jnp.where(swap, w1, w2), cls


def _merge_body(x_ref, ya_ref, u_ref, vg_ref, gate_ref, ws_ref, bs_ref, wpa_ref, wpb_ref, wo_ref,
                lng_ref, lnb_ref, g1_ref, sh2_ref, sc2_ref, wrh_ref, wrl_ref, br_ref, x1_ref, rec_ref, cnt_ref):
    tile, d = x_ref.shape
    lane = _iota((BLOCK, LANES), 1)
    below = (lane < _iota((BLOCK, LANES), 0)).astype(BF16)
    chunks = []
    for c in range(tile // BLOCK):
        rows = slice(c * BLOCK, (c + 1) * BLOCK)
        pieces = []
        for j in range(GM_W // LANES):
            v = vg_ref[rows, j * LANES:(j + 1) * LANES]
            per = LANES // GM_HEAD
            sp = _dot(ws_ref[per * j], v)
            for k in range(1, per):
                sp = jnp.where(lane < k * GM_HEAD, sp, _dot(ws_ref[per * j + k], v))
            pieces.append(sp)
        chunks.append(jnp.concatenate(pieces, axis=1) + bs_ref[...])
    sp = jnp.concatenate(chunks, axis=0)
    y_gm = (u_ref[...].astype(F32) * sp).astype(BF16)
    gate = gate_ref[...].astype(F32)
    y = gate[:, :d] * _dot(ya_ref[...], wpa_ref[...]) + gate[:, d:] * _dot(y_gm, wpb_ref[...])
    mix = _dot(y.astype(BF16), wo_ref[...])
    x1 = _ln(ALPHA * x_ref[...] + g1_ref[...] * mix) * lng_ref[...] + lnb_ref[...]
    x1_ref[...] = x1
    h2 = _ln(x1) * (1.0 + sc2_ref[...]) + sh2_ref[...]
    w_lo, w_hi, cls = _route(_dot_split(h2, wrh_ref[...], wrl_ref[...]) + br_ref[...])
    for c in range(tile // BLOCK):
        rows = slice(c * BLOCK, (c + 1) * BLOCK)
        onehot = lane == cls[rows]
        hot = onehot.astype(BF16)
        rank = jnp.sum(jnp.where(onehot, _dot(below, hot), 0.0), -1, keepdims=True)
        cnt_ref[c] = jnp.sum(hot.astype(F32), 0, keepdims=True)
        rec = jnp.where(lane == REC_WLO, w_lo[rows], 0.0)
        rec = jnp.where(lane == REC_WHI, w_hi[rows], rec)
        rec = jnp.where(lane == REC_CLS, cls[rows].astype(F32), rec)
        rec_ref[rows, :] = jnp.where(lane == REC_RANK, rank, rec)


def _merge(x, y_attn, u, vg, gates, ws_bf, bs_rows, wpa, wpb, wo, ln_g, ln_b, mod3, wr_hi, wr_lo, br, tile):
    b, s, d = x.shape
    nt = s // tile
    per = tile // BLOCK
    row = lambda width: pl.BlockSpec((None, tile, width), lambda i, j: (i, j, 0))
    full = lambda a: pl.BlockSpec(a.shape, lambda i, j: (0,) * a.ndim)
    modv = lambda k: pl.BlockSpec((None, 1, d), lambda i, j: (i, 0, k))
    ln_g = ln_g.reshape(1, d)
    ln_b = ln_b.reshape(1, d)
    return pl.pallas_call(
        _merge_body,
        out_shape=[jax.ShapeDtypeStruct((b, s, d), F32), jax.ShapeDtypeStruct((b, s, ROUTE_W), F32),
                   jax.ShapeDtypeStruct((b * s // BLOCK, 1, LANES), F32)],
        grid=(b, nt),
        in_specs=[
            row(d), row(ATTN_W), row(GM_W), row(GM_W), row(2 * d),
            full(ws_bf), full(bs_rows), full(wpa), full(wpb), full(wo), full(ln_g), full(ln_b),
            modv(2), modv(3), modv(4),
            full(wr_hi), full(wr_lo), full(br),
        ],
        out_specs=[row(d), row(ROUTE_W), pl.BlockSpec((per, 1, LANES), lambda i, j: (i * nt + j, 0, 0))],
        compiler_params=_params("arbitrary", "arbitrary"),
        name="merge",
    )(x, y_attn, u, vg, gates, ws_bf, bs_rows, wpa, wpb, wo, ln_g, ln_b, mod3, mod3, mod3, wr_hi, wr_lo, br)


def _plan_body(cnt_ref, rec_ref, lpos_ref, tab_ref, cblk_ref, run_ref, cntt_ref, lst_ref, base_ref, lstf_ref, *, chunks):
    step = pl.program_id(0)
    n_chunks = cnt_ref.shape[0]
    n_tiles = run_ref.shape[0]
    max_blocks = tab_ref.shape[0]

    @pl.when(step == 0)
    def _():
        cnt = cnt_ref[...]
        total = jnp.sum(cnt, 0, keepdims=True)
        n_blk = jnp.floor((total + (BLOCK - 1)) * (1.0 / BLOCK))
        before = (_iota((LANES, LANES), 0) < _iota((LANES, LANES), 1)).astype(BF16)
        nb_hi, nb_lo = _split(jnp.broadcast_to(n_blk, (SUBLANES, LANES)))
        blk_start = (_dot(nb_hi, before) + _dot(nb_lo, before))[0:1]
        lane1 = _iota((1, LANES), 1)
        packed = blk_start
        shift = jnp.zeros((1, 1), F32)
        for g in range(1, N_GROUPS):
            first = jnp.sum(jnp.where(lane1 == g * PAIRS, blk_start, 0.0), -1, keepdims=True) + shift
            over = first - MOE_PACK * jnp.floor(first * (1.0 / MOE_PACK))
            shift = shift + jnp.where(over > 0.0, MOE_PACK - over, 0.0)
            packed = jnp.where(lane1 >= g * PAIRS, blk_start + shift, packed)
        blk_start = packed
        earlier = (_iota((n_chunks, n_chunks), 1) < _iota((n_chunks, n_chunks), 0)).astype(BF16)
        base_ref[...] = blk_start * float(BLOCK) + _dot(earlier, cnt.astype(BF16))
        blk_end = blk_start + n_blk
        sub = _iota((SUBLANES, LANES), 0)
        cblk_ref[...] = jnp.where(sub == 0, blk_start, jnp.where(sub == 1, blk_end, 0.0)).astype(I32)
        lane = _iota((max_blocks, LANES), 1)
        blk = _iota((max_blocks, LANES), 0).astype(F32)
        is_cls = lane < N_CLASSES
        n_real = jnp.max(jnp.where(is_cls, blk_end, 0.0), -1, keepdims=True)
        u = jnp.minimum(blk, n_real - 1.0)
        cls_of = jnp.sum(jnp.where(is_cls & (blk_end <= u), 1.0, 0.0), -1, keepdims=True)
        n_steps = jnp.floor((n_real + (MOE_PACK - 1)) * (1.0 / MOE_PACK))
        tab = jnp.where(lane == 0, cls_of, jnp.where(lane == 1, n_steps, 0.0))
        tab_ref[...] = tab.astype(I32)
        tile = _iota((n_tiles, n_chunks), 0)
        chunk = _iota((n_tiles, n_chunks), 1)
        owner = ((chunk >= tile * chunks) & (chunk < (tile + 1) * chunks)).astype(BF16)
        cnt_tile = _dot(owner, cnt.astype(BF16))
        ct_hi, ct_lo = _split(cnt_tile)
        local_start = _dot(ct_hi, before) + _dot(ct_lo, before)
        run_ref[...] = base_ref[pl.ds(0, n_tiles, stride=chunks), :].astype(I32)
        cntt_ref[...] = cnt_tile.astype(I32)
        lst_ref[...] = local_start.astype(I32)
        lstf_ref[...] = local_start

    lane = _iota((BLOCK, LANES), 1)
    tile_off = lstf_ref[pl.ds(step, 1), :] - base_ref[pl.ds(step * chunks, 1), :]
    pick_rank = (_iota((2 * SUBLANES, LANES), 1) == REC_RANK).astype(BF16)
    for c in range(chunks):
        rec = rec_ref[c * BLOCK:(c + 1) * BLOCK, :]
        hot = (lane == rec[:, REC_CLS:REC_CLS + 1].astype(I32)).astype(BF16)
        base = base_ref[pl.ds(step * chunks + c, 1), :] + tile_off
        b_hi, b_lo = _split(jnp.broadcast_to(base, (2 * SUBLANES, LANES)))
        lpos = _dot_nt(b_hi, hot) + _dot_nt(b_lo, hot) + _dot_nt(pick_rank, rec.astype(BF16))
        lpos_ref[c] = lpos[0:1].astype(I32)


def _plan(counts, rec, max_blocks, chunks):
    n = rec.shape[0]
    n_chunks = n // BLOCK
    n_tiles = n_chunks // chunks
    tile_tab = jax.ShapeDtypeStruct((n_tiles, LANES), I32)
    whole = lambda rows: pl.BlockSpec((rows, LANES), lambda i: (0, 0))
    return pl.pallas_call(
        functools.partial(_plan_body, chunks=chunks),
        out_shape=[jax.ShapeDtypeStruct((n_chunks, 1, LANES), I32), jax.ShapeDtypeStruct((max_blocks, LANES), I32),
                   jax.ShapeDtypeStruct((SUBLANES, LANES), I32), tile_tab, tile_tab, tile_tab],
        grid=(n_tiles,),
        in_specs=[whole(n_chunks), pl.BlockSpec((chunks * BLOCK, ROUTE_W), lambda i: (i, 0))],
        out_specs=[pl.BlockSpec((chunks, 1, LANES), lambda i: (i, 0, 0)), whole(max_blocks), whole(SUBLANES),
                   whole(n_tiles), whole(n_tiles), whole(n_tiles)],
        scratch_shapes=[pltpu.VMEM((n_chunks, LANES), F32), pltpu.VMEM((n_tiles, LANES), F32)],
        compiler_params=_params("arbitrary"),
        name="plan",
    )(counts, rec)


def _tile_rows(base, j, pitch):
    return pl.ds(base + j, BLOCK, stride=pitch)


def _load_tiles(ref, base, pitch, width):
    return jnp.concatenate([ref[_tile_rows(base, j, pitch), :] for j in range(width // LANES)], axis=1)


def _store_tiles(ref, base, pitch, val):
    for j in range(val.shape[1] // LANES):
        ref[_tile_rows(base, j, pitch), :] = val[:, j * LANES:(j + 1) * LANES]


def _start_runs(cnt_ref, lst_ref, run_ref, buf, buf_base, hbm, sem, to_hbm):
    for c in range(N_CLASSES):
        n = cnt_ref[0, c]

        @pl.when(n > 0)
        def _():
            rows = n * SUBLANES
            v = buf.at[pl.ds(pl.multiple_of(buf_base + lst_ref[0, c] * SUBLANES, SUBLANES), rows)]
            h = hbm.at[pl.ds(pl.multiple_of(run_ref[0, c] * SUBLANES, SUBLANES), rows)]
            (pltpu.make_async_copy(v, h, sem) if to_hbm else pltpu.make_async_copy(h, v, sem)).start(priority=c % 2)


def _wait_runs(buf, buf_base, rows, hbm, sem):
    pltpu.make_async_copy(hbm.at[pl.ds(0, rows)], buf.at[pl.ds(buf_base, rows)], sem).wait()


def _scatter_body(cls_blk_ref, run_ref, cnt_ref, lst_ref, lpos_ref, x1_ref, sh_ref, sc_ref, xs_hbm,
                  abuf, bbuf, zbuf, ssem, zsem):
    u = pl.program_id(0)
    slot = u % 2
    chunks = x1_ref.shape[0] // BLOCK
    tile_rows = BLOCK * SUBLANES
    sort_rows = chunks * tile_rows

    @pl.when(u == 0)
    def _():
        zbuf[...] = jnp.zeros_like(zbuf)

        def zero_copy(c):
            start = pl.multiple_of((cls_blk_ref[1, c] - 1) * tile_rows, tile_rows)
            return pltpu.make_async_copy(zbuf, xs_hbm.at[pl.ds(start, tile_rows)], zsem)

        for c in range(N_CLASSES):
            @pl.when(cls_blk_ref[1, c] > cls_blk_ref[0, c])
            def _():
                zero_copy(c).start()
        for c in range(N_CLASSES):
            @pl.when(cls_blk_ref[1, c] > cls_blk_ref[0, c])
            def _():
                zero_copy(c).wait()

        def gap_copy(blk):
            return pltpu.make_async_copy(zbuf, xs_hbm.at[pl.ds(pl.multiple_of(blk * tile_rows, tile_rows), tile_rows)],
                                         zsem)

        gaps = [(cls_blk_ref[1, g * PAIRS - 1], cls_blk_ref[0, g * PAIRS]) for g in range(1, N_GROUPS)]
        gaps.append((cls_blk_ref[1, N_CLASSES - 1], xs_hbm.shape[0] // tile_rows))
        for lo, hi in gaps:
            lax.fori_loop(lo, hi, lambda blk, carry: (gap_copy(blk).start(), carry)[1], 0)
        for lo, hi in gaps:
            lax.fori_loop(lo, hi, lambda blk, carry: (gap_copy(blk).wait(), carry)[1], 0)

    @pl.when(u >= 2)
    def _():
        _wait_runs(bbuf, slot * sort_rows, sort_rows, xs_hbm, ssem.at[slot])

    scale = 1.0 + sc_ref[...]
    shift = sh_ref[...]

    unroll = abuf.shape[0] // tile_rows

    def sort_chunks(i, carry):
        for k in range(unroll):
            c = i * unroll + k
            stage = k * tile_rows
            rows = pl.ds(pl.multiple_of(c * BLOCK, BLOCK), BLOCK)
            _store_tiles(abuf, stage, SUBLANES, _ln(x1_ref[rows, :]) * scale + shift)
            for r in range(BLOCK):
                dst = pl.multiple_of(slot * sort_rows + lpos_ref[c, 0, r] * SUBLANES, SUBLANES)
                bbuf[pl.ds(dst, SUBLANES), :] = abuf[stage + r * SUBLANES:stage + (r + 1) * SUBLANES, :]
        return carry

    lax.fori_loop(0, chunks // unroll, sort_chunks, 0)
    _start_runs(cnt_ref, lst_ref, run_ref, bbuf, slot * sort_rows, xs_hbm, ssem.at[slot], True)

    @pl.when(u == pl.num_programs(0) - 1)
    def _():
        _wait_runs(bbuf, slot * sort_rows, sort_rows, xs_hbm, ssem.at[slot])

        @pl.when(u >= 1)
        def _():
            _wait_runs(bbuf, (1 - slot) * sort_rows, sort_rows, xs_hbm, ssem.at[1 - slot])


def _stage_chunks(chunks):
    return next(p for p in (4, 2, 1) if chunks % p == 0)


def _tile_tab_spec(off, n_tiles):
    return pl.BlockSpec((None, 1, LANES), lambda u: (jnp.minimum(u + off, n_tiles - 1), 0, 0),
                        memory_space=pltpu.SMEM)


def _scatter(x1, mod3, lpos, tile_tabs, cls_blk, max_blocks, seq, chunks):
    n, d = x1.shape
    n_tiles = n // (chunks * BLOCK)
    per_batch = seq // (chunks * BLOCK)
    tile_rows = BLOCK * SUBLANES
    return pl.pallas_call(
        _scatter_body,
        out_shape=jax.ShapeDtypeStruct((max_blocks * tile_rows, LANES), F32),
        grid=(n_tiles,),
        in_specs=[
            pl.BlockSpec(memory_space=pltpu.SMEM),
            _tile_tab_spec(0, n_tiles), _tile_tab_spec(0, n_tiles), _tile_tab_spec(0, n_tiles),
            pl.BlockSpec((chunks, 1, LANES), lambda u: (u, 0, 0), memory_space=pltpu.SMEM),
            pl.BlockSpec((chunks * BLOCK, d), lambda u: (u, 0)),
            pl.BlockSpec((None, 1, d), lambda u: (u // per_batch, 0, 3)),
            pl.BlockSpec((None, 1, d), lambda u: (u // per_batch, 0, 4)),
        ],
        out_specs=pl.BlockSpec(memory_space=pl.ANY),
        scratch_shapes=[
            pltpu.VMEM((_stage_chunks(chunks) * tile_rows, LANES), F32),
            pltpu.VMEM((2 * chunks * tile_rows, LANES), F32),
            pltpu.VMEM((tile_rows, LANES), F32),
            pltpu.SemaphoreType.DMA((2,)),
            pltpu.SemaphoreType.DMA,
        ],
        compiler_params=_params("arbitrary"),
        name="scatter",
    )(cls_blk, *tile_tabs, lpos, x1, mod3, mod3)


def _expert(h, w1, w3, w2):
    hid = jax.nn.silu(_dot(h, w1)) * _dot(h, w3)
    return _dot(hid, w2)


def _moe_body(sgrp_ref, blo_ref, bhi_ref, nsteps_ref, xs_ref, w1_ref, w3_ref, w2_ref, olo_ref, ohi_ref):
    u = pl.program_id(0)
    d = w1_ref.shape[1]
    tile_rows = BLOCK * SUBLANES

    @pl.when(u < nsteps_ref[0])
    def _():
        jobs = [(k, o_ref, e) for k in range(MOE_PACK)
                for o_ref, e in ((olo_ref, blo_ref[MOE_PACK * u + k]), (ohi_ref, bhi_ref[MOE_PACK * u + k]))]
        h = [_load_tiles(xs_ref, k * tile_rows, SUBLANES, d).astype(BF16) for k in range(MOE_PACK)]
        up = [(_dot(h[k], w1_ref[e]), _dot(h[k], w3_ref[e])) for k, _, e in jobs]
        hid = [(jax.nn.silu(a) * b).astype(BF16) for a, b in up]
        for (k, o_ref, e), g in zip(jobs, hid):
            _store_tiles(o_ref, k * tile_rows, SUBLANES, _dot(g, w2_ref[e]))

    @pl.when(u >= nsteps_ref[0])
    def _():
        olo_ref[...] = jnp.zeros_like(olo_ref)
        ohi_ref[...] = jnp.zeros_like(ohi_ref)


def _moe(xs, tables, w1, w3, w2, d):
    sgrp, blo, bhi, n_steps = tables
    max_steps = sgrp.shape[0]
    de = w1.shape[1]
    in_rows = MOE_PACK * BLOCK * SUBLANES
    w1g = w1.reshape(N_GROUPS, EPG, d, de)
    w3g = w3.reshape(N_GROUPS, EPG, d, de)
    w2g = w2.reshape(N_GROUPS, EPG, de, d)

    def w_spec(a):
        return pl.BlockSpec((None,) + a.shape[1:], lambda u, g, lo, hi, n_steps: (g[u], 0, 0, 0))

    grid_spec = pltpu.PrefetchScalarGridSpec(
        num_scalar_prefetch=4,
        grid=(max_steps,),
        in_specs=[
            pl.BlockSpec((in_rows, LANES), lambda u, g, lo, hi, n_steps: (jnp.minimum(u, n_steps[0] - 1), 0)),
            w_spec(w1g), w_spec(w3g), w_spec(w2g),
        ],
        out_specs=[pl.BlockSpec((in_rows, LANES), lambda u, *_: (u, 0))] * 2,
    )
    return pl.pallas_call(
        _moe_body,
        out_shape=[jax.ShapeDtypeStruct((max_steps * in_rows, LANES), F32)] * 2,
        grid_spec=grid_spec,
        compiler_params=pltpu.CompilerParams(dimension_semantics=("arbitrary",), vmem_limit_bytes=MOE_VMEM_LIMIT_BYTES),
        name="moe",
    )(sgrp, blo, bhi, n_steps, xs, w1g, w3g, w2g)


def _final_body(run_ref, cnt_ref, lst_ref, runn_ref, cntn_ref, lstn_ref, lpos_ref, x1_ref, rec_ref, g2_ref,
                lng_ref, lnb_ref, mlo_hbm, mhi_hbm, o_ref, glo, ghi, tlo, thi, gsem):
    u = pl.program_id(0)
    slot = u % 2
    d = x1_ref.shape[-1]
    chunks = x1_ref.shape[0] // BLOCK
    sort_rows = chunks * BLOCK * SUBLANES
    sides = ((mlo_hbm, glo), (mhi_hbm, ghi))

    def start_runs(tabs, s):
        for hbm, buf in sides:
            _start_runs(tabs[1], tabs[2], tabs[0], buf, s * sort_rows, hbm, gsem.at[s], False)

    def wait_runs(s):
        for hbm, buf in sides:
            _wait_runs(buf, s * sort_rows, sort_rows, hbm, gsem.at[s])

    @pl.when(u == 0)
    def _():
        start_runs((run_ref, cnt_ref, lst_ref), 0)

    wait_runs(slot)
    start_runs((runn_ref, cntn_ref, lstn_ref), 1 - slot)
    gain = g2_ref[...]
    ln_g = lng_ref[...]
    ln_b = lnb_ref[...]

    lanes_rows = BLOCK * SUBLANES
    unroll = tlo.shape[0] // lanes_rows

    def finish_chunks(i, carry):
        for k in range(unroll):
            c = i * unroll + k
            stage = k * lanes_rows
            for r in range(BLOCK):
                src = pl.ds(pl.multiple_of(slot * sort_rows + lpos_ref[c, 0, r] * SUBLANES, SUBLANES), SUBLANES)
                tlo[stage + r * SUBLANES:stage + (r + 1) * SUBLANES, :] = glo[src, :]
                thi[stage + r * SUBLANES:stage + (r + 1) * SUBLANES, :] = ghi[src, :]
            rows = pl.ds(pl.multiple_of(c * BLOCK, BLOCK), BLOCK)
            rec = rec_ref[rows, :]
            moe = (rec[:, REC_WLO:REC_WLO + 1] * _load_tiles(tlo, stage, SUBLANES, d)
                   + rec[:, REC_WHI:REC_WHI + 1] * _load_tiles(thi, stage, SUBLANES, d))
            o_ref[rows, :] = _ln(ALPHA * x1_ref[rows, :] + gain * moe) * ln_g + ln_b
        return carry

    lax.fori_loop(0, chunks // unroll, finish_chunks, 0)

    @pl.when(u == pl.num_programs(0) - 1)
    def _():
        wait_runs(1 - slot)


def _final(x1, rec, ms_lo, ms_hi, lpos, tile_tabs, mod3, ln_g, ln_b, seq, chunks):
    n, d = x1.shape
    n_tiles = n // (chunks * BLOCK)
    per_batch = seq // (chunks * BLOCK)
    tile_rows = BLOCK * SUBLANES
    vec = pl.BlockSpec((1, d), lambda u: (0, 0))
    hbm = pl.BlockSpec(memory_space=pl.ANY)
    return pl.pallas_call(
        _final_body,
        out_shape=jax.ShapeDtypeStruct((n, d), F32),
        grid=(n_tiles,),
        in_specs=[
            _tile_tab_spec(0, n_tiles), _tile_tab_spec(0, n_tiles), _tile_tab_spec(0, n_tiles),
            _tile_tab_spec(1, n_tiles), _tile_tab_spec(1, n_tiles), _tile_tab_spec(1, n_tiles),
            pl.BlockSpec((chunks, 1, LANES), lambda u: (u, 0, 0), memory_space=pltpu.SMEM),
            pl.BlockSpec((chunks * BLOCK, d), lambda u: (u, 0)),
            pl.BlockSpec((chunks * BLOCK, ROUTE_W), lambda u: (u, 0)),
            pl.BlockSpec((None, 1, d), lambda u: (u // per_batch, 0, 5)),
            vec, vec, hbm, hbm,
        ],
        out_specs=pl.BlockSpec((chunks * BLOCK, d), lambda u: (u, 0)),
        scratch_shapes=[
            pltpu.VMEM((2 * chunks * tile_rows, LANES), F32),
            pltpu.VMEM((2 * chunks * tile_rows, LANES), F32),
            pltpu.VMEM((_stage_chunks(chunks) * tile_rows, LANES), F32),
            pltpu.VMEM((_stage_chunks(chunks) * tile_rows, LANES), F32),
            pltpu.SemaphoreType.DMA((2,)),
        ],
        compiler_params=_params("arbitrary"),
        name="final",
    )(*tile_tabs, *tile_tabs, lpos, x1, rec, mod3, ln_g.reshape(1, d), ln_b.reshape(1, d), ms_lo, ms_hi)


def _block_tables(tab):
    cls = tab[:, 0]
    pair = cls % PAIRS
    first = jnp.array([i for i in range(EPG) for _ in range(i + 1, EPG)], I32)
    second = jnp.array([j for i in range(EPG) for j in range(i + 1, EPG)], I32)
    return (cls // PAIRS)[::MOE_PACK], first[pair], second[pair], tab[0, 1].reshape(1)


def kernel(x, c, ctx, c_ctx, w_ada, b_ada, w_in, attn_sink, gm_ln_g, gm_ln_b, gm_ws, gm_bs, w_pa, w_pb, w_o,
           ln1_g, ln1_b, router_g_w, router_g_b, router_e_w, router_e_b, moe_w1, moe_w3, moe_w2, ln2_g, ln2_b):
    b, s, d = x.shape
    assert w_ada.shape[0] == DEPTH and s % BLOCK == 0 and s % GRID_W == 0
    mod_rows = 2 * SUBLANES
    assert b + 1 <= mod_rows
    c_rows = jnp.zeros((mod_rows, d), F32).at[:b].set(c).at[b].set(c_ctx)
    mod = _adaln_mod(c_rows, w_ada[0], b_ada[0])
    mod3 = mod.reshape(mod_rows, 1, 6 * d)
    w_in_bf = w_in[0].astype(BF16)
    kv_ctx = _ctx_kv(ctx, mod3, b, w_in_bf)
    tile = next(t for t in (512, BLOCK) if s % t == 0)
    experts, _, d_expert = moe_w1.shape[1:]
    moe_w = [moe_w1[0].reshape(experts * d, d_expert), moe_w3[0].reshape(experts * d, d_expert),
             moe_w2[0].reshape(experts * d_expert, d)]
    steps = b * (s // tile)
    sliced = all(w.shape[0] % (steps * 2 * SUBLANES) == 0 for w in moe_w)
    (q, kv, u, vg, gates), moe_w_bf = _project(x, mod3, w_in_bf, _rope_tables(s), gm_ln_g[0], gm_ln_b[0], tile,
                                                moe_w if sliced else [])
    if not sliced:
        moe_w_bf = [w.astype(BF16) for w in moe_w]
    nb = s // BLOCK
    y_attn = _attend(q, kv, kv_ctx, attn_sink[0], next(p for p in (4, 2, 1) if nb % p == 0))
    bs_rows = jnp.repeat(gm_bs[0].T, GM_HEAD, axis=1)
    w_route = jnp.concatenate([router_g_w[0], jnp.transpose(router_e_w[0], (1, 0, 2)).reshape(d, N_EXPERTS)], axis=1)
    w_route = jnp.pad(w_route, ((0, 0), (0, ROUTE_W - w_route.shape[1])))
    b_route = jnp.pad(jnp.concatenate([router_g_b[0], router_e_b[0].reshape(-1)]), (0, ROUTE_W - N_GROUPS - N_EXPERTS))
    wr_hi = w_route.astype(BF16)
    wr_lo = (w_route - wr_hi.astype(F32)).astype(BF16)
    merge_tile = BLOCK * next(p for p in (4, 2, 1) if nb % p == 0)
    x1, rec, counts = _merge(x, y_attn, u, vg, gates, gm_ws[0].astype(BF16), bs_rows, w_pa[0].astype(BF16),
                             w_pb[0].astype(BF16), w_o[0].astype(BF16), ln1_g[0], ln1_b[0], mod3, wr_hi, wr_lo,
                             b_route.reshape(1, ROUTE_W), merge_tile)
    n = b * s
    x1 = x1.reshape(n, d)
    rec = rec.reshape(n, ROUTE_W)
    max_blocks = n // BLOCK + N_CLASSES - 1 + (N_GROUPS - 1) * (MOE_PACK - 1)
    max_blocks = -(-max_blocks // MOE_PACK) * MOE_PACK
    sort_chunks = next(p for p in (SUBLANES, 4, 2, 1) if nb % p == 0)
    lpos, tab, cls_blk, *tile_tabs = _plan(counts.reshape(n // BLOCK, LANES), rec, max_blocks, sort_chunks)
    tile_tabs = [t.reshape(t.shape[0], 1, LANES) for t in tile_tabs]
    xs = _scatter(x1, mod3, lpos, tile_tabs, cls_blk, max_blocks, s, sort_chunks)
    ms_lo, ms_hi = _moe(xs, _block_tables(tab), *moe_w_bf, d)
    out = _final(x1, rec, ms_lo, ms_hi, lpos, tile_tabs, mod3, ln2_g[0], ln2_b[0], s, sort_chunks)
    return out.reshape(b, s, d)
```

```python
import functools

import jax
import jax.numpy as jnp
from jax import lax
from jax.experimental import pallas as pl
from jax.experimental.pallas import tpu as pltpu

F32 = jnp.float32
BF16 = jnp.bfloat16
I32 = jnp.int32

GRID_W = 64
HEAD_DIM = 64
Q_HEADS = 8
KV_HEADS = 2
GROUP = Q_HEADS // KV_HEADS
BLOCK = 128
ROPE_BASE = 10000.0
ROPE_FREQS = HEAD_DIM // 4
GM_GROUPS = 8
GM_HEAD = 64
ATTN_W = Q_HEADS * HEAD_DIM
KV_W = KV_HEADS * HEAD_DIM
GM_W = GM_GROUPS * GM_HEAD
N_GROUPS = 4
EPG = 8
N_EXPERTS = N_GROUPS * EPG
PAIRS = EPG * (EPG - 1) // 2
N_CLASSES = N_GROUPS * PAIRS
LN_EPS = 1e-6
NEG_INF = -1e30
DEPTH = 1
ALPHA = (2.0 * DEPTH) ** 0.25

LANES = 128
SUBLANES = 8
VMEM_LIMIT_BYTES = 56 * 1024 * 1024
MOE_VMEM_LIMIT_BYTES = 60 * 1024 * 1024
MOE_PACK = 2
ROUTE_W = LANES
REC_WLO, REC_WHI, REC_CLS, REC_RANK = 0, 1, 2, 3


def _params(*sem):
    return pltpu.CompilerParams(dimension_semantics=sem, vmem_limit_bytes=VMEM_LIMIT_BYTES)


def _ln(x):
    mu = jnp.mean(x, -1, keepdims=True)
    xc = x - mu
    var = jnp.mean(xc * xc, -1, keepdims=True)
    return xc * lax.rsqrt(var + LN_EPS)


def _dot(a, b):
    return jnp.dot(a, b, preferred_element_type=F32)


def _dot_nt(a, b):
    return lax.dot_general(a, b, (((1,), (1,)), ((), ())), preferred_element_type=F32)


def _split(a):
    hi = a.astype(BF16)
    lo = (a - hi.astype(F32)).astype(BF16)
    return hi, lo


def _dot_split(a, b_hi, b_lo):
    a_hi, a_lo = _split(a)
    return _dot(a_hi, b_hi) + _dot(a_hi, b_lo) + _dot(a_lo, b_hi)


def _iota(shape, axis):
    return lax.broadcasted_iota(I32, shape, axis)


def _mod_body(c_ref, w_ref, b_ref, o_ref):
    w_hi, w_lo = _split(w_ref[...])
    o_ref[...] = _dot_split(jax.nn.silu(c_ref[...]), w_hi, w_lo) + b_ref[...]


def _adaln_mod(c_rows, w_ada, b_ada):
    rows, d = c_rows.shape
    width = w_ada.shape[1]
    tn = 6 * LANES
    return pl.pallas_call(
        _mod_body,
        out_shape=jax.ShapeDtypeStruct((rows, width), F32),
        grid=(width // tn,),
        in_specs=[
            pl.BlockSpec((rows, d), lambda j: (0, 0)),
            pl.BlockSpec((d, tn), lambda j: (0, j)),
            pl.BlockSpec((1, tn), lambda j: (0, j)),
        ],
        out_specs=pl.BlockSpec((rows, tn), lambda j: (0, j)),
        compiler_params=_params("arbitrary"),
        name="mod",
    )(c_rows, w_ada, b_ada.reshape(1, width))


def _ctx_body(ctx_ref, sh_ref, sc_ref, w_ref, o_ref):
    h = _ln(ctx_ref[...]) * (1.0 + sc_ref[...]) + sh_ref[...]
    o_ref[...] = _dot(h.astype(BF16), w_ref[...]).astype(BF16)


def _ctx_kv(ctx, mod3, ctx_row, w_in_bf):
    b, c_len, d = ctx.shape
    kvw = 2 * KV_W
    return pl.pallas_call(
        _ctx_body,
        out_shape=jax.ShapeDtypeStruct((b, c_len, kvw), BF16),
        grid=(b,),
        in_specs=[
            pl.BlockSpec((None, c_len, d), lambda i: (i, 0, 0)),
            pl.BlockSpec((None, 1, d), lambda i: (ctx_row, 0, 0)),
            pl.BlockSpec((None, 1, d), lambda i: (ctx_row, 0, 1)),
            pl.BlockSpec((d, kvw), lambda i: (0, ATTN_W // kvw)),
        ],
        out_specs=pl.BlockSpec((None, c_len, kvw), lambda i: (i, 0, 0)),
        compiler_params=_params("arbitrary"),
        name="ctx_kv",
    )(ctx, mod3, mod3, w_in_bf)


def _rope(x, cos, s_up, s_dn):
    return x * cos + pltpu.roll(x, LANES - ROPE_FREQS, 1) * s_up + pltpu.roll(x, ROPE_FREQS, 1) * s_dn


def _project_body(x_ref, sh_ref, sc_ref, w_ref, cos_ref, sup_ref, sdn_ref, lng_ref, lnb_ref, *rest):
    n_side = (len(rest) - 5) // 2
    side_in, (q_ref, kv_ref, u_ref, vg_ref, gate_ref), side_out = rest[:n_side], rest[n_side:n_side + 5], rest[n_side + 5:]
    for src, dst in zip(side_in, side_out):
        dst[...] = src[...].astype(BF16)
    h = (_ln(x_ref[...]) * (1.0 + sc_ref[...]) + sh_ref[...]).astype(BF16)
    cos, s_up, s_dn = cos_ref[...], sup_ref[...], sdn_ref[...]
    q_scale = HEAD_DIM ** -0.5
    k0 = ATTN_W
    u0 = k0 + 2 * KV_W
    vg0 = u0 + GM_W
    gate0 = vg0 + GM_W
    q = _dot(h, w_ref[:, 0:k0])
    for j in range(ATTN_W // LANES):
        sl = slice(j * LANES, (j + 1) * LANES)
        q_ref[:, sl] = (_rope(q[:, sl], cos, s_up, s_dn) * q_scale).astype(BF16)
    kv = _dot(h, w_ref[:, k0:u0])
    kv_ref[:, 0:KV_W] = _rope(kv[:, 0:KV_W], cos, s_up, s_dn).astype(BF16)
    kv_ref[:, KV_W:] = kv[:, KV_W:].astype(BF16)
    u_ref[...] = jax.nn.gelu(_dot(h, w_ref[:, u0:vg0])).astype(BF16)
    vg = _ln(jax.nn.gelu(_dot(h, w_ref[:, vg0:gate0]))) * lng_ref[...] + lnb_ref[...]
    vg_ref[...] = vg.astype(BF16)
    gate_ref[...] = jax.nn.sigmoid(_dot(h, w_ref[:, gate0:])).astype(BF16)


def _project(x, mod3, w_in_bf, rope_tabs, gm_ln_g, gm_ln_b, tile, side):
    b, s, d = x.shape
    in_w = w_in_bf.shape[1]
    gate_w = in_w - (ATTN_W + 2 * KV_W + 2 * GM_W)
    nt = s // tile
    steps = b * nt
    row = lambda width: pl.BlockSpec((None, tile, width), lambda i, t: (i, t, 0))
    tab = pl.BlockSpec((tile, LANES), lambda i, t: (t, 0))
    vec = lambda width: pl.BlockSpec((1, width), lambda i, t: (0, 0))
    side_specs = [pl.BlockSpec((a.shape[0] // steps, a.shape[1]), lambda i, t: (i * nt + t, 0)) for a in side]
    outs = pl.pallas_call(
        _project_body,
        out_shape=[
            jax.ShapeDtypeStruct((b, s, ATTN_W), BF16),
            jax.ShapeDtypeStruct((b, s, 2 * KV_W), BF16),
            jax.ShapeDtypeStruct((b, s, GM_W), BF16),
            jax.ShapeDtypeStruct((b, s, GM_W), BF16),
            jax.ShapeDtypeStruct((b, s, gate_w), BF16),
        ] + [jax.ShapeDtypeStruct(a.shape, BF16) for a in side],
        grid=(b, nt),
        in_specs=[
            row(d),
            pl.BlockSpec((None, 1, d), lambda i, t: (i, 0, 0)),
            pl.BlockSpec((None, 1, d), lambda i, t: (i, 0, 1)),
            pl.BlockSpec((d, in_w), lambda i, t: (0, 0)),
            tab, tab, tab,
            vec(GM_W), vec(GM_W),
        ] + side_specs,
        out_specs=[row(ATTN_W), row(2 * KV_W), row(GM_W), row(GM_W), row(gate_w)] + side_specs,
        compiler_params=_params("arbitrary", "arbitrary"),
        name="project",
    )(x, mod3, mod3, w_in_bf, *rope_tabs, gm_ln_g.reshape(1, GM_W), gm_ln_b.reshape(1, GM_W), *side)
    return outs[:5], outs[5:]


def _rope_tables(s):
    pos = jnp.arange(s, dtype=I32)
    inv = 1.0 / (ROPE_BASE ** (jnp.arange(ROPE_FREQS, dtype=F32) / ROPE_FREQS))
    ang_r = (pos // GRID_W).astype(F32)[:, None] * inv[None, :]
    ang_c = (pos % GRID_W).astype(F32)[:, None] * inv[None, :]
    cr, sr, cc, sn = jnp.cos(ang_r), jnp.sin(ang_r), jnp.cos(ang_c), jnp.sin(ang_c)
    z = jnp.zeros_like(sr)
    reps = LANES // HEAD_DIM
    cos = jnp.tile(jnp.concatenate([cr, cr, cc, cc], -1), (1, reps))
    s_up = jnp.tile(jnp.concatenate([-sr, z, -sn, z], -1), (1, reps))
    s_dn = jnp.tile(jnp.concatenate([z, sr, z, sn], -1), (1, reps))
    return cos, s_up, s_dn


def _attend_body(sink_ref, q_ref, kvp_ref, kvc_ref, kvn_ref, ctx_ref, o_ref):
    i = pl.program_id(1)
    n_steps = pl.num_programs(1)
    per = q_ref.shape[0] // BLOCK
    rows = GROUP * BLOCK
    r = _iota((rows, 3 * BLOCK), 0) & (BLOCK - 1)
    m = _iota((rows, 3 * BLOCK), 1)
    band = (m >= r) & (m <= r + 2 * BLOCK)
    grp = _iota((rows, 1), 0) // BLOCK
    kv_all = jnp.concatenate([kvp_ref[...], kvc_ref[...], kvn_ref[...]], axis=0)
    kv_ctx = ctx_ref[...]
    for t in range(per):
        mask = band
        if t == 0:
            mask = mask & ((m >= BLOCK) | (i > 0))
        if t == per - 1:
            mask = mask & ((m < 2 * BLOCK) | (i < n_steps - 1))
        q = q_ref[t * BLOCK:(t + 1) * BLOCK, :]
        kv_loc = kv_all[t * BLOCK:(t + 3) * BLOCK]
        scores = []
        for h in range(KV_HEADS):
            ks = slice(h * HEAD_DIM, (h + 1) * HEAD_DIM)
            qg = jnp.concatenate(
                [q[:, (h * GROUP + g) * HEAD_DIM:(h * GROUP + g + 1) * HEAD_DIM] for g in range(GROUP)], axis=0)
            scores.append((_dot_nt(qg, kv_ctx[:, ks]), jnp.where(mask, _dot_nt(qg, kv_loc[:, ks]), NEG_INF)))
        probs = []
        for h, (s_ctx, s_loc) in enumerate(scores):
            sink = jnp.zeros((rows, 1), F32)
            for g in range(GROUP):
                sink = jnp.where(grp == g, sink_ref[h * GROUP + g], sink)
            top = jnp.maximum(jnp.maximum(jnp.max(s_ctx, -1, keepdims=True), jnp.max(s_loc, -1, keepdims=True)), sink)
            e_ctx = jnp.exp(s_ctx - top)
            e_loc = jnp.exp(s_loc - top)
            den = jnp.sum(e_ctx, -1, keepdims=True) + jnp.sum(e_loc, -1, keepdims=True) + jnp.exp(sink - top)
            probs.append((e_ctx.astype(BF16), e_loc.astype(BF16), 1.0 / den))
        outs = []
        for h, (e_ctx, e_loc, inv) in enumerate(probs):
            vs = slice(KV_W + h * HEAD_DIM, KV_W + (h + 1) * HEAD_DIM)
            o = (_dot(e_ctx, kv_ctx[:, vs]) + _dot(e_loc, kv_loc[:, vs])) * inv
            outs += [o[g * BLOCK:(g + 1) * BLOCK] for g in range(GROUP)]
        o_ref[t * BLOCK:(t + 1) * BLOCK, :] = jnp.concatenate(outs, axis=1).astype(BF16)


def _attend(q, kv, kv_ctx, sink, per):
    b, s, _ = q.shape
    nb = s // BLOCK
    c_len = kv_ctx.shape[1]
    kvw = 2 * KV_W
    return pl.pallas_call(
        _attend_body,
        out_shape=jax.ShapeDtypeStruct((b, s, ATTN_W), BF16),
        grid=(b, nb // per),
        in_specs=[
            pl.BlockSpec(memory_space=pltpu.SMEM),
            pl.BlockSpec((None, per * BLOCK, ATTN_W), lambda i, j: (i, j, 0)),
            pl.BlockSpec((None, BLOCK, kvw), lambda i, j: (i, jnp.maximum(j * per - 1, 0), 0)),
            pl.BlockSpec((None, per * BLOCK, kvw), lambda i, j: (i, j, 0)),
            pl.BlockSpec((None, BLOCK, kvw), lambda i, j: (i, jnp.minimum((j + 1) * per, nb - 1), 0)),
            pl.BlockSpec((None, c_len, kvw), lambda i, j: (i, 0, 0)),
        ],
        out_specs=pl.BlockSpec((None, per * BLOCK, ATTN_W), lambda i, j: (i, j, 0)),
        compiler_params=_params("arbitrary", "arbitrary"),
        name="attend",
    )(sink, q, kv, kv, kv, kv_ctx)


def _route(logits):
    lane = _iota(logits.shape, 1)
    far = jnp.int32(2 * ROUTE_W)

    def first_max(vals):
        top = jnp.max(vals, -1, keepdims=True)
        return top, jnp.min(jnp.where(vals == top, lane, far), -1, keepdims=True)

    is_g = lane < N_GROUPS
    g_top, g_idx = first_max(jnp.where(is_g, logits, NEG_INF))
    g_w = 1.0 / jnp.sum(jnp.where(is_g, jnp.exp(logits - g_top), 0.0), -1, keepdims=True)
    lo = N_GROUPS + EPG * g_idx
    e_vals = jnp.where((lane >= lo) & (lane < lo + EPG), logits, NEG_INF)
    v1, i1 = first_max(e_vals)
    v2, i2 = first_max(jnp.where(lane == i1, NEG_INF, e_vals))
    t = jnp.exp(v2 - v1)
    w1 = g_w * (1.0 / (1.0 + t))
    w2 = g_w * (t / (1.0 + t))
    swap = i2 < i1
    a = jnp.where(swap, i2, i1) - lo
    b = jnp.where(swap, i1, i2) - lo
    cls = g_idx * PAIRS + ((a * (2 * EPG - 1 - a)) >> 1) + (b - a - 1)
    return jnp.where(swap, w2, w1), jnp.where(swap, w1, w2), cls


def _merge_body(x_ref, ya_ref, u_ref, vg_ref, gate_ref, ws_ref, bs_ref, wpa_ref, wpb_ref, wo_ref,
                lng_ref, lnb_ref, g1_ref, sh2_ref, sc2_ref, wrh_ref, wrl_ref, br_ref, x1_ref, rec_ref, cnt_ref):
    tile, d = x_ref.shape
    lane = _iota((BLOCK, LANES), 1)
    below = (lane < _iota((BLOCK, LANES), 0)).astype(BF16)
    chunks = []
    for c in range(tile // BLOCK):
        rows = slice(c * BLOCK, (c + 1) * BLOCK)
        pieces = []
        for j in range(GM_W // LANES):
            v = vg_ref[rows, j * LANES:(j + 1) * LANES]
            per = LANES // GM_HEAD
            sp = _dot(ws_ref[per * j], v)
            for k in range(1, per):
                sp = jnp.where(lane < k * GM_HEAD, sp, _dot(ws_ref[per * j + k], v))
            pieces.append(sp)
        chunks.append(jnp.concatenate(pieces, axis=1) + bs_ref[...])
    sp = jnp.concatenate(chunks, axis=0)
    y_gm = (u_ref[...].astype(F32) * sp).astype(BF16)
    gate = gate_ref[...].astype(F32)
    y = gate[:, :d] * _dot(ya_ref[...], wpa_ref[...]) + gate[:, d:] * _dot(y_gm, wpb_ref[...])
    mix = _dot(y.astype(BF16), wo_ref[...])
    x1 = _ln(ALPHA * x_ref[...] + g1_ref[...] * mix) * lng_ref[...] + lnb_ref[...]
    x1_ref[...] = x1
    h2 = _ln(x1) * (1.0 + sc2_ref[...]) + sh2_ref[...]
    w_lo, w_hi, cls = _route(_dot_split(h2, wrh_ref[...], wrl_ref[...]) + br_ref[...])
    for c in range(tile // BLOCK):
        rows = slice(c * BLOCK, (c + 1) * BLOCK)
        onehot = lane == cls[rows]
        hot = onehot.astype(BF16)
        rank = jnp.sum(jnp.where(onehot, _dot(below, hot), 0.0), -1, keepdims=True)
        cnt_ref[c] = jnp.sum(hot.astype(F32), 0, keepdims=True)
        rec = jnp.where(lane == REC_WLO, w_lo[rows], 0.0)
        rec = jnp.where(lane == REC_WHI, w_hi[rows], rec)
        rec = jnp.where(lane == REC_CLS, cls[rows].astype(F32), rec)
        rec_ref[rows, :] = jnp.where(lane == REC_RANK, rank, rec)


def _merge(x, y_attn, u, vg, gates, ws_bf, bs_rows, wpa, wpb, wo, ln_g, ln_b, mod3, wr_hi, wr_lo, br, tile):
    b, s, d = x.shape
    nt = s // tile
    per = tile // BLOCK
    row = lambda width: pl.BlockSpec((None, tile, width), lambda i, j: (i, j, 0))
    full = lambda a: pl.BlockSpec(a.shape, lambda i, j: (0,) * a.ndim)
    modv = lambda k: pl.BlockSpec((None, 1, d), lambda i, j: (i, 0, k))
    ln_g = ln_g.reshape(1, d)
    ln_b = ln_b.reshape(1, d)
    return pl.pallas_call(
        _merge_body,
        out_shape=[jax.ShapeDtypeStruct((b, s, d), F32), jax.ShapeDtypeStruct((b, s, ROUTE_W), F32),
                   jax.ShapeDtypeStruct((b * s // BLOCK, 1, LANES), F32)],
        grid=(b, nt),
        in_specs=[
            row(d), row(ATTN_W), row(GM_W), row(GM_W), row(2 * d),
            full(ws_bf), full(bs_rows), full(wpa), full(wpb), full(wo), full(ln_g), full(ln_b),
            modv(2), modv(3), modv(4),
            full(wr_hi), full(wr_lo), full(br),
        ],
        out_specs=[row(d), row(ROUTE_W), pl.BlockSpec((per, 1, LANES), lambda i, j: (i * nt + j, 0, 0))],
        compiler_params=_params("arbitrary", "arbitrary"),
        name="merge",
    )(x, y_attn, u, vg, gates, ws_bf, bs_rows, wpa, wpb, wo, ln_g, ln_b, mod3, mod3, mod3, wr_hi, wr_lo, br)


def _plan_body(cnt_ref, rec_ref, lpos_ref, tab_ref, cblk_ref, run_ref, cntt_ref, lst_ref, base_ref, lstf_ref, *, chunks):
    step = pl.program_id(0)
    n_chunks = cnt_ref.shape[0]
    n_tiles = run_ref.shape[0]
    max_blocks = tab_ref.shape[0]

    @pl.when(step == 0)
    def _():
        cnt = cnt_ref[...]
        total = jnp.sum(cnt, 0, keepdims=True)
        n_blk = jnp.floor((total + (BLOCK - 1)) * (1.0 / BLOCK))
        before = (_iota((LANES, LANES), 0) < _iota((LANES, LANES), 1)).astype(BF16)
        nb_hi, nb_lo = _split(jnp.broadcast_to(n_blk, (SUBLANES, LANES)))
        blk_start = (_dot(nb_hi, before) + _dot(nb_lo, before))[0:1]
        lane1 = _iota((1, LANES), 1)
        packed = blk_start
        shift = jnp.zeros((1, 1), F32)
        for g in range(1, N_GROUPS):
            first = jnp.sum(jnp.where(lane1 == g * PAIRS, blk_start, 0.0), -1, keepdims=True) + shift
            over = first - MOE_PACK * jnp.floor(first * (1.0 / MOE_PACK))
            shift = shift + jnp.where(over > 0.0, MOE_PACK - over, 0.0)
            packed = jnp.where(lane1 >= g * PAIRS, blk_start + shift, packed)
        blk_start = packed
        earlier = (_iota((n_chunks, n_chunks), 1) < _iota((n_chunks, n_chunks), 0)).astype(BF16)
        base_ref[...] = blk_start * float(BLOCK) + _dot(earlier, cnt.astype(BF16))
        blk_end = blk_start + n_blk
        sub = _iota((SUBLANES, LANES), 0)
        cblk_ref[...] = jnp.where(sub == 0, blk_start, jnp.where(sub == 1, blk_end, 0.0)).astype(I32)
        lane = _iota((max_blocks, LANES), 1)
        blk = _iota((max_blocks, LANES), 0).astype(F32)
        is_cls = lane < N_CLASSES
        n_real = jnp.max(jnp.where(is_cls, blk_end, 0.0), -1, keepdims=True)
        u = jnp.minimum(blk, n_real - 1.0)
        cls_of = jnp.sum(jnp.where(is_cls & (blk_end <= u), 1.0, 0.0), -1, keepdims=True)
        n_steps = jnp.floor((n_real + (MOE_PACK - 1)) * (1.0 / MOE_PACK))
        tab = jnp.where(lane == 0, cls_of, jnp.where(lane == 1, n_steps, 0.0))
        tab_ref[...] = tab.astype(I32)
        tile = _iota((n_tiles, n_chunks), 0)
        chunk = _iota((n_tiles, n_chunks), 1)
        owner = ((chunk >= tile * chunks) & (chunk < (tile + 1) * chunks)).astype(BF16)
        cnt_tile = _dot(owner, cnt.astype(BF16))
        ct_hi, ct_lo = _split(cnt_tile)
        local_start = _dot(ct_hi, before) + _dot(ct_lo, before)
        run_ref[...] = base_ref[pl.ds(0, n_tiles, stride=chunks), :].astype(I32)
        cntt_ref[...] = cnt_tile.astype(I32)
        lst_ref[...] = local_start.astype(I32)
        lstf_ref[...] = local_start

    lane = _iota((BLOCK, LANES), 1)
    tile_off = lstf_ref[pl.ds(step, 1), :] - base_ref[pl.ds(step * chunks, 1), :]
    pick_rank = (_iota((2 * SUBLANES, LANES), 1) == REC_RANK).astype(BF16)
    for c in range(chunks):
        rec = rec_ref[c * BLOCK:(c + 1) * BLOCK, :]
        hot = (lane == rec[:, REC_CLS:REC_CLS + 1].astype(I32)).astype(BF16)
        base = base_ref[pl.ds(step * chunks + c, 1), :] + tile_off
        b_hi, b_lo = _split(jnp.broadcast_to(base, (2 * SUBLANES, LANES)))
        lpos = _dot_nt(b_hi, hot) + _dot_nt(b_lo, hot) + _dot_nt(pick_rank, rec.astype(BF16))
        lpos_ref[c] = lpos[0:1].astype(I32)


def _plan(counts, rec, max_blocks, chunks):
    n = rec.shape[0]
    n_chunks = n // BLOCK
    n_tiles = n_chunks // chunks
    tile_tab = jax.ShapeDtypeStruct((n_tiles, LANES), I32)
    whole = lambda rows: pl.BlockSpec((rows, LANES), lambda i: (0, 0))
    return pl.pallas_call(
        functools.partial(_plan_body, chunks=chunks),
        out_shape=[jax.ShapeDtypeStruct((n_chunks, 1, LANES), I32), jax.ShapeDtypeStruct((max_blocks, LANES), I32),
                   jax.ShapeDtypeStruct((SUBLANES, LANES), I32), tile_tab, tile_tab, tile_tab],
        grid=(n_tiles,),
        in_specs=[whole(n_chunks), pl.BlockSpec((chunks * BLOCK, ROUTE_W), lambda i: (i, 0))],
        out_specs=[pl.BlockSpec((chunks, 1, LANES), lambda i: (i, 0, 0)), whole(max_blocks), whole(SUBLANES),
                   whole(n_tiles), whole(n_tiles), whole(n_tiles)],
        scratch_shapes=[pltpu.VMEM((n_chunks, LANES), F32), pltpu.VMEM((n_tiles, LANES), F32)],
        compiler_params=_params("arbitrary"),
        name="plan",
    )(counts, rec)


def _tile_rows(base, j, pitch):
    return pl.ds(base + j, BLOCK, stride=pitch)


def _load_tiles(ref, base, pitch, width):
    return jnp.concatenate([ref[_tile_rows(base, j, pitch), :] for j in range(width // LANES)], axis=1)


def _store_tiles(ref, base, pitch, val):
    for j in range(val.shape[1] // LANES):
        ref[_tile_rows(base, j, pitch), :] = val[:, j * LANES:(j + 1) * LANES]


RUN_GROUP = 4


def _start_runs(cnt_ref, lst_ref, run_ref, streams, sem, to_hbm):
    def issue(c):
        for buf, base, hbm, pitch in streams:
            rows = cnt_ref[0, c] * pitch
            aligned = (lambda row: pl.multiple_of(row, pitch)) if pitch > 1 else (lambda row: row)
            v = buf.at[pl.ds(aligned(base + lst_ref[0, c] * pitch), rows)]
            h = hbm.at[pl.ds(aligned(run_ref[0, c] * pitch), rows)]
            (pltpu.make_async_copy(v, h, sem) if to_hbm else pltpu.make_async_copy(h, v, sem)).start(priority=c % 2)

    for c0 in range(0, N_CLASSES, RUN_GROUP):
        group = range(c0, min(c0 + RUN_GROUP, N_CLASSES))
        full = cnt_ref[0, c0] > 0
        for c in group[1:]:
            full &= cnt_ref[0, c] > 0

        @pl.when(full)
        def _():
            for c in group:
                issue(c)

        @pl.when(jnp.logical_not(full))
        def _():
            for c in group:
                pl.when(cnt_ref[0, c] > 0)(functools.partial(issue, c))


def _wait_runs(streams, tokens, sem):
    for buf, base, hbm, pitch in streams:
        rows = tokens * pitch
        pltpu.make_async_copy(hbm.at[pl.ds(0, rows)], buf.at[pl.ds(base, rows)], sem).wait()


def _scatter_body(cls_blk_ref, run_ref, cnt_ref, lst_ref, lpos_ref, x1_ref, rec_ref, sh_ref, sc_ref, xs_hbm, meta_hbm,
                  abuf, bbuf, mbuf, zbuf, ssem, zsem):
    u = pl.program_id(0)
    slot = u % 2
    tokens = x1_ref.shape[0]
    chunks = tokens // BLOCK
    tile_rows = BLOCK * SUBLANES
    sort_rows = chunks * tile_rows

    def streams(s):
        return ((bbuf, s * sort_rows, xs_hbm, SUBLANES), (mbuf, s * tokens, meta_hbm, 1))

    @pl.when(u == 0)
    def _():
        zbuf[...] = jnp.zeros_like(zbuf)

        def zero_block(blk, start):
            for hbm, rows in ((xs_hbm, tile_rows), (meta_hbm, BLOCK)):
                cp = pltpu.make_async_copy(zbuf.at[pl.ds(0, rows)], hbm.at[pl.ds(pl.multiple_of(blk * rows, rows), rows)],
                                           zsem)
                cp.start() if start else cp.wait()

        for start in (True, False):
            for c in range(N_CLASSES):
                pl.when(cls_blk_ref[1, c] > cls_blk_ref[0, c])(functools.partial(zero_block, cls_blk_ref[1, c] - 1, start))
        gaps = [(cls_blk_ref[1, g * PAIRS - 1], cls_blk_ref[0, g * PAIRS]) for g in range(1, N_GROUPS)]
        gaps.append((cls_blk_ref[1, N_CLASSES - 1], xs_hbm.shape[0] // tile_rows))
        for start in (True, False):
            for lo, hi in gaps:
                lax.fori_loop(lo, hi, lambda blk, carry: (zero_block(blk, start), carry)[1], 0)

    @pl.when(u >= 2)
    def _():
        _wait_runs(streams(slot), tokens, ssem.at[slot])

    scale = 1.0 + sc_ref[...]
    shift = sh_ref[...]

    unroll = abuf.shape[0] // tile_rows

    def sort_chunks(i, carry):
        for k in range(unroll):
            c = i * unroll + k
            stage = k * tile_rows
            rows = pl.ds(pl.multiple_of(c * BLOCK, BLOCK), BLOCK)
            _store_tiles(abuf, stage, SUBLANES, _ln(x1_ref[rows, :]) * scale + shift)
            rec = rec_ref[rows, :]
            for r in range(BLOCK):
                p = lpos_ref[c, 0, r]
                dst = pl.multiple_of(slot * sort_rows + p * SUBLANES, SUBLANES)
                bbuf[pl.ds(dst, SUBLANES), :] = abuf[stage + r * SUBLANES:stage + (r + 1) * SUBLANES, :]
                mbuf[pl.ds(slot * tokens + p, 1), :] = rec[r:r + 1, :]
        return carry

    lax.fori_loop(0, chunks // unroll, sort_chunks, 0)
    _start_runs(cnt_ref, lst_ref, run_ref, streams(slot), ssem.at[slot], True)

    @pl.when(u == pl.num_programs(0) - 1)
    def _():
        _wait_runs(streams(slot), tokens, ssem.at[slot])

        @pl.when(u >= 1)
        def _():
            _wait_runs(streams(1 - slot), tokens, ssem.at[1 - slot])


def _stage_chunks(chunks):
    return next(p for p in (4, 2, 1) if chunks % p == 0)


def _tile_tab_spec(off, n_tiles):
    return pl.BlockSpec((None, 1, LANES), lambda u: (jnp.minimum(u + off, n_tiles - 1), 0, 0),
                        memory_space=pltpu.SMEM)


def _scatter(x1, rec, mod3, lpos, tile_tabs, cls_blk, max_blocks, seq, chunks):
    n, d = x1.shape
    n_tiles = n // (chunks * BLOCK)
    per_batch = seq // (chunks * BLOCK)
    tile_rows = BLOCK * SUBLANES
    hbm = pl.BlockSpec(memory_space=pl.ANY)
    return pl.pallas_call(
        _scatter_body,
        out_shape=[jax.ShapeDtypeStruct((max_blocks * tile_rows, LANES), F32),
                   jax.ShapeDtypeStruct((max_blocks * BLOCK, ROUTE_W), F32)],
        grid=(n_tiles,),
        in_specs=[
            pl.BlockSpec(memory_space=pltpu.SMEM),
            _tile_tab_spec(0, n_tiles), _tile_tab_spec(0, n_tiles), _tile_tab_spec(0, n_tiles),
            pl.BlockSpec((chunks, 1, LANES), lambda u: (u, 0, 0), memory_space=pltpu.SMEM),
            pl.BlockSpec((chunks * BLOCK, d), lambda u: (u, 0)),
            pl.BlockSpec((chunks * BLOCK, ROUTE_W), lambda u: (u, 0)),
            pl.BlockSpec((None, 1, d), lambda u: (u // per_batch, 0, 3)),
            pl.BlockSpec((None, 1, d), lambda u: (u // per_batch, 0, 4)),
        ],
        out_specs=[hbm, hbm],
        scratch_shapes=[
            pltpu.VMEM((_stage_chunks(chunks) * tile_rows, LANES), F32),
            pltpu.VMEM((2 * chunks * tile_rows, LANES), F32),
            pltpu.VMEM((2 * chunks * BLOCK, ROUTE_W), F32),
            pltpu.VMEM((tile_rows, LANES), F32),
            pltpu.SemaphoreType.DMA((2,)),
            pltpu.SemaphoreType.DMA,
        ],
        compiler_params=_params("arbitrary"),
        name="scatter",
    )(cls_blk, *tile_tabs, lpos, x1, rec, mod3, mod3)


def _expert(h, w1, w3, w2):
    hid = jax.nn.silu(_dot(h, w1)) * _dot(h, w3)
    return _dot(hid, w2)


def _moe_body(sgrp_ref, blo_ref, bhi_ref, nsteps_ref, xs_ref, meta_ref, w1_ref, w3_ref, w2_ref, o_ref):
    u = pl.program_id(0)
    d = w1_ref.shape[1]
    tile_rows = BLOCK * SUBLANES

    @pl.when(u < nsteps_ref[0])
    def _():
        jobs = [(k, e) for k in range(MOE_PACK) for e in (blo_ref[MOE_PACK * u + k], bhi_ref[MOE_PACK * u + k])]
        h = [_load_tiles(xs_ref, k * tile_rows, SUBLANES, d).astype(BF16) for k in range(MOE_PACK)]
        up = [(_dot(h[k], w1_ref[e]), _dot(h[k], w3_ref[e])) for k, e in jobs]
        hid = [(jax.nn.silu(a) * b).astype(BF16) for a, b in up]
        y = [_dot(g, w2_ref[e]) for (k, e), g in zip(jobs, hid)]
        for k in range(MOE_PACK):
            meta = meta_ref[k * BLOCK:(k + 1) * BLOCK, :]
            _store_tiles(o_ref, k * tile_rows, SUBLANES,
                         meta[:, REC_WLO:REC_WLO + 1] * y[2 * k] + meta[:, REC_WHI:REC_WHI + 1] * y[2 * k + 1])

    @pl.when(u >= nsteps_ref[0])
    def _():
        o_ref[...] = jnp.zeros_like(o_ref)


def _moe(xs, meta, tables, w1, w3, w2, d):
    sgrp, blo, bhi, n_steps = tables
    max_steps = sgrp.shape[0]
    de = w1.shape[1]
    in_rows = MOE_PACK * BLOCK * SUBLANES
    w1g = w1.reshape(N_GROUPS, EPG, d, de)
    w3g = w3.reshape(N_GROUPS, EPG, d, de)
    w2g = w2.reshape(N_GROUPS, EPG, de, d)

    def w_spec(a):
        return pl.BlockSpec((None,) + a.shape[1:], lambda u, g, lo, hi, n_steps: (g[u], 0, 0, 0))

    grid_spec = pltpu.PrefetchScalarGridSpec(
        num_scalar_prefetch=4,
        grid=(max_steps,),
        in_specs=[
            pl.BlockSpec((in_rows, LANES), lambda u, g, lo, hi, n_steps: (jnp.minimum(u, n_steps[0] - 1), 0)),
            pl.BlockSpec((MOE_PACK * BLOCK, ROUTE_W), lambda u, g, lo, hi, n_steps: (jnp.minimum(u, n_steps[0] - 1), 0)),
            w_spec(w1g), w_spec(w3g), w_spec(w2g),
        ],
        out_specs=pl.BlockSpec((in_rows, LANES), lambda u, *_: (u, 0)),
    )
    return pl.pallas_call(
        _moe_body,
        out_shape=jax.ShapeDtypeStruct((max_steps * in_rows, LANES), F32),
        grid_spec=grid_spec,
        compiler_params=pltpu.CompilerParams(dimension_semantics=("arbitrary",), vmem_limit_bytes=MOE_VMEM_LIMIT_BYTES),
        name="moe",
    )(sgrp, blo, bhi, n_steps, xs, meta, w1g, w3g, w2g)


def _final_body(run_ref, cnt_ref, lst_ref, runn_ref, cntn_ref, lstn_ref, lpos_ref, x1_ref, g2_ref,
                lng_ref, lnb_ref, ms_hbm, o_ref, gbuf, tbuf, gsem):
    u = pl.program_id(0)
    slot = u % 2
    d = x1_ref.shape[-1]
    tokens = x1_ref.shape[0]
    chunks = tokens // BLOCK
    tile_rows = BLOCK * SUBLANES
    sort_rows = chunks * tile_rows

    def streams(s):
        return ((gbuf, s * sort_rows, ms_hbm, SUBLANES),)

    @pl.when(u == 0)
    def _():
        _start_runs(cnt_ref, lst_ref, run_ref, streams(0), gsem.at[0], False)

    _wait_runs(streams(slot), tokens, gsem.at[slot])
    _start_runs(cntn_ref, lstn_ref, runn_ref, streams(1 - slot), gsem.at[1 - slot], False)
    gain = g2_ref[...]
    ln_g = lng_ref[...]
    ln_b = lnb_ref[...]
    unroll = tbuf.shape[0] // tile_rows

    def finish_chunks(i, carry):
        for k in range(unroll):
            c = i * unroll + k
            stage = k * tile_rows
            for r in range(BLOCK):
                src = pl.ds(pl.multiple_of(slot * sort_rows + lpos_ref[c, 0, r] * SUBLANES, SUBLANES), SUBLANES)
                tbuf[stage + r * SUBLANES:stage + (r + 1) * SUBLANES, :] = gbuf[src, :]
            rows = pl.ds(pl.multiple_of(c * BLOCK, BLOCK), BLOCK)
            moe = _load_tiles(tbuf, stage, SUBLANES, d)
            o_ref[rows, :] = _ln(ALPHA * x1_ref[rows, :] + gain * moe) * ln_g + ln_b
        return carry

    lax.fori_loop(0, chunks // unroll, finish_chunks, 0)

    @pl.when(u == pl.num_programs(0) - 1)
    def _():
        _wait_runs(streams(1 - slot), tokens, gsem.at[1 - slot])


def _final(x1, ms, lpos, tile_tabs, mod3, ln_g, ln_b, seq, chunks):
    n, d = x1.shape
    n_tiles = n // (chunks * BLOCK)
    per_batch = seq // (chunks * BLOCK)
    tile_rows = BLOCK * SUBLANES
    vec = pl.BlockSpec((1, d), lambda u: (0, 0))
    return pl.pallas_call(
        _final_body,
        out_shape=jax.ShapeDtypeStruct((n, d), F32),
        grid=(n_tiles,),
        in_specs=[
            _tile_tab_spec(0, n_tiles), _tile_tab_spec(0, n_tiles), _tile_tab_spec(0, n_tiles),
            _tile_tab_spec(1, n_tiles), _tile_tab_spec(1, n_tiles), _tile_tab_spec(1, n_tiles),
            pl.BlockSpec((chunks, 1, LANES), lambda u: (u, 0, 0), memory_space=pltpu.SMEM),
            pl.BlockSpec((chunks * BLOCK, d), lambda u: (u, 0)),
            pl.BlockSpec((None, 1, d), lambda u: (u // per_batch, 0, 5)),
            vec, vec, pl.BlockSpec(memory_space=pl.ANY),
        ],
        out_specs=pl.BlockSpec((chunks * BLOCK, d), lambda u: (u, 0)),
        scratch_shapes=[
            pltpu.VMEM((2 * chunks * tile_rows, LANES), F32),
            pltpu.VMEM((_stage_chunks(chunks) * tile_rows, LANES), F32),
            pltpu.SemaphoreType.DMA((2,)),
        ],
        compiler_params=_params("arbitrary"),
        name="final",
    )(*tile_tabs, *tile_tabs, lpos, x1, mod3, ln_g.reshape(1, d), ln_b.reshape(1, d), ms)


def _block_tables(tab):
    cls = tab[:, 0]
    pair = cls % PAIRS
    first = jnp.array([i for i in range(EPG) for _ in range(i + 1, EPG)], I32)
    second = jnp.array([j for i in range(EPG) for j in range(i + 1, EPG)], I32)
    return (cls // PAIRS)[::MOE_PACK], first[pair], second[pair], tab[0, 1].reshape(1)


def kernel(x, c, ctx, c_ctx, w_ada, b_ada, w_in, attn_sink, gm_ln_g, gm_ln_b, gm_ws, gm_bs, w_pa, w_pb, w_o,
           ln1_g, ln1_b, router_g_w, router_g_b, router_e_w, router_e_b, moe_w1, moe_w3, moe_w2, ln2_g, ln2_b):
    b, s, d = x.shape
    assert w_ada.shape[0] == DEPTH and s % BLOCK == 0 and s % GRID_W == 0
    mod_rows = 2 * SUBLANES
    assert b + 1 <= mod_rows
    c_rows = jnp.zeros((mod_rows, d), F32).at[:b].set(c).at[b].set(c_ctx)
    mod = _adaln_mod(c_rows, w_ada[0], b_ada[0])
    mod3 = mod.reshape(mod_rows, 1, 6 * d)
    w_in_bf = w_in[0].astype(BF16)
    kv_ctx = _ctx_kv(ctx, mod3, b, w_in_bf)
    tile = next(t for t in (512, BLOCK) if s % t == 0)
    experts, _, d_expert = moe_w1.shape[1:]
    moe_w = [moe_w1[0].reshape(experts * d, d_expert), moe_w3[0].reshape(experts * d, d_expert),
             moe_w2[0].reshape(experts * d_expert, d)]
    steps = b * (s // tile)
    sliced = all(w.shape[0] % (steps * 2 * SUBLANES) == 0 for w in moe_w)
    (q, kv, u, vg, gates), moe_w_bf = _project(x, mod3, w_in_bf, _rope_tables(s), gm_ln_g[0], gm_ln_b[0], tile,
                                                moe_w if sliced else [])
    if not sliced:
        moe_w_bf = [w.astype(BF16) for w in moe_w]
    nb = s // BLOCK
    y_attn = _attend(q, kv, kv_ctx, attn_sink[0], next(p for p in (4, 2, 1) if nb % p == 0))
    bs_rows = jnp.repeat(gm_bs[0].T, GM_HEAD, axis=1)
    w_route = jnp.concatenate([router_g_w[0], jnp.transpose(router_e_w[0], (1, 0, 2)).reshape(d, N_EXPERTS)], axis=1)
    w_route = jnp.pad(w_route, ((0, 0), (0, ROUTE_W - w_route.shape[1])))
    b_route = jnp.pad(jnp.concatenate([router_g_b[0], router_e_b[0].reshape(-1)]), (0, ROUTE_W - N_GROUPS - N_EXPERTS))
    wr_hi = w_route.astype(BF16)
    wr_lo = (w_route - wr_hi.astype(F32)).astype(BF16)
    merge_tile = BLOCK * next(p for p in (4, 2, 1) if nb % p == 0)
    x1, rec, counts = _merge(x, y_attn, u, vg, gates, gm_ws[0].astype(BF16), bs_rows, w_pa[0].astype(BF16),
                             w_pb[0].astype(BF16), w_o[0].astype(BF16), ln1_g[0], ln1_b[0], mod3, wr_hi, wr_lo,
                             b_route.reshape(1, ROUTE_W), merge_tile)
    n = b * s
    x1 = x1.reshape(n, d)
    rec = rec.reshape(n, ROUTE_W)
    max_blocks = n // BLOCK + N_CLASSES - 1 + (N_GROUPS - 1) * (MOE_PACK - 1)
    max_blocks = -(-max_blocks // MOE_PACK) * MOE_PACK
    sort_chunks = next(p for p in (SUBLANES, 4, 2, 1) if nb % p == 0)
    lpos, tab, cls_blk, *tile_tabs = _plan(counts.reshape(n // BLOCK, LANES), rec, max_blocks, sort_chunks)
    tile_tabs = [t.reshape(t.shape[0], 1, LANES) for t in tile_tabs]
    xs, meta = _scatter(x1, rec, mod3, lpos, tile_tabs, cls_blk, max_blocks, s, sort_chunks)
    ms = _moe(xs, meta, _block_tables(tab), *moe_w_bf, d)
    out = _final(x1, ms, lpos, tile_tabs, mod3, ln2_g[0], ln2_b[0], s, sort_chunks)
    return out.reshape(b, s, d)
```

```python
import functools

import jax
import jax.numpy as jnp
from jax import lax
from jax.experimental import pallas as pl
from jax.experimental.pallas import tpu as pltpu

F32 = jnp.float32
BF16 = jnp.bfloat16
I32 = jnp.int32

GRID_W = 64
HEAD_DIM = 64
Q_HEADS = 8
KV_HEADS = 2
GROUP = Q_HEADS // KV_HEADS
BLOCK = 128
ROPE_BASE = 10000.0
ROPE_FREQS = HEAD_DIM // 4
GM_GROUPS = 8
GM_HEAD = 64
ATTN_W = Q_HEADS * HEAD_DIM
KV_W = KV_HEADS * HEAD_DIM
GM_W = GM_GROUPS * GM_HEAD
N_GROUPS = 4
EPG = 8
N_EXPERTS = N_GROUPS * EPG
PAIRS = EPG * (EPG - 1) // 2
N_CLASSES = N_GROUPS * PAIRS
LN_EPS = 1e-6
NEG_INF = -1e30
DEPTH = 1
ALPHA = (2.0 * DEPTH) ** 0.25

LANES = 128
SUBLANES = 8
VMEM_LIMIT_BYTES = 56 * 1024 * 1024
MOE_VMEM_LIMIT_BYTES = 60 * 1024 * 1024
MOE_PACK = 2
ROUTE_W = LANES
REC_WLO, REC_WHI, REC_CLS, REC_RANK = 0, 1, 2, 3


def _params(*sem):
    return pltpu.CompilerParams(dimension_semantics=sem, vmem_limit_bytes=VMEM_LIMIT_BYTES)


def _ln(x):
    mu = jnp.mean(x, -1, keepdims=True)
    xc = x - mu
    var = jnp.mean(xc * xc, -1, keepdims=True)
    return xc * lax.rsqrt(var + LN_EPS)


def _dot(a, b):
    return jnp.dot(a, b, preferred_element_type=F32)


def _dot_nt(a, b):
    return lax.dot_general(a, b, (((1,), (1,)), ((), ())), preferred_element_type=F32)


def _split(a):
    hi = a.astype(BF16)
    lo = (a - hi.astype(F32)).astype(BF16)
    return hi, lo


def _dot_split(a, b_hi, b_lo):
    a_hi, a_lo = _split(a)
    return _dot(a_hi, b_hi) + _dot(a_hi, b_lo) + _dot(a_lo, b_hi)


def _iota(shape, axis):
    return lax.broadcasted_iota(I32, shape, axis)


def _mod_body(c_ref, w_ref, b_ref, o_ref):
    w_hi, w_lo = _split(w_ref[...])
    o_ref[...] = _dot_split(jax.nn.silu(c_ref[...]), w_hi, w_lo) + b_ref[...]


def _adaln_mod(c_rows, w_ada, b_ada):
    rows, d = c_rows.shape
    width = w_ada.shape[1]
    tn = 6 * LANES
    return pl.pallas_call(
        _mod_body,
        out_shape=jax.ShapeDtypeStruct((rows, width), F32),
        grid=(width // tn,),
        in_specs=[
            pl.BlockSpec((rows, d), lambda j: (0, 0)),
            pl.BlockSpec((d, tn), lambda j: (0, j)),
            pl.BlockSpec((1, tn), lambda j: (0, j)),
        ],
        out_specs=pl.BlockSpec((rows, tn), lambda j: (0, j)),
        compiler_params=_params("arbitrary"),
        name="mod",
    )(c_rows, w_ada, b_ada.reshape(1, width))


def _ctx_body(ctx_ref, sh_ref, sc_ref, w_ref, o_ref):
    h = _ln(ctx_ref[...]) * (1.0 + sc_ref[...]) + sh_ref[...]
    o_ref[...] = _dot(h.astype(BF16), w_ref[...]).astype(BF16)


def _ctx_kv(ctx, mod3, ctx_row, w_in_bf):
    b, c_len, d = ctx.shape
    kvw = 2 * KV_W
    return pl.pallas_call(
        _ctx_body,
        out_shape=jax.ShapeDtypeStruct((b, c_len, kvw), BF16),
        grid=(b,),
        in_specs=[
            pl.BlockSpec((None, c_len, d), lambda i: (i, 0, 0)),
            pl.BlockSpec((None, 1, d), lambda i: (ctx_row, 0, 0)),
            pl.BlockSpec((None, 1, d), lambda i: (ctx_row, 0, 1)),
            pl.BlockSpec((d, kvw), lambda i: (0, ATTN_W // kvw)),
        ],
        out_specs=pl.BlockSpec((None, c_len, kvw), lambda i: (i, 0, 0)),
        compiler_params=_params("arbitrary"),
        name="ctx_kv",
    )(ctx, mod3, mod3, w_in_bf)


def _rope(x, cos, s_up, s_dn):
    return x * cos + pltpu.roll(x, LANES - ROPE_FREQS, 1) * s_up + pltpu.roll(x, ROPE_FREQS, 1) * s_dn


def _project_body(x_ref, sh_ref, sc_ref, w_ref, cos_ref, sup_ref, sdn_ref, lng_ref, lnb_ref, *rest):
    n_side = (len(rest) - 5) // 2
    side_in, (q_ref, kv_ref, u_ref, vg_ref, gate_ref), side_out = rest[:n_side], rest[n_side:n_side + 5], rest[n_side + 5:]
    for src, dst in zip(side_in, side_out):
        dst[...] = src[...].astype(BF16)
    h = (_ln(x_ref[...]) * (1.0 + sc_ref[...]) + sh_ref[...]).astype(BF16)
    cos, s_up, s_dn = cos_ref[...], sup_ref[...], sdn_ref[...]
    q_scale = HEAD_DIM ** -0.5
    k0 = ATTN_W
    u0 = k0 + 2 * KV_W
    vg0 = u0 + GM_W
    gate0 = vg0 + GM_W
    q = _dot(h, w_ref[:, 0:k0])
    for j in range(ATTN_W // LANES):
        sl = slice(j * LANES, (j + 1) * LANES)
        q_ref[:, sl] = (_rope(q[:, sl], cos, s_up, s_dn) * q_scale).astype(BF16)
    kv = _dot(h, w_ref[:, k0:u0])
    kv_ref[:, 0:KV_W] = _rope(kv[:, 0:KV_W], cos, s_up, s_dn).astype(BF16)
    kv_ref[:, KV_W:] = kv[:, KV_W:].astype(BF16)
    u_ref[...] = jax.nn.gelu(_dot(h, w_ref[:, u0:vg0])).astype(BF16)
    vg = _ln(jax.nn.gelu(_dot(h, w_ref[:, vg0:gate0]))) * lng_ref[...] + lnb_ref[...]
    vg_ref[...] = vg.astype(BF16)
    gate_ref[...] = jax.nn.sigmoid(_dot(h, w_ref[:, gate0:])).astype(BF16)


def _project(x, mod3, w_in_bf, rope_tabs, gm_ln_g, gm_ln_b, tile, side):
    b, s, d = x.shape
    in_w = w_in_bf.shape[1]
    gate_w = in_w - (ATTN_W + 2 * KV_W + 2 * GM_W)
    nt = s // tile
    steps = b * nt
    row = lambda width: pl.BlockSpec((None, tile, width), lambda i, t: (i, t, 0))
    tab = pl.BlockSpec((tile, LANES), lambda i, t: (t, 0))
    vec = lambda width: pl.BlockSpec((1, width), lambda i, t: (0, 0))
    side_specs = [pl.BlockSpec((a.shape[0] // steps, a.shape[1]), lambda i, t: (i * nt + t, 0)) for a in side]
    outs = pl.pallas_call(
        _project_body,
        out_shape=[
            jax.ShapeDtypeStruct((b, s, ATTN_W), BF16),
            jax.ShapeDtypeStruct((b, s, 2 * KV_W), BF16),
            jax.ShapeDtypeStruct((b, s, GM_W), BF16),
            jax.ShapeDtypeStruct((b, s, GM_W), BF16),
            jax.ShapeDtypeStruct((b, s, gate_w), BF16),
        ] + [jax.ShapeDtypeStruct(a.shape, BF16) for a in side],
        grid=(b, nt),
        in_specs=[
            row(d),
            pl.BlockSpec((None, 1, d), lambda i, t: (i, 0, 0)),
            pl.BlockSpec((None, 1, d), lambda i, t: (i, 0, 1)),
            pl.BlockSpec((d, in_w), lambda i, t: (0, 0)),
            tab, tab, tab,
            vec(GM_W), vec(GM_W),
        ] + side_specs,
        out_specs=[row(ATTN_W), row(2 * KV_W), row(GM_W), row(GM_W), row(gate_w)] + side_specs,
        compiler_params=_params("arbitrary", "arbitrary"),
        name="project",
    )(x, mod3, mod3, w_in_bf, *rope_tabs, gm_ln_g.reshape(1, GM_W), gm_ln_b.reshape(1, GM_W), *side)
    return outs[:5], outs[5:]


def _rope_tables(s):
    pos = jnp.arange(s, dtype=I32)
    inv = 1.0 / (ROPE_BASE ** (jnp.arange(ROPE_FREQS, dtype=F32) / ROPE_FREQS))
    ang_r = (pos // GRID_W).astype(F32)[:, None] * inv[None, :]
    ang_c = (pos % GRID_W).astype(F32)[:, None] * inv[None, :]
    cr, sr, cc, sn = jnp.cos(ang_r), jnp.sin(ang_r), jnp.cos(ang_c), jnp.sin(ang_c)
    z = jnp.zeros_like(sr)
    reps = LANES // HEAD_DIM
    cos = jnp.tile(jnp.concatenate([cr, cr, cc, cc], -1), (1, reps))
    s_up = jnp.tile(jnp.concatenate([-sr, z, -sn, z], -1), (1, reps))
    s_dn = jnp.tile(jnp.concatenate([z, sr, z, sn], -1), (1, reps))
    return cos, s_up, s_dn


def _attend_body(sink_ref, q_ref, kvp_ref, kvc_ref, kvn_ref, ctx_ref, o_ref):
    i = pl.program_id(1)
    n_steps = pl.num_programs(1)
    per = q_ref.shape[0] // BLOCK
    rows = GROUP * BLOCK
    r = _iota((rows, 3 * BLOCK), 0) & (BLOCK - 1)
    m = _iota((rows, 3 * BLOCK), 1)
    band = (m >= r) & (m <= r + 2 * BLOCK)
    grp = _iota((rows, 1), 0) // BLOCK
    kv_all = jnp.concatenate([kvp_ref[...], kvc_ref[...], kvn_ref[...]], axis=0)
    kv_ctx = ctx_ref[...]
    for t in range(per):
        mask = band
        if t == 0:
            mask = mask & ((m >= BLOCK) | (i > 0))
        if t == per - 1:
            mask = mask & ((m < 2 * BLOCK) | (i < n_steps - 1))
        q = q_ref[t * BLOCK:(t + 1) * BLOCK, :]
        kv_loc = kv_all[t * BLOCK:(t + 3) * BLOCK]
        scores = []
        for h in range(KV_HEADS):
            ks = slice(h * HEAD_DIM, (h + 1) * HEAD_DIM)
            qg = jnp.concatenate(
                [q[:, (h * GROUP + g) * HEAD_DIM:(h * GROUP + g + 1) * HEAD_DIM] for g in range(GROUP)], axis=0)
            scores.append((_dot_nt(qg, kv_ctx[:, ks]), jnp.where(mask, _dot_nt(qg, kv_loc[:, ks]), NEG_INF)))
        probs = []
        for h, (s_ctx, s_loc) in enumerate(scores):
            sink = jnp.zeros((rows, 1), F32)
            for g in range(GROUP):
                sink = jnp.where(grp == g, sink_ref[h * GROUP + g], sink)
            top = jnp.maximum(jnp.maximum(jnp.max(s_ctx, -1, keepdims=True), jnp.max(s_loc, -1, keepdims=True)), sink)
            e_ctx = jnp.exp(s_ctx - top)
            e_loc = jnp.exp(s_loc - top)
            den = jnp.sum(e_ctx, -1, keepdims=True) + jnp.sum(e_loc, -1, keepdims=True) + jnp.exp(sink - top)
            probs.append((e_ctx.astype(BF16), e_loc.astype(BF16), 1.0 / den))
        outs = []
        for h, (e_ctx, e_loc, inv) in enumerate(probs):
            vs = slice(KV_W + h * HEAD_DIM, KV_W + (h + 1) * HEAD_DIM)
            o = (_dot(e_ctx, kv_ctx[:, vs]) + _dot(e_loc, kv_loc[:, vs])) * inv
            outs += [o[g * BLOCK:(g + 1) * BLOCK] for g in range(GROUP)]
        o_ref[t * BLOCK:(t + 1) * BLOCK, :] = jnp.concatenate(outs, axis=1).astype(BF16)


def _attend(q, kv, kv_ctx, sink, per):
    b, s, _ = q.shape
    nb = s // BLOCK
    c_len = kv_ctx.shape[1]
    kvw = 2 * KV_W
    return pl.pallas_call(
        _attend_body,
        out_shape=jax.ShapeDtypeStruct((b, s, ATTN_W), BF16),
        grid=(b, nb // per),
        in_specs=[
            pl.BlockSpec(memory_space=pltpu.SMEM),
            pl.BlockSpec((None, per * BLOCK, ATTN_W), lambda i, j: (i, j, 0)),
            pl.BlockSpec((None, BLOCK, kvw), lambda i, j: (i, jnp.maximum(j * per - 1, 0), 0)),
            pl.BlockSpec((None, per * BLOCK, kvw), lambda i, j: (i, j, 0)),
            pl.BlockSpec((None, BLOCK, kvw), lambda i, j: (i, jnp.minimum((j + 1) * per, nb - 1), 0)),
            pl.BlockSpec((None, c_len, kvw), lambda i, j: (i, 0, 0)),
        ],
        out_specs=pl.BlockSpec((None, per * BLOCK, ATTN_W), lambda i, j: (i, j, 0)),
        compiler_params=_params("arbitrary", "arbitrary"),
        name="attend",
    )(sink, q, kv, kv, kv, kv_ctx)


def _route(logits):
    lane = _iota(logits.shape, 1)
    far = jnp.int32(2 * ROUTE_W)

    def first_max(vals):
        top = jnp.max(vals, -1, keepdims=True)
        return top, jnp.min(jnp.where(vals == top, lane, far), -1, keepdims=True)

    is_g = lane < N_GROUPS
    g_top, g_idx = first_max(jnp.where(is_g, logits, NEG_INF))
    g_w = 1.0 / jnp.sum(jnp.where(is_g, jnp.exp(logits - g_top), 0.0), -1, keepdims=True)
    lo = N_GROUPS + EPG * g_idx
    e_vals = jnp.where((lane >= lo) & (lane < lo + EPG), logits, NEG_INF)
    v1, i1 = first_max(e_vals)
    v2, i2 = first_max(jnp.where(lane == i1, NEG_INF, e_vals))
    t = jnp.exp(v2 - v1)
    w1 = g_w * (1.0 / (1.0 + t))
    w2 = g_w * (t / (1.0 + t))
    swap = i2 < i1
    a = jnp.where(swap, i2, i1) - lo
    b = jnp.where(swap, i1, i2) - lo
    cls = g_idx * PAIRS + ((a * (2 * EPG - 1 - a)) >> 1) + (b - a - 1)
    return jnp.where(swap, w2, w1), jnp.where(swap, w1, w2), cls


def _merge_body(x_ref, ya_ref, u_ref, vg_ref, gate_ref, ws_ref, bs_ref, wpa_ref, wpb_ref, wo_ref,
                lng_ref, lnb_ref, g1_ref, sh2_ref, sc2_ref, wrh_ref, wrl_ref, br_ref, x1_ref, rec_ref, cnt_ref):
    tile, d = x_ref.shape
    lane = _iota((BLOCK, LANES), 1)
    below = (lane < _iota((BLOCK, LANES), 0)).astype(BF16)
    chunks = []
    for c in range(tile // BLOCK):
        rows = slice(c * BLOCK, (c + 1) * BLOCK)
        pieces = []
        for j in range(GM_W // LANES):
            v = vg_ref[rows, j * LANES:(j + 1) * LANES]
            per = LANES // GM_HEAD
            sp = _dot(ws_ref[per * j], v)
            for k in range(1, per):
                sp = jnp.where(lane < k * GM_HEAD, sp, _dot(ws_ref[per * j + k], v))
            pieces.append(sp)
        chunks.append(jnp.concatenate(pieces, axis=1) + bs_ref[...])
    sp = jnp.concatenate(chunks, axis=0)
    y_gm = (u_ref[...].astype(F32) * sp).astype(BF16)
    gate = gate_ref[...].astype(F32)
    y = gate[:, :d] * _dot(ya_ref[...], wpa_ref[...]) + gate[:, d:] * _dot(y_gm, wpb_ref[...])
    mix = _dot(y.astype(BF16), wo_ref[...])
    x1 = _ln(ALPHA * x_ref[...] + g1_ref[...] * mix) * lng_ref[...] + lnb_ref[...]
    x1_ref[...] = x1
    h2 = _ln(x1) * (1.0 + sc2_ref[...]) + sh2_ref[...]
    w_lo, w_hi, cls = _route(_dot_split(h2, wrh_ref[...], wrl_ref[...]) + br_ref[...])
    for c in range(tile // BLOCK):
        rows = slice(c * BLOCK, (c + 1) * BLOCK)
        onehot = lane == cls[rows]
        hot = onehot.astype(BF16)
        rank = jnp.sum(jnp.where(onehot, _dot(below, hot), 0.0), -1, keepdims=True)
        cnt_ref[c] = jnp.sum(hot.astype(F32), 0, keepdims=True)
        rec = jnp.where(lane == REC_WLO, w_lo[rows], 0.0)
        rec = jnp.where(lane == REC_WHI, w_hi[rows], rec)
        rec = jnp.where(lane == REC_CLS, cls[rows].astype(F32), rec)
        rec_ref[rows, :] = jnp.where(lane == REC_RANK, rank, rec)


def _merge(x, y_attn, u, vg, gates, ws_bf, bs_rows, wpa, wpb, wo, ln_g, ln_b, mod3, wr_hi, wr_lo, br, tile):
    b, s, d = x.shape
    nt = s // tile
    per = tile // BLOCK
    row = lambda width: pl.BlockSpec((None, tile, width), lambda i, j: (i, j, 0))
    full = lambda a: pl.BlockSpec(a.shape, lambda i, j: (0,) * a.ndim)
    modv = lambda k: pl.BlockSpec((None, 1, d), lambda i, j: (i, 0, k))
    ln_g = ln_g.reshape(1, d)
    ln_b = ln_b.reshape(1, d)
    return pl.pallas_call(
        _merge_body,
        out_shape=[jax.ShapeDtypeStruct((b, s, d), F32), jax.ShapeDtypeStruct((b, s, ROUTE_W), F32),
                   jax.ShapeDtypeStruct((b * s // BLOCK, 1, LANES), F32)],
        grid=(b, nt),
        in_specs=[
            row(d), row(ATTN_W), row(GM_W), row(GM_W), row(2 * d),
            full(ws_bf), full(bs_rows), full(wpa), full(wpb), full(wo), full(ln_g), full(ln_b),
            modv(2), modv(3), modv(4),
            full(wr_hi), full(wr_lo), full(br),
        ],
        out_specs=[row(d), row(ROUTE_W), pl.BlockSpec((per, 1, LANES), lambda i, j: (i * nt + j, 0, 0))],
        compiler_params=_params("arbitrary", "arbitrary"),
        name="merge",
    )(x, y_attn, u, vg, gates, ws_bf, bs_rows, wpa, wpb, wo, ln_g, ln_b, mod3, mod3, mod3, wr_hi, wr_lo, br)


def _plan_body(cnt_ref, rec_ref, lpos_ref, tab_ref, cblk_ref, run_ref, cntt_ref, lst_ref, base_ref, lstf_ref, *, chunks):
    step = pl.program_id(0)
    n_chunks = cnt_ref.shape[0]
    n_tiles = run_ref.shape[0]
    max_blocks = tab_ref.shape[0]

    @pl.when(step == 0)
    def _():
        cnt = cnt_ref[...]
        total = jnp.sum(cnt, 0, keepdims=True)
        n_blk = jnp.floor((total + (BLOCK - 1)) * (1.0 / BLOCK))
        before = (_iota((LANES, LANES), 0) < _iota((LANES, LANES), 1)).astype(BF16)
        nb_hi, nb_lo = _split(jnp.broadcast_to(n_blk, (SUBLANES, LANES)))
        blk_start = (_dot(nb_hi, before) + _dot(nb_lo, before))[0:1]
        lane1 = _iota((1, LANES), 1)
        packed = blk_start
        shift = jnp.zeros((1, 1), F32)
        for g in range(1, N_GROUPS):
            first = jnp.sum(jnp.where(lane1 == g * PAIRS, blk_start, 0.0), -1, keepdims=True) + shift
            over = first - MOE_PACK * jnp.floor(first * (1.0 / MOE_PACK))
            shift = shift + jnp.where(over > 0.0, MOE_PACK - over, 0.0)
            packed = jnp.where(lane1 >= g * PAIRS, blk_start + shift, packed)
        blk_start = packed
        earlier = (_iota((n_chunks, n_chunks), 1) < _iota((n_chunks, n_chunks), 0)).astype(BF16)
        base_ref[...] = blk_start * float(BLOCK) + _dot(earlier, cnt.astype(BF16))
        blk_end = blk_start + n_blk
        sub = _iota((SUBLANES, LANES), 0)
        cblk_ref[...] = jnp.where(sub == 0, blk_start, jnp.where(sub == 1, blk_end, 0.0)).astype(I32)
        lane = _iota((max_blocks, LANES), 1)
        blk = _iota((max_blocks, LANES), 0).astype(F32)
        is_cls = lane < N_CLASSES
        n_real = jnp.max(jnp.where(is_cls, blk_end, 0.0), -1, keepdims=True)
        u = jnp.minimum(blk, n_real - 1.0)
        cls_of = jnp.sum(jnp.where(is_cls & (blk_end <= u), 1.0, 0.0), -1, keepdims=True)
        n_steps = jnp.floor((n_real + (MOE_PACK - 1)) * (1.0 / MOE_PACK))
        tab = jnp.where(lane == 0, cls_of, jnp.where(lane == 1, n_steps, 0.0))
        tab_ref[...] = tab.astype(I32)
        tile = _iota((n_tiles, n_chunks), 0)
        chunk = _iota((n_tiles, n_chunks), 1)
        owner = ((chunk >= tile * chunks) & (chunk < (tile + 1) * chunks)).astype(BF16)
        cnt_tile = _dot(owner, cnt.astype(BF16))
        ct_hi, ct_lo = _split(cnt_tile)
        local_start = _dot(ct_hi, before) + _dot(ct_lo, before)
        run_ref[...] = base_ref[pl.ds(0, n_tiles, stride=chunks), :].astype(I32)
        cntt_ref[...] = cnt_tile.astype(I32)
        lst_ref[...] = local_start.astype(I32)
        lstf_ref[...] = local_start

    lane = _iota((BLOCK, LANES), 1)
    tile_off = lstf_ref[pl.ds(step, 1), :] - base_ref[pl.ds(step * chunks, 1), :]
    pick_rank = (_iota((2 * SUBLANES, LANES), 1) == REC_RANK).astype(BF16)
    for c in range(chunks):
        rec = rec_ref[c * BLOCK:(c + 1) * BLOCK, :]
        hot = (lane == rec[:, REC_CLS:REC_CLS + 1].astype(I32)).astype(BF16)
        base = base_ref[pl.ds(step * chunks + c, 1), :] + tile_off
        b_hi, b_lo = _split(jnp.broadcast_to(base, (2 * SUBLANES, LANES)))
        lpos = _dot_nt(b_hi, hot) + _dot_nt(b_lo, hot) + _dot_nt(pick_rank, rec.astype(BF16))
        lpos_ref[c] = lpos[0:1].astype(I32)


def _plan(counts, rec, max_blocks, chunks):
    n = rec.shape[0]
    n_chunks = n // BLOCK
    n_tiles = n_chunks // chunks
    tile_tab = jax.ShapeDtypeStruct((n_tiles, LANES), I32)
    whole = lambda rows: pl.BlockSpec((rows, LANES), lambda i: (0, 0))
    return pl.pallas_call(
        functools.partial(_plan_body, chunks=chunks),
        out_shape=[jax.ShapeDtypeStruct((n_chunks, 1, LANES), I32), jax.ShapeDtypeStruct((max_blocks, LANES), I32),
                   jax.ShapeDtypeStruct((SUBLANES, LANES), I32), tile_tab, tile_tab, tile_tab],
        grid=(n_tiles,),
        in_specs=[whole(n_chunks), pl.BlockSpec((chunks * BLOCK, ROUTE_W), lambda i: (i, 0))],
        out_specs=[pl.BlockSpec((chunks, 1, LANES), lambda i: (i, 0, 0)), whole(max_blocks), whole(SUBLANES),
                   whole(n_tiles), whole(n_tiles), whole(n_tiles)],
        scratch_shapes=[pltpu.VMEM((n_chunks, LANES), F32), pltpu.VMEM((n_tiles, LANES), F32)],
        compiler_params=_params("arbitrary"),
        name="plan",
    )(counts, rec)


def _tile_rows(base, j, pitch):
    return pl.ds(base + j, BLOCK, stride=pitch)


def _load_tiles(ref, base, pitch, width):
    return jnp.concatenate([ref[_tile_rows(base, j, pitch), :] for j in range(width // LANES)], axis=1)


def _store_tiles(ref, base, pitch, val):
    for j in range(val.shape[1] // LANES):
        ref[_tile_rows(base, j, pitch), :] = val[:, j * LANES:(j + 1) * LANES]


RUN_GROUP = 4


def _start_runs(cnt_ref, lst_ref, run_ref, streams, sem, to_hbm):
    def issue(c):
        for buf, base, hbm, pitch in streams:
            rows = cnt_ref[0, c] * pitch
            v = buf.at[pl.ds(pl.multiple_of(base + lst_ref[0, c] * pitch, pitch), rows)]
            h = hbm.at[pl.ds(pl.multiple_of(run_ref[0, c] * pitch, pitch), rows)]
            (pltpu.make_async_copy(v, h, sem) if to_hbm else pltpu.make_async_copy(h, v, sem)).start(priority=c % 2)

    for c0 in range(0, N_CLASSES, RUN_GROUP):
        group = range(c0, min(c0 + RUN_GROUP, N_CLASSES))
        full = cnt_ref[0, c0] > 0
        for c in group[1:]:
            full &= cnt_ref[0, c] > 0

        @pl.when(full)
        def _():
            for c in group:
                issue(c)

        @pl.when(jnp.logical_not(full))
        def _():
            for c in group:
                pl.when(cnt_ref[0, c] > 0)(functools.partial(issue, c))


def _wait_runs(streams, tokens, sem):
    for buf, base, hbm, pitch in streams:
        rows = tokens * pitch
        pltpu.make_async_copy(hbm.at[pl.ds(0, rows)], buf.at[pl.ds(base, rows)], sem).wait()


def _scatter_body(cls_blk_ref, run_ref, cnt_ref, lst_ref, lpos_ref, x1_ref, sh_ref, sc_ref, xs_hbm,
                  abuf, bbuf, zbuf, ssem, zsem):
    u = pl.program_id(0)
    slot = u % 2
    tokens = x1_ref.shape[0]
    chunks = tokens // BLOCK
    tile_rows = BLOCK * SUBLANES
    sort_rows = chunks * tile_rows

    def streams(s):
        return ((bbuf, s * sort_rows, xs_hbm, SUBLANES),)

    @pl.when(u == 0)
    def _():
        zbuf[...] = jnp.zeros_like(zbuf)

        def zero_block(blk, start):
            cp = pltpu.make_async_copy(zbuf, xs_hbm.at[pl.ds(pl.multiple_of(blk * tile_rows, tile_rows), tile_rows)], zsem)
            cp.start() if start else cp.wait()

        for start in (True, False):
            for c in range(N_CLASSES):
                pl.when(cls_blk_ref[1, c] > cls_blk_ref[0, c])(functools.partial(zero_block, cls_blk_ref[1, c] - 1, start))
        gaps = [(cls_blk_ref[1, g * PAIRS - 1], cls_blk_ref[0, g * PAIRS]) for g in range(1, N_GROUPS)]
        gaps.append((cls_blk_ref[1, N_CLASSES - 1], xs_hbm.shape[0] // tile_rows))
        for start in (True, False):
            for lo, hi in gaps:
                lax.fori_loop(lo, hi, lambda blk, carry: (zero_block(blk, start), carry)[1], 0)

    @pl.when(u >= 2)
    def _():
        _wait_runs(streams(slot), tokens, ssem.at[slot])

    scale = 1.0 + sc_ref[...]
    shift = sh_ref[...]

    unroll = abuf.shape[0] // tile_rows

    def sort_chunks(i, carry):
        for k in range(unroll):
            c = i * unroll + k
            stage = k * tile_rows
            rows = pl.ds(pl.multiple_of(c * BLOCK, BLOCK), BLOCK)
            _store_tiles(abuf, stage, SUBLANES, _ln(x1_ref[rows, :]) * scale + shift)
            for r in range(BLOCK):
                dst = pl.multiple_of(slot * sort_rows + lpos_ref[c, 0, r] * SUBLANES, SUBLANES)
                bbuf[pl.ds(dst, SUBLANES), :] = abuf[stage + r * SUBLANES:stage + (r + 1) * SUBLANES, :]
        return carry

    lax.fori_loop(0, chunks // unroll, sort_chunks, 0)
    _start_runs(cnt_ref, lst_ref, run_ref, streams(slot), ssem.at[slot], True)

    @pl.when(u == pl.num_programs(0) - 1)
    def _():
        _wait_runs(streams(slot), tokens, ssem.at[slot])

        @pl.when(u >= 1)
        def _():
            _wait_runs(streams(1 - slot), tokens, ssem.at[1 - slot])


def _stage_chunks(chunks):
    return next(p for p in (4, 2, 1) if chunks % p == 0)


def _tile_tab_spec(off, n_tiles):
    return pl.BlockSpec((None, 1, LANES), lambda u: (jnp.minimum(u + off, n_tiles - 1), 0, 0),
                        memory_space=pltpu.SMEM)


def _scatter(x1, mod3, lpos, tile_tabs, cls_blk, max_blocks, seq, chunks):
    n, d = x1.shape
    n_tiles = n // (chunks * BLOCK)
    per_batch = seq // (chunks * BLOCK)
    tile_rows = BLOCK * SUBLANES
    return pl.pallas_call(
        _scatter_body,
        out_shape=jax.ShapeDtypeStruct((max_blocks * tile_rows, LANES), F32),
        grid=(n_tiles,),
        in_specs=[
            pl.BlockSpec(memory_space=pltpu.SMEM),
            _tile_tab_spec(0, n_tiles), _tile_tab_spec(0, n_tiles), _tile_tab_spec(0, n_tiles),
            pl.BlockSpec((chunks, 1, LANES), lambda u: (u, 0, 0), memory_space=pltpu.SMEM),
            pl.BlockSpec((chunks * BLOCK, d), lambda u: (u, 0)),
            pl.BlockSpec((None, 1, d), lambda u: (u // per_batch, 0, 3)),
            pl.BlockSpec((None, 1, d), lambda u: (u // per_batch, 0, 4)),
        ],
        out_specs=pl.BlockSpec(memory_space=pl.ANY),
        scratch_shapes=[
            pltpu.VMEM((_stage_chunks(chunks) * tile_rows, LANES), F32),
            pltpu.VMEM((2 * chunks * tile_rows, LANES), F32),
            pltpu.VMEM((tile_rows, LANES), F32),
            pltpu.SemaphoreType.DMA((2,)),
            pltpu.SemaphoreType.DMA,
        ],
        compiler_params=_params("arbitrary"),
        name="scatter",
    )(cls_blk, *tile_tabs, lpos, x1, mod3, mod3)


def _expert(h, w1, w3, w2):
    hid = jax.nn.silu(_dot(h, w1)) * _dot(h, w3)
    return _dot(hid, w2)


def _moe_body(sgrp_ref, blo_ref, bhi_ref, nsteps_ref, xs_ref, w1_ref, w3_ref, w2_ref, olo_ref, ohi_ref):
    u = pl.program_id(0)
    d = w1_ref.shape[1]
    tile_rows = BLOCK * SUBLANES

    @pl.when(u < nsteps_ref[0])
    def _():
        jobs = [(k, o_ref, e) for k in range(MOE_PACK)
                for o_ref, e in ((olo_ref, blo_ref[MOE_PACK * u + k]), (ohi_ref, bhi_ref[MOE_PACK * u + k]))]
        h = [_load_tiles(xs_ref, k * tile_rows, SUBLANES, d).astype(BF16) for k in range(MOE_PACK)]
        up = [(_dot(h[k], w1_ref[e]), _dot(h[k], w3_ref[e])) for k, _, e in jobs]
        hid = [(jax.nn.silu(a) * b).astype(BF16) for a, b in up]
        for (k, o_ref, e), g in zip(jobs, hid):
            _store_tiles(o_ref, k * tile_rows, SUBLANES, _dot(g, w2_ref[e]))

    @pl.when(u >= nsteps_ref[0])
    def _():
        olo_ref[...] = jnp.zeros_like(olo_ref)
        ohi_ref[...] = jnp.zeros_like(ohi_ref)


def _moe(xs, tables, w1, w3, w2, d):
    sgrp, blo, bhi, n_steps = tables
    max_steps = sgrp.shape[0]
    de = w1.shape[1]
    in_rows = MOE_PACK * BLOCK * SUBLANES
    w1g = w1.reshape(N_GROUPS, EPG, d, de)
    w3g = w3.reshape(N_GROUPS, EPG, d, de)
    w2g = w2.reshape(N_GROUPS, EPG, de, d)

    def w_spec(a):
        return pl.BlockSpec((None,) + a.shape[1:], lambda u, g, lo, hi, n_steps: (g[u], 0, 0, 0))

    grid_spec = pltpu.PrefetchScalarGridSpec(
        num_scalar_prefetch=4,
        grid=(max_steps,),
        in_specs=[
            pl.BlockSpec((in_rows, LANES), lambda u, g, lo, hi, n_steps: (jnp.minimum(u, n_steps[0] - 1), 0)),
            w_spec(w1g), w_spec(w3g), w_spec(w2g),
        ],
        out_specs=[pl.BlockSpec((in_rows, LANES), lambda u, *_: (u, 0))] * 2,
    )
    return pl.pallas_call(
        _moe_body,
        out_shape=[jax.ShapeDtypeStruct((max_steps * in_rows, LANES), F32)] * 2,
        grid_spec=grid_spec,
        compiler_params=pltpu.CompilerParams(dimension_semantics=("arbitrary",), vmem_limit_bytes=MOE_VMEM_LIMIT_BYTES),
        name="moe",
    )(sgrp, blo, bhi, n_steps, xs, w1g, w3g, w2g)


def _final_body(run_ref, cnt_ref, lst_ref, runn_ref, cntn_ref, lstn_ref, lpos_ref, x1_ref, rec_ref, g2_ref,
                lng_ref, lnb_ref, mlo_hbm, mhi_hbm, o_ref, glo, ghi, tlo, thi, gsem):
    u = pl.program_id(0)
    slot = u % 2
    d = x1_ref.shape[-1]
    tokens = x1_ref.shape[0]
    chunks = tokens // BLOCK
    tile_rows = BLOCK * SUBLANES
    sort_rows = chunks * tile_rows

    def streams(s):
        return ((glo, s * sort_rows, mlo_hbm, SUBLANES), (ghi, s * sort_rows, mhi_hbm, SUBLANES))

    @pl.when(u == 0)
    def _():
        _start_runs(cnt_ref, lst_ref, run_ref, streams(0), gsem.at[0], False)

    _wait_runs(streams(slot), tokens, gsem.at[slot])
    _start_runs(cntn_ref, lstn_ref, runn_ref, streams(1 - slot), gsem.at[1 - slot], False)
    gain = g2_ref[...]
    ln_g = lng_ref[...]
    ln_b = lnb_ref[...]
    unroll = tlo.shape[0] // tile_rows

    def finish_chunks(i, carry):
        for k in range(unroll):
            c = i * unroll + k
            stage = k * tile_rows
            for r in range(BLOCK):
                src = pl.ds(pl.multiple_of(slot * sort_rows + lpos_ref[c, 0, r] * SUBLANES, SUBLANES), SUBLANES)
                tlo[stage + r * SUBLANES:stage + (r + 1) * SUBLANES, :] = glo[src, :]
                thi[stage + r * SUBLANES:stage + (r + 1) * SUBLANES, :] = ghi[src, :]
            rows = pl.ds(pl.multiple_of(c * BLOCK, BLOCK), BLOCK)
            rec = rec_ref[rows, :]
            moe = (rec[:, REC_WLO:REC_WLO + 1] * _load_tiles(tlo, stage, SUBLANES, d)
                   + rec[:, REC_WHI:REC_WHI + 1] * _load_tiles(thi, stage, SUBLANES, d))
            o_ref[rows, :] = _ln(ALPHA * x1_ref[rows, :] + gain * moe) * ln_g + ln_b
        return carry

    lax.fori_loop(0, chunks // unroll, finish_chunks, 0)

    @pl.when(u == pl.num_programs(0) - 1)
    def _():
        _wait_runs(streams(1 - slot), tokens, gsem.at[1 - slot])


def _final(x1, rec, ms_lo, ms_hi, lpos, tile_tabs, mod3, ln_g, ln_b, seq, chunks):
    n, d = x1.shape
    n_tiles = n // (chunks * BLOCK)
    per_batch = seq // (chunks * BLOCK)
    tile_rows = BLOCK * SUBLANES
    vec = pl.BlockSpec((1, d), lambda u: (0, 0))
    hbm = pl.BlockSpec(memory_space=pl.ANY)
    return pl.pallas_call(
        _final_body,
        out_shape=jax.ShapeDtypeStruct((n, d), F32),
        grid=(n_tiles,),
        in_specs=[
            _tile_tab_spec(0, n_tiles), _tile_tab_spec(0, n_tiles), _tile_tab_spec(0, n_tiles),
            _tile_tab_spec(1, n_tiles), _tile_tab_spec(1, n_tiles), _tile_tab_spec(1, n_tiles),
            pl.BlockSpec((chunks, 1, LANES), lambda u: (u, 0, 0), memory_space=pltpu.SMEM),
            pl.BlockSpec((chunks * BLOCK, d), lambda u: (u, 0)),
            pl.BlockSpec((chunks * BLOCK, ROUTE_W), lambda u: (u, 0)),
            pl.BlockSpec((None, 1, d), lambda u: (u // per_batch, 0, 5)),
            vec, vec, hbm, hbm,
        ],
        out_specs=pl.BlockSpec((chunks * BLOCK, d), lambda u: (u, 0)),
        scratch_shapes=[
            pltpu.VMEM((2 * chunks * tile_rows, LANES), F32),
            pltpu.VMEM((2 * chunks * tile_rows, LANES), F32),
            pltpu.VMEM((_stage_chunks(chunks) * tile_rows, LANES), F32),
            pltpu.VMEM((_stage_chunks(chunks) * tile_rows, LANES), F32),
            pltpu.SemaphoreType.DMA((2,)),
        ],
        compiler_params=_params("arbitrary"),
        name="final",
    )(*tile_tabs, *tile_tabs, lpos, x1, rec, mod3, ln_g.reshape(1, d), ln_b.reshape(1, d), ms_lo, ms_hi)


def _block_tables(tab):
    cls = tab[:, 0]
    pair = cls % PAIRS
    first = jnp.array([i for i in range(EPG) for _ in range(i + 1, EPG)], I32)
    second = jnp.array([j for i in range(EPG) for j in range(i + 1, EPG)], I32)
    return (cls // PAIRS)[::MOE_PACK], first[pair], second[pair], tab[0, 1].reshape(1)


def kernel(x, c, ctx, c_ctx, w_ada, b_ada, w_in, attn_sink, gm_ln_g, gm_ln_b, gm_ws, gm_bs, w_pa, w_pb, w_o,
           ln1_g, ln1_b, router_g_w, router_g_b, router_e_w, router_e_b, moe_w1, moe_w3, moe_w2, ln2_g, ln2_b):
    b, s, d = x.shape
    assert w_ada.shape[0] == DEPTH and s % BLOCK == 0 and s % GRID_W == 0
    mod_rows = 2 * SUBLANES
    assert b + 1 <= mod_rows
    c_rows = jnp.zeros((mod_rows, d), F32).at[:b].set(c).at[b].set(c_ctx)
    mod = _adaln_mod(c_rows, w_ada[0], b_ada[0])
    mod3 = mod.reshape(mod_rows, 1, 6 * d)
    w_in_bf = w_in[0].astype(BF16)
    kv_ctx = _ctx_kv(ctx, mod3, b, w_in_bf)
    tile = next(t for t in (512, BLOCK) if s % t == 0)
    experts, _, d_expert = moe_w1.shape[1:]
    moe_w = [moe_w1[0].reshape(experts * d, d_expert), moe_w3[0].reshape(experts * d, d_expert),
             moe_w2[0].reshape(experts * d_expert, d)]
    steps = b * (s // tile)
    sliced = all(w.shape[0] % (steps * 2 * SUBLANES) == 0 for w in moe_w)
    (q, kv, u, vg, gates), moe_w_bf = _project(x, mod3, w_in_bf, _rope_tables(s), gm_ln_g[0], gm_ln_b[0], tile,
                                                moe_w if sliced else [])
    if not sliced:
        moe_w_bf = [w.astype(BF16) for w in moe_w]
    nb = s // BLOCK
    y_attn = _attend(q, kv, kv_ctx, attn_sink[0], next(p for p in (4, 2, 1) if nb % p == 0))
    bs_rows = jnp.repeat(gm_bs[0].T, GM_HEAD, axis=1)
    w_route = jnp.concatenate([router_g_w[0], jnp.transpose(router_e_w[0], (1, 0, 2)).reshape(d, N_EXPERTS)], axis=1)
    w_route = jnp.pad(w_route, ((0, 0), (0, ROUTE_W - w_route.shape[1])))
    b_route = jnp.pad(jnp.concatenate([router_g_b[0], router_e_b[0].reshape(-1)]), (0, ROUTE_W - N_GROUPS - N_EXPERTS))
    wr_hi = w_route.astype(BF16)
    wr_lo = (w_route - wr_hi.astype(F32)).astype(BF16)
    merge_tile = BLOCK * next(p for p in (4, 2, 1) if nb % p == 0)
    x1, rec, counts = _merge(x, y_attn, u, vg, gates, gm_ws[0].astype(BF16), bs_rows, w_pa[0].astype(BF16),
                             w_pb[0].astype(BF16), w_o[0].astype(BF16), ln1_g[0], ln1_b[0], mod3, wr_hi, wr_lo,
                             b_route.reshape(1, ROUTE_W), merge_tile)
    n = b * s
    x1 = x1.reshape(n, d)
    rec = rec.reshape(n, ROUTE_W)
    max_blocks = n // BLOCK + N_CLASSES - 1 + (N_GROUPS - 1) * (MOE_PACK - 1)
    max_blocks = -(-max_blocks // MOE_PACK) * MOE_PACK
    sort_chunks = next(p for p in (SUBLANES, 4, 2, 1) if nb % p == 0)
    lpos, tab, cls_blk, *tile_tabs = _plan(counts.reshape(n // BLOCK, LANES), rec, max_blocks, sort_chunks)
    tile_tabs = [t.reshape(t.shape[0], 1, LANES) for t in tile_tabs]
    xs = _scatter(x1, mod3, lpos, tile_tabs, cls_blk, max_blocks, s, sort_chunks)
    ms_lo, ms_hi = _moe(xs, _block_tables(tab), *moe_w_bf, d)
    out = _final(x1, rec, ms_lo, ms_hi, lpos, tile_tabs, mod3, ln2_g[0], ln2_b[0], s, sort_chunks)
    return out.reshape(b, s, d)
```

```python
import functools

import jax
import jax.numpy as jnp
from jax import lax
from jax.experimental import pallas as pl
from jax.experimental.pallas import tpu as pltpu

F32 = jnp.float32
BF16 = jnp.bfloat16
I32 = jnp.int32

GRID_W = 64
HEAD_DIM = 64
Q_HEADS = 8
KV_HEADS = 2
GROUP = Q_HEADS // KV_HEADS
BLOCK = 128
ROPE_BASE = 10000.0
ROPE_FREQS = HEAD_DIM // 4
GM_GROUPS = 8
GM_HEAD = 64
ATTN_W = Q_HEADS * HEAD_DIM
KV_W = KV_HEADS * HEAD_DIM
GM_W = GM_GROUPS * GM_HEAD
N_GROUPS = 4
EPG = 8
N_EXPERTS = N_GROUPS * EPG
PAIRS = EPG * (EPG - 1) // 2
N_CLASSES = N_GROUPS * PAIRS
LN_EPS = 1e-6
NEG_INF = -1e30
DEPTH = 1
ALPHA = (2.0 * DEPTH) ** 0.25

LANES = 128
SUBLANES = 8
VMEM_LIMIT_BYTES = 56 * 1024 * 1024
MOE_VMEM_LIMIT_BYTES = 60 * 1024 * 1024
MOE_PACK = 2
ROUTE_W = LANES
REC_WLO, REC_WHI, REC_CLS, REC_RANK = 0, 1, 2, 3


def _params(*sem):
    return pltpu.CompilerParams(dimension_semantics=sem, vmem_limit_bytes=VMEM_LIMIT_BYTES)


def _ln(x):
    mu = jnp.mean(x, -1, keepdims=True)
    xc = x - mu
    var = jnp.mean(xc * xc, -1, keepdims=True)
    return xc * lax.rsqrt(var + LN_EPS)


def _dot(a, b):
    return jnp.dot(a, b, preferred_element_type=F32)


def _dot_nt(a, b):
    return lax.dot_general(a, b, (((1,), (1,)), ((), ())), preferred_element_type=F32)


def _split(a):
    hi = a.astype(BF16)
    lo = (a - hi.astype(F32)).astype(BF16)
    return hi, lo


def _dot_split(a, b_hi, b_lo):
    a_hi, a_lo = _split(a)
    return _dot(a_hi, b_hi) + _dot(a_hi, b_lo) + _dot(a_lo, b_hi)


def _iota(shape, axis):
    return lax.broadcasted_iota(I32, shape, axis)


def _mod_body(c_ref, w_ref, b_ref, o_ref):
    w_hi, w_lo = _split(w_ref[...])
    o_ref[...] = _dot_split(jax.nn.silu(c_ref[...]), w_hi, w_lo) + b_ref[...]


def _adaln_mod(c_rows, w_ada, b_ada):
    rows, d = c_rows.shape
    width = w_ada.shape[1]
    tn = 6 * LANES
    return pl.pallas_call(
        _mod_body,
        out_shape=jax.ShapeDtypeStruct((rows, width), F32),
        grid=(width // tn,),
        in_specs=[
            pl.BlockSpec((rows, d), lambda j: (0, 0)),
            pl.BlockSpec((d, tn), lambda j: (0, j)),
            pl.BlockSpec((1, tn), lambda j: (0, j)),
        ],
        out_specs=pl.BlockSpec((rows, tn), lambda j: (0, j)),
        compiler_params=_params("arbitrary"),
        name="mod",
    )(c_rows, w_ada, b_ada.reshape(1, width))


def _ctx_body(ctx_ref, sh_ref, sc_ref, w_ref, o_ref):
    h = _ln(ctx_ref[...]) * (1.0 + sc_ref[...]) + sh_ref[...]
    o_ref[...] = _dot(h.astype(BF16), w_ref[...]).astype(BF16)


def _ctx_kv(ctx, mod3, ctx_row, w_in_bf):
    b, c_len, d = ctx.shape
    kvw = 2 * KV_W
    return pl.pallas_call(
        _ctx_body,
        out_shape=jax.ShapeDtypeStruct((b, c_len, kvw), BF16),
        grid=(b,),
        in_specs=[
            pl.BlockSpec((None, c_len, d), lambda i: (i, 0, 0)),
            pl.BlockSpec((None, 1, d), lambda i: (ctx_row, 0, 0)),
            pl.BlockSpec((None, 1, d), lambda i: (ctx_row, 0, 1)),
            pl.BlockSpec((d, kvw), lambda i: (0, ATTN_W // kvw)),
        ],
        out_specs=pl.BlockSpec((None, c_len, kvw), lambda i: (i, 0, 0)),
        compiler_params=_params("arbitrary"),
        name="ctx_kv",
    )(ctx, mod3, mod3, w_in_bf)


def _rope(x, cos, s_up, s_dn):
    return x * cos + pltpu.roll(x, LANES - ROPE_FREQS, 1) * s_up + pltpu.roll(x, ROPE_FREQS, 1) * s_dn


def _project_body(x_ref, sh_ref, sc_ref, w_ref, cos_ref, sup_ref, sdn_ref, lng_ref, lnb_ref, *rest):
    n_side = (len(rest) - 5) // 2
    side_in, (q_ref, kv_ref, u_ref, vg_ref, gate_ref), side_out = rest[:n_side], rest[n_side:n_side + 5], rest[n_side + 5:]
    for src, dst in zip(side_in, side_out):
        dst[...] = src[...].astype(BF16)
    h = (_ln(x_ref[...]) * (1.0 + sc_ref[...]) + sh_ref[...]).astype(BF16)
    cos, s_up, s_dn = cos_ref[...], sup_ref[...], sdn_ref[...]
    q_scale = HEAD_DIM ** -0.5
    k0 = ATTN_W
    u0 = k0 + 2 * KV_W
    vg0 = u0 + GM_W
    gate0 = vg0 + GM_W
    q = _dot(h, w_ref[:, 0:k0])
    for j in range(ATTN_W // LANES):
        sl = slice(j * LANES, (j + 1) * LANES)
        q_ref[:, sl] = (_rope(q[:, sl], cos, s_up, s_dn) * q_scale).astype(BF16)
    kv = _dot(h, w_ref[:, k0:u0])
    kv_ref[:, 0:KV_W] = _rope(kv[:, 0:KV_W], cos, s_up, s_dn).astype(BF16)
    kv_ref[:, KV_W:] = kv[:, KV_W:].astype(BF16)
    u_ref[...] = jax.nn.gelu(_dot(h, w_ref[:, u0:vg0])).astype(BF16)
    vg = _ln(jax.nn.gelu(_dot(h, w_ref[:, vg0:gate0]))) * lng_ref[...] + lnb_ref[...]
    vg_ref[...] = vg.astype(BF16)
    gate_ref[...] = jax.nn.sigmoid(_dot(h, w_ref[:, gate0:])).astype(BF16)


def _project(x, mod3, w_in_bf, rope_tabs, gm_ln_g, gm_ln_b, tile, side):
    b, s, d = x.shape
    in_w = w_in_bf.shape[1]
    gate_w = in_w - (ATTN_W + 2 * KV_W + 2 * GM_W)
    nt = s // tile
    steps = b * nt
    row = lambda width: pl.BlockSpec((None, tile, width), lambda i, t: (i, t, 0))
    tab = pl.BlockSpec((tile, LANES), lambda i, t: (t, 0))
    vec = lambda width: pl.BlockSpec((1, width), lambda i, t: (0, 0))
    side_specs = [pl.BlockSpec((a.shape[0] // steps, a.shape[1]), lambda i, t: (i * nt + t, 0)) for a in side]
    outs = pl.pallas_call(
        _project_body,
        out_shape=[
            jax.ShapeDtypeStruct((b, s, ATTN_W), BF16),
            jax.ShapeDtypeStruct((b, s, 2 * KV_W), BF16),
            jax.ShapeDtypeStruct((b, s, GM_W), BF16),
            jax.ShapeDtypeStruct((b, s, GM_W), BF16),
            jax.ShapeDtypeStruct((b, s, gate_w), BF16),
        ] + [jax.ShapeDtypeStruct(a.shape, BF16) for a in side],
        grid=(b, nt),
        in_specs=[
            row(d),
            pl.BlockSpec((None, 1, d), lambda i, t: (i, 0, 0)),
            pl.BlockSpec((None, 1, d), lambda i, t: (i, 0, 1)),
            pl.BlockSpec((d, in_w), lambda i, t: (0, 0)),
            tab, tab, tab,
            vec(GM_W), vec(GM_W),
        ] + side_specs,
        out_specs=[row(ATTN_W), row(2 * KV_W), row(GM_W), row(GM_W), row(gate_w)] + side_specs,
        compiler_params=_params("arbitrary", "arbitrary"),
        name="project",
    )(x, mod3, mod3, w_in_bf, *rope_tabs, gm_ln_g.reshape(1, GM_W), gm_ln_b.reshape(1, GM_W), *side)
    return outs[:5], outs[5:]


def _rope_tables(s):
    pos = jnp.arange(s, dtype=I32)
    inv = 1.0 / (ROPE_BASE ** (jnp.arange(ROPE_FREQS, dtype=F32) / ROPE_FREQS))
    ang_r = (pos // GRID_W).astype(F32)[:, None] * inv[None, :]
    ang_c = (pos % GRID_W).astype(F32)[:, None] * inv[None, :]
    cr, sr, cc, sn = jnp.cos(ang_r), jnp.sin(ang_r), jnp.cos(ang_c), jnp.sin(ang_c)
    z = jnp.zeros_like(sr)
    reps = LANES // HEAD_DIM
    cos = jnp.tile(jnp.concatenate([cr, cr, cc, cc], -1), (1, reps))
    s_up = jnp.tile(jnp.concatenate([-sr, z, -sn, z], -1), (1, reps))
    s_dn = jnp.tile(jnp.concatenate([z, sr, z, sn], -1), (1, reps))
    return cos, s_up, s_dn


def _attend_body(sink_ref, q_ref, kvp_ref, kvc_ref, kvn_ref, ctx_ref, o_ref):
    i = pl.program_id(1)
    n_steps = pl.num_programs(1)
    per = q_ref.shape[0] // BLOCK
    rows = GROUP * BLOCK
    r = _iota((rows, 3 * BLOCK), 0) & (BLOCK - 1)
    m = _iota((rows, 3 * BLOCK), 1)
    band = (m >= r) & (m <= r + 2 * BLOCK)
    grp = _iota((rows, 1), 0) // BLOCK
    kv_all = jnp.concatenate([kvp_ref[...], kvc_ref[...], kvn_ref[...]], axis=0)
    kv_ctx = ctx_ref[...]
    for t in range(per):
        mask = band
        if t == 0:
            mask = mask & ((m >= BLOCK) | (i > 0))
        if t == per - 1:
            mask = mask & ((m < 2 * BLOCK) | (i < n_steps - 1))
        q = q_ref[t * BLOCK:(t + 1) * BLOCK, :]
        kv_loc = kv_all[t * BLOCK:(t + 3) * BLOCK]
        scores = []
        for h in range(KV_HEADS):
            ks = slice(h * HEAD_DIM, (h + 1) * HEAD_DIM)
            qg = jnp.concatenate(
                [q[:, (h * GROUP + g) * HEAD_DIM:(h * GROUP + g + 1) * HEAD_DIM] for g in range(GROUP)], axis=0)
            scores.append((_dot_nt(qg, kv_ctx[:, ks]), jnp.where(mask, _dot_nt(qg, kv_loc[:, ks]), NEG_INF)))
        probs = []
        for h, (s_ctx, s_loc) in enumerate(scores):
            sink = jnp.zeros((rows, 1), F32)
            for g in range(GROUP):
                sink = jnp.where(grp == g, sink_ref[h * GROUP + g], sink)
            top = jnp.maximum(jnp.maximum(jnp.max(s_ctx, -1, keepdims=True), jnp.max(s_loc, -1, keepdims=True)), sink)
            e_ctx = jnp.exp(s_ctx - top)
            e_loc = jnp.exp(s_loc - top)
            den = jnp.sum(e_ctx, -1, keepdims=True) + jnp.sum(e_loc, -1, keepdims=True) + jnp.exp(sink - top)
            probs.append((e_ctx.astype(BF16), e_loc.astype(BF16), 1.0 / den))
        outs = []
        for h, (e_ctx, e_loc, inv) in enumerate(probs):
            vs = slice(KV_W + h * HEAD_DIM, KV_W + (h + 1) * HEAD_DIM)
            o = (_dot(e_ctx, kv_ctx[:, vs]) + _dot(e_loc, kv_loc[:, vs])) * inv
            outs += [o[g * BLOCK:(g + 1) * BLOCK] for g in range(GROUP)]
        o_ref[t * BLOCK:(t + 1) * BLOCK, :] = jnp.concatenate(outs, axis=1).astype(BF16)


def _attend(q, kv, kv_ctx, sink, per):
    b, s, _ = q.shape
    nb = s // BLOCK
    c_len = kv_ctx.shape[1]
    kvw = 2 * KV_W
    return pl.pallas_call(
        _attend_body,
        out_shape=jax.ShapeDtypeStruct((b, s, ATTN_W), BF16),
        grid=(b, nb // per),
        in_specs=[
            pl.BlockSpec(memory_space=pltpu.SMEM),
            pl.BlockSpec((None, per * BLOCK, ATTN_W), lambda i, j: (i, j, 0)),
            pl.BlockSpec((None, BLOCK, kvw), lambda i, j: (i, jnp.maximum(j * per - 1, 0), 0)),
            pl.BlockSpec((None, per * BLOCK, kvw), lambda i, j: (i, j, 0)),
            pl.BlockSpec((None, BLOCK, kvw), lambda i, j: (i, jnp.minimum((j + 1) * per, nb - 1), 0)),
            pl.BlockSpec((None, c_len, kvw), lambda i, j: (i, 0, 0)),
        ],
        out_specs=pl.BlockSpec((None, per * BLOCK, ATTN_W), lambda i, j: (i, j, 0)),
        compiler_params=_params("arbitrary", "arbitrary"),
        name="attend",
    )(sink, q, kv, kv, kv, kv_ctx)


def _route(logits):
    lane = _iota(logits.shape, 1)
    far = jnp.int32(2 * ROUTE_W)

    def first_max(vals):
        top = jnp.max(vals, -1, keepdims=True)
        return top, jnp.min(jnp.where(vals == top, lane, far), -1, keepdims=True)

    is_g = lane < N_GROUPS
    g_top, g_idx = first_max(jnp.where(is_g, logits, NEG_INF))
    g_w = 1.0 / jnp.sum(jnp.where(is_g, jnp.exp(logits - g_top), 0.0), -1, keepdims=True)
    lo = N_GROUPS + EPG * g_idx
    e_vals = jnp.where((lane >= lo) & (lane < lo + EPG), logits, NEG_INF)
    v1, i1 = first_max(e_vals)
    v2, i2 = first_max(jnp.where(lane == i1, NEG_INF, e_vals))
    t = jnp.exp(v2 - v1)
    w1 = g_w * (1.0 / (1.0 + t))
    w2 = g_w * (t / (1.0 + t))
    swap = i2 < i1
    a = jnp.where(swap, i2, i1) - lo
    b = jnp.where(swap, i1, i2) - lo
    cls = g_idx * PAIRS + ((a * (2 * EPG - 1 - a)) >> 1) + (b - a - 1)
    return jnp.where(swap, w2, w1), jnp.where(swap, w1, w2), cls


def _merge_body(x_ref, ya_ref, u_ref, vg_ref, gate_ref, ws_ref, bs_ref, wpa_ref, wpb_ref, wo_ref,
                lng_ref, lnb_ref, g1_ref, sh2_ref, sc2_ref, wrh_ref, wrl_ref, br_ref, x1_ref, rec_ref, cnt_ref):
    tile, d = x_ref.shape
    lane = _iota((BLOCK, LANES), 1)
    below = (lane < _iota((BLOCK, LANES), 0)).astype(BF16)
    chunks = []
    for c in range(tile // BLOCK):
        rows = slice(c * BLOCK, (c + 1) * BLOCK)
        pieces = []
        for j in range(GM_W // LANES):
            v = vg_ref[rows, j * LANES:(j + 1) * LANES]
            per = LANES // GM_HEAD
            sp = _dot(ws_ref[per * j], v)
            for k in range(1, per):
                sp = jnp.where(lane < k * GM_HEAD, sp, _dot(ws_ref[per * j + k], v))
            pieces.append(sp)
        chunks.append(jnp.concatenate(pieces, axis=1) + bs_ref[...])
    sp = jnp.concatenate(chunks, axis=0)
    y_gm = (u_ref[...].astype(F32) * sp).astype(BF16)
    gate = gate_ref[...].astype(F32)
    y = gate[:, :d] * _dot(ya_ref[...], wpa_ref[...]) + gate[:, d:] * _dot(y_gm, wpb_ref[...])
    mix = _dot(y.astype(BF16), wo_ref[...])
    x1 = _ln(ALPHA * x_ref[...] + g1_ref[...] * mix) * lng_ref[...] + lnb_ref[...]
    x1_ref[...] = x1
    h2 = _ln(x1) * (1.0 + sc2_ref[...]) + sh2_ref[...]
    w_lo, w_hi, cls = _route(_dot_split(h2, wrh_ref[...], wrl_ref[...]) + br_ref[...])
    for c in range(tile // BLOCK):
        rows = slice(c * BLOCK, (c + 1) * BLOCK)
        onehot = lane == cls[rows]
        hot = onehot.astype(BF16)
        rank = jnp.sum(jnp.where(onehot, _dot(below, hot), 0.0), -1, keepdims=True)
        cnt_ref[c] = jnp.sum(hot.astype(F32), 0, keepdims=True)
        rec = jnp.where(lane == REC_WLO, w_lo[rows], 0.0)
        rec = jnp.where(lane == REC_WHI, w_hi[rows], rec)
        rec = jnp.where(lane == REC_CLS, cls[rows].astype(F32), rec)
        rec_ref[rows, :] = jnp.where(lane == REC_RANK, rank, rec)


def _merge(x, y_attn, u, vg, gates, ws_bf, bs_rows, wpa, wpb, wo, ln_g, ln_b, mod3, wr_hi, wr_lo, br, tile):
    b, s, d = x.shape
    nt = s // tile
    per = tile // BLOCK
    row = lambda width: pl.BlockSpec((None, tile, width), lambda i, j: (i, j, 0))
    full = lambda a: pl.BlockSpec(a.shape, lambda i, j: (0,) * a.ndim)
    modv = lambda k: pl.BlockSpec((None, 1, d), lambda i, j: (i, 0, k))
    ln_g = ln_g.reshape(1, d)
    ln_b = ln_b.reshape(1, d)
    return pl.pallas_call(
        _merge_body,
        out_shape=[jax.ShapeDtypeStruct((b, s, d), F32), jax.ShapeDtypeStruct((b, s, ROUTE_W), F32),
                   jax.ShapeDtypeStruct((b * s // BLOCK, 1, LANES), F32)],
        grid=(b, nt),
        in_specs=[
            row(d), row(ATTN_W), row(GM_W), row(GM_W), row(2 * d),
            full(ws_bf), full(bs_rows), full(wpa), full(wpb), full(wo), full(ln_g), full(ln_b),
            modv(2), modv(3), modv(4),
            full(wr_hi), full(wr_lo), full(br),
        ],
        out_specs=[row(d), row(ROUTE_W), pl.BlockSpec((per, 1, LANES), lambda i, j: (i * nt + j, 0, 0))],
        compiler_params=_params("arbitrary", "arbitrary"),
        name="merge",
    )(x, y_attn, u, vg, gates, ws_bf, bs_rows, wpa, wpb, wo, ln_g, ln_b, mod3, mod3, mod3, wr_hi, wr_lo, br)


def _plan_body(cnt_ref, rec_ref, lpos_ref, tab_ref, cblk_ref, run_ref, cntt_ref, lst_ref, base_ref, lstf_ref, *, chunks):
    step = pl.program_id(0)
    n_chunks = cnt_ref.shape[0]
    n_tiles = run_ref.shape[0]
    max_blocks = tab_ref.shape[0]

    @pl.when(step == 0)
    def _():
        cnt = cnt_ref[...]
        total = jnp.sum(cnt, 0, keepdims=True)
        n_blk = jnp.floor((total + (BLOCK - 1)) * (1.0 / BLOCK))
        before = (_iota((LANES, LANES), 0) < _iota((LANES, LANES), 1)).astype(BF16)
        nb_hi, nb_lo = _split(jnp.broadcast_to(n_blk, (SUBLANES, LANES)))
        blk_start = (_dot(nb_hi, before) + _dot(nb_lo, before))[0:1]
        lane1 = _iota((1, LANES), 1)
        packed = blk_start
        shift = jnp.zeros((1, 1), F32)
        for g in range(1, N_GROUPS):
            first = jnp.sum(jnp.where(lane1 == g * PAIRS, blk_start, 0.0), -1, keepdims=True) + shift
            over = first - MOE_PACK * jnp.floor(first * (1.0 / MOE_PACK))
            shift = shift + jnp.where(over > 0.0, MOE_PACK - over, 0.0)
            packed = jnp.where(lane1 >= g * PAIRS, blk_start + shift, packed)
        blk_start = packed
        earlier = (_iota((n_chunks, n_chunks), 1) < _iota((n_chunks, n_chunks), 0)).astype(BF16)
        base_ref[...] = blk_start * float(BLOCK) + _dot(earlier, cnt.astype(BF16))
        blk_end = blk_start + n_blk
        sub = _iota((SUBLANES, LANES), 0)
        cblk_ref[...] = jnp.where(sub == 0, blk_start, jnp.where(sub == 1, blk_end, 0.0)).astype(I32)
        lane = _iota((max_blocks, LANES), 1)
        blk = _iota((max_blocks, LANES), 0).astype(F32)
        is_cls = lane < N_CLASSES
        n_real = jnp.max(jnp.where(is_cls, blk_end, 0.0), -1, keepdims=True)
        u = jnp.minimum(blk, n_real - 1.0)
        cls_of = jnp.sum(jnp.where(is_cls & (blk_end <= u), 1.0, 0.0), -1, keepdims=True)
        n_steps = jnp.floor((n_real + (MOE_PACK - 1)) * (1.0 / MOE_PACK))
        tab = jnp.where(lane == 0, cls_of, jnp.where(lane == 1, n_steps, 0.0))
        tab_ref[...] = tab.astype(I32)
        tile = _iota((n_tiles, n_chunks), 0)
        chunk = _iota((n_tiles, n_chunks), 1)
        owner = ((chunk >= tile * chunks) & (chunk < (tile + 1) * chunks)).astype(BF16)
        cnt_tile = _dot(owner, cnt.astype(BF16))
        ct_hi, ct_lo = _split(cnt_tile)
        local_start = _dot(ct_hi, before) + _dot(ct_lo, before)
        run_ref[...] = base_ref[pl.ds(0, n_tiles, stride=chunks), :].astype(I32)
        cntt_ref[...] = cnt_tile.astype(I32)
        lst_ref[...] = local_start.astype(I32)
        lstf_ref[...] = local_start

    lane = _iota((BLOCK, LANES), 1)
    tile_off = lstf_ref[pl.ds(step, 1), :] - base_ref[pl.ds(step * chunks, 1), :]
    pick_rank = (_iota((2 * SUBLANES, LANES), 1) == REC_RANK).astype(BF16)
    for c in range(chunks):
        rec = rec_ref[c * BLOCK:(c + 1) * BLOCK, :]
        hot = (lane == rec[:, REC_CLS:REC_CLS + 1].astype(I32)).astype(BF16)
        base = base_ref[pl.ds(step * chunks + c, 1), :] + tile_off
        b_hi, b_lo = _split(jnp.broadcast_to(base, (2 * SUBLANES, LANES)))
        lpos = _dot_nt(b_hi, hot) + _dot_nt(b_lo, hot) + _dot_nt(pick_rank, rec.astype(BF16))
        lpos_ref[c] = lpos[0:1].astype(I32)


def _plan(counts, rec, max_blocks, chunks):
    n = rec.shape[0]
    n_chunks = n // BLOCK
    n_tiles = n_chunks // chunks
    tile_tab = jax.ShapeDtypeStruct((n_tiles, LANES), I32)
    whole = lambda rows: pl.BlockSpec((rows, LANES), lambda i: (0, 0))
    return pl.pallas_call(
        functools.partial(_plan_body, chunks=chunks),
        out_shape=[jax.ShapeDtypeStruct((n_chunks, 1, LANES), I32), jax.ShapeDtypeStruct((max_blocks, LANES), I32),
                   jax.ShapeDtypeStruct((SUBLANES, LANES), I32), tile_tab, tile_tab, tile_tab],
        grid=(n_tiles,),
        in_specs=[whole(n_chunks), pl.BlockSpec((chunks * BLOCK, ROUTE_W), lambda i: (i, 0))],
        out_specs=[pl.BlockSpec((chunks, 1, LANES), lambda i: (i, 0, 0)), whole(max_blocks), whole(SUBLANES),
                   whole(n_tiles), whole(n_tiles), whole(n_tiles)],
        scratch_shapes=[pltpu.VMEM((n_chunks, LANES), F32), pltpu.VMEM((n_tiles, LANES), F32)],
        compiler_params=_params("arbitrary"),
        name="plan",
    )(counts, rec)


def _tile_rows(base, j, pitch):
    return pl.ds(base + j, BLOCK, stride=pitch)


def _load_tiles(ref, base, pitch, width):
    return jnp.concatenate([ref[_tile_rows(base, j, pitch), :] for j in range(width // LANES)], axis=1)


def _store_tiles(ref, base, pitch, val):
    for j in range(val.shape[1] // LANES):
        ref[_tile_rows(base, j, pitch), :] = val[:, j * LANES:(j + 1) * LANES]


def _start_runs(cnt_ref, lst_ref, run_ref, streams, sem, to_hbm):
    def issue(c):
        for buf, base, hbm, pitch in streams:
            rows = cnt_ref[0, c] * pitch
            aligned = (lambda row: pl.multiple_of(row, pitch)) if pitch > 1 else (lambda row: row)
            v = buf.at[pl.ds(aligned(base + lst_ref[0, c] * pitch), rows)]
            h = hbm.at[pl.ds(aligned(run_ref[0, c] * pitch), rows)]
            (pltpu.make_async_copy(v, h, sem) if to_hbm else pltpu.make_async_copy(h, v, sem)).start(priority=c % 2)

    for c in range(N_CLASSES):
        pl.when(cnt_ref[0, c] > 0)(functools.partial(issue, c))


def _wait_runs(streams, tokens, sem):
    for buf, base, hbm, pitch in streams:
        rows = tokens * pitch
        pltpu.make_async_copy(hbm.at[pl.ds(0, rows)], buf.at[pl.ds(base, rows)], sem).wait()


def _scatter_body(cls_blk_ref, run_ref, cnt_ref, lst_ref, lpos_ref, x1_ref, rec_ref, sh_ref, sc_ref, xs_hbm, meta_hbm,
                  abuf, bbuf, mbuf, zbuf, ssem, zsem):
    u = pl.program_id(0)
    slot = u % 2
    tokens = x1_ref.shape[0]
    chunks = tokens // BLOCK
    tile_rows = BLOCK * SUBLANES
    sort_rows = chunks * tile_rows

    def streams(s):
        return ((bbuf, s * sort_rows, xs_hbm, SUBLANES), (mbuf, s * tokens, meta_hbm, 1))

    @pl.when(u == 0)
    def _():
        zbuf[...] = jnp.zeros_like(zbuf)

        def zero_block(blk, start):
            for hbm, rows in ((xs_hbm, tile_rows), (meta_hbm, BLOCK)):
                cp = pltpu.make_async_copy(zbuf.at[pl.ds(0, rows)], hbm.at[pl.ds(pl.multiple_of(blk * rows, rows), rows)],
                                           zsem)
                cp.start() if start else cp.wait()

        for start in (True, False):
            for c in range(N_CLASSES):
                pl.when(cls_blk_ref[1, c] > cls_blk_ref[0, c])(functools.partial(zero_block, cls_blk_ref[1, c] - 1, start))
        gaps = [(cls_blk_ref[1, g * PAIRS - 1], cls_blk_ref[0, g * PAIRS]) for g in range(1, N_GROUPS)]
        gaps.append((cls_blk_ref[1, N_CLASSES - 1], xs_hbm.shape[0] // tile_rows))
        for start in (True, False):
            for lo, hi in gaps:
                lax.fori_loop(lo, hi, lambda blk, carry: (zero_block(blk, start), carry)[1], 0)

    @pl.when(u >= 2)
    def _():
        _wait_runs(streams(slot), tokens, ssem.at[slot])

    scale = 1.0 + sc_ref[...]
    shift = sh_ref[...]

    unroll = abuf.shape[0] // tile_rows

    def sort_chunks(i, carry):
        for k in range(unroll):
            c = i * unroll + k
            stage = k * tile_rows
            rows = pl.ds(pl.multiple_of(c * BLOCK, BLOCK), BLOCK)
            _store_tiles(abuf, stage, SUBLANES, _ln(x1_ref[rows, :]) * scale + shift)
            rec = rec_ref[rows, :]
            for r in range(BLOCK):
                p = lpos_ref[c, 0, r]
                dst = pl.multiple_of(slot * sort_rows + p * SUBLANES, SUBLANES)
                bbuf[pl.ds(dst, SUBLANES), :] = abuf[stage + r * SUBLANES:stage + (r + 1) * SUBLANES, :]
                mbuf[pl.ds(slot * tokens + p, 1), :] = rec[r:r + 1, :]
        return carry

    lax.fori_loop(0, chunks // unroll, sort_chunks, 0)
    _start_runs(cnt_ref, lst_ref, run_ref, streams(slot), ssem.at[slot], True)

    @pl.when(u == pl.num_programs(0) - 1)
    def _():
        _wait_runs(streams(slot), tokens, ssem.at[slot])

        @pl.when(u >= 1)
        def _():
            _wait_runs(streams(1 - slot), tokens, ssem.at[1 - slot])


def _stage_chunks(chunks):
    return next(p for p in (4, 2, 1) if chunks % p == 0)


def _tile_tab_spec(off, n_tiles):
    return pl.BlockSpec((None, 1, LANES), lambda u: (jnp.minimum(u + off, n_tiles - 1), 0, 0),
                        memory_space=pltpu.SMEM)


def _scatter(x1, rec, mod3, lpos, tile_tabs, cls_blk, max_blocks, seq, chunks):
    n, d = x1.shape
    n_tiles = n // (chunks * BLOCK)
    per_batch = seq // (chunks * BLOCK)
    tile_rows = BLOCK * SUBLANES
    hbm = pl.BlockSpec(memory_space=pl.ANY)
    return pl.pallas_call(
        _scatter_body,
        out_shape=[jax.ShapeDtypeStruct((max_blocks * tile_rows, LANES), F32),
                   jax.ShapeDtypeStruct((max_blocks * BLOCK, ROUTE_W), F32)],
        grid=(n_tiles,),
        in_specs=[
            pl.BlockSpec(memory_space=pltpu.SMEM),
            _tile_tab_spec(0, n_tiles), _tile_tab_spec(0, n_tiles), _tile_tab_spec(0, n_tiles),
            pl.BlockSpec((chunks, 1, LANES), lambda u: (u, 0, 0), memory_space=pltpu.SMEM),
            pl.BlockSpec((chunks * BLOCK, d), lambda u: (u, 0)),
            pl.BlockSpec((chunks * BLOCK, ROUTE_W), lambda u: (u, 0)),
            pl.BlockSpec((None, 1, d), lambda u: (u // per_batch, 0, 3)),
            pl.BlockSpec((None, 1, d), lambda u: (u // per_batch, 0, 4)),
        ],
        out_specs=[hbm, hbm],
        scratch_shapes=[
            pltpu.VMEM((_stage_chunks(chunks) * tile_rows, LANES), F32),
            pltpu.VMEM((2 * chunks * tile_rows, LANES), F32),
            pltpu.VMEM((2 * chunks * BLOCK, ROUTE_W), F32),
            pltpu.VMEM((tile_rows, LANES), F32),
            pltpu.SemaphoreType.DMA((2,)),
            pltpu.SemaphoreType.DMA,
        ],
        compiler_params=_params("arbitrary"),
        name="scatter",
    )(cls_blk, *tile_tabs, lpos, x1, rec, mod3, mod3)


def _expert(h, w1, w3, w2):
    hid = jax.nn.silu(_dot(h, w1)) * _dot(h, w3)
    return _dot(hid, w2)


def _moe_body(sgrp_ref, blo_ref, bhi_ref, nsteps_ref, xs_ref, meta_ref, w1_ref, w3_ref, w2_ref, o_ref):
    u = pl.program_id(0)
    d = w1_ref.shape[1]
    tile_rows = BLOCK * SUBLANES

    @pl.when(u < nsteps_ref[0])
    def _():
        jobs = [(k, e) for k in range(MOE_PACK) for e in (blo_ref[MOE_PACK * u + k], bhi_ref[MOE_PACK * u + k])]
        h = [_load_tiles(xs_ref, k * tile_rows, SUBLANES, d).astype(BF16) for k in range(MOE_PACK)]
        up = [(_dot(h[k], w1_ref[e]), _dot(h[k], w3_ref[e])) for k, e in jobs]
        hid = [(jax.nn.silu(a) * b).astype(BF16) for a, b in up]
        y = [_dot(g, w2_ref[e]) for (k, e), g in zip(jobs, hid)]
        for k in range(MOE_PACK):
            meta = meta_ref[k * BLOCK:(k + 1) * BLOCK, :]
            _store_tiles(o_ref, k * tile_rows, SUBLANES,
                         meta[:, REC_WLO:REC_WLO + 1] * y[2 * k] + meta[:, REC_WHI:REC_WHI + 1] * y[2 * k + 1])

    @pl.when(u >= nsteps_ref[0])
    def _():
        o_ref[...] = jnp.zeros_like(o_ref)


def _moe(xs, meta, tables, w1, w3, w2, d):
    sgrp, blo, bhi, n_steps = tables
    max_steps = sgrp.shape[0]
    de = w1.shape[1]
    in_rows = MOE_PACK * BLOCK * SUBLANES
    w1g = w1.reshape(N_GROUPS, EPG, d, de)
    w3g = w3.reshape(N_GROUPS, EPG, d, de)
    w2g = w2.reshape(N_GROUPS, EPG, de, d)

    def w_spec(a):
        return pl.BlockSpec((None,) + a.shape[1:], lambda u, g, lo, hi, n_steps: (g[u], 0, 0, 0))

    grid_spec = pltpu.PrefetchScalarGridSpec(
        num_scalar_prefetch=4,
        grid=(max_steps,),
        in_specs=[
            pl.BlockSpec((in_rows, LANES), lambda u, g, lo, hi, n_steps: (jnp.minimum(u, n_steps[0] - 1), 0)),
            pl.BlockSpec((MOE_PACK * BLOCK, ROUTE_W), lambda u, g, lo, hi, n_steps: (jnp.minimum(u, n_steps[0] - 1), 0)),
            w_spec(w1g), w_spec(w3g), w_spec(w2g),
        ],
        out_specs=pl.BlockSpec((in_rows, LANES), lambda u, *_: (u, 0)),
    )
    return pl.pallas_call(
        _moe_body,
        out_shape=jax.ShapeDtypeStruct((max_steps * in_rows, LANES), F32),
        grid_spec=grid_spec,
        compiler_params=pltpu.CompilerParams(dimension_semantics=("arbitrary",), vmem_limit_bytes=MOE_VMEM_LIMIT_BYTES),
        name="moe",
    )(sgrp, blo, bhi, n_steps, xs, meta, w1g, w3g, w2g)


def _final_body(run_ref, cnt_ref, lst_ref, runn_ref, cntn_ref, lstn_ref, lpos_ref, x1_ref, g2_ref,
                lng_ref, lnb_ref, ms_hbm, o_ref, gbuf, tbuf, gsem):
    u = pl.program_id(0)
    slot = u % 2
    d = x1_ref.shape[-1]
    tokens = x1_ref.shape[0]
    chunks = tokens // BLOCK
    tile_rows = BLOCK * SUBLANES
    sort_rows = chunks * tile_rows

    def streams(s):
        return ((gbuf, s * sort_rows, ms_hbm, SUBLANES),)

    @pl.when(u == 0)
    def _():
        _start_runs(cnt_ref, lst_ref, run_ref, streams(0), gsem.at[0], False)

    _wait_runs(streams(slot), tokens, gsem.at[slot])
    _start_runs(cntn_ref, lstn_ref, runn_ref, streams(1 - slot), gsem.at[1 - slot], False)
    gain = g2_ref[...]
    ln_g = lng_ref[...]
    ln_b = lnb_ref[...]
    unroll = tbuf.shape[0] // tile_rows

    def finish_chunks(i, carry):
        for k in range(unroll):
            c = i * unroll + k
            stage = k * tile_rows
            for r in range(BLOCK):
                src = pl.ds(pl.multiple_of(slot * sort_rows + lpos_ref[c, 0, r] * SUBLANES, SUBLANES), SUBLANES)
                tbuf[stage + r * SUBLANES:stage + (r + 1) * SUBLANES, :] = gbuf[src, :]
            rows = pl.ds(pl.multiple_of(c * BLOCK, BLOCK), BLOCK)
            moe = _load_tiles(tbuf, stage, SUBLANES, d)
            o_ref[rows, :] = _ln(ALPHA * x1_ref[rows, :] + gain * moe) * ln_g + ln_b
        return carry

    lax.fori_loop(0, chunks // unroll, finish_chunks, 0)

    @pl.when(u == pl.num_programs(0) - 1)
    def _():
        _wait_runs(streams(1 - slot), tokens, gsem.at[1 - slot])


def _final(x1, ms, lpos, tile_tabs, mod3, ln_g, ln_b, seq, chunks):
    n, d = x1.shape
    n_tiles = n // (chunks * BLOCK)
    per_batch = seq // (chunks * BLOCK)
    tile_rows = BLOCK * SUBLANES
    vec = pl.BlockSpec((1, d), lambda u: (0, 0))
    return pl.pallas_call(
        _final_body,
        out_shape=jax.ShapeDtypeStruct((n, d), F32),
        grid=(n_tiles,),
        in_specs=[
            _tile_tab_spec(0, n_tiles), _tile_tab_spec(0, n_tiles), _tile_tab_spec(0, n_tiles),
            _tile_tab_spec(1, n_tiles), _tile_tab_spec(1, n_tiles), _tile_tab_spec(1, n_tiles),
            pl.BlockSpec((chunks, 1, LANES), lambda u: (u, 0, 0), memory_space=pltpu.SMEM),
            pl.BlockSpec((chunks * BLOCK, d), lambda u: (u, 0)),
            pl.BlockSpec((None, 1, d), lambda u: (u // per_batch, 0, 5)),
            vec, vec, pl.BlockSpec(memory_space=pl.ANY),
        ],
        out_specs=pl.BlockSpec((chunks * BLOCK, d), lambda u: (u, 0)),
        scratch_shapes=[
            pltpu.VMEM((2 * chunks * tile_rows, LANES), F32),
            pltpu.VMEM((_stage_chunks(chunks) * tile_rows, LANES), F32),
            pltpu.SemaphoreType.DMA((2,)),
        ],
        compiler_params=_params("arbitrary"),
        name="final",
    )(*tile_tabs, *tile_tabs, lpos, x1, mod3, ln_g.reshape(1, d), ln_b.reshape(1, d), ms)


def _block_tables(tab):
    cls = tab[:, 0]
    pair = cls % PAIRS
    first = jnp.array([i for i in range(EPG) for _ in range(i + 1, EPG)], I32)
    second = jnp.array([j for i in range(EPG) for j in range(i + 1, EPG)], I32)
    return (cls // PAIRS)[::MOE_PACK], first[pair], second[pair], tab[0, 1].reshape(1)


def kernel(x, c, ctx, c_ctx, w_ada, b_ada, w_in, attn_sink, gm_ln_g, gm_ln_b, gm_ws, gm_bs, w_pa, w_pb, w_o,
           ln1_g, ln1_b, router_g_w, router_g_b, router_e_w, router_e_b, moe_w1, moe_w3, moe_w2, ln2_g, ln2_b):
    b, s, d = x.shape
    assert w_ada.shape[0] == DEPTH and s % BLOCK == 0 and s % GRID_W == 0
    mod_rows = 2 * SUBLANES
    assert b + 1 <= mod_rows
    c_rows = jnp.zeros((mod_rows, d), F32).at[:b].set(c).at[b].set(c_ctx)
    mod = _adaln_mod(c_rows, w_ada[0], b_ada[0])
    mod3 = mod.reshape(mod_rows, 1, 6 * d)
    w_in_bf = w_in[0].astype(BF16)
    kv_ctx = _ctx_kv(ctx, mod3, b, w_in_bf)
    tile = next(t for t in (512, BLOCK) if s % t == 0)
    experts, _, d_expert = moe_w1.shape[1:]
    moe_w = [moe_w1[0].reshape(experts * d, d_expert), moe_w3[0].reshape(experts * d, d_expert),
             moe_w2[0].reshape(experts * d_expert, d)]
    steps = b * (s // tile)
    sliced = all(w.shape[0] % (steps * 2 * SUBLANES) == 0 for w in moe_w)
    (q, kv, u, vg, gates), moe_w_bf = _project(x, mod3, w_in_bf, _rope_tables(s), gm_ln_g[0], gm_ln_b[0], tile,
                                                moe_w if sliced else [])
    if not sliced:
        moe_w_bf = [w.astype(BF16) for w in moe_w]
    nb = s // BLOCK
    y_attn = _attend(q, kv, kv_ctx, attn_sink[0], next(p for p in (4, 2, 1) if nb % p == 0))
    bs_rows = jnp.repeat(gm_bs[0].T, GM_HEAD, axis=1)
    w_route = jnp.concatenate([router_g_w[0], jnp.transpose(router_e_w[0], (1, 0, 2)).reshape(d, N_EXPERTS)], axis=1)
    w_route = jnp.pad(w_route, ((0, 0), (0, ROUTE_W - w_route.shape[1])))
    b_route = jnp.pad(jnp.concatenate([router_g_b[0], router_e_b[0].reshape(-1)]), (0, ROUTE_W - N_GROUPS - N_EXPERTS))
    wr_hi = w_route.astype(BF16)
    wr_lo = (w_route - wr_hi.astype(F32)).astype(BF16)
    merge_tile = BLOCK * next(p for p in (4, 2, 1) if nb % p == 0)
    x1, rec, counts = _merge(x, y_attn, u, vg, gates, gm_ws[0].astype(BF16), bs_rows, w_pa[0].astype(BF16),
                             w_pb[0].astype(BF16), w_o[0].astype(BF16), ln1_g[0], ln1_b[0], mod3, wr_hi, wr_lo,
                             b_route.reshape(1, ROUTE_W), merge_tile)
    n = b * s
    x1 = x1.reshape(n, d)
    rec = rec.reshape(n, ROUTE_W)
    max_blocks = n // BLOCK + N_CLASSES - 1 + (N_GROUPS - 1) * (MOE_PACK - 1)
    max_blocks = -(-max_blocks // MOE_PACK) * MOE_PACK
    sort_chunks = next(p for p in (SUBLANES, 4, 2, 1) if nb % p == 0)
    lpos, tab, cls_blk, *tile_tabs = _plan(counts.reshape(n // BLOCK, LANES), rec, max_blocks, sort_chunks)
    tile_tabs = [t.reshape(t.shape[0], 1, LANES) for t in tile_tabs]
    xs, meta = _scatter(x1, rec, mod3, lpos, tile_tabs, cls_blk, max_blocks, s, sort_chunks)
    ms = _moe(xs, meta, _block_tables(tab), *moe_w_bf, d)
    out = _final(x1, ms, lpos, tile_tabs, mod3, ln2_g[0], ln2_b[0], s, sort_chunks)
    return out.reshape(b, s, d)
```

```python
import functools

import jax
import jax.numpy as jnp
from jax import lax
from jax.experimental import pallas as pl
from jax.experimental.pallas import tpu as pltpu

F32 = jnp.float32
BF16 = jnp.bfloat16
I32 = jnp.int32

GRID_W = 64
HEAD_DIM = 64
Q_HEADS = 8
KV_HEADS = 2
GROUP = Q_HEADS // KV_HEADS
BLOCK = 128
ROPE_BASE = 10000.0
ROPE_FREQS = HEAD_DIM // 4
GM_GROUPS = 8
GM_HEAD = 64
ATTN_W = Q_HEADS * HEAD_DIM
KV_W = KV_HEADS * HEAD_DIM
GM_W = GM_GROUPS * GM_HEAD
N_GROUPS = 4
EPG = 8
N_EXPERTS = N_GROUPS * EPG
PAIRS = EPG * (EPG - 1) // 2
N_CLASSES = N_GROUPS * PAIRS
LN_EPS = 1e-6
NEG_INF = -1e30
DEPTH = 1
ALPHA = (2.0 * DEPTH) ** 0.25

LANES = 128
SUBLANES = 8
VMEM_LIMIT_BYTES = 56 * 1024 * 1024
MOE_VMEM_LIMIT_BYTES = 60 * 1024 * 1024
MOE_PACK = 4
ROUTE_W = LANES
REC_WLO, REC_WHI, REC_CLS, REC_RANK = 0, 1, 2, 3


def _params(*sem):
    return pltpu.CompilerParams(dimension_semantics=sem, vmem_limit_bytes=VMEM_LIMIT_BYTES)


def _ln(x):
    mu = jnp.mean(x, -1, keepdims=True)
    xc = x - mu
    var = jnp.mean(xc * xc, -1, keepdims=True)
    return xc * lax.rsqrt(var + LN_EPS)


def _dot(a, b):
    return jnp.dot(a, b, preferred_element_type=F32)


def _dot_nt(a, b):
    return lax.dot_general(a, b, (((1,), (1,)), ((), ())), preferred_element_type=F32)


def _split(a):
    hi = a.astype(BF16)
    lo = (a - hi.astype(F32)).astype(BF16)
    return hi, lo


def _dot_split(a, b_hi, b_lo):
    a_hi, a_lo = _split(a)
    return _dot(a_hi, b_hi) + _dot(a_hi, b_lo) + _dot(a_lo, b_hi)


def _iota(shape, axis):
    return lax.broadcasted_iota(I32, shape, axis)


def _mod_body(c_ref, w_ref, b_ref, o_ref):
    w_hi, w_lo = _split(w_ref[...])
    o_ref[...] = _dot_split(jax.nn.silu(c_ref[...]), w_hi, w_lo) + b_ref[...]


def _adaln_mod(c_rows, w_ada, b_ada):
    rows, d = c_rows.shape
    width = w_ada.shape[1]
    tn = 6 * LANES
    return pl.pallas_call(
        _mod_body,
        out_shape=jax.ShapeDtypeStruct((rows, width), F32),
        grid=(width // tn,),
        in_specs=[
            pl.BlockSpec((rows, d), lambda j: (0, 0)),
            pl.BlockSpec((d, tn), lambda j: (0, j)),
            pl.BlockSpec((1, tn), lambda j: (0, j)),
        ],
        out_specs=pl.BlockSpec((rows, tn), lambda j: (0, j)),
        compiler_params=_params("arbitrary"),
        name="mod",
    )(c_rows, w_ada, b_ada.reshape(1, width))


def _ctx_body(ctx_ref, sh_ref, sc_ref, w_ref, o_ref):
    h = _ln(ctx_ref[...]) * (1.0 + sc_ref[...]) + sh_ref[...]
    o_ref[...] = _dot(h.astype(BF16), w_ref[...]).astype(BF16)


def _ctx_kv(ctx, mod3, ctx_row, w_in_bf):
    b, c_len, d = ctx.shape
    kvw = 2 * KV_W
    return pl.pallas_call(
        _ctx_body,
        out_shape=jax.ShapeDtypeStruct((b, c_len, kvw), BF16),
        grid=(b,),
        in_specs=[
            pl.BlockSpec((None, c_len, d), lambda i: (i, 0, 0)),
            pl.BlockSpec((None, 1, d), lambda i: (ctx_row, 0, 0)),
            pl.BlockSpec((None, 1, d), lambda i: (ctx_row, 0, 1)),
            pl.BlockSpec((d, kvw), lambda i: (0, ATTN_W // kvw)),
        ],
        out_specs=pl.BlockSpec((None, c_len, kvw), lambda i: (i, 0, 0)),
        compiler_params=_params("arbitrary"),
        name="ctx_kv",
    )(ctx, mod3, mod3, w_in_bf)


def _rope(x, cos, s_up, s_dn):
    return x * cos + pltpu.roll(x, LANES - ROPE_FREQS, 1) * s_up + pltpu.roll(x, ROPE_FREQS, 1) * s_dn


def _project_body(x_ref, sh_ref, sc_ref, w_ref, cos_ref, sup_ref, sdn_ref, lng_ref, lnb_ref, *rest):
    n_side = (len(rest) - 5) // 2
    side_in, (q_ref, kv_ref, u_ref, vg_ref, gate_ref), side_out = rest[:n_side], rest[n_side:n_side + 5], rest[n_side + 5:]
    for src, dst in zip(side_in, side_out):
        dst[...] = src[...].astype(BF16)
    h = (_ln(x_ref[...]) * (1.0 + sc_ref[...]) + sh_ref[...]).astype(BF16)
    cos, s_up, s_dn = cos_ref[...], sup_ref[...], sdn_ref[...]
    q_scale = HEAD_DIM ** -0.5
    k0 = ATTN_W
    u0 = k0 + 2 * KV_W
    vg0 = u0 + GM_W
    gate0 = vg0 + GM_W
    q = _dot(h, w_ref[:, 0:k0])
    for j in range(ATTN_W // LANES):
        sl = slice(j * LANES, (j + 1) * LANES)
        q_ref[:, sl] = (_rope(q[:, sl], cos, s_up, s_dn) * q_scale).astype(BF16)
    kv = _dot(h, w_ref[:, k0:u0])
    kv_ref[:, 0:KV_W] = _rope(kv[:, 0:KV_W], cos, s_up, s_dn).astype(BF16)
    kv_ref[:, KV_W:] = kv[:, KV_W:].astype(BF16)
    u_ref[...] = jax.nn.gelu(_dot(h, w_ref[:, u0:vg0])).astype(BF16)
    vg = _ln(jax.nn.gelu(_dot(h, w_ref[:, vg0:gate0]))) * lng_ref[...] + lnb_ref[...]
    vg_ref[...] = vg.astype(BF16)
    gate_ref[...] = jax.nn.sigmoid(_dot(h, w_ref[:, gate0:])).astype(BF16)


def _project(x, mod3, w_in_bf, rope_tabs, gm_ln_g, gm_ln_b, tile, side):
    b, s, d = x.shape
    in_w = w_in_bf.shape[1]
    gate_w = in_w - (ATTN_W + 2 * KV_W + 2 * GM_W)
    nt = s // tile
    steps = b * nt
    row = lambda width: pl.BlockSpec((None, tile, width), lambda i, t: (i, t, 0))
    tab = pl.BlockSpec((tile, LANES), lambda i, t: (t, 0))
    vec = lambda width: pl.BlockSpec((1, width), lambda i, t: (0, 0))
    side_specs = [pl.BlockSpec((a.shape[0] // steps, a.shape[1]), lambda i, t: (i * nt + t, 0)) for a in side]
    outs = pl.pallas_call(
        _project_body,
        out_shape=[
            jax.ShapeDtypeStruct((b, s, ATTN_W), BF16),
            jax.ShapeDtypeStruct((b, s, 2 * KV_W), BF16),
            jax.ShapeDtypeStruct((b, s, GM_W), BF16),
            jax.ShapeDtypeStruct((b, s, GM_W), BF16),
            jax.ShapeDtypeStruct((b, s, gate_w), BF16),
        ] + [jax.ShapeDtypeStruct(a.shape, BF16) for a in side],
        grid=(b, nt),
        in_specs=[
            row(d),
            pl.BlockSpec((None, 1, d), lambda i, t: (i, 0, 0)),
            pl.BlockSpec((None, 1, d), lambda i, t: (i, 0, 1)),
            pl.BlockSpec((d, in_w), lambda i, t: (0, 0)),
            tab, tab, tab,
            vec(GM_W), vec(GM_W),
        ] + side_specs,
        out_specs=[row(ATTN_W), row(2 * KV_W), row(GM_W), row(GM_W), row(gate_w)] + side_specs,
        compiler_params=_params("arbitrary", "arbitrary"),
        name="project",
    )(x, mod3, mod3, w_in_bf, *rope_tabs, gm_ln_g.reshape(1, GM_W), gm_ln_b.reshape(1, GM_W), *side)
    return outs[:5], outs[5:]


def _rope_tables(s):
    pos = jnp.arange(s, dtype=I32)
    inv = 1.0 / (ROPE_BASE ** (jnp.arange(ROPE_FREQS, dtype=F32) / ROPE_FREQS))
    ang_r = (pos // GRID_W).astype(F32)[:, None] * inv[None, :]
    ang_c = (pos % GRID_W).astype(F32)[:, None] * inv[None, :]
    cr, sr, cc, sn = jnp.cos(ang_r), jnp.sin(ang_r), jnp.cos(ang_c), jnp.sin(ang_c)
    z = jnp.zeros_like(sr)
    reps = LANES // HEAD_DIM
    cos = jnp.tile(jnp.concatenate([cr, cr, cc, cc], -1), (1, reps))
    s_up = jnp.tile(jnp.concatenate([-sr, z, -sn, z], -1), (1, reps))
    s_dn = jnp.tile(jnp.concatenate([z, sr, z, sn], -1), (1, reps))
    return cos, s_up, s_dn


def _attend_body(sink_ref, q_ref, kvp_ref, kvc_ref, kvn_ref, ctx_ref, o_ref):
    i = pl.program_id(1)
    n_steps = pl.num_programs(1)
    per = q_ref.shape[0] // BLOCK
    rows = GROUP * BLOCK
    r = _iota((rows, 3 * BLOCK), 0) & (BLOCK - 1)
    m = _iota((rows, 3 * BLOCK), 1)
    band = (m >= r) & (m <= r + 2 * BLOCK)
    grp = _iota((rows, 1), 0) // BLOCK
    kv_all = jnp.concatenate([kvp_ref[...], kvc_ref[...], kvn_ref[...]], axis=0)
    kv_ctx = ctx_ref[...]
    for t in range(per):
        mask = band
        if t == 0:
            mask = mask & ((m >= BLOCK) | (i > 0))
        if t == per - 1:
            mask = mask & ((m < 2 * BLOCK) | (i < n_steps - 1))
        q = q_ref[t * BLOCK:(t + 1) * BLOCK, :]
        kv_loc = kv_all[t * BLOCK:(t + 3) * BLOCK]
        scores = []
        for h in range(KV_HEADS):
            ks = slice(h * HEAD_DIM, (h + 1) * HEAD_DIM)
            qg = jnp.concatenate(
                [q[:, (h * GROUP + g) * HEAD_DIM:(h * GROUP + g + 1) * HEAD_DIM] for g in range(GROUP)], axis=0)
            scores.append((_dot_nt(qg, kv_ctx[:, ks]), jnp.where(mask, _dot_nt(qg, kv_loc[:, ks]), NEG_INF)))
        probs = []
        for h, (s_ctx, s_loc) in enumerate(scores):
            sink = jnp.zeros((rows, 1), F32)
            for g in range(GROUP):
                sink = jnp.where(grp == g, sink_ref[h * GROUP + g], sink)
            top = jnp.maximum(jnp.maximum(jnp.max(s_ctx, -1, keepdims=True), jnp.max(s_loc, -1, keepdims=True)), sink)
            e_ctx = jnp.exp(s_ctx - top)
            e_loc = jnp.exp(s_loc - top)
            den = jnp.sum(e_ctx, -1, keepdims=True) + jnp.sum(e_loc, -1, keepdims=True) + jnp.exp(sink - top)
            probs.append((e_ctx.astype(BF16), e_loc.astype(BF16), 1.0 / den))
        outs = []
        for h, (e_ctx, e_loc, inv) in enumerate(probs):
            vs = slice(KV_W + h * HEAD_DIM, KV_W + (h + 1) * HEAD_DIM)
            o = (_dot(e_ctx, kv_ctx[:, vs]) + _dot(e_loc, kv_loc[:, vs])) * inv
            outs += [o[g * BLOCK:(g + 1) * BLOCK] for g in range(GROUP)]
        o_ref[t * BLOCK:(t + 1) * BLOCK, :] = jnp.concatenate(outs, axis=1).astype(BF16)


def _attend(q, kv, kv_ctx, sink, per):
    b, s, _ = q.shape
    nb = s // BLOCK
    c_len = kv_ctx.shape[1]
    kvw = 2 * KV_W
    return pl.pallas_call(
        _attend_body,
        out_shape=jax.ShapeDtypeStruct((b, s, ATTN_W), BF16),
        grid=(b, nb // per),
        in_specs=[
            pl.BlockSpec(memory_space=pltpu.SMEM),
            pl.BlockSpec((None, per * BLOCK, ATTN_W), lambda i, j: (i, j, 0)),
            pl.BlockSpec((None, BLOCK, kvw), lambda i, j: (i, jnp.maximum(j * per - 1, 0), 0)),
            pl.BlockSpec((None, per * BLOCK, kvw), lambda i, j: (i, j, 0)),
            pl.BlockSpec((None, BLOCK, kvw), lambda i, j: (i, jnp.minimum((j + 1) * per, nb - 1), 0)),
            pl.BlockSpec((None, c_len, kvw), lambda i, j: (i, 0, 0)),
        ],
        out_specs=pl.BlockSpec((None, per * BLOCK, ATTN_W), lambda i, j: (i, j, 0)),
        compiler_params=_params("arbitrary", "arbitrary"),
        name="attend",
    )(sink, q, kv, kv, kv, kv_ctx)


def _route(logits):
    lane = _iota(logits.shape, 1)
    far = jnp.int32(2 * ROUTE_W)

    def first_max(vals):
        top = jnp.max(vals, -1, keepdims=True)
        return top, jnp.min(jnp.where(vals == top, lane, far), -1, keepdims=True)

    is_g = lane < N_GROUPS
    g_top, g_idx = first_max(jnp.where(is_g, logits, NEG_INF))
    g_w = 1.0 / jnp.sum(jnp.where(is_g, jnp.exp(logits - g_top), 0.0), -1, keepdims=True)
    lo = N_GROUPS + EPG * g_idx
    e_vals = jnp.where((lane >= lo) & (lane < lo + EPG), logits, NEG_INF)
    v1, i1 = first_max(e_vals)
    v2, i2 = first_max(jnp.where(lane == i1, NEG_INF, e_vals))
    t = jnp.exp(v2 - v1)
    w1 = g_w * (1.0 / (1.0 + t))
    w2 = g_w * (t / (1.0 + t))
    swap = i2 < i1
    a = jnp.where(swap, i2, i1) - lo
    b = jnp.where(swap, i1, i2) - lo
    cls = g_idx * PAIRS + ((a * (2 * EPG - 1 - a)) >> 1) + (b - a - 1)
    return jnp.where(swap, w2, w1), jnp.where(swap, w1, w2), cls


def _merge_body(x_ref, ya_ref, u_ref, vg_ref, gate_ref, ws_ref, bs_ref, wpa_ref, wpb_ref, wo_ref,
                lng_ref, lnb_ref, g1_ref, sh2_ref, sc2_ref, wrh_ref, wrl_ref, br_ref, x1_ref, rec_ref, cnt_ref):
    tile, d = x_ref.shape
    lane = _iota((BLOCK, LANES), 1)
    below = (lane < _iota((BLOCK, LANES), 0)).astype(BF16)
    chunks = []
    for c in range(tile // BLOCK):
        rows = slice(c * BLOCK, (c + 1) * BLOCK)
        pieces = []
        for j in range(GM_W // LANES):
            v = vg_ref[rows, j * LANES:(j + 1) * LANES]
            per = LANES // GM_HEAD
            sp = _dot(ws_ref[per * j], v)
            for k in range(1, per):
                sp = jnp.where(lane < k * GM_HEAD, sp, _dot(ws_ref[per * j + k], v))
            pieces.append(sp)
        chunks.append(jnp.concatenate(pieces, axis=1) + bs_ref[...])
    sp = jnp.concatenate(chunks, axis=0)
    y_gm = (u_ref[...].astype(F32) * sp).astype(BF16)
    gate = gate_ref[...].astype(F32)
    y = gate[:, :d] * _dot(ya_ref[...], wpa_ref[...]) + gate[:, d:] * _dot(y_gm, wpb_ref[...])
    mix = _dot(y.astype(BF16), wo_ref[...])
    x1 = _ln(ALPHA * x_ref[...] + g1_ref[...] * mix) * lng_ref[...] + lnb_ref[...]
    x1_ref[...] = x1
    h2 = _ln(x1) * (1.0 + sc2_ref[...]) + sh2_ref[...]
    w_lo, w_hi, cls = _route(_dot_split(h2, wrh_ref[...], wrl_ref[...]) + br_ref[...])
    for c in range(tile // BLOCK):
        rows = slice(c * BLOCK, (c + 1) * BLOCK)
        onehot = lane == cls[rows]
        hot = onehot.astype(BF16)
        rank = jnp.sum(jnp.where(onehot, _dot(below, hot), 0.0), -1, keepdims=True)
        cnt_ref[c] = jnp.sum(hot.astype(F32), 0, keepdims=True)
        rec = jnp.where(lane == REC_WLO, w_lo[rows], 0.0)
        rec = jnp.where(lane == REC_WHI, w_hi[rows], rec)
        rec = jnp.where(lane == REC_CLS, cls[rows].astype(F32), rec)
        rec_ref[rows, :] = jnp.where(lane == REC_RANK, rank, rec)


def _merge(x, y_attn, u, vg, gates, ws_bf, bs_rows, wpa, wpb, wo, ln_g, ln_b, mod3, wr_hi, wr_lo, br, tile):
    b, s, d = x.shape
    nt = s // tile
    per = tile // BLOCK
    row = lambda width: pl.BlockSpec((None, tile, width), lambda i, j: (i, j, 0))
    full = lambda a: pl.BlockSpec(a.shape, lambda i, j: (0,) * a.ndim)
    modv = lambda k: pl.BlockSpec((None, 1, d), lambda i, j: (i, 0, k))
    ln_g = ln_g.reshape(1, d)
    ln_b = ln_b.reshape(1, d)
    return pl.pallas_call(
        _merge_body,
        out_shape=[jax.ShapeDtypeStruct((b, s, d), F32), jax.ShapeDtypeStruct((b, s, ROUTE_W), F32),
                   jax.ShapeDtypeStruct((b * s // BLOCK, 1, LANES), F32)],
        grid=(b, nt),
        in_specs=[
            row(d), row(ATTN_W), row(GM_W), row(GM_W), row(2 * d),
            full(ws_bf), full(bs_rows), full(wpa), full(wpb), full(wo), full(ln_g), full(ln_b),
            modv(2), modv(3), modv(4),
            full(wr_hi), full(wr_lo), full(br),
        ],
        out_specs=[row(d), row(ROUTE_W), pl.BlockSpec((per, 1, LANES), lambda i, j: (i * nt + j, 0, 0))],
        compiler_params=_params("arbitrary", "arbitrary"),
        name="merge",
    )(x, y_attn, u, vg, gates, ws_bf, bs_rows, wpa, wpb, wo, ln_g, ln_b, mod3, mod3, mod3, wr_hi, wr_lo, br)


def _plan_body(cnt_ref, rec_ref, lpos_ref, tab_ref, cblk_ref, run_ref, cntt_ref, lst_ref, base_ref, lstf_ref, *, chunks):
    step = pl.program_id(0)
    n_chunks = cnt_ref.shape[0]
    n_tiles = run_ref.shape[0]
    max_blocks = tab_ref.shape[0]

    @pl.when(step == 0)
    def _():
        cnt = cnt_ref[...]
        total = jnp.sum(cnt, 0, keepdims=True)
        n_blk = jnp.floor((total + (BLOCK - 1)) * (1.0 / BLOCK))
        before = (_iota((LANES, LANES), 0) < _iota((LANES, LANES), 1)).astype(BF16)
        nb_hi, nb_lo = _split(jnp.broadcast_to(n_blk, (SUBLANES, LANES)))
        blk_start = (_dot(nb_hi, before) + _dot(nb_lo, before))[0:1]
        lane1 = _iota((1, LANES), 1)
        packed = blk_start
        shift = jnp.zeros((1, 1), F32)
        for g in range(1, N_GROUPS):
            first = jnp.sum(jnp.where(lane1 == g * PAIRS, blk_start, 0.0), -1, keepdims=True) + shift
            over = first - MOE_PACK * jnp.floor(first * (1.0 / MOE_PACK))
            shift = shift + jnp.where(over > 0.0, MOE_PACK - over, 0.0)
            packed = jnp.where(lane1 >= g * PAIRS, blk_start + shift, packed)
        blk_start = packed
        earlier = (_iota((n_chunks, n_chunks), 1) < _iota((n_chunks, n_chunks), 0)).astype(BF16)
        base_ref[...] = blk_start * float(BLOCK) + _dot(earlier, cnt.astype(BF16))
        blk_end = blk_start + n_blk
        sub = _iota((SUBLANES, LANES), 0)
        cblk_ref[...] = jnp.where(sub == 0, blk_start, jnp.where(sub == 1, blk_end, 0.0)).astype(I32)
        lane = _iota((max_blocks, LANES), 1)
        blk = _iota((max_blocks, LANES), 0).astype(F32)
        is_cls = lane < N_CLASSES
        n_real = jnp.max(jnp.where(is_cls, blk_end, 0.0), -1, keepdims=True)
        u = jnp.minimum(blk, n_real - 1.0)
        cls_of = jnp.sum(jnp.where(is_cls & (blk_end <= u), 1.0, 0.0), -1, keepdims=True)
        n_steps = jnp.floor((n_real + (MOE_PACK - 1)) * (1.0 / MOE_PACK))
        tab = jnp.where(lane == 0, cls_of, jnp.where(lane == 1, n_steps, 0.0))
        tab_ref[...] = tab.astype(I32)
        tile = _iota((n_tiles, n_chunks), 0)
        chunk = _iota((n_tiles, n_chunks), 1)
        owner = ((chunk >= tile * chunks) & (chunk < (tile + 1) * chunks)).astype(BF16)
        cnt_tile = _dot(owner, cnt.astype(BF16))
        ct_hi, ct_lo = _split(cnt_tile)
        local_start = _dot(ct_hi, before) + _dot(ct_lo, before)
        run_ref[...] = base_ref[pl.ds(0, n_tiles, stride=chunks), :].astype(I32)
        cntt_ref[...] = cnt_tile.astype(I32)
        lst_ref[...] = local_start.astype(I32)
        lstf_ref[...] = local_start

    lane = _iota((BLOCK, LANES), 1)
    tile_off = lstf_ref[pl.ds(step, 1), :] - base_ref[pl.ds(step * chunks, 1), :]
    pick_rank = (_iota((2 * SUBLANES, LANES), 1) == REC_RANK).astype(BF16)
    for c in range(chunks):
        rec = rec_ref[c * BLOCK:(c + 1) * BLOCK, :]
        hot = (lane == rec[:, REC_CLS:REC_CLS + 1].astype(I32)).astype(BF16)
        base = base_ref[pl.ds(step * chunks + c, 1), :] + tile_off
        b_hi, b_lo = _split(jnp.broadcast_to(base, (2 * SUBLANES, LANES)))
        lpos = _dot_nt(b_hi, hot) + _dot_nt(b_lo, hot) + _dot_nt(pick_rank, rec.astype(BF16))
        lpos_ref[c] = lpos[0:1].astype(I32)


def _plan(counts, rec, max_blocks, chunks):
    n = rec.shape[0]
    n_chunks = n // BLOCK
    n_tiles = n_chunks // chunks
    tile_tab = jax.ShapeDtypeStruct((n_tiles, LANES), I32)
    whole = lambda rows: pl.BlockSpec((rows, LANES), lambda i: (0, 0))
    return pl.pallas_call(
        functools.partial(_plan_body, chunks=chunks),
        out_shape=[jax.ShapeDtypeStruct((n_chunks, 1, LANES), I32), jax.ShapeDtypeStruct((max_blocks, LANES), I32),
                   jax.ShapeDtypeStruct((SUBLANES, LANES), I32), tile_tab, tile_tab, tile_tab],
        grid=(n_tiles,),
        in_specs=[whole(n_chunks), pl.BlockSpec((chunks * BLOCK, ROUTE_W), lambda i: (i, 0))],
        out_specs=[pl.BlockSpec((chunks, 1, LANES), lambda i: (i, 0, 0)), whole(max_blocks), whole(SUBLANES),
                   whole(n_tiles), whole(n_tiles), whole(n_tiles)],
        scratch_shapes=[pltpu.VMEM((n_chunks, LANES), F32), pltpu.VMEM((n_tiles, LANES), F32)],
        compiler_params=_params("arbitrary"),
        name="plan",
    )(counts, rec)


def _tile_rows(base, j, pitch):
    return pl.ds(base + j, BLOCK, stride=pitch)


def _load_tiles(ref, base, pitch, width):
    return jnp.concatenate([ref[_tile_rows(base, j, pitch), :] for j in range(width // LANES)], axis=1)


def _store_tiles(ref, base, pitch, val):
    for j in range(val.shape[1] // LANES):
        ref[_tile_rows(base, j, pitch), :] = val[:, j * LANES:(j + 1) * LANES]


def _start_runs(cnt_ref, lst_ref, run_ref, streams, sem, to_hbm):
    def issue(c):
        for buf, base, hbm, pitch in streams:
            rows = cnt_ref[0, c] * pitch
            aligned = (lambda row: pl.multiple_of(row, pitch)) if pitch > 1 else (lambda row: row)
            v = buf.at[pl.ds(aligned(base + lst_ref[0, c] * pitch), rows)]
            h = hbm.at[pl.ds(aligned(run_ref[0, c] * pitch), rows)]
            (pltpu.make_async_copy(v, h, sem) if to_hbm else pltpu.make_async_copy(h, v, sem)).start(priority=c % 2)

    for c in range(N_CLASSES):
        pl.when(cnt_ref[0, c] > 0)(functools.partial(issue, c))


def _wait_runs(streams, tokens, sem):
    for buf, base, hbm, pitch in streams:
        rows = tokens * pitch
        pltpu.make_async_copy(hbm.at[pl.ds(0, rows)], buf.at[pl.ds(base, rows)], sem).wait()


def _scatter_body(cls_blk_ref, run_ref, cnt_ref, lst_ref, lpos_ref, x1_ref, rec_ref, sh_ref, sc_ref, xs_hbm, meta_hbm,
                  abuf, bbuf, mbuf, zbuf, ssem, zsem):
    u = pl.program_id(0)
    slot = u % 2
    tokens = x1_ref.shape[0]
    chunks = tokens // BLOCK
    tile_rows = BLOCK * SUBLANES
    sort_rows = chunks * tile_rows

    def streams(s):
        return ((bbuf, s * sort_rows, xs_hbm, SUBLANES), (mbuf, s * tokens, meta_hbm, 1))

    @pl.when(u == 0)
    def _():
        zbuf[...] = jnp.zeros_like(zbuf)

        def zero_block(blk, start):
            for hbm, rows in ((xs_hbm, tile_rows), (meta_hbm, BLOCK)):
                cp = pltpu.make_async_copy(zbuf.at[pl.ds(0, rows)], hbm.at[pl.ds(pl.multiple_of(blk * rows, rows), rows)],
                                           zsem)
                cp.start() if start else cp.wait()

        for start in (True, False):
            for c in range(N_CLASSES):
                pl.when(cls_blk_ref[1, c] > cls_blk_ref[0, c])(functools.partial(zero_block, cls_blk_ref[1, c] - 1, start))
        gaps = [(cls_blk_ref[1, g * PAIRS - 1], cls_blk_ref[0, g * PAIRS]) for g in range(1, N_GROUPS)]
        gaps.append((cls_blk_ref[1, N_CLASSES - 1], xs_hbm.shape[0] // tile_rows))
        for start in (True, False):
            for lo, hi in gaps:
                lax.fori_loop(lo, hi, lambda blk, carry: (zero_block(blk, start), carry)[1], 0)

    @pl.when(u >= 2)
    def _():
        _wait_runs(streams(slot), tokens, ssem.at[slot])

    scale = 1.0 + sc_ref[...]
    shift = sh_ref[...]

    unroll = abuf.shape[0] // tile_rows

    def sort_chunks(i, carry):
        for k in range(unroll):
            c = i * unroll + k
            stage = k * tile_rows
            rows = pl.ds(pl.multiple_of(c * BLOCK, BLOCK), BLOCK)
            _store_tiles(abuf, stage, SUBLANES, _ln(x1_ref[rows, :]) * scale + shift)
            rec = rec_ref[rows, :]
            for r in range(BLOCK):
                p = lpos_ref[c, 0, r]
                dst = pl.multiple_of(slot * sort_rows + p * SUBLANES, SUBLANES)
                bbuf[pl.ds(dst, SUBLANES), :] = abuf[stage + r * SUBLANES:stage + (r + 1) * SUBLANES, :]
                mbuf[pl.ds(slot * tokens + p, 1), :] = rec[r:r + 1, :]
        return carry

    lax.fori_loop(0, chunks // unroll, sort_chunks, 0)
    _start_runs(cnt_ref, lst_ref, run_ref, streams(slot), ssem.at[slot], True)

    @pl.when(u == pl.num_programs(0) - 1)
    def _():
        _wait_runs(streams(slot), tokens, ssem.at[slot])

        @pl.when(u >= 1)
        def _():
            _wait_runs(streams(1 - slot), tokens, ssem.at[1 - slot])


def _stage_chunks(chunks):
    return next(p for p in (4, 2, 1) if chunks % p == 0)


def _tile_tab_spec(off, n_tiles):
    return pl.BlockSpec((None, 1, LANES), lambda u: (jnp.minimum(u + off, n_tiles - 1), 0, 0),
                        memory_space=pltpu.SMEM)


def _scatter(x1, rec, mod3, lpos, tile_tabs, cls_blk, max_blocks, seq, chunks):
    n, d = x1.shape
    n_tiles = n // (chunks * BLOCK)
    per_batch = seq // (chunks * BLOCK)
    tile_rows = BLOCK * SUBLANES
    hbm = pl.BlockSpec(memory_space=pl.ANY)
    return pl.pallas_call(
        _scatter_body,
        out_shape=[jax.ShapeDtypeStruct((max_blocks * tile_rows, LANES), F32),
                   jax.ShapeDtypeStruct((max_blocks * BLOCK, ROUTE_W), F32)],
        grid=(n_tiles,),
        in_specs=[
            pl.BlockSpec(memory_space=pltpu.SMEM),
            _tile_tab_spec(0, n_tiles), _tile_tab_spec(0, n_tiles), _tile_tab_spec(0, n_tiles),
            pl.BlockSpec((chunks, 1, LANES), lambda u: (u, 0, 0), memory_space=pltpu.SMEM),
            pl.BlockSpec((chunks * BLOCK, d), lambda u: (u, 0)),
            pl.BlockSpec((chunks * BLOCK, ROUTE_W), lambda u: (u, 0)),
            pl.BlockSpec((None, 1, d), lambda u: (u // per_batch, 0, 3)),
            pl.BlockSpec((None, 1, d), lambda u: (u // per_batch, 0, 4)),
        ],
        out_specs=[hbm, hbm],
        scratch_shapes=[
            pltpu.VMEM((_stage_chunks(chunks) * tile_rows, LANES), F32),
            pltpu.VMEM((2 * chunks * tile_rows, LANES), F32),
            pltpu.VMEM((2 * chunks * BLOCK, ROUTE_W), F32),
            pltpu.VMEM((tile_rows, LANES), F32),
            pltpu.SemaphoreType.DMA((2,)),
            pltpu.SemaphoreType.DMA,
        ],
        compiler_params=_params("arbitrary"),
        name="scatter",
    )(cls_blk, *tile_tabs, lpos, x1, rec, mod3, mod3)


def _expert(h, w1, w3, w2):
    hid = jax.nn.silu(_dot(h, w1)) * _dot(h, w3)
    return _dot(hid, w2)


def _moe_body(sgrp_ref, blo_ref, bhi_ref, nsteps_ref, xs_ref, meta_ref, w1_ref, w3_ref, w2_ref, o_ref):
    u = pl.program_id(0)
    d = w1_ref.shape[1]
    tile_rows = BLOCK * SUBLANES

    @pl.when(u < nsteps_ref[0])
    def _():
        jobs = [(k, e) for k in range(MOE_PACK) for e in (blo_ref[MOE_PACK * u + k], bhi_ref[MOE_PACK * u + k])]
        h = [_load_tiles(xs_ref, k * tile_rows, SUBLANES, d).astype(BF16) for k in range(MOE_PACK)]
        up = [(_dot(h[k], w1_ref[e]), _dot(h[k], w3_ref[e])) for k, e in jobs]
        hid = [(jax.nn.silu(a) * b).astype(BF16) for a, b in up]
        y = [_dot(g, w2_ref[e]) for (k, e), g in zip(jobs, hid)]
        for k in range(MOE_PACK):
            meta = meta_ref[k * BLOCK:(k + 1) * BLOCK, :]
            _store_tiles(o_ref, k * tile_rows, SUBLANES,
                         meta[:, REC_WLO:REC_WLO + 1] * y[2 * k] + meta[:, REC_WHI:REC_WHI + 1] * y[2 * k + 1])

    @pl.when(u >= nsteps_ref[0])
    def _():
        o_ref[...] = jnp.zeros_like(o_ref)


def _moe(xs, meta, tables, w1, w3, w2, d):
    sgrp, blo, bhi, n_steps = tables
    max_steps = sgrp.shape[0]
    de = w1.shape[1]
    in_rows = MOE_PACK * BLOCK * SUBLANES
    w1g = w1.reshape(N_GROUPS, EPG, d, de)
    w3g = w3.reshape(N_GROUPS, EPG, d, de)
    w2g = w2.reshape(N_GROUPS, EPG, de, d)

    def w_spec(a):
        return pl.BlockSpec((None,) + a.shape[1:], lambda u, g, lo, hi, n_steps: (g[u], 0, 0, 0))

    grid_spec = pltpu.PrefetchScalarGridSpec(
        num_scalar_prefetch=4,
        grid=(max_steps,),
        in_specs=[
            pl.BlockSpec((in_rows, LANES), lambda u, g, lo, hi, n_steps: (jnp.minimum(u, n_steps[0] - 1), 0)),
            pl.BlockSpec((MOE_PACK * BLOCK, ROUTE_W), lambda u, g, lo, hi, n_steps: (jnp.minimum(u, n_steps[0] - 1), 0)),
            w_spec(w1g), w_spec(w3g), w_spec(w2g),
        ],
        out_specs=pl.BlockSpec((in_rows, LANES), lambda u, *_: (u, 0)),
    )
    return pl.pallas_call(
        _moe_body,
        out_shape=jax.ShapeDtypeStruct((max_steps * in_rows, LANES), F32),
        grid_spec=grid_spec,
        compiler_params=pltpu.CompilerParams(dimension_semantics=("arbitrary",), vmem_limit_bytes=MOE_VMEM_LIMIT_BYTES),
        name="moe",
    )(sgrp, blo, bhi, n_steps, xs, meta, w1g, w3g, w2g)


def _final_body(run_ref, cnt_ref, lst_ref, runn_ref, cntn_ref, lstn_ref, lpos_ref, x1_ref, g2_ref,
                lng_ref, lnb_ref, ms_hbm, o_ref, gbuf, tbuf, gsem):
    u = pl.program_id(0)
    slot = u % 2
    d = x1_ref.shape[-1]
    tokens = x1_ref.shape[0]
    chunks = tokens // BLOCK
    tile_rows = BLOCK * SUBLANES
    sort_rows = chunks * tile_rows

    def streams(s):
        return ((gbuf, s * sort_rows, ms_hbm, SUBLANES),)

    @pl.when(u == 0)
    def _():
        _start_runs(cnt_ref, lst_ref, run_ref, streams(0), gsem.at[0], False)

    _wait_runs(streams(slot), tokens, gsem.at[slot])
    _start_runs(cntn_ref, lstn_ref, runn_ref, streams(1 - slot), gsem.at[1 - slot], False)
    gain = g2_ref[...]
    ln_g = lng_ref[...]
    ln_b = lnb_ref[...]
    unroll = tbuf.shape[0] // tile_rows

    def finish_chunks(i, carry):
        for k in range(unroll):
            c = i * unroll + k
            stage = k * tile_rows
            for r in range(BLOCK):
                src = pl.ds(pl.multiple_of(slot * sort_rows + lpos_ref[c, 0, r] * SUBLANES, SUBLANES), SUBLANES)
                tbuf[stage + r * SUBLANES:stage + (r + 1) * SUBLANES, :] = gbuf[src, :]
            rows = pl.ds(pl.multiple_of(c * BLOCK, BLOCK), BLOCK)
            moe = _load_tiles(tbuf, stage, SUBLANES, d)
            o_ref[rows, :] = _ln(ALPHA * x1_ref[rows, :] + gain * moe) * ln_g + ln_b
        return carry

    lax.fori_loop(0, chunks // unroll, finish_chunks, 0)

    @pl.when(u == pl.num_programs(0) - 1)
    def _():
        _wait_runs(streams(1 - slot), tokens, gsem.at[1 - slot])


def _final(x1, ms, lpos, tile_tabs, mod3, ln_g, ln_b, seq, chunks):
    n, d = x1.shape
    n_tiles = n // (chunks * BLOCK)
    per_batch = seq // (chunks * BLOCK)
    tile_rows = BLOCK * SUBLANES
    vec = pl.BlockSpec((1, d), lambda u: (0, 0))
    return pl.pallas_call(
        _final_body,
        out_shape=jax.ShapeDtypeStruct((n, d), F32),
        grid=(n_tiles,),
        in_specs=[
            _tile_tab_spec(0, n_tiles), _tile_tab_spec(0, n_tiles), _tile_tab_spec(0, n_tiles),
            _tile_tab_spec(1, n_tiles), _tile_tab_spec(1, n_tiles), _tile_tab_spec(1, n_tiles),
            pl.BlockSpec((chunks, 1, LANES), lambda u: (u, 0, 0), memory_space=pltpu.SMEM),
            pl.BlockSpec((chunks * BLOCK, d), lambda u: (u, 0)),
            pl.BlockSpec((None, 1, d), lambda u: (u // per_batch, 0, 5)),
            vec, vec, pl.BlockSpec(memory_space=pl.ANY),
        ],
        out_specs=pl.BlockSpec((chunks * BLOCK, d), lambda u: (u, 0)),
        scratch_shapes=[
            pltpu.VMEM((2 * chunks * tile_rows, LANES), F32),
            pltpu.VMEM((_stage_chunks(chunks) * tile_rows, LANES), F32),
            pltpu.SemaphoreType.DMA((2,)),
        ],
        compiler_params=_params("arbitrary"),
        name="final",
    )(*tile_tabs, *tile_tabs, lpos, x1, mod3, ln_g.reshape(1, d), ln_b.reshape(1, d), ms)


def _block_tables(tab):
    cls = tab[:, 0]
    pair = cls % PAIRS
    first = jnp.array([i for i in range(EPG) for _ in range(i + 1, EPG)], I32)
    second = jnp.array([j for i in range(EPG) for j in range(i + 1, EPG)], I32)
    return (cls // PAIRS)[::MOE_PACK], first[pair], second[pair], tab[0, 1].reshape(1)


def kernel(x, c, ctx, c_ctx, w_ada, b_ada, w_in, attn_sink, gm_ln_g, gm_ln_b, gm_ws, gm_bs, w_pa, w_pb, w_o,
           ln1_g, ln1_b, router_g_w, router_g_b, router_e_w, router_e_b, moe_w1, moe_w3, moe_w2, ln2_g, ln2_b):
    b, s, d = x.shape
    assert w_ada.shape[0] == DEPTH and s % BLOCK == 0 and s % GRID_W == 0
    mod_rows = 2 * SUBLANES
    assert b + 1 <= mod_rows
    c_rows = jnp.zeros((mod_rows, d), F32).at[:b].set(c).at[b].set(c_ctx)
    mod = _adaln_mod(c_rows, w_ada[0], b_ada[0])
    mod3 = mod.reshape(mod_rows, 1, 6 * d)
    w_in_bf = w_in[0].astype(BF16)
    kv_ctx = _ctx_kv(ctx, mod3, b, w_in_bf)
    tile = next(t for t in (512, BLOCK) if s % t == 0)
    experts, _, d_expert = moe_w1.shape[1:]
    moe_w = [moe_w1[0].reshape(experts * d, d_expert), moe_w3[0].reshape(experts * d, d_expert),
             moe_w2[0].reshape(experts * d_expert, d)]
    steps = b * (s // tile)
    sliced = all(w.shape[0] % (steps * 2 * SUBLANES) == 0 for w in moe_w)
    (q, kv, u, vg, gates), moe_w_bf = _project(x, mod3, w_in_bf, _rope_tables(s), gm_ln_g[0], gm_ln_b[0], tile,
                                                moe_w if sliced else [])
    if not sliced:
        moe_w_bf = [w.astype(BF16) for w in moe_w]
    nb = s // BLOCK
    y_attn = _attend(q, kv, kv_ctx, attn_sink[0], next(p for p in (4, 2, 1) if nb % p == 0))
    bs_rows = jnp.repeat(gm_bs[0].T, GM_HEAD, axis=1)
    w_route = jnp.concatenate([router_g_w[0], jnp.transpose(router_e_w[0], (1, 0, 2)).reshape(d, N_EXPERTS)], axis=1)
    w_route = jnp.pad(w_route, ((0, 0), (0, ROUTE_W - w_route.shape[1])))
    b_route = jnp.pad(jnp.concatenate([router_g_b[0], router_e_b[0].reshape(-1)]), (0, ROUTE_W - N_GROUPS - N_EXPERTS))
    wr_hi = w_route.astype(BF16)
    wr_lo = (w_route - wr_hi.astype(F32)).astype(BF16)
    merge_tile = BLOCK * next(p for p in (4, 2, 1) if nb % p == 0)
    x1, rec, counts = _merge(x, y_attn, u, vg, gates, gm_ws[0].astype(BF16), bs_rows, w_pa[0].astype(BF16),
                             w_pb[0].astype(BF16), w_o[0].astype(BF16), ln1_g[0], ln1_b[0], mod3, wr_hi, wr_lo,
                             b_route.reshape(1, ROUTE_W), merge_tile)
    n = b * s
    x1 = x1.reshape(n, d)
    rec = rec.reshape(n, ROUTE_W)
    max_blocks = n // BLOCK + N_CLASSES - 1 + (N_GROUPS - 1) * (MOE_PACK - 1)
    max_blocks = -(-max_blocks // MOE_PACK) * MOE_PACK
    sort_chunks = next(p for p in (SUBLANES, 4, 2, 1) if nb % p == 0)
    lpos, tab, cls_blk, *tile_tabs = _plan(counts.reshape(n // BLOCK, LANES), rec, max_blocks, sort_chunks)
    tile_tabs = [t.reshape(t.shape[0], 1, LANES) for t in tile_tabs]
    xs, meta = _scatter(x1, rec, mod3, lpos, tile_tabs, cls_blk, max_blocks, s, sort_chunks)
    ms = _moe(xs, meta, _block_tables(tab), *moe_w_bf, d)
    out = _final(x1, ms, lpos, tile_tabs, mod3, ln2_g[0], ln2_b[0], s, sort_chunks)
    return out.reshape(b, s, d)
```

```python
import functools

import jax
import jax.numpy as jnp
from jax import lax
from jax.experimental import pallas as pl
from jax.experimental.pallas import tpu as pltpu

F32 = jnp.float32
BF16 = jnp.bfloat16
I32 = jnp.int32

GRID_W = 64
HEAD_DIM = 64
Q_HEADS = 8
KV_HEADS = 2
GROUP = Q_HEADS // KV_HEADS
BLOCK = 128
ROPE_BASE = 10000.0
ROPE_FREQS = HEAD_DIM // 4
GM_GROUPS = 8
GM_HEAD = 64
ATTN_W = Q_HEADS * HEAD_DIM
KV_W = KV_HEADS * HEAD_DIM
GM_W = GM_GROUPS * GM_HEAD
N_GROUPS = 4
EPG = 8
N_EXPERTS = N_GROUPS * EPG
PAIRS = EPG * (EPG - 1) // 2
N_CLASSES = N_GROUPS * PAIRS
LN_EPS = 1e-6
NEG_INF = -1e30
DEPTH = 1
ALPHA = (2.0 * DEPTH) ** 0.25

LANES = 128
SUBLANES = 8
VMEM_LIMIT_BYTES = 56 * 1024 * 1024
MOE_VMEM_LIMIT_BYTES = 60 * 1024 * 1024
MOE_PACK = 4
ROUTE_W = LANES
REC_WLO, REC_WHI, REC_CLS, REC_RANK = 0, 1, 2, 3


def _params(*sem):
    return pltpu.CompilerParams(dimension_semantics=sem, vmem_limit_bytes=VMEM_LIMIT_BYTES)


def _ln(x):
    mu = jnp.mean(x, -1, keepdims=True)
    xc = x - mu
    var = jnp.mean(xc * xc, -1, keepdims=True)
    return xc * lax.rsqrt(var + LN_EPS)


def _dot(a, b):
    return jnp.dot(a, b, preferred_element_type=F32)


def _dot_nt(a, b):
    return lax.dot_general(a, b, (((1,), (1,)), ((), ())), preferred_element_type=F32)


def _split(a):
    hi = a.astype(BF16)
    lo = (a - hi.astype(F32)).astype(BF16)
    return hi, lo


def _dot_split(a, b_hi, b_lo):
    a_hi, a_lo = _split(a)
    return _dot(a_hi, b_hi) + _dot(a_hi, b_lo) + _dot(a_lo, b_hi)


def _iota(shape, axis):
    return lax.broadcasted_iota(I32, shape, axis)


def _mod_body(c_ref, w_ref, b_ref, o_ref):
    w_hi, w_lo = _split(w_ref[...])
    o_ref[...] = _dot_split(jax.nn.silu(c_ref[...]), w_hi, w_lo) + b_ref[...]


def _adaln_mod(c_rows, w_ada, b_ada):
    rows, d = c_rows.shape
    width = w_ada.shape[1]
    tn = 6 * LANES
    return pl.pallas_call(
        _mod_body,
        out_shape=jax.ShapeDtypeStruct((rows, width), F32),
        grid=(width // tn,),
        in_specs=[
            pl.BlockSpec((rows, d), lambda j: (0, 0)),
            pl.BlockSpec((d, tn), lambda j: (0, j)),
            pl.BlockSpec((1, tn), lambda j: (0, j)),
        ],
        out_specs=pl.BlockSpec((rows, tn), lambda j: (0, j)),
        compiler_params=_params("arbitrary"),
        name="mod",
    )(c_rows, w_ada, b_ada.reshape(1, width))


def _ctx_body(ctx_ref, sh_ref, sc_ref, w_ref, o_ref):
    h = _ln(ctx_ref[...]) * (1.0 + sc_ref[...]) + sh_ref[...]
    o_ref[...] = _dot(h.astype(BF16), w_ref[...]).astype(BF16)


def _ctx_kv(ctx, mod3, ctx_row, w_in_bf):
    b, c_len, d = ctx.shape
    kvw = 2 * KV_W
    return pl.pallas_call(
        _ctx_body,
        out_shape=jax.ShapeDtypeStruct((b, c_len, kvw), BF16),
        grid=(b,),
        in_specs=[
            pl.BlockSpec((None, c_len, d), lambda i: (i, 0, 0)),
            pl.BlockSpec((None, 1, d), lambda i: (ctx_row, 0, 0)),
            pl.BlockSpec((None, 1, d), lambda i: (ctx_row, 0, 1)),
            pl.BlockSpec((d, kvw), lambda i: (0, ATTN_W // kvw)),
        ],
        out_specs=pl.BlockSpec((None, c_len, kvw), lambda i: (i, 0, 0)),
        compiler_params=_params("arbitrary"),
        name="ctx_kv",
    )(ctx, mod3, mod3, w_in_bf)


def _rope(x, cos, s_up, s_dn):
    return x * cos + pltpu.roll(x, LANES - ROPE_FREQS, 1) * s_up + pltpu.roll(x, ROPE_FREQS, 1) * s_dn


def _project_body(x_ref, sh_ref, sc_ref, w_ref, cos_ref, sup_ref, sdn_ref, lng_ref, lnb_ref, *rest):
    n_side = (len(rest) - 5) // 2
    side_in, (q_ref, kv_ref, u_ref, vg_ref, gate_ref), side_out = rest[:n_side], rest[n_side:n_side + 5], rest[n_side + 5:]
    for src, dst in zip(side_in, side_out):
        dst[...] = src[...].astype(BF16)
    h = (_ln(x_ref[...]) * (1.0 + sc_ref[...]) + sh_ref[...]).astype(BF16)
    cos, s_up, s_dn = cos_ref[...], sup_ref[...], sdn_ref[...]
    q_scale = HEAD_DIM ** -0.5
    k0 = ATTN_W
    u0 = k0 + 2 * KV_W
    vg0 = u0 + GM_W
    gate0 = vg0 + GM_W
    q = _dot(h, w_ref[:, 0:k0])
    for j in range(ATTN_W // LANES):
        sl = slice(j * LANES, (j + 1) * LANES)
        q_ref[:, sl] = (_rope(q[:, sl], cos, s_up, s_dn) * q_scale).astype(BF16)
    kv = _dot(h, w_ref[:, k0:u0])
    kv_ref[:, 0:KV_W] = _rope(kv[:, 0:KV_W], cos, s_up, s_dn).astype(BF16)
    kv_ref[:, KV_W:] = kv[:, KV_W:].astype(BF16)
    u_ref[...] = jax.nn.gelu(_dot(h, w_ref[:, u0:vg0])).astype(BF16)
    vg = _ln(jax.nn.gelu(_dot(h, w_ref[:, vg0:gate0]))) * lng_ref[...] + lnb_ref[...]
    vg_ref[...] = vg.astype(BF16)
    gate_ref[...] = jax.nn.sigmoid(_dot(h, w_ref[:, gate0:])).astype(BF16)


def _project(x, mod3, w_in_bf, rope_tabs, gm_ln_g, gm_ln_b, tile, side):
    b, s, d = x.shape
    in_w = w_in_bf.shape[1]
    gate_w = in_w - (ATTN_W + 2 * KV_W + 2 * GM_W)
    nt = s // tile
    steps = b * nt
    row = lambda width: pl.BlockSpec((None, tile, width), lambda i, t: (i, t, 0))
    tab = pl.BlockSpec((tile, LANES), lambda i, t: (t, 0))
    vec = lambda width: pl.BlockSpec((1, width), lambda i, t: (0, 0))
    side_specs = [pl.BlockSpec((a.shape[0] // steps, a.shape[1]), lambda i, t: (i * nt + t, 0)) for a in side]
    outs = pl.pallas_call(
        _project_body,
        out_shape=[
            jax.ShapeDtypeStruct((b, s, ATTN_W), BF16),
            jax.ShapeDtypeStruct((b, s, 2 * KV_W), BF16),
            jax.ShapeDtypeStruct((b, s, GM_W), BF16),
            jax.ShapeDtypeStruct((b, s, GM_W), BF16),
            jax.ShapeDtypeStruct((b, s, gate_w), BF16),
        ] + [jax.ShapeDtypeStruct(a.shape, BF16) for a in side],
        grid=(b, nt),
        in_specs=[
            row(d),
            pl.BlockSpec((None, 1, d), lambda i, t: (i, 0, 0)),
            pl.BlockSpec((None, 1, d), lambda i, t: (i, 0, 1)),
            pl.BlockSpec((d, in_w), lambda i, t: (0, 0)),
            tab, tab, tab,
            vec(GM_W), vec(GM_W),
        ] + side_specs,
        out_specs=[row(ATTN_W), row(2 * KV_W), row(GM_W), row(GM_W), row(gate_w)] + side_specs,
        compiler_params=_params("arbitrary", "arbitrary"),
        name="project",
    )(x, mod3, mod3, w_in_bf, *rope_tabs, gm_ln_g.reshape(1, GM_W), gm_ln_b.reshape(1, GM_W), *side)
    return outs[:5], outs[5:]


def _rope_tables(s):
    pos = jnp.arange(s, dtype=I32)
    inv = 1.0 / (ROPE_BASE ** (jnp.arange(ROPE_FREQS, dtype=F32) / ROPE_FREQS))
    ang_r = (pos // GRID_W).astype(F32)[:, None] * inv[None, :]
    ang_c = (pos % GRID_W).astype(F32)[:, None] * inv[None, :]
    cr, sr, cc, sn = jnp.cos(ang_r), jnp.sin(ang_r), jnp.cos(ang_c), jnp.sin(ang_c)
    z = jnp.zeros_like(sr)
    reps = LANES // HEAD_DIM
    cos = jnp.tile(jnp.concatenate([cr, cr, cc, cc], -1), (1, reps))
    s_up = jnp.tile(jnp.concatenate([-sr, z, -sn, z], -1), (1, reps))
    s_dn = jnp.tile(jnp.concatenate([z, sr, z, sn], -1), (1, reps))
    return cos, s_up, s_dn


def _attend_body(sink_ref, q_ref, kvp_ref, kvc_ref, kvn_ref, ctx_ref, o_ref):
    i = pl.program_id(1)
    n_steps = pl.num_programs(1)
    per = q_ref.shape[0] // BLOCK
    rows = GROUP * BLOCK
    r = _iota((rows, 3 * BLOCK), 0) & (BLOCK - 1)
    m = _iota((rows, 3 * BLOCK), 1)
    band = (m >= r) & (m <= r + 2 * BLOCK)
    grp = _iota((rows, 1), 0) // BLOCK
    kv_all = jnp.concatenate([kvp_ref[...], kvc_ref[...], kvn_ref[...]], axis=0)
    kv_ctx = ctx_ref[...]
    for t in range(per):
        mask = band
        if t == 0:
            mask = mask & ((m >= BLOCK) | (i > 0))
        if t == per - 1:
            mask = mask & ((m < 2 * BLOCK) | (i < n_steps - 1))
        q = q_ref[t * BLOCK:(t + 1) * BLOCK, :]
        kv_loc = kv_all[t * BLOCK:(t + 3) * BLOCK]
        scores = []
        for h in range(KV_HEADS):
            ks = slice(h * HEAD_DIM, (h + 1) * HEAD_DIM)
            qg = jnp.concatenate(
                [q[:, (h * GROUP + g) * HEAD_DIM:(h * GROUP + g + 1) * HEAD_DIM] for g in range(GROUP)], axis=0)
            scores.append((_dot_nt(qg, kv_ctx[:, ks]), jnp.where(mask, _dot_nt(qg, kv_loc[:, ks]), NEG_INF)))
        probs = []
        for h, (s_ctx, s_loc) in enumerate(scores):
            sink = jnp.zeros((rows, 1), F32)
            for g in range(GROUP):
                sink = jnp.where(grp == g, sink_ref[h * GROUP + g], sink)
            top = jnp.maximum(jnp.maximum(jnp.max(s_ctx, -1, keepdims=True), jnp.max(s_loc, -1, keepdims=True)), sink)
            e_ctx = jnp.exp(s_ctx - top)
            e_loc = jnp.exp(s_loc - top)
            den = jnp.sum(e_ctx, -1, keepdims=True) + jnp.sum(e_loc, -1, keepdims=True) + jnp.exp(sink - top)
            probs.append((e_ctx.astype(BF16), e_loc.astype(BF16), 1.0 / den))
        outs = []
        for h, (e_ctx, e_loc, inv) in enumerate(probs):
            vs = slice(KV_W + h * HEAD_DIM, KV_W + (h + 1) * HEAD_DIM)
            o = (_dot(e_ctx, kv_ctx[:, vs]) + _dot(e_loc, kv_loc[:, vs])) * inv
            outs += [o[g * BLOCK:(g + 1) * BLOCK] for g in range(GROUP)]
        o_ref[t * BLOCK:(t + 1) * BLOCK, :] = jnp.concatenate(outs, axis=1).astype(BF16)


def _attend(q, kv, kv_ctx, sink, per):
    b, s, _ = q.shape
    nb = s // BLOCK
    c_len = kv_ctx.shape[1]
    kvw = 2 * KV_W
    return pl.pallas_call(
        _attend_body,
        out_shape=jax.ShapeDtypeStruct((b, s, ATTN_W), BF16),
        grid=(b, nb // per),
        in_specs=[
            pl.BlockSpec(memory_space=pltpu.SMEM),
            pl.BlockSpec((None, per * BLOCK, ATTN_W), lambda i, j: (i, j, 0)),
            pl.BlockSpec((None, BLOCK, kvw), lambda i, j: (i, jnp.maximum(j * per - 1, 0), 0)),
            pl.BlockSpec((None, per * BLOCK, kvw), lambda i, j: (i, j, 0)),
            pl.BlockSpec((None, BLOCK, kvw), lambda i, j: (i, jnp.minimum((j + 1) * per, nb - 1), 0)),
            pl.BlockSpec((None, c_len, kvw), lambda i, j: (i, 0, 0)),
        ],
        out_specs=pl.BlockSpec((None, per * BLOCK, ATTN_W), lambda i, j: (i, j, 0)),
        compiler_params=_params("arbitrary", "arbitrary"),
        name="attend",
    )(sink, q, kv, kv, kv, kv_ctx)


def _route(logits):
    lane = _iota(logits.shape, 1)
    far = jnp.int32(2 * ROUTE_W)

    def first_max(vals):
        top = jnp.max(vals, -1, keepdims=True)
        return top, jnp.min(jnp.where(vals == top, lane, far), -1, keepdims=True)

    is_g = lane < N_GROUPS
    g_top, g_idx = first_max(jnp.where(is_g, logits, NEG_INF))
    g_w = 1.0 / jnp.sum(jnp.where(is_g, jnp.exp(logits - g_top), 0.0), -1, keepdims=True)
    lo = N_GROUPS + EPG * g_idx
    e_vals = jnp.where((lane >= lo) & (lane < lo + EPG), logits, NEG_INF)
    v1, i1 = first_max(e_vals)
    v2, i2 = first_max(jnp.where(lane == i1, NEG_INF, e_vals))
    t = jnp.exp(v2 - v1)
    w1 = g_w * (1.0 / (1.0 + t))
    w2 = g_w * (t / (1.0 + t))
    swap = i2 < i1
    a = jnp.where(swap, i2, i1) - lo
    b = jnp.where(swap, i1, i2) - lo
    cls = g_idx * PAIRS + ((a * (2 * EPG - 1 - a)) >> 1) + (b - a - 1)
    return jnp.where(swap, w2, w1), jnp.where(swap, w1, w2), cls


def _merge_body(x_ref, ya_ref, u_ref, vg_ref, gate_ref, ws_ref, bs_ref, wpa_ref, wpb_ref, wo_ref,
                lng_ref, lnb_ref, g1_ref, sh2_ref, sc2_ref, wrh_ref, wrl_ref, br_ref, x1_ref, rec_ref, cnt_ref):
    tile, d = x_ref.shape
    lane = _iota((BLOCK, LANES), 1)
    below = (lane < _iota((BLOCK, LANES), 0)).astype(BF16)
    chunks = []
    for c in range(tile // BLOCK):
        rows = slice(c * BLOCK, (c + 1) * BLOCK)
        pieces = []
        for j in range(GM_W // LANES):
            v = vg_ref[rows, j * LANES:(j + 1) * LANES]
            per = LANES // GM_HEAD
            sp = _dot(ws_ref[per * j], v)
            for k in range(1, per):
                sp = jnp.where(lane < k * GM_HEAD, sp, _dot(ws_ref[per * j + k], v))
            pieces.append(sp)
        chunks.append(jnp.concatenate(pieces, axis=1) + bs_ref[...])
    sp = jnp.concatenate(chunks, axis=0)
    y_gm = (u_ref[...].astype(F32) * sp).astype(BF16)
    gate = gate_ref[...].astype(F32)
    y = gate[:, :d] * _dot(ya_ref[...], wpa_ref[...]) + gate[:, d:] * _dot(y_gm, wpb_ref[...])
    mix = _dot(y.astype(BF16), wo_ref[...])
    x1 = _ln(ALPHA * x_ref[...] + g1_ref[...] * mix) * lng_ref[...] + lnb_ref[...]
    x1_ref[...] = x1
    h2 = _ln(x1) * (1.0 + sc2_ref[...]) + sh2_ref[...]
    w_lo, w_hi, cls = _route(_dot_split(h2, wrh_ref[...], wrl_ref[...]) + br_ref[...])
    for c in range(tile // BLOCK):
        rows = slice(c * BLOCK, (c + 1) * BLOCK)
        onehot = lane == cls[rows]
        hot = onehot.astype(BF16)
        rank = jnp.sum(jnp.where(onehot, _dot(below, hot), 0.0), -1, keepdims=True)
        cnt_ref[c] = jnp.sum(hot.astype(F32), 0, keepdims=True)
        rec = jnp.where(lane == REC_WLO, w_lo[rows], 0.0)
        rec = jnp.where(lane == REC_WHI, w_hi[rows], rec)
        rec = jnp.where(lane == REC_CLS, cls[rows].astype(F32), rec)
        rec_ref[rows, :] = jnp.where(lane == REC_RANK, rank, rec)


def _merge(x, y_attn, u, vg, gates, ws_bf, bs_rows, wpa, wpb, wo, ln_g, ln_b, mod3, wr_hi, wr_lo, br, tile):
    b, s, d = x.shape
    nt = s // tile
    per = tile // BLOCK
    row = lambda width: pl.BlockSpec((None, tile, width), lambda i, j: (i, j, 0))
    full = lambda a: pl.BlockSpec(a.shape, lambda i, j: (0,) * a.ndim)
    modv = lambda k: pl.BlockSpec((None, 1, d), lambda i, j: (i, 0, k))
    ln_g = ln_g.reshape(1, d)
    ln_b = ln_b.reshape(1, d)
    return pl.pallas_call(
        _merge_body,
        out_shape=[jax.ShapeDtypeStruct((b, s, d), F32), jax.ShapeDtypeStruct((b, s, ROUTE_W), F32),
                   jax.ShapeDtypeStruct((b * s // BLOCK, 1, LANES), F32)],
        grid=(b, nt),
        in_specs=[
            row(d), row(ATTN_W), row(GM_W), row(GM_W), row(2 * d),
            full(ws_bf), full(bs_rows), full(wpa), full(wpb), full(wo), full(ln_g), full(ln_b),
            modv(2), modv(3), modv(4),
            full(wr_hi), full(wr_lo), full(br),
        ],
        out_specs=[row(d), row(ROUTE_W), pl.BlockSpec((per, 1, LANES), lambda i, j: (i * nt + j, 0, 0))],
        compiler_params=_params("arbitrary", "arbitrary"),
        name="merge",
    )(x, y_attn, u, vg, gates, ws_bf, bs_rows, wpa, wpb, wo, ln_g, ln_b, mod3, mod3, mod3, wr_hi, wr_lo, br)


def _plan_body(cnt_ref, rec_ref, lpos_ref, tab_ref, cblk_ref, run_ref, cntt_ref, lst_ref, base_ref, lstf_ref, *, chunks):
    step = pl.program_id(0)
    n_chunks = cnt_ref.shape[0]
    n_tiles = run_ref.shape[0]
    max_blocks = tab_ref.shape[0]

    @pl.when(step == 0)
    def _():
        cnt = cnt_ref[...]
        total = jnp.sum(cnt, 0, keepdims=True)
        n_blk = jnp.floor((total + (BLOCK - 1)) * (1.0 / BLOCK))
        before = (_iota((LANES, LANES), 0) < _iota((LANES, LANES), 1)).astype(BF16)
        nb_hi, nb_lo = _split(jnp.broadcast_to(n_blk, (SUBLANES, LANES)))
        blk_start = (_dot(nb_hi, before) + _dot(nb_lo, before))[0:1]
        lane1 = _iota((1, LANES), 1)
        packed = blk_start
        shift = jnp.zeros((1, 1), F32)
        for g in range(1, N_GROUPS):
            first = jnp.sum(jnp.where(lane1 == g * PAIRS, blk_start, 0.0), -1, keepdims=True) + shift
            over = first - MOE_PACK * jnp.floor(first * (1.0 / MOE_PACK))
            shift = shift + jnp.where(over > 0.0, MOE_PACK - over, 0.0)
            packed = jnp.where(lane1 >= g * PAIRS, blk_start + shift, packed)
        blk_start = packed
        earlier = (_iota((n_chunks, n_chunks), 1) < _iota((n_chunks, n_chunks), 0)).astype(BF16)
        base_ref[...] = blk_start * float(BLOCK) + _dot(earlier, cnt.astype(BF16))
        blk_end = blk_start + n_blk
        sub = _iota((SUBLANES, LANES), 0)
        cblk_ref[...] = jnp.where(sub == 0, blk_start, jnp.where(sub == 1, blk_end, 0.0)).astype(I32)
        lane = _iota((max_blocks, LANES), 1)
        blk = _iota((max_blocks, LANES), 0).astype(F32)
        is_cls = lane < N_CLASSES
        n_real = jnp.max(jnp.where(is_cls, blk_end, 0.0), -1, keepdims=True)
        u = jnp.minimum(blk, n_real - 1.0)
        cls_of = jnp.sum(jnp.where(is_cls & (blk_end <= u), 1.0, 0.0), -1, keepdims=True)
        n_steps = jnp.floor((n_real + (MOE_PACK - 1)) * (1.0 / MOE_PACK))
        tab = jnp.where(lane == 0, cls_of, jnp.where(lane == 1, n_steps, 0.0))
        tab_ref[...] = tab.astype(I32)
        tile = _iota((n_tiles, n_chunks), 0)
        chunk = _iota((n_tiles, n_chunks), 1)
        owner = ((chunk >= tile * chunks) & (chunk < (tile + 1) * chunks)).astype(BF16)
        cnt_tile = _dot(owner, cnt.astype(BF16))
        ct_hi, ct_lo = _split(cnt_tile)
        local_start = _dot(ct_hi, before) + _dot(ct_lo, before)
        run_ref[...] = base_ref[pl.ds(0, n_tiles, stride=chunks), :].astype(I32)
        cntt_ref[...] = cnt_tile.astype(I32)
        lst_ref[...] = local_start.astype(I32)
        lstf_ref[...] = local_start

    lane = _iota((BLOCK, LANES), 1)
    tile_off = lstf_ref[pl.ds(step, 1), :] - base_ref[pl.ds(step * chunks, 1), :]
    pick_rank = (_iota((2 * SUBLANES, LANES), 1) == REC_RANK).astype(BF16)
    for c in range(chunks):
        rec = rec_ref[c * BLOCK:(c + 1) * BLOCK, :]
        hot = (lane == rec[:, REC_CLS:REC_CLS + 1].astype(I32)).astype(BF16)
        base = base_ref[pl.ds(step * chunks + c, 1), :] + tile_off
        b_hi, b_lo = _split(jnp.broadcast_to(base, (2 * SUBLANES, LANES)))
        lpos = _dot_nt(b_hi, hot) + _dot_nt(b_lo, hot) + _dot_nt(pick_rank, rec.astype(BF16))
        lpos_ref[c] = lpos[0:1].astype(I32)


def _plan(counts, rec, max_blocks, chunks):
    n = rec.shape[0]
    n_chunks = n // BLOCK
    n_tiles = n_chunks // chunks
    tile_tab = jax.ShapeDtypeStruct((n_tiles, LANES), I32)
    whole = lambda rows: pl.BlockSpec((rows, LANES), lambda i: (0, 0))
    return pl.pallas_call(
        functools.partial(_plan_body, chunks=chunks),
        out_shape=[jax.ShapeDtypeStruct((n_chunks, 1, LANES), I32), jax.ShapeDtypeStruct((max_blocks, LANES), I32),
                   jax.ShapeDtypeStruct((SUBLANES, LANES), I32), tile_tab, tile_tab, tile_tab],
        grid=(n_tiles,),
        in_specs=[whole(n_chunks), pl.BlockSpec((chunks * BLOCK, ROUTE_W), lambda i: (i, 0))],
        out_specs=[pl.BlockSpec((chunks, 1, LANES), lambda i: (i, 0, 0)), whole(max_blocks), whole(SUBLANES),
                   whole(n_tiles), whole(n_tiles), whole(n_tiles)],
        scratch_shapes=[pltpu.VMEM((n_chunks, LANES), F32), pltpu.VMEM((n_tiles, LANES), F32)],
        compiler_params=_params("arbitrary"),
        name="plan",
    )(counts, rec)


def _tile_rows(base, j, pitch):
    return pl.ds(base + j, BLOCK, stride=pitch)


def _load_tiles(ref, base, pitch, width):
    return jnp.concatenate([ref[_tile_rows(base, j, pitch), :] for j in range(width // LANES)], axis=1)


def _store_tiles(ref, base, pitch, val):
    for j in range(val.shape[1] // LANES):
        ref[_tile_rows(base, j, pitch), :] = val[:, j * LANES:(j + 1) * LANES]


def _start_runs(cnt_ref, lst_ref, run_ref, streams, sem, to_hbm):
    def issue(c):
        for buf, base, hbm, pitch in streams:
            rows = cnt_ref[0, c] * pitch
            aligned = (lambda row: pl.multiple_of(row, pitch)) if pitch > 1 else (lambda row: row)
            v = buf.at[pl.ds(aligned(base + lst_ref[0, c] * pitch), rows)]
            h = hbm.at[pl.ds(aligned(run_ref[0, c] * pitch), rows)]
            (pltpu.make_async_copy(v, h, sem) if to_hbm else pltpu.make_async_copy(h, v, sem)).start(priority=c % 2)

    for c in range(N_CLASSES):
        pl.when(cnt_ref[0, c] > 0)(functools.partial(issue, c))


def _wait_runs(streams, tokens, sem):
    for buf, base, hbm, pitch in streams:
        rows = tokens * pitch
        pltpu.make_async_copy(hbm.at[pl.ds(0, rows)], buf.at[pl.ds(base, rows)], sem).wait()


def _scatter_body(cls_blk_ref, run_ref, cnt_ref, lst_ref, lpos_ref, x1_ref, rec_ref, sh_ref, sc_ref, xs_hbm, meta_hbm,
                  abuf, bbuf, mbuf, zbuf, ssem, zsem):
    u = pl.program_id(0)
    slot = u % 2
    tokens = x1_ref.shape[0]
    chunks = tokens // BLOCK
    tile_rows = BLOCK * SUBLANES
    sort_rows = chunks * tile_rows

    def streams(s):
        return ((bbuf, s * sort_rows, xs_hbm, SUBLANES), (mbuf, s * tokens, meta_hbm, 1))

    @pl.when(u == 0)
    def _():
        zbuf[...] = jnp.zeros_like(zbuf)

        def zero_block(blk, start):
            for hbm, rows in ((xs_hbm, tile_rows), (meta_hbm, BLOCK)):
                cp = pltpu.make_async_copy(zbuf.at[pl.ds(0, rows)], hbm.at[pl.ds(pl.multiple_of(blk * rows, rows), rows)],
                                           zsem)
                cp.start() if start else cp.wait()

        for start in (True, False):
            for c in range(N_CLASSES):
                pl.when(cls_blk_ref[1, c] > cls_blk_ref[0, c])(functools.partial(zero_block, cls_blk_ref[1, c] - 1, start))
        gaps = [(cls_blk_ref[1, g * PAIRS - 1], cls_blk_ref[0, g * PAIRS]) for g in range(1, N_GROUPS)]
        gaps.append((cls_blk_ref[1, N_CLASSES - 1], xs_hbm.shape[0] // tile_rows))
        for start in (True, False):
            for lo, hi in gaps:
                lax.fori_loop(lo, hi, lambda blk, carry: (zero_block(blk, start), carry)[1], 0)

    @pl.when(u >= 2)
    def _():
        _wait_runs(streams(slot), tokens, ssem.at[slot])

    scale = 1.0 + sc_ref[...]
    shift = sh_ref[...]

    unroll = abuf.shape[0] // tile_rows

    def sort_chunks(i, carry):
        for k in range(unroll):
            c = i * unroll + k
            stage = k * tile_rows
            rows = pl.ds(pl.multiple_of(c * BLOCK, BLOCK), BLOCK)
            _store_tiles(abuf, stage, SUBLANES, _ln(x1_ref[rows, :]) * scale + shift)
            rec = rec_ref[rows, :]
            for r in range(BLOCK):
                p = lpos_ref[c, 0, r]
                dst = pl.multiple_of(slot * sort_rows + p * SUBLANES, SUBLANES)
                bbuf[pl.ds(dst, SUBLANES), :] = abuf[stage + r * SUBLANES:stage + (r + 1) * SUBLANES, :]
                mbuf[pl.ds(slot * tokens + p, 1), :] = rec[r:r + 1, :]
        return carry

    lax.fori_loop(0, chunks // unroll, sort_chunks, 0)
    _start_runs(cnt_ref, lst_ref, run_ref, streams(slot), ssem.at[slot], True)

    @pl.when(u == pl.num_programs(0) - 1)
    def _():
        _wait_runs(streams(slot), tokens, ssem.at[slot])

        @pl.when(u >= 1)
        def _():
            _wait_runs(streams(1 - slot), tokens, ssem.at[1 - slot])


def _stage_chunks(chunks):
    return next(p for p in (4, 2, 1) if chunks % p == 0)


def _tile_tab_spec(off, n_tiles):
    return pl.BlockSpec((None, 1, LANES), lambda u: (jnp.minimum(u + off, n_tiles - 1), 0, 0),
                        memory_space=pltpu.SMEM)


def _scatter(x1, rec, mod3, lpos, tile_tabs, cls_blk, max_blocks, seq, chunks):
    n, d = x1.shape
    n_tiles = n // (chunks * BLOCK)
    per_batch = seq // (chunks * BLOCK)
    tile_rows = BLOCK * SUBLANES
    hbm = pl.BlockSpec(memory_space=pl.ANY)
    return pl.pallas_call(
        _scatter_body,
        out_shape=[jax.ShapeDtypeStruct((max_blocks * tile_rows, LANES), F32),
                   jax.ShapeDtypeStruct((max_blocks * BLOCK, ROUTE_W), F32)],
        grid=(n_tiles,),
        in_specs=[
            pl.BlockSpec(memory_space=pltpu.SMEM),
            _tile_tab_spec(0, n_tiles), _tile_tab_spec(0, n_tiles), _tile_tab_spec(0, n_tiles),
            pl.BlockSpec((chunks, 1, LANES), lambda u: (u, 0, 0), memory_space=pltpu.SMEM),
            pl.BlockSpec((chunks * BLOCK, d), lambda u: (u, 0)),
            pl.BlockSpec((chunks * BLOCK, ROUTE_W), lambda u: (u, 0)),
            pl.BlockSpec((None, 1, d), lambda u: (u // per_batch, 0, 3)),
            pl.BlockSpec((None, 1, d), lambda u: (u // per_batch, 0, 4)),
        ],
        out_specs=[hbm, hbm],
        scratch_shapes=[
            pltpu.VMEM((_stage_chunks(chunks) * tile_rows, LANES), F32),
            pltpu.VMEM((2 * chunks * tile_rows, LANES), F32),
            pltpu.VMEM((2 * chunks * BLOCK, ROUTE_W), F32),
            pltpu.VMEM((tile_rows, LANES), F32),
            pltpu.SemaphoreType.DMA((2,)),
            pltpu.SemaphoreType.DMA,
        ],
        compiler_params=_params("arbitrary"),
        name="scatter",
    )(cls_blk, *tile_tabs, lpos, x1, rec, mod3, mod3)


def _expert(h, w1, w3, w2):
    hid = jax.nn.silu(_dot(h, w1)) * _dot(h, w3)
    return _dot(hid, w2)


def _moe_body(sgrp_ref, blo_ref, bhi_ref, nsteps_ref, xs_ref, meta_ref, w1_ref, w3_ref, w2_ref, o_ref):
    u = pl.program_id(0)
    d = w1_ref.shape[1]
    tile_rows = BLOCK * SUBLANES

    @pl.when(u < nsteps_ref[0])
    def _():
        jobs = [(k, e) for k in range(MOE_PACK) for e in (blo_ref[MOE_PACK * u + k], bhi_ref[MOE_PACK * u + k])]
        h = [_load_tiles(xs_ref, k * tile_rows, SUBLANES, d).astype(BF16) for k in range(MOE_PACK)]
        up = [(_dot(h[k], w1_ref[e]), _dot(h[k], w3_ref[e])) for k, e in jobs]
        hid = [(jax.nn.silu(a) * b).astype(BF16) for a, b in up]
        y = [_dot(g, w2_ref[e]) for (k, e), g in zip(jobs, hid)]
        for k in range(MOE_PACK):
            meta = meta_ref[k * BLOCK:(k + 1) * BLOCK, :]
            _store_tiles(o_ref, k * tile_rows, SUBLANES,
                         meta[:, REC_WLO:REC_WLO + 1] * y[2 * k] + meta[:, REC_WHI:REC_WHI + 1] * y[2 * k + 1])

    @pl.when(u >= nsteps_ref[0])
    def _():
        o_ref[...] = jnp.zeros_like(o_ref)


def _moe(xs, meta, tables, w1, w3, w2, d):
    sgrp, blo, bhi, n_steps = tables
    max_steps = sgrp.shape[0]
    de = w1.shape[1]
    in_rows = MOE_PACK * BLOCK * SUBLANES
    w1g = w1.reshape(N_GROUPS, EPG, d, de)
    w3g = w3.reshape(N_GROUPS, EPG, d, de)
    w2g = w2.reshape(N_GROUPS, EPG, de, d)

    def w_spec(a):
        return pl.BlockSpec((None,) + a.shape[1:], lambda u, g, lo, hi, n_steps: (g[u], 0, 0, 0))

    grid_spec = pltpu.PrefetchScalarGridSpec(
        num_scalar_prefetch=4,
        grid=(max_steps,),
        in_specs=[
            pl.BlockSpec((in_rows, LANES), lambda u, g, lo, hi, n_steps: (jnp.minimum(u, n_steps[0] - 1), 0)),
            pl.BlockSpec((MOE_PACK * BLOCK, ROUTE_W), lambda u, g, lo, hi, n_steps: (jnp.minimum(u, n_steps[0] - 1), 0)),
            w_spec(w1g), w_spec(w3g), w_spec(w2g),
        ],
        out_specs=pl.BlockSpec((in_rows, LANES), lambda u, *_: (u, 0)),
    )
    return pl.pallas_call(
        _moe_body,
        out_shape=jax.ShapeDtypeStruct((max_steps * in_rows, LANES), F32),
        grid_spec=grid_spec,
        compiler_params=pltpu.CompilerParams(dimension_semantics=("arbitrary",), vmem_limit_bytes=MOE_VMEM_LIMIT_BYTES),
        name="moe",
    )(sgrp, blo, bhi, n_steps, xs, meta, w1g, w3g, w2g)


def _final_body(run_ref, cnt_ref, lst_ref, runn_ref, cntn_ref, lstn_ref, lpos_ref, x1_ref, g2_ref,
                lng_ref, lnb_ref, ms_hbm, o_ref, gbuf, tbuf, gsem):
    u = pl.program_id(0)
    slot = u % 2
    d = x1_ref.shape[-1]
    tokens = x1_ref.shape[0]
    chunks = tokens // BLOCK
    tile_rows = BLOCK * SUBLANES
    sort_rows = chunks * tile_rows

    def streams(s):
        return ((gbuf, s * sort_rows, ms_hbm, SUBLANES),)

    @pl.when(u == 0)
    def _():
        _start_runs(cnt_ref, lst_ref, run_ref, streams(0), gsem.at[0], False)

    _wait_runs(streams(slot), tokens, gsem.at[slot])
    _start_runs(cntn_ref, lstn_ref, runn_ref, streams(1 - slot), gsem.at[1 - slot], False)
    gain = g2_ref[...]
    ln_g = lng_ref[...]
    ln_b = lnb_ref[...]
    unroll = tbuf.shape[0] // tile_rows

    def finish_chunks(i, carry):
        for k in range(unroll):
            c = i * unroll + k
            stage = k * tile_rows
            for r in range(BLOCK):
                src = pl.ds(pl.multiple_of(slot * sort_rows + lpos_ref[c, 0, r] * SUBLANES, SUBLANES), SUBLANES)
                tbuf[stage + r * SUBLANES:stage + (r + 1) * SUBLANES, :] = gbuf[src, :]
            rows = pl.ds(pl.multiple_of(c * BLOCK, BLOCK), BLOCK)
            moe = _load_tiles(tbuf, stage, SUBLANES, d)
            o_ref[rows, :] = _ln(ALPHA * x1_ref[rows, :] + gain * moe) * ln_g + ln_b
        return carry

    lax.fori_loop(0, chunks // unroll, finish_chunks, 0)

    @pl.when(u == pl.num_programs(0) - 1)
    def _():
        _wait_runs(streams(1 - slot), tokens, gsem.at[1 - slot])


def _final(x1, ms, lpos, tile_tabs, mod3, ln_g, ln_b, seq, chunks):
    n, d = x1.shape
    n_tiles = n // (chunks * BLOCK)
    per_batch = seq // (chunks * BLOCK)
    tile_rows = BLOCK * SUBLANES
    vec = pl.BlockSpec((1, d), lambda u: (0, 0))
    return pl.pallas_call(
        _final_body,
        out_shape=jax.ShapeDtypeStruct((n, d), F32),
        grid=(n_tiles,),
        in_specs=[
            _tile_tab_spec(0, n_tiles), _tile_tab_spec(0, n_tiles), _tile_tab_spec(0, n_tiles),
            _tile_tab_spec(1, n_tiles), _tile_tab_spec(1, n_tiles), _tile_tab_spec(1, n_tiles),
            pl.BlockSpec((chunks, 1, LANES), lambda u: (u, 0, 0), memory_space=pltpu.SMEM),
            pl.BlockSpec((chunks * BLOCK, d), lambda u: (u, 0)),
            pl.BlockSpec((None, 1, d), lambda u: (u // per_batch, 0, 5)),
            vec, vec, pl.BlockSpec(memory_space=pl.ANY),
        ],
        out_specs=pl.BlockSpec((chunks * BLOCK, d), lambda u: (u, 0)),
        scratch_shapes=[
            pltpu.VMEM((2 * chunks * tile_rows, LANES), F32),
            pltpu.VMEM((_stage_chunks(chunks) * tile_rows, LANES), F32),
            pltpu.SemaphoreType.DMA((2,)),
        ],
        compiler_params=_params("arbitrary"),
        name="final",
    )(*tile_tabs, *tile_tabs, lpos, x1, mod3, ln_g.reshape(1, d), ln_b.reshape(1, d), ms)


def _block_tables(tab):
    cls = tab[:, 0]
    pair = cls % PAIRS
    first = jnp.array([i for i in range(EPG) for _ in range(i + 1, EPG)], I32)
    second = jnp.array([j for i in range(EPG) for j in range(i + 1, EPG)], I32)
    return (cls // PAIRS)[::MOE_PACK], first[pair], second[pair], tab[0, 1].reshape(1)


def kernel(x, c, ctx, c_ctx, w_ada, b_ada, w_in, attn_sink, gm_ln_g, gm_ln_b, gm_ws, gm_bs, w_pa, w_pb, w_o,
           ln1_g, ln1_b, router_g_w, router_g_b, router_e_w, router_e_b, moe_w1, moe_w3, moe_w2, ln2_g, ln2_b):
    b, s, d = x.shape
    assert w_ada.shape[0] == DEPTH and s % BLOCK == 0 and s % GRID_W == 0
    mod_rows = 2 * SUBLANES
    assert b + 1 <= mod_rows
    c_rows = jnp.zeros((mod_rows, d), F32).at[:b].set(c).at[b].set(c_ctx)
    mod = _adaln_mod(c_rows, w_ada[0], b_ada[0])
    mod3 = mod.reshape(mod_rows, 1, 6 * d)
    w_in_bf = w_in[0].astype(BF16)
    kv_ctx = _ctx_kv(ctx, mod3, b, w_in_bf)
    tile = next(t for t in (512, BLOCK) if s % t == 0)
    experts, _, d_expert = moe_w1.shape[1:]
    moe_w = [moe_w1[0].reshape(experts * d, d_expert), moe_w3[0].reshape(experts * d, d_expert),
             moe_w2[0].reshape(experts * d_expert, d)]
    steps = b * (s // tile)
    sliced = all(w.shape[0] % (steps * 2 * SUBLANES) == 0 for w in moe_w)
    (q, kv, u, vg, gates), moe_w_bf = _project(x, mod3, w_in_bf, _rope_tables(s), gm_ln_g[0], gm_ln_b[0], tile,
                                                moe_w if sliced else [])
    if not sliced:
        moe_w_bf = [w.astype(BF16) for w in moe_w]
    nb = s // BLOCK
    y_attn = _attend(q, kv, kv_ctx, attn_sink[0], next(p for p in (8, 4, 2, 1) if nb % p == 0))
    bs_rows = jnp.repeat(gm_bs[0].T, GM_HEAD, axis=1)
    w_route = jnp.concatenate([router_g_w[0], jnp.transpose(router_e_w[0], (1, 0, 2)).reshape(d, N_EXPERTS)], axis=1)
    w_route = jnp.pad(w_route, ((0, 0), (0, ROUTE_W - w_route.shape[1])))
    b_route = jnp.pad(jnp.concatenate([router_g_b[0], router_e_b[0].reshape(-1)]), (0, ROUTE_W - N_GROUPS - N_EXPERTS))
    wr_hi = w_route.astype(BF16)
    wr_lo = (w_route - wr_hi.astype(F32)).astype(BF16)
    merge_tile = BLOCK * next(p for p in (8, 4, 2, 1) if nb % p == 0)
    x1, rec, counts = _merge(x, y_attn, u, vg, gates, gm_ws[0].astype(BF16), bs_rows, w_pa[0].astype(BF16),
                             w_pb[0].astype(BF16), w_o[0].astype(BF16), ln1_g[0], ln1_b[0], mod3, wr_hi, wr_lo,
                             b_route.reshape(1, ROUTE_W), merge_tile)
    n = b * s
    x1 = x1.reshape(n, d)
    rec = rec.reshape(n, ROUTE_W)
    max_blocks = n // BLOCK + N_CLASSES - 1 + (N_GROUPS - 1) * (MOE_PACK - 1)
    max_blocks = -(-max_blocks // MOE_PACK) * MOE_PACK
    sort_chunks = next(p for p in (SUBLANES, 4, 2, 1) if nb % p == 0)
    lpos, tab, cls_blk, *tile_tabs = _plan(counts.reshape(n // BLOCK, LANES), rec, max_blocks, sort_chunks)
    tile_tabs = [t.reshape(t.shape[0], 1, LANES) for t in tile_tabs]
    xs, meta = _scatter(x1, rec, mod3, lpos, tile_tabs, cls_blk, max_blocks, s, sort_chunks)
    ms = _moe(xs, meta, _block_tables(tab), *moe_w_bf, d)
    out = _final(x1, ms, lpos, tile_tabs, mod3, ln2_g[0], ln2_b[0], s, sort_chunks)
    return out.reshape(b, s, d)
```

```python
import functools

import jax
import jax.numpy as jnp
from jax import lax
from jax.experimental import pallas as pl
from jax.experimental.pallas import tpu as pltpu

F32 = jnp.float32
BF16 = jnp.bfloat16
I32 = jnp.int32

GRID_W = 64
HEAD_DIM = 64
Q_HEADS = 8
KV_HEADS = 2
GROUP = Q_HEADS // KV_HEADS
BLOCK = 128
ROPE_BASE = 10000.0
ROPE_FREQS = HEAD_DIM // 4
GM_GROUPS = 8
GM_HEAD = 64
ATTN_W = Q_HEADS * HEAD_DIM
KV_W = KV_HEADS * HEAD_DIM
GM_W = GM_GROUPS * GM_HEAD
N_GROUPS = 4
EPG = 8
N_EXPERTS = N_GROUPS * EPG
PAIRS = EPG * (EPG - 1) // 2
N_CLASSES = N_GROUPS * PAIRS
LN_EPS = 1e-6
NEG_INF = -1e30
DEPTH = 1
ALPHA = (2.0 * DEPTH) ** 0.25

LANES = 128
SUBLANES = 8
VMEM_LIMIT_BYTES = 56 * 1024 * 1024
MOE_VMEM_LIMIT_BYTES = 60 * 1024 * 1024
MOE_PACK = 4
ROUTE_W = LANES
REC_WLO, REC_WHI, REC_CLS, REC_RANK = 0, 1, 2, 3


def _params(*sem):
    return pltpu.CompilerParams(dimension_semantics=sem, vmem_limit_bytes=VMEM_LIMIT_BYTES)


def _ln(x):
    mu = jnp.mean(x, -1, keepdims=True)
    xc = x - mu
    var = jnp.mean(xc * xc, -1, keepdims=True)
    return xc * lax.rsqrt(var + LN_EPS)


def _dot(a, b):
    return jnp.dot(a, b, preferred_element_type=F32)


def _dot_nt(a, b):
    return lax.dot_general(a, b, (((1,), (1,)), ((), ())), preferred_element_type=F32)


def _split(a):
    hi = a.astype(BF16)
    lo = (a - hi.astype(F32)).astype(BF16)
    return hi, lo


def _dot_split(a, b_hi, b_lo):
    a_hi, a_lo = _split(a)
    return _dot(a_hi, b_hi) + _dot(a_hi, b_lo) + _dot(a_lo, b_hi)


def _iota(shape, axis):
    return lax.broadcasted_iota(I32, shape, axis)


def _mod_body(c_ref, w_ref, b_ref, o_ref):
    w_hi, w_lo = _split(w_ref[...])
    o_ref[...] = _dot_split(jax.nn.silu(c_ref[...]), w_hi, w_lo) + b_ref[...]


def _adaln_mod(c_rows, w_ada, b_ada):
    rows, d = c_rows.shape
    width = w_ada.shape[1]
    tn = 6 * LANES
    return pl.pallas_call(
        _mod_body,
        out_shape=jax.ShapeDtypeStruct((rows, width), F32),
        grid=(width // tn,),
        in_specs=[
            pl.BlockSpec((rows, d), lambda j: (0, 0)),
            pl.BlockSpec((d, tn), lambda j: (0, j)),
            pl.BlockSpec((1, tn), lambda j: (0, j)),
        ],
        out_specs=pl.BlockSpec((rows, tn), lambda j: (0, j)),
        compiler_params=_params("arbitrary"),
        name="mod",
    )(c_rows, w_ada, b_ada.reshape(1, width))


def _ctx_body(ctx_ref, sh_ref, sc_ref, w_ref, o_ref):
    h = _ln(ctx_ref[...]) * (1.0 + sc_ref[...]) + sh_ref[...]
    o_ref[...] = _dot(h.astype(BF16), w_ref[...]).astype(BF16)


def _ctx_kv(ctx, mod3, ctx_row, w_in_bf):
    b, c_len, d = ctx.shape
    kvw = 2 * KV_W
    return pl.pallas_call(
        _ctx_body,
        out_shape=jax.ShapeDtypeStruct((b, c_len, kvw), BF16),
        grid=(b,),
        in_specs=[
            pl.BlockSpec((None, c_len, d), lambda i: (i, 0, 0)),
            pl.BlockSpec((None, 1, d), lambda i: (ctx_row, 0, 0)),
            pl.BlockSpec((None, 1, d), lambda i: (ctx_row, 0, 1)),
            pl.BlockSpec((d, kvw), lambda i: (0, ATTN_W // kvw)),
        ],
        out_specs=pl.BlockSpec((None, c_len, kvw), lambda i: (i, 0, 0)),
        compiler_params=_params("arbitrary"),
        name="ctx_kv",
    )(ctx, mod3, mod3, w_in_bf)


def _rope(x, cos, s_up, s_dn):
    return x * cos + pltpu.roll(x, LANES - ROPE_FREQS, 1) * s_up + pltpu.roll(x, ROPE_FREQS, 1) * s_dn


def _project_body(x_ref, sh_ref, sc_ref, w_ref, cos_ref, sup_ref, sdn_ref, lng_ref, lnb_ref,
                  q_ref, kv_ref, u_ref, vg_ref, gate_ref):
    h =(_ln(x_ref[...]) * (1.0 + sc_ref[...]) + sh_ref[...]).astype(BF16)
    cos, s_up, s_dn = cos_ref[...], sup_ref[...], sdn_ref[...]
    q_scale = HEAD_DIM ** -0.5
    k0 = ATTN_W
    u0 = k0 + 2 * KV_W
    vg0 = u0 + GM_W
    gate0 = vg0 + GM_W
    q = _dot(h, w_ref[:, 0:k0])
    for j in range(ATTN_W // LANES):
        sl = slice(j * LANES, (j + 1) * LANES)
        q_ref[:, sl] = (_rope(q[:, sl], cos, s_up, s_dn) * q_scale).astype(BF16)
    kv = _dot(h, w_ref[:, k0:u0])
    kv_ref[:, 0:KV_W] = _rope(kv[:, 0:KV_W], cos, s_up, s_dn).astype(BF16)
    kv_ref[:, KV_W:] = kv[:, KV_W:].astype(BF16)
    u_ref[...] = jax.nn.gelu(_dot(h, w_ref[:, u0:vg0])).astype(BF16)
    vg = _ln(jax.nn.gelu(_dot(h, w_ref[:, vg0:gate0]))) * lng_ref[...] + lnb_ref[...]
    vg_ref[...] = vg.astype(BF16)
    gate_ref[...] = jax.nn.sigmoid(_dot(h, w_ref[:, gate0:])).astype(BF16)


def _project(x, mod3, w_in_bf, rope_tabs, gm_ln_g, gm_ln_b, tile):
    b, s, d = x.shape
    in_w = w_in_bf.shape[1]
    gate_w = in_w - (ATTN_W + 2 * KV_W + 2 * GM_W)
    row = lambda width: pl.BlockSpec((None, tile, width), lambda i, t: (i, t, 0))
    tab = pl.BlockSpec((tile, LANES), lambda i, t: (t, 0))
    vec = lambda width: pl.BlockSpec((1, width), lambda i, t: (0, 0))
    return pl.pallas_call(
        _project_body,
        out_shape=[
            jax.ShapeDtypeStruct((b, s, ATTN_W), BF16),
            jax.ShapeDtypeStruct((b, s, 2 * KV_W), BF16),
            jax.ShapeDtypeStruct((b, s, GM_W), BF16),
            jax.ShapeDtypeStruct((b, s, GM_W), BF16),
            jax.ShapeDtypeStruct((b, s, gate_w), BF16),
        ],
        grid=(b, s // tile),
        in_specs=[
            row(d),
            pl.BlockSpec((None, 1, d), lambda i, t: (i, 0, 0)),
            pl.BlockSpec((None, 1, d), lambda i, t: (i, 0, 1)),
            pl.BlockSpec((d, in_w), lambda i, t: (0, 0)),
            tab, tab, tab,
            vec(GM_W), vec(GM_W),
        ],
        out_specs=[row(ATTN_W), row(2 * KV_W), row(GM_W), row(GM_W), row(gate_w)],
        compiler_params=_params("arbitrary", "arbitrary"),
        name="project",
    )(x, mod3, mod3, w_in_bf, *rope_tabs, gm_ln_g.reshape(1, GM_W), gm_ln_b.reshape(1, GM_W))


def _rope_tables(s):
    pos = jnp.arange(s, dtype=I32)
    inv = 1.0 / (ROPE_BASE ** (jnp.arange(ROPE_FREQS, dtype=F32) / ROPE_FREQS))
    ang_r = (pos // GRID_W).astype(F32)[:, None] * inv[None, :]
    ang_c = (pos % GRID_W).astype(F32)[:, None] * inv[None, :]
    cr, sr, cc, sn = jnp.cos(ang_r), jnp.sin(ang_r), jnp.cos(ang_c), jnp.sin(ang_c)
    z = jnp.zeros_like(sr)
    reps = LANES // HEAD_DIM
    cos = jnp.tile(jnp.concatenate([cr, cr, cc, cc], -1), (1, reps))
    s_up = jnp.tile(jnp.concatenate([-sr, z, -sn, z], -1), (1, reps))
    s_dn = jnp.tile(jnp.concatenate([z, sr, z, sn], -1), (1, reps))
    return cos, s_up, s_dn


def _attend_body(sink_ref, q_ref, kvp_ref, kvc_ref, kvn_ref, ctx_ref, *rest):
    n_side = (len(rest) - 1) // 2
    side_in, o_ref, side_out = rest[:n_side], rest[n_side], rest[n_side + 1:]
    for src, dst in zip(side_in, side_out):
        dst[...] = src[...].astype(BF16)
    i = pl.program_id(1)
    n_steps = pl.num_programs(1)
    per = q_ref.shape[0] // BLOCK
    rows = GROUP * BLOCK
    r = _iota((rows, 3 * BLOCK), 0) & (BLOCK - 1)
    m = _iota((rows, 3 * BLOCK), 1)
    band = (m >= r) & (m <= r + 2 * BLOCK)
    grp = _iota((rows, 1), 0) // BLOCK
    kv_all = jnp.concatenate([kvp_ref[...], kvc_ref[...], kvn_ref[...]], axis=0)
    kv_ctx = ctx_ref[...]
    for t in range(per):
        mask = band
        if t == 0:
            mask = mask & ((m >= BLOCK) | (i > 0))
        if t == per - 1:
            mask = mask & ((m < 2 * BLOCK) | (i < n_steps - 1))
        q = q_ref[t * BLOCK:(t + 1) * BLOCK, :]
        kv_loc = kv_all[t * BLOCK:(t + 3) * BLOCK]
        scores = []
        for h in range(KV_HEADS):
            ks = slice(h * HEAD_DIM, (h + 1) * HEAD_DIM)
            qg = jnp.concatenate(
                [q[:, (h * GROUP + g) * HEAD_DIM:(h * GROUP + g + 1) * HEAD_DIM] for g in range(GROUP)], axis=0)
            scores.append((_dot_nt(qg, kv_ctx[:, ks]), jnp.where(mask, _dot_nt(qg, kv_loc[:, ks]), NEG_INF)))
        probs = []
        for h, (s_ctx, s_loc) in enumerate(scores):
            sink = jnp.zeros((rows, 1), F32)
            for g in range(GROUP):
                sink = jnp.where(grp == g, sink_ref[h * GROUP + g], sink)
            top = jnp.maximum(jnp.maximum(jnp.max(s_ctx, -1, keepdims=True), jnp.max(s_loc, -1, keepdims=True)), sink)
            e_ctx = jnp.exp(s_ctx - top)
            e_loc = jnp.exp(s_loc - top)
            den = jnp.sum(e_ctx, -1, keepdims=True) + jnp.sum(e_loc, -1, keepdims=True) + jnp.exp(sink - top)
            probs.append((e_ctx.astype(BF16), e_loc.astype(BF16), 1.0 / den))
        outs = []
        for h, (e_ctx, e_loc, inv) in enumerate(probs):
            vs = slice(KV_W + h * HEAD_DIM, KV_W + (h + 1) * HEAD_DIM)
            o = (_dot(e_ctx, kv_ctx[:, vs]) + _dot(e_loc, kv_loc[:, vs])) * inv
            outs += [o[g * BLOCK:(g + 1) * BLOCK] for g in range(GROUP)]
        o_ref[t * BLOCK:(t + 1) * BLOCK, :] = jnp.concatenate(outs, axis=1).astype(BF16)


def _attend(q, kv, kv_ctx, sink, per, side):
    b, s, _ = q.shape
    nb = s // BLOCK
    nt = nb // per
    c_len = kv_ctx.shape[1]
    kvw = 2 * KV_W
    side_specs = [pl.BlockSpec((a.shape[0] // (b * nt), a.shape[1]), lambda i, j: (i * nt + j, 0)) for a in side]
    out_spec = pl.BlockSpec((None, per * BLOCK, ATTN_W), lambda i, j: (i, j, 0))
    outs = pl.pallas_call(
        _attend_body,
        out_shape=[jax.ShapeDtypeStruct((b, s, ATTN_W), BF16)] + [jax.ShapeDtypeStruct(a.shape, BF16) for a in side],
        grid=(b, nt),
        in_specs=[
            pl.BlockSpec(memory_space=pltpu.SMEM),
            pl.BlockSpec((None, per * BLOCK, ATTN_W), lambda i, j: (i, j, 0)),
            pl.BlockSpec((None, BLOCK, kvw), lambda i, j: (i, jnp.maximum(j * per - 1, 0), 0)),
            pl.BlockSpec((None, per * BLOCK, kvw), lambda i, j: (i, j, 0)),
            pl.BlockSpec((None, BLOCK, kvw), lambda i, j: (i, jnp.minimum((j + 1) * per, nb - 1), 0)),
            pl.BlockSpec((None, c_len, kvw), lambda i, j: (i, 0, 0)),
        ] + side_specs,
        out_specs=[out_spec] + side_specs,
        compiler_params=_params("arbitrary", "arbitrary"),
        name="attend",
    )(sink, q, kv, kv, kv, kv_ctx, *side)
    return outs[0], outs[1:]


def _route(logits):
    lane = _iota(logits.shape, 1)
    far = jnp.int32(2 * ROUTE_W)

    def first_max(vals):
        top = jnp.max(vals, -1, keepdims=True)
        return top, jnp.min(jnp.where(vals == top, lane, far), -1, keepdims=True)

    is_g = lane < N_GROUPS
    g_top, g_idx = first_max(jnp.where(is_g, logits, NEG_INF))
    g_w = 1.0 / jnp.sum(jnp.where(is_g, jnp.exp(logits - g_top), 0.0), -1, keepdims=True)
    lo = N_GROUPS + EPG * g_idx
    e_vals = jnp.where((lane >= lo) & (lane < lo + EPG), logits, NEG_INF)
    v1, i1 = first_max(e_vals)
    v2, i2 = first_max(jnp.where(lane == i1, NEG_INF, e_vals))
    t = jnp.exp(v2 - v1)
    w1 = g_w * (1.0 / (1.0 + t))
    w2 = g_w * (t / (1.0 + t))
    swap = i2 < i1
    a = jnp.where(swap, i2, i1) - lo
    b = jnp.where(swap, i1, i2) - lo
    cls = g_idx * PAIRS + ((a * (2 * EPG - 1 - a)) >> 1) + (b - a - 1)
    return jnp.where(swap, w2, w1), jnp.where(swap, w1, w2), cls


def _merge_body(x_ref, ya_ref, u_ref, vg_ref, gate_ref, ws_ref, bs_ref, wpa_ref, wpb_ref, wo_ref,
                lng_ref, lnb_ref, g1_ref, sh2_ref, sc2_ref, wrh_ref, wrl_ref, br_ref, x1_ref, rec_ref, cnt_ref):
    tile, d = x_ref.shape
    lane = _iota((BLOCK, LANES), 1)
    below = (lane < _iota((BLOCK, LANES), 0)).astype(BF16)
    chunks = []
    for c in range(tile // BLOCK):
        rows = slice(c * BLOCK, (c + 1) * BLOCK)
        pieces = []
        for j in range(GM_W // LANES):
            v = vg_ref[rows, j * LANES:(j + 1) * LANES]
            per = LANES // GM_HEAD
            sp = _dot(ws_ref[per * j], v)
            for k in range(1, per):
                sp = jnp.where(lane < k * GM_HEAD, sp, _dot(ws_ref[per * j + k], v))
            pieces.append(sp)
        chunks.append(jnp.concatenate(pieces, axis=1) + bs_ref[...])
    sp = jnp.concatenate(chunks, axis=0)
    y_gm = (u_ref[...].astype(F32) * sp).astype(BF16)
    gate = gate_ref[...].astype(F32)
    y = gate[:, :d] * _dot(ya_ref[...], wpa_ref[...]) + gate[:, d:] * _dot(y_gm, wpb_ref[...])
    mix = _dot(y.astype(BF16), wo_ref[...])
    x1 = _ln(ALPHA * x_ref[...] + g1_ref[...] * mix) * lng_ref[...] + lnb_ref[...]
    x1_ref[...] = x1
    h2 = _ln(x1) * (1.0 + sc2_ref[...]) + sh2_ref[...]
    w_lo, w_hi, cls = _route(_dot_split(h2, wrh_ref[...], wrl_ref[...]) + br_ref[...])
    for c in range(tile // BLOCK):
        rows = slice(c * BLOCK, (c + 1) * BLOCK)
        onehot = lane == cls[rows]
        hot = onehot.astype(BF16)
        rank = jnp.sum(jnp.where(onehot, _dot(below, hot), 0.0), -1, keepdims=True)
        cnt_ref[c] = jnp.sum(hot.astype(F32), 0, keepdims=True)
        rec = jnp.where(lane == REC_WLO, w_lo[rows], 0.0)
        rec = jnp.where(lane == REC_WHI, w_hi[rows], rec)
        rec = jnp.where(lane == REC_CLS, cls[rows].astype(F32), rec)
        rec_ref[rows, :] = jnp.where(lane == REC_RANK, rank, rec)


def _merge(x, y_attn, u, vg, gates, ws_bf, bs_rows, wpa, wpb, wo, ln_g, ln_b, mod3, wr_hi, wr_lo, br, tile):
    b, s, d = x.shape
    nt = s // tile
    per = tile // BLOCK
    row = lambda width: pl.BlockSpec((None, tile, width), lambda i, j: (i, j, 0))
    full = lambda a: pl.BlockSpec(a.shape, lambda i, j: (0,) * a.ndim)
    modv = lambda k: pl.BlockSpec((None, 1, d), lambda i, j: (i, 0, k))
    ln_g = ln_g.reshape(1, d)
    ln_b = ln_b.reshape(1, d)
    return pl.pallas_call(
        _merge_body,
        out_shape=[jax.ShapeDtypeStruct((b, s, d), F32), jax.ShapeDtypeStruct((b, s, ROUTE_W), F32),
                   jax.ShapeDtypeStruct((b * s // BLOCK, 1, LANES), F32)],
        grid=(b, nt),
        in_specs=[
            row(d), row(ATTN_W), row(GM_W), row(GM_W), row(2 * d),
            full(ws_bf), full(bs_rows), full(wpa), full(wpb), full(wo), full(ln_g), full(ln_b),
            modv(2), modv(3), modv(4),
            full(wr_hi), full(wr_lo), full(br),
        ],
        out_specs=[row(d), row(ROUTE_W), pl.BlockSpec((per, 1, LANES), lambda i, j: (i * nt + j, 0, 0))],
        compiler_params=_params("arbitrary", "arbitrary"),
        name="merge",
    )(x, y_attn, u, vg, gates, ws_bf, bs_rows, wpa, wpb, wo, ln_g, ln_b, mod3, mod3, mod3, wr_hi, wr_lo, br)


def _plan_body(cnt_ref, rec_ref, lpos_ref, tab_ref, cblk_ref, run_ref, cntt_ref, lst_ref, base_ref, lstf_ref, *, chunks):
    step = pl.program_id(0)
    n_chunks = cnt_ref.shape[0]
    n_tiles = run_ref.shape[0]
    max_blocks = tab_ref.shape[0]

    @pl.when(step == 0)
    def _():
        cnt = cnt_ref[...]
        total = jnp.sum(cnt, 0, keepdims=True)
        n_blk = jnp.floor((total + (BLOCK - 1)) * (1.0 / BLOCK))
        before = (_iota((LANES, LANES), 0) < _iota((LANES, LANES), 1)).astype(BF16)
        nb_hi, nb_lo = _split(jnp.broadcast_to(n_blk, (SUBLANES, LANES)))
        blk_start = (_dot(nb_hi, before) + _dot(nb_lo, before))[0:1]
        lane1 = _iota((1, LANES), 1)
        packed = blk_start
        shift = jnp.zeros((1, 1), F32)
        for g in range(1, N_GROUPS):
            first = jnp.sum(jnp.where(lane1 == g * PAIRS, blk_start, 0.0), -1, keepdims=True) + shift
            over = first - MOE_PACK * jnp.floor(first * (1.0 / MOE_PACK))
            shift = shift + jnp.where(over > 0.0, MOE_PACK - over, 0.0)
            packed = jnp.where(lane1 >= g * PAIRS, blk_start + shift, packed)
        blk_start = packed
        earlier = (_iota((n_chunks, n_chunks), 1) < _iota((n_chunks, n_chunks), 0)).astype(BF16)
        base_ref[...] = blk_start * float(BLOCK) + _dot(earlier, cnt.astype(BF16))
        blk_end = blk_start + n_blk
        sub = _iota((SUBLANES, LANES), 0)
        cblk_ref[...] = jnp.where(sub == 0, blk_start, jnp.where(sub == 1, blk_end, 0.0)).astype(I32)
        lane = _iota((max_blocks, LANES), 1)
        blk = _iota((max_blocks, LANES), 0).astype(F32)
        is_cls = lane < N_CLASSES
        n_real = jnp.max(jnp.where(is_cls, blk_end, 0.0), -1, keepdims=True)
        u = jnp.minimum(blk, n_real - 1.0)
        cls_of = jnp.sum(jnp.where(is_cls & (blk_end <= u), 1.0, 0.0), -1, keepdims=True)
        n_steps = jnp.floor((n_real + (MOE_PACK - 1)) * (1.0 / MOE_PACK))
        tab = jnp.where(lane == 0, cls_of, jnp.where(lane == 1, n_steps, 0.0))
        tab_ref[...] = tab.astype(I32)
        tile = _iota((n_tiles, n_chunks), 0)
        chunk = _iota((n_tiles, n_chunks), 1)
        owner = ((chunk >= tile * chunks) & (chunk < (tile + 1) * chunks)).astype(BF16)
        cnt_tile = _dot(owner, cnt.astype(BF16))
        ct_hi, ct_lo = _split(cnt_tile)
        local_start = _dot(ct_hi, before) + _dot(ct_lo, before)
        run_ref[...] = base_ref[pl.ds(0, n_tiles, stride=chunks), :].astype(I32)
        cntt_ref[...] = cnt_tile.astype(I32)
        lst_ref[...] = local_start.astype(I32)
        lstf_ref[...] = local_start

    lane = _iota((BLOCK, LANES), 1)
    tile_off = lstf_ref[pl.ds(step, 1), :] - base_ref[pl.ds(step * chunks, 1), :]
    pick_rank = (_iota((2 * SUBLANES, LANES), 1) == REC_RANK).astype(BF16)
    for c in range(chunks):
        rec = rec_ref[c * BLOCK:(c + 1) * BLOCK, :]
        hot = (lane == rec[:, REC_CLS:REC_CLS + 1].astype(I32)).astype(BF16)
        base = base_ref[pl.ds(step * chunks + c, 1), :] + tile_off
        b_hi, b_lo = _split(jnp.broadcast_to(base, (2 * SUBLANES, LANES)))
        lpos = _dot_nt(b_hi, hot) + _dot_nt(b_lo, hot) + _dot_nt(pick_rank, rec.astype(BF16))
        lpos_ref[c] = lpos[0:1].astype(I32)


def _plan(counts, rec, max_blocks, chunks):
    n = rec.shape[0]
    n_chunks = n // BLOCK
    n_tiles = n_chunks // chunks
    tile_tab = jax.ShapeDtypeStruct((n_tiles, LANES), I32)
    whole = lambda rows: pl.BlockSpec((rows, LANES), lambda i: (0, 0))
    return pl.pallas_call(
        functools.partial(_plan_body, chunks=chunks),
        out_shape=[jax.ShapeDtypeStruct((n_chunks, 1, LANES), I32), jax.ShapeDtypeStruct((max_blocks, LANES), I32),
                   jax.ShapeDtypeStruct((SUBLANES, LANES), I32), tile_tab, tile_tab, tile_tab],
        grid=(n_tiles,),
        in_specs=[whole(n_chunks), pl.BlockSpec((chunks * BLOCK, ROUTE_W), lambda i: (i, 0))],
        out_specs=[pl.BlockSpec((chunks, 1, LANES), lambda i: (i, 0, 0)), whole(max_blocks), whole(SUBLANES),
                   whole(n_tiles), whole(n_tiles), whole(n_tiles)],
        scratch_shapes=[pltpu.VMEM((n_chunks, LANES), F32), pltpu.VMEM((n_tiles, LANES), F32)],
        compiler_params=_params("arbitrary"),
        name="plan",
    )(counts, rec)


def _tile_rows(base, j, pitch):
    return pl.ds(base + j, BLOCK, stride=pitch)


def _load_tiles(ref, base, pitch, width):
    return jnp.concatenate([ref[_tile_rows(base, j, pitch), :] for j in range(width // LANES)], axis=1)


def _store_tiles(ref, base, pitch, val):
    for j in range(val.shape[1] // LANES):
        ref[_tile_rows(base, j, pitch), :] = val[:, j * LANES:(j + 1) * LANES]


def _start_runs(cnt_ref, lst_ref, run_ref, streams, sem, to_hbm):
    def issue(c):
        for buf, base, hbm, pitch in streams:
            rows = cnt_ref[0, c] * pitch
            aligned = (lambda row: pl.multiple_of(row, pitch)) if pitch > 1 else (lambda row: row)
            v = buf.at[pl.ds(aligned(base + lst_ref[0, c] * pitch), rows)]
            h = hbm.at[pl.ds(aligned(run_ref[0, c] * pitch), rows)]
            (pltpu.make_async_copy(v, h, sem) if to_hbm else pltpu.make_async_copy(h, v, sem)).start(priority=c % 2)

    for c in range(N_CLASSES):
        pl.when(cnt_ref[0, c] > 0)(functools.partial(issue, c))


def _wait_runs(streams, tokens, sem):
    for buf, base, hbm, pitch in streams:
        rows = tokens * pitch
        pltpu.make_async_copy(hbm.at[pl.ds(0, rows)], buf.at[pl.ds(base, rows)], sem).wait()


def _scatter_body(cls_blk_ref, run_ref, cnt_ref, lst_ref, lpos_ref, x1_ref, rec_ref, sh_ref, sc_ref, xs_hbm, meta_hbm,
                  abuf, bbuf, mbuf, zbuf, ssem, zsem):
    u = pl.program_id(0)
    slot = u % 2
    tokens = x1_ref.shape[0]
    chunks = tokens // BLOCK
    tile_rows = BLOCK * SUBLANES
    sort_rows = chunks * tile_rows

    def streams(s):
        return ((bbuf, s * sort_rows, xs_hbm, SUBLANES), (mbuf, s * tokens, meta_hbm, 1))

    @pl.when(u == 0)
    def _():
        zbuf[...] = jnp.zeros_like(zbuf)

        def zero_block(blk, start):
            for hbm, rows in ((xs_hbm, tile_rows), (meta_hbm, BLOCK)):
                cp = pltpu.make_async_copy(zbuf.at[pl.ds(0, rows)], hbm.at[pl.ds(pl.multiple_of(blk * rows, rows), rows)],
                                           zsem)
                cp.start() if start else cp.wait()

        for start in (True, False):
            for c in range(N_CLASSES):
                pl.when(cls_blk_ref[1, c] > cls_blk_ref[0, c])(functools.partial(zero_block, cls_blk_ref[1, c] - 1, start))
        gaps = [(cls_blk_ref[1, g * PAIRS - 1], cls_blk_ref[0, g * PAIRS]) for g in range(1, N_GROUPS)]
        gaps.append((cls_blk_ref[1, N_CLASSES - 1], xs_hbm.shape[0] // tile_rows))
        for start in (True, False):
            for lo, hi in gaps:
                lax.fori_loop(lo, hi, lambda blk, carry: (zero_block(blk, start), carry)[1], 0)

    @pl.when(u >= 2)
    def _():
        _wait_runs(streams(slot), tokens, ssem.at[slot])

    scale = 1.0 + sc_ref[...]
    shift = sh_ref[...]

    unroll = abuf.shape[0] // tile_rows

    def sort_chunks(i, carry):
        for k in range(unroll):
            c = i * unroll + k
            stage = k * tile_rows
            rows = pl.ds(pl.multiple_of(c * BLOCK, BLOCK), BLOCK)
            _store_tiles(abuf, stage, SUBLANES, _ln(x1_ref[rows, :]) * scale + shift)
            rec = rec_ref[rows, :]
            for r in range(BLOCK):
                p = lpos_ref[c, 0, r]
                dst = pl.multiple_of(slot * sort_rows + p * SUBLANES, SUBLANES)
                bbuf[pl.ds(dst, SUBLANES), :] = abuf[stage + r * SUBLANES:stage + (r + 1) * SUBLANES, :]
                mbuf[pl.ds(slot * tokens + p, 1), :] = rec[r:r + 1, :]
        return carry

    lax.fori_loop(0, chunks // unroll, sort_chunks, 0)
    _start_runs(cnt_ref, lst_ref, run_ref, streams(slot), ssem.at[slot], True)

    @pl.when(u == pl.num_programs(0) - 1)
    def _():
        _wait_runs(streams(slot), tokens, ssem.at[slot])

        @pl.when(u >= 1)
        def _():
            _wait_runs(streams(1 - slot), tokens, ssem.at[1 - slot])


def _stage_chunks(chunks):
    return next(p for p in (4, 2, 1) if chunks % p == 0)


def _tile_tab_spec(off, n_tiles):
    return pl.BlockSpec((None, 1, LANES), lambda u: (jnp.minimum(u + off, n_tiles - 1), 0, 0),
                        memory_space=pltpu.SMEM)


def _scatter(x1, rec, mod3, lpos, tile_tabs, cls_blk, max_blocks, seq, chunks):
    n, d = x1.shape
    n_tiles = n // (chunks * BLOCK)
    per_batch = seq // (chunks * BLOCK)
    tile_rows = BLOCK * SUBLANES
    hbm = pl.BlockSpec(memory_space=pl.ANY)
    return pl.pallas_call(
        _scatter_body,
        out_shape=[jax.ShapeDtypeStruct((max_blocks * tile_rows, LANES), F32),
                   jax.ShapeDtypeStruct((max_blocks * BLOCK, ROUTE_W), F32)],
        grid=(n_tiles,),
        in_specs=[
            pl.BlockSpec(memory_space=pltpu.SMEM),
            _tile_tab_spec(0, n_tiles), _tile_tab_spec(0, n_tiles), _tile_tab_spec(0, n_tiles),
            pl.BlockSpec((chunks, 1, LANES), lambda u: (u, 0, 0), memory_space=pltpu.SMEM),
            pl.BlockSpec((chunks * BLOCK, d), lambda u: (u, 0)),
            pl.BlockSpec((chunks * BLOCK, ROUTE_W), lambda u: (u, 0)),
            pl.BlockSpec((None, 1, d), lambda u: (u // per_batch, 0, 3)),
            pl.BlockSpec((None, 1, d), lambda u: (u // per_batch, 0, 4)),
        ],
        out_specs=[hbm, hbm],
        scratch_shapes=[
            pltpu.VMEM((_stage_chunks(chunks) * tile_rows, LANES), F32),
            pltpu.VMEM((2 * chunks * tile_rows, LANES), F32),
            pltpu.VMEM((2 * chunks * BLOCK, ROUTE_W), F32),
            pltpu.VMEM((tile_rows, LANES), F32),
            pltpu.SemaphoreType.DMA((2,)),
            pltpu.SemaphoreType.DMA,
        ],
        compiler_params=_params("arbitrary"),
        name="scatter",
    )(cls_blk, *tile_tabs, lpos, x1, rec, mod3, mod3)


def _expert(h, w1, w3, w2):
    hid = jax.nn.silu(_dot(h, w1)) * _dot(h, w3)
    return _dot(hid, w2)


def _moe_body(sgrp_ref, blo_ref, bhi_ref, nsteps_ref, xs_ref, meta_ref, w1_ref, w3_ref, w2_ref, o_ref):
    u = pl.program_id(0)
    d = w1_ref.shape[1]
    tile_rows = BLOCK * SUBLANES

    @pl.when(u < nsteps_ref[0])
    def _():
        jobs = [(k, e) for k in range(MOE_PACK) for e in (blo_ref[MOE_PACK * u + k], bhi_ref[MOE_PACK * u + k])]
        h = [_load_tiles(xs_ref, k * tile_rows, SUBLANES, d).astype(BF16) for k in range(MOE_PACK)]
        up = [(_dot(h[k], w1_ref[e]), _dot(h[k], w3_ref[e])) for k, e in jobs]
        hid = [(jax.nn.silu(a) * b).astype(BF16) for a, b in up]
        y = [_dot(g, w2_ref[e]) for (k, e), g in zip(jobs, hid)]
        for k in range(MOE_PACK):
            meta = meta_ref[k * BLOCK:(k + 1) * BLOCK, :]
            _store_tiles(o_ref, k * tile_rows, SUBLANES,
                         meta[:, REC_WLO:REC_WLO + 1] * y[2 * k] + meta[:, REC_WHI:REC_WHI + 1] * y[2 * k + 1])

    @pl.when(u >= nsteps_ref[0])
    def _():
        o_ref[...] = jnp.zeros_like(o_ref)


def _moe(xs, meta, tables, w1, w3, w2, d):
    sgrp, blo, bhi, n_steps = tables
    max_steps = sgrp.shape[0]
    de = w1.shape[1]
    in_rows = MOE_PACK * BLOCK * SUBLANES
    w1g = w1.reshape(N_GROUPS, EPG, d, de)
    w3g = w3.reshape(N_GROUPS, EPG, d, de)
    w2g = w2.reshape(N_GROUPS, EPG, de, d)

    def w_spec(a):
        return pl.BlockSpec((None,) + a.shape[1:], lambda u, g, lo, hi, n_steps: (g[u], 0, 0, 0))

    grid_spec = pltpu.PrefetchScalarGridSpec(
        num_scalar_prefetch=4,
        grid=(max_steps,),
        in_specs=[
            pl.BlockSpec((in_rows, LANES), lambda u, g, lo, hi, n_steps: (jnp.minimum(u, n_steps[0] - 1), 0)),
            pl.BlockSpec((MOE_PACK * BLOCK, ROUTE_W), lambda u, g, lo, hi, n_steps: (jnp.minimum(u, n_steps[0] - 1), 0)),
            w_spec(w1g), w_spec(w3g), w_spec(w2g),
        ],
        out_specs=pl.BlockSpec((in_rows, LANES), lambda u, *_: (u, 0)),
    )
    return pl.pallas_call(
        _moe_body,
        out_shape=jax.ShapeDtypeStruct((max_steps * in_rows, LANES), F32),
        grid_spec=grid_spec,
        compiler_params=pltpu.CompilerParams(dimension_semantics=("arbitrary",), vmem_limit_bytes=MOE_VMEM_LIMIT_BYTES),
        name="moe",
    )(sgrp, blo, bhi, n_steps, xs, meta, w1g, w3g, w2g)


def _final_body(run_ref, cnt_ref, lst_ref, runn_ref, cntn_ref, lstn_ref, lpos_ref, x1_ref, g2_ref,
                lng_ref, lnb_ref, ms_hbm, o_ref, gbuf, tbuf, gsem):
    u = pl.program_id(0)
    slot = u % 2
    d = x1_ref.shape[-1]
    tokens = x1_ref.shape[0]
    chunks = tokens // BLOCK
    tile_rows = BLOCK * SUBLANES
    sort_rows = chunks * tile_rows

    def streams(s):
        return ((gbuf, s * sort_rows, ms_hbm, SUBLANES),)

    @pl.when(u == 0)
    def _():
        _start_runs(cnt_ref, lst_ref, run_ref, streams(0), gsem.at[0], False)

    _wait_runs(streams(slot), tokens, gsem.at[slot])
    _start_runs(cntn_ref, lstn_ref, runn_ref, streams(1 - slot), gsem.at[1 - slot], False)
    gain = g2_ref[...]
    ln_g = lng_ref[...]
    ln_b = lnb_ref[...]
    unroll = tbuf.shape[0] // tile_rows

    def finish_chunks(i, carry):
        for k in range(unroll):
            c = i * unroll + k
            stage = k * tile_rows
            for r in range(BLOCK):
                src = pl.ds(pl.multiple_of(slot * sort_rows + lpos_ref[c, 0, r] * SUBLANES, SUBLANES), SUBLANES)
                tbuf[stage + r * SUBLANES:stage + (r + 1) * SUBLANES, :] = gbuf[src, :]
            rows = pl.ds(pl.multiple_of(c * BLOCK, BLOCK), BLOCK)
            moe = _load_tiles(tbuf, stage, SUBLANES, d)
            o_ref[rows, :] = _ln(ALPHA * x1_ref[rows, :] + gain * moe) * ln_g + ln_b
        return carry

    lax.fori_loop(0, chunks // unroll, finish_chunks, 0)

    @pl.when(u == pl.num_programs(0) - 1)
    def _():
        _wait_runs(streams(1 - slot), tokens, gsem.at[1 - slot])


def _final(x1, ms, lpos, tile_tabs, mod3, ln_g, ln_b, seq, chunks):
    n, d = x1.shape
    n_tiles = n // (chunks * BLOCK)
    per_batch = seq // (chunks * BLOCK)
    tile_rows = BLOCK * SUBLANES
    vec = pl.BlockSpec((1, d), lambda u: (0, 0))
    return pl.pallas_call(
        _final_body,
        out_shape=jax.ShapeDtypeStruct((n, d), F32),
        grid=(n_tiles,),
        in_specs=[
            _tile_tab_spec(0, n_tiles), _tile_tab_spec(0, n_tiles), _tile_tab_spec(0, n_tiles),
            _tile_tab_spec(1, n_tiles), _tile_tab_spec(1, n_tiles), _tile_tab_spec(1, n_tiles),
            pl.BlockSpec((chunks, 1, LANES), lambda u: (u, 0, 0), memory_space=pltpu.SMEM),
            pl.BlockSpec((chunks * BLOCK, d), lambda u: (u, 0)),
            pl.BlockSpec((None, 1, d), lambda u: (u // per_batch, 0, 5)),
            vec, vec, pl.BlockSpec(memory_space=pl.ANY),
        ],
        out_specs=pl.BlockSpec((chunks * BLOCK, d), lambda u: (u, 0)),
        scratch_shapes=[
            pltpu.VMEM((2 * chunks * tile_rows, LANES), F32),
            pltpu.VMEM((_stage_chunks(chunks) * tile_rows, LANES), F32),
            pltpu.SemaphoreType.DMA((2,)),
        ],
        compiler_params=_params("arbitrary"),
        name="final",
    )(*tile_tabs, *tile_tabs, lpos, x1, mod3, ln_g.reshape(1, d), ln_b.reshape(1, d), ms)


def _block_tables(tab):
    cls = tab[:, 0]
    pair = cls % PAIRS
    first = jnp.array([i for i in range(EPG) for _ in range(i + 1, EPG)], I32)
    second = jnp.array([j for i in range(EPG) for j in range(i + 1, EPG)], I32)
    return (cls // PAIRS)[::MOE_PACK], first[pair], second[pair], tab[0, 1].reshape(1)


def kernel(x, c, ctx, c_ctx, w_ada, b_ada, w_in, attn_sink, gm_ln_g, gm_ln_b, gm_ws, gm_bs, w_pa, w_pb, w_o,
           ln1_g, ln1_b, router_g_w, router_g_b, router_e_w, router_e_b, moe_w1, moe_w3, moe_w2, ln2_g, ln2_b):
    b, s, d = x.shape
    assert w_ada.shape[0] == DEPTH and s % BLOCK == 0 and s % GRID_W == 0
    mod_rows = 2 * SUBLANES
    assert b + 1 <= mod_rows
    c_rows = jnp.zeros((mod_rows, d), F32).at[:b].set(c).at[b].set(c_ctx)
    mod = _adaln_mod(c_rows, w_ada[0], b_ada[0])
    mod3 = mod.reshape(mod_rows, 1, 6 * d)
    w_in_bf = w_in[0].astype(BF16)
    kv_ctx = _ctx_kv(ctx, mod3, b, w_in_bf)
    tile = next(t for t in (1024, 512, BLOCK) if s % t == 0)
    q, kv, u, vg, gates = _project(x, mod3, w_in_bf, _rope_tables(s), gm_ln_g[0], gm_ln_b[0], tile)
    nb = s // BLOCK
    experts, _, d_expert = moe_w1.shape[1:]
    moe_w = [moe_w1[0].reshape(experts * d, d_expert), moe_w3[0].reshape(experts * d, d_expert),
             moe_w2[0].reshape(experts * d_expert, d)]
    per = next(p for p in (8, 4, 2, 1) if nb % p == 0)
    steps = b * (nb // per)
    sliced = all(w.shape[0] % (steps * 2 * SUBLANES) == 0 for w in moe_w)
    y_attn, moe_w_bf = _attend(q, kv, kv_ctx, attn_sink[0], per, moe_w if sliced else [])
    if not sliced:
        moe_w_bf = [w.astype(BF16) for w in moe_w]
    bs_rows = jnp.repeat(gm_bs[0].T, GM_HEAD, axis=1)
    w_route = jnp.concatenate([router_g_w[0], jnp.transpose(router_e_w[0], (1, 0, 2)).reshape(d, N_EXPERTS)], axis=1)
    w_route = jnp.pad(w_route, ((0, 0), (0, ROUTE_W - w_route.shape[1])))
    b_route = jnp.pad(jnp.concatenate([router_g_b[0], router_e_b[0].reshape(-1)]), (0, ROUTE_W - N_GROUPS - N_EXPERTS))
    wr_hi = w_route.astype(BF16)
    wr_lo = (w_route - wr_hi.astype(F32)).astype(BF16)
    merge_tile = BLOCK * next(p for p in (8, 4, 2, 1) if nb % p == 0)
    x1, rec, counts = _merge(x, y_attn, u, vg, gates, gm_ws[0].astype(BF16), bs_rows, w_pa[0].astype(BF16),
                             w_pb[0].astype(BF16), w_o[0].astype(BF16), ln1_g[0], ln1_b[0], mod3, wr_hi, wr_lo,
                             b_route.reshape(1, ROUTE_W), merge_tile)
    n = b * s
    x1 = x1.reshape(n, d)
    rec = rec.reshape(n, ROUTE_W)
    max_blocks = n // BLOCK + N_CLASSES - 1 + (N_GROUPS - 1) * (MOE_PACK - 1)
    max_blocks = -(-max_blocks // MOE_PACK) * MOE_PACK
    sort_chunks = next(p for p in (SUBLANES, 4, 2, 1) if nb % p == 0)
    lpos, tab, cls_blk, *tile_tabs = _plan(counts.reshape(n // BLOCK, LANES), rec, max_blocks, sort_chunks)
    tile_tabs = [t.reshape(t.shape[0], 1, LANES) for t in tile_tabs]
    xs, meta = _scatter(x1, rec, mod3, lpos, tile_tabs, cls_blk, max_blocks, s, sort_chunks)
    ms = _moe(xs, meta, _block_tables(tab), *moe_w_bf, d)
    out = _final(x1, ms, lpos, tile_tabs, mod3, ln2_g[0], ln2_b[0], s, sort_chunks)
    return out.reshape(b, s, d)
```

```python
import functools

import jax
import jax.numpy as jnp
import numpy as np
from jax import lax
from jax.experimental import pallas as pl
from jax.experimental.pallas import tpu as pltpu

F32 = jnp.float32
BF16 = jnp.bfloat16
I32 = jnp.int32

GRID_W = 64
HEAD_DIM = 64
Q_HEADS = 8
KV_HEADS = 2
GROUP = Q_HEADS // KV_HEADS
BLOCK = 128
ROPE_BASE = 10000.0
ROPE_FREQS = HEAD_DIM // 4
GM_GROUPS = 8
GM_HEAD = 64
ATTN_W = Q_HEADS * HEAD_DIM
KV_W = KV_HEADS * HEAD_DIM
GM_W = GM_GROUPS * GM_HEAD
N_GROUPS = 4
EPG = 8
N_EXPERTS = N_GROUPS * EPG
PAIRS = EPG * (EPG - 1) // 2
N_CLASSES = N_GROUPS * PAIRS
LN_EPS = 1e-6
NEG_INF = -1e30
DEPTH = 1
ALPHA = (2.0 * DEPTH) ** 0.25

LANES = 128
SUBLANES = 8
VMEM_LIMIT_BYTES = 56 * 1024 * 1024
MOE_VMEM_LIMIT_BYTES = 60 * 1024 * 1024
MOE_PACK = 4
ROUTE_W = LANES
REC_WLO, REC_WHI, REC_CLS, REC_RANK = 0, 1, 2, 3


def _params(*sem):
    return pltpu.CompilerParams(dimension_semantics=sem, vmem_limit_bytes=VMEM_LIMIT_BYTES)


def _ln(x):
    mu = jnp.mean(x, -1, keepdims=True)
    xc = x - mu
    var = jnp.mean(xc * xc, -1, keepdims=True)
    return xc * lax.rsqrt(var + LN_EPS)


def _dot(a, b):
    return jnp.dot(a, b, preferred_element_type=F32)


def _dot_nt(a, b):
    return lax.dot_general(a, b, (((1,), (1,)), ((), ())), preferred_element_type=F32)


def _split(a):
    hi = a.astype(BF16)
    lo = (a - hi.astype(F32)).astype(BF16)
    return hi, lo


def _dot_split(a, b_hi, b_lo):
    a_hi, a_lo = _split(a)
    return _dot(a_hi, b_hi) + _dot(a_hi, b_lo) + _dot(a_lo, b_hi)


def _iota(shape, axis):
    return lax.broadcasted_iota(I32, shape, axis)


def _mod_body(c_ref, w_ref, b_ref, o_ref):
    w_hi, w_lo = _split(w_ref[...])
    o_ref[...] = _dot_split(jax.nn.silu(c_ref[...]), w_hi, w_lo) + b_ref[...]


def _adaln_mod(c_rows, w_ada, b_ada):
    rows, d = c_rows.shape
    width = w_ada.shape[1]
    tn = 6 * LANES
    return pl.pallas_call(
        _mod_body,
        out_shape=jax.ShapeDtypeStruct((rows, width), F32),
        grid=(width // tn,),
        in_specs=[
            pl.BlockSpec((rows, d), lambda j: (0, 0)),
            pl.BlockSpec((d, tn), lambda j: (0, j)),
            pl.BlockSpec((1, tn), lambda j: (0, j)),
        ],
        out_specs=pl.BlockSpec((rows, tn), lambda j: (0, j)),
        compiler_params=_params("arbitrary"),
        name="mod",
    )(c_rows, w_ada, b_ada.reshape(1, width))


def _ctx_body(ctx_ref, sh_ref, sc_ref, w_ref, o_ref):
    h = _ln(ctx_ref[...]) * (1.0 + sc_ref[...]) + sh_ref[...]
    o_ref[...] = _dot(h.astype(BF16), w_ref[...]).astype(BF16)


def _ctx_kv(ctx, mod3, ctx_row, w_in_bf):
    b, c_len, d = ctx.shape
    kvw = 2 * KV_W
    return pl.pallas_call(
        _ctx_body,
        out_shape=jax.ShapeDtypeStruct((b, c_len, kvw), BF16),
        grid=(b,),
        in_specs=[
            pl.BlockSpec((None, c_len, d), lambda i: (i, 0, 0)),
            pl.BlockSpec((None, 1, d), lambda i: (ctx_row, 0, 0)),
            pl.BlockSpec((None, 1, d), lambda i: (ctx_row, 0, 1)),
            pl.BlockSpec((d, kvw), lambda i: (0, ATTN_W // kvw)),
        ],
        out_specs=pl.BlockSpec((None, c_len, kvw), lambda i: (i, 0, 0)),
        compiler_params=_params("arbitrary"),
        name="ctx_kv",
    )(ctx, mod3, mod3, w_in_bf)


def _rope(x, cos, s_up, s_dn):
    return x * cos + pltpu.roll(x, LANES - ROPE_FREQS, 1) * s_up + pltpu.roll(x, ROPE_FREQS, 1) * s_dn


def _project_body(x_ref, sh_ref, sc_ref, w_ref, cos_ref, sup_ref, sdn_ref, lng_ref, lnb_ref,
                  q_ref, kv_ref, u_ref, vg_ref, gate_ref):
    h = (_ln(x_ref[...]) * (1.0 + sc_ref[...]) + sh_ref[...]).astype(BF16)
    cos, s_up, s_dn = cos_ref[...], sup_ref[...], sdn_ref[...]
    q_scale = HEAD_DIM ** -0.5
    k0 = ATTN_W
    u0 = k0 + 2 * KV_W
    vg0 = u0 + GM_W
    gate0 = vg0 + GM_W
    q = _dot(h, w_ref[:, 0:k0])
    for j in range(ATTN_W // LANES):
        sl = slice(j * LANES, (j + 1) * LANES)
        q_ref[:, sl] = (_rope(q[:, sl], cos, s_up, s_dn) * q_scale).astype(BF16)
    kv = _dot(h, w_ref[:, k0:u0])
    kv_ref[:, 0:KV_W] = _rope(kv[:, 0:KV_W], cos, s_up, s_dn).astype(BF16)
    kv_ref[:, KV_W:] = kv[:, KV_W:].astype(BF16)
    u_ref[...] = jax.nn.gelu(_dot(h, w_ref[:, u0:vg0])).astype(BF16)
    vg = _ln(jax.nn.gelu(_dot(h, w_ref[:, vg0:gate0]))) * lng_ref[...] + lnb_ref[...]
    vg_ref[...] = vg.astype(BF16)
    gate_ref[...] = jax.nn.sigmoid(_dot(h, w_ref[:, gate0:])).astype(BF16)


def _project(x, mod3, w_in_bf, rope_tabs, gm_ln_g, gm_ln_b, tile):
    b, s, d = x.shape
    in_w = w_in_bf.shape[1]
    gate_w = in_w - (ATTN_W + 2 * KV_W + 2 * GM_W)
    row = lambda width: pl.BlockSpec((None, tile, width), lambda i, t: (i, t, 0))
    tab = pl.BlockSpec((tile, LANES), lambda i, t: (t, 0))
    vec = lambda width: pl.BlockSpec((1, width), lambda i, t: (0, 0))
    return pl.pallas_call(
        _project_body,
        out_shape=[
            jax.ShapeDtypeStruct((b, s, ATTN_W), BF16),
            jax.ShapeDtypeStruct((b, s, 2 * KV_W), BF16),
            jax.ShapeDtypeStruct((b, s, GM_W), BF16),
            jax.ShapeDtypeStruct((b, s, GM_W), BF16),
            jax.ShapeDtypeStruct((b, s, gate_w), BF16),
        ],
        grid=(b, s // tile),
        in_specs=[
            row(d),
            pl.BlockSpec((None, 1, d), lambda i, t: (i, 0, 0)),
            pl.BlockSpec((None, 1, d), lambda i, t: (i, 0, 1)),
            pl.BlockSpec((d, in_w), lambda i, t: (0, 0)),
            tab, tab, tab,
            vec(GM_W), vec(GM_W),
        ],
        out_specs=[row(ATTN_W), row(2 * KV_W), row(GM_W), row(GM_W), row(gate_w)],
        compiler_params=_params("arbitrary", "arbitrary"),
        name="project",
    )(x, mod3, mod3, w_in_bf, *rope_tabs, gm_ln_g.reshape(1, GM_W), gm_ln_b.reshape(1, GM_W))


def _rope_tables(s):
    pos = np.arange(s)
    inv = 1.0 / (ROPE_BASE ** (np.arange(ROPE_FREQS, dtype=np.float64) / ROPE_FREQS))
    ang_r = (pos // GRID_W)[:, None] * inv[None, :]
    ang_c = (pos % GRID_W)[:, None] * inv[None, :]
    cr, sr, cc, sn = np.cos(ang_r), np.sin(ang_r), np.cos(ang_c), np.sin(ang_c)
    z = np.zeros_like(sr)
    reps = LANES // HEAD_DIM
    tabs = ([cr, cr, cc, cc], [-sr, z, -sn, z], [z, sr, z, sn])
    return tuple(jnp.asarray(np.tile(np.concatenate(t, -1), (1, reps)), F32) for t in tabs)


def _attend_body(sink_ref, q_ref, kvp_ref, kvc_ref, kvn_ref, ctx_ref, *rest):
    n_side = (len(rest) - 1) // 2
    side_in, o_ref, side_out = rest[:n_side], rest[n_side], rest[n_side + 1:]
    for src, dst in zip(side_in, side_out):
        dst[...] = src[...].astype(BF16)
    i = pl.program_id(1)
    n_steps = pl.num_programs(1)
    per = q_ref.shape[0] // BLOCK
    rows = GROUP * BLOCK
    r = _iota((rows, 3 * BLOCK), 0) & (BLOCK - 1)
    m = _iota((rows, 3 * BLOCK), 1)
    band = (m >= r) & (m <= r + 2 * BLOCK)
    grp = _iota((rows, 1), 0) // BLOCK
    kv_all = jnp.concatenate([kvp_ref[...], kvc_ref[...], kvn_ref[...]], axis=0)
    kv_ctx = ctx_ref[...]
    for t in range(per):
        mask = band
        if t == 0:
            mask = mask & ((m >= BLOCK) | (i > 0))
        if t == per - 1:
            mask = mask & ((m < 2 * BLOCK) | (i < n_steps - 1))
        q = q_ref[t * BLOCK:(t + 1) * BLOCK, :]
        kv_loc = kv_all[t * BLOCK:(t + 3) * BLOCK]
        scores = []
        for h in range(KV_HEADS):
            ks = slice(h * HEAD_DIM, (h + 1) * HEAD_DIM)
            qg = jnp.concatenate(
                [q[:, (h * GROUP + g) * HEAD_DIM:(h * GROUP + g + 1) * HEAD_DIM] for g in range(GROUP)], axis=0)
            scores.append((_dot_nt(qg, kv_ctx[:, ks]), jnp.where(mask, _dot_nt(qg, kv_loc[:, ks]), NEG_INF)))
        probs = []
        for h, (s_ctx, s_loc) in enumerate(scores):
            sink = jnp.zeros((rows, 1), F32)
            for g in range(GROUP):
                sink = jnp.where(grp == g, sink_ref[h * GROUP + g], sink)
            top = jnp.maximum(jnp.maximum(jnp.max(s_ctx, -1, keepdims=True), jnp.max(s_loc, -1, keepdims=True)), sink)
            e_ctx = jnp.exp(s_ctx - top)
            e_loc = jnp.exp(s_loc - top)
            den = jnp.sum(e_ctx, -1, keepdims=True) + jnp.sum(e_loc, -1, keepdims=True) + jnp.exp(sink - top)
            probs.append((e_ctx.astype(BF16), e_loc.astype(BF16), 1.0 / den))
        outs = []
        for h, (e_ctx, e_loc, inv) in enumerate(probs):
            vs = slice(KV_W + h * HEAD_DIM, KV_W + (h + 1) * HEAD_DIM)
            o = (_dot(e_ctx, kv_ctx[:, vs]) + _dot(e_loc, kv_loc[:, vs])) * inv
            outs += [o[g * BLOCK:(g + 1) * BLOCK] for g in range(GROUP)]
        o_ref[t * BLOCK:(t + 1) * BLOCK, :] = jnp.concatenate(outs, axis=1).astype(BF16)


def _attend(q, kv, kv_ctx, sink, per, side):
    b, s, _ = q.shape
    nb = s // BLOCK
    nt = nb // per
    c_len = kv_ctx.shape[1]
    kvw = 2 * KV_W
    side_specs = [pl.BlockSpec((a.shape[0] // (b * nt), a.shape[1]), lambda i, j: (i * nt + j, 0)) for a in side]
    out_spec = pl.BlockSpec((None, per * BLOCK, ATTN_W), lambda i, j: (i, j, 0))
    outs = pl.pallas_call(
        _attend_body,
        out_shape=[jax.ShapeDtypeStruct((b, s, ATTN_W), BF16)] + [jax.ShapeDtypeStruct(a.shape, BF16) for a in side],
        grid=(b, nt),
        in_specs=[
            pl.BlockSpec(memory_space=pltpu.SMEM),
            pl.BlockSpec((None, per * BLOCK, ATTN_W), lambda i, j: (i, j, 0)),
            pl.BlockSpec((None, BLOCK, kvw), lambda i, j: (i, jnp.maximum(j * per - 1, 0), 0)),
            pl.BlockSpec((None, per * BLOCK, kvw), lambda i, j: (i, j, 0)),
            pl.BlockSpec((None, BLOCK, kvw), lambda i, j: (i, jnp.minimum((j + 1) * per, nb - 1), 0)),
            pl.BlockSpec((None, c_len, kvw), lambda i, j: (i, 0, 0)),
        ] + side_specs,
        out_specs=[out_spec] + side_specs,
        compiler_params=_params("arbitrary", "arbitrary"),
        name="attend",
    )(sink, q, kv, kv, kv, kv_ctx, *side)
    return outs[0], outs[1:]


def _route(logits):
    lane = _iota(logits.shape, 1)
    far = jnp.int32(2 * ROUTE_W)

    def first_max(vals):
        top = jnp.max(vals, -1, keepdims=True)
        return top, jnp.min(jnp.where(vals == top, lane, far), -1, keepdims=True)

    is_g = lane < N_GROUPS
    g_top, g_idx = first_max(jnp.where(is_g, logits, NEG_INF))
    g_w = 1.0 / jnp.sum(jnp.where(is_g, jnp.exp(logits - g_top), 0.0), -1, keepdims=True)
    lo = N_GROUPS + EPG * g_idx
    e_vals = jnp.where((lane >= lo) & (lane < lo + EPG), logits, NEG_INF)
    v1, i1 = first_max(e_vals)
    v2, i2 = first_max(jnp.where(lane == i1, NEG_INF, e_vals))
    t = jnp.exp(v2 - v1)
    w1 = g_w * (1.0 / (1.0 + t))
    w2 = g_w * (t / (1.0 + t))
    swap = i2 < i1
    a = jnp.where(swap, i2, i1) - lo
    b = jnp.where(swap, i1, i2) - lo
    cls = g_idx * PAIRS + ((a * (2 * EPG - 1 - a)) >> 1) + (b - a - 1)
    return jnp.where(swap, w2, w1), jnp.where(swap, w1, w2), cls


def _merge_body(x_ref, ya_ref, u_ref, vg_ref, gate_ref, ws_ref, bs_ref, wpa_ref, wpb_ref, wo_ref,
                lng_ref, lnb_ref, g1_ref, sh2_ref, sc2_ref, wrh_ref, wrl_ref, br_ref, x1_ref, rec_ref, cnt_ref):
    tile, d = x_ref.shape
    lane = _iota((BLOCK, LANES), 1)
    below = (lane < _iota((BLOCK, LANES), 0)).astype(BF16)
    chunks = []
    for c in range(tile // BLOCK):
        rows = slice(c * BLOCK, (c + 1) * BLOCK)
        pieces = []
        for j in range(GM_W // LANES):
            v = vg_ref[rows, j * LANES:(j + 1) * LANES]
            per = LANES // GM_HEAD
            sp = _dot(ws_ref[per * j], v)
            for k in range(1, per):
                sp = jnp.where(lane < k * GM_HEAD, sp, _dot(ws_ref[per * j + k], v))
            pieces.append(sp)
        chunks.append(jnp.concatenate(pieces, axis=1) + bs_ref[...])
    sp = jnp.concatenate(chunks, axis=0)
    y_gm = (u_ref[...].astype(F32) * sp).astype(BF16)
    gate = gate_ref[...].astype(F32)
    y = gate[:, :d] * _dot(ya_ref[...], wpa_ref[...]) + gate[:, d:] * _dot(y_gm, wpb_ref[...])
    mix = _dot(y.astype(BF16), wo_ref[...])
    x1 = _ln(ALPHA * x_ref[...] + g1_ref[...] * mix) * lng_ref[...] + lnb_ref[...]
    x1_ref[...] = x1
    h2 = _ln(x1) * (1.0 + sc2_ref[...]) + sh2_ref[...]
    w_lo, w_hi, cls = _route(_dot_split(h2, wrh_ref[...], wrl_ref[...]) + br_ref[...])
    for c in range(tile // BLOCK):
        rows = slice(c * BLOCK, (c + 1) * BLOCK)
        onehot = lane == cls[rows]
        hot = onehot.astype(BF16)
        rank = jnp.sum(jnp.where(onehot, _dot(below, hot), 0.0), -1, keepdims=True)
        cnt_ref[c] = jnp.sum(hot.astype(F32), 0, keepdims=True)
        rec = jnp.where(lane == REC_WLO, w_lo[rows], 0.0)
        rec = jnp.where(lane == REC_WHI, w_hi[rows], rec)
        rec = jnp.where(lane == REC_CLS, cls[rows].astype(F32), rec)
        rec_ref[rows, :] = jnp.where(lane == REC_RANK, rank, rec)


def _merge(x, y_attn, u, vg, gates, ws_bf, bs_rows, wpa, wpb, wo, ln_g, ln_b, mod3, wr_hi, wr_lo, br, tile):
    b, s, d = x.shape
    nt = s // tile
    per = tile // BLOCK
    row = lambda width: pl.BlockSpec((None, tile, width), lambda i, j: (i, j, 0))
    full = lambda a: pl.BlockSpec(a.shape, lambda i, j: (0,) * a.ndim)
    modv = lambda k: pl.BlockSpec((None, 1, d), lambda i, j: (i, 0, k))
    ln_g = ln_g.reshape(1, d)
    ln_b = ln_b.reshape(1, d)
    return pl.pallas_call(
        _merge_body,
        out_shape=[jax.ShapeDtypeStruct((b, s, d), F32), jax.ShapeDtypeStruct((b, s, ROUTE_W), F32),
                   jax.ShapeDtypeStruct((b * s // BLOCK, 1, LANES), F32)],
        grid=(b, nt),
        in_specs=[
            row(d), row(ATTN_W), row(GM_W), row(GM_W), row(2 * d),
            full(ws_bf), full(bs_rows), full(wpa), full(wpb), full(wo), full(ln_g), full(ln_b),
            modv(2), modv(3), modv(4),
            full(wr_hi), full(wr_lo), full(br),
        ],
        out_specs=[row(d), row(ROUTE_W), pl.BlockSpec((per, 1, LANES), lambda i, j: (i * nt + j, 0, 0))],
        compiler_params=_params("arbitrary", "arbitrary"),
        name="merge",
    )(x, y_attn, u, vg, gates, ws_bf, bs_rows, wpa, wpb, wo, ln_g, ln_b, mod3, mod3, mod3, wr_hi, wr_lo, br)


def _plan_body(cnt_ref, rec_ref, lpos_ref, tab_ref, cblk_ref, run_ref, cntt_ref, lst_ref, base_ref, lstf_ref, *, chunks):
    step = pl.program_id(0)
    n_chunks = cnt_ref.shape[0]
    n_tiles = run_ref.shape[0]
    max_blocks = tab_ref.shape[0]

    @pl.when(step == 0)
    def _():
        cnt = cnt_ref[...]
        total = jnp.sum(cnt, 0, keepdims=True)
        n_blk = jnp.floor((total + (BLOCK - 1)) * (1.0 / BLOCK))
        before = (_iota((LANES, LANES), 0) < _iota((LANES, LANES), 1)).astype(BF16)
        nb_hi, nb_lo = _split(jnp.broadcast_to(n_blk, (SUBLANES, LANES)))
        blk_start = (_dot(nb_hi, before) + _dot(nb_lo, before))[0:1]
        lane1 = _iota((1, LANES), 1)
        packed = blk_start
        shift = jnp.zeros((1, 1), F32)
        for g in range(1, N_GROUPS):
            first = jnp.sum(jnp.where(lane1 == g * PAIRS, blk_start, 0.0), -1, keepdims=True) + shift
            over = first - MOE_PACK * jnp.floor(first * (1.0 / MOE_PACK))
            shift = shift + jnp.where(over > 0.0, MOE_PACK - over, 0.0)
            packed = jnp.where(lane1 >= g * PAIRS, blk_start + shift, packed)
        blk_start = packed
        earlier = (_iota((n_chunks, n_chunks), 1) < _iota((n_chunks, n_chunks), 0)).astype(BF16)
        base_ref[...] = blk_start * float(BLOCK) + _dot(earlier, cnt.astype(BF16))
        blk_end = blk_start + n_blk
        sub = _iota((SUBLANES, LANES), 0)
        cblk_ref[...] = jnp.where(sub == 0, blk_start, jnp.where(sub == 1, blk_end, 0.0)).astype(I32)
        lane = _iota((max_blocks, LANES), 1)
        blk = _iota((max_blocks, LANES), 0).astype(F32)
        is_cls = lane < N_CLASSES
        n_real = jnp.max(jnp.where(is_cls, blk_end, 0.0), -1, keepdims=True)
        u = jnp.minimum(blk, n_real - 1.0)
        cls_of = jnp.sum(jnp.where(is_cls & (blk_end <= u), 1.0, 0.0), -1, keepdims=True)
        n_steps = jnp.floor((n_real + (MOE_PACK - 1)) * (1.0 / MOE_PACK))
        tab = jnp.where(lane == 0, cls_of, jnp.where(lane == 1, n_steps, 0.0))
        tab_ref[...] = tab.astype(I32)
        tile = _iota((n_tiles, n_chunks), 0)
        chunk = _iota((n_tiles, n_chunks), 1)
        owner = ((chunk >= tile * chunks) & (chunk < (tile + 1) * chunks)).astype(BF16)
        cnt_tile = _dot(owner, cnt.astype(BF16))
        ct_hi, ct_lo = _split(cnt_tile)
        local_start = _dot(ct_hi, before) + _dot(ct_lo, before)
        run_ref[...] = base_ref[pl.ds(0, n_tiles, stride=chunks), :].astype(I32)
        cntt_ref[...] = cnt_tile.astype(I32)
        lst_ref[...] = local_start.astype(I32)
        lstf_ref[...] = local_start

    lane = _iota((BLOCK, LANES), 1)
    tile_off = lstf_ref[pl.ds(step, 1), :] - base_ref[pl.ds(step * chunks, 1), :]
    pick_rank = (_iota((2 * SUBLANES, LANES), 1) == REC_RANK).astype(BF16)
    for c in range(chunks):
        rec = rec_ref[c * BLOCK:(c + 1) * BLOCK, :]
        hot = (lane == rec[:, REC_CLS:REC_CLS + 1].astype(I32)).astype(BF16)
        base = base_ref[pl.ds(step * chunks + c, 1), :] + tile_off
        b_hi, b_lo = _split(jnp.broadcast_to(base, (2 * SUBLANES, LANES)))
        lpos = _dot_nt(b_hi, hot) + _dot_nt(b_lo, hot) + _dot_nt(pick_rank, rec.astype(BF16))
        lpos_ref[c] = lpos[0:1].astype(I32)


def _plan(counts, rec, max_blocks, chunks):
    n = rec.shape[0]
    n_chunks = n // BLOCK
    n_tiles = n_chunks // chunks
    tile_tab = jax.ShapeDtypeStruct((n_tiles, LANES), I32)
    whole = lambda rows: pl.BlockSpec((rows, LANES), lambda i: (0, 0))
    return pl.pallas_call(
        functools.partial(_plan_body, chunks=chunks),
        out_shape=[jax.ShapeDtypeStruct((n_chunks, 1, LANES), I32), jax.ShapeDtypeStruct((max_blocks, LANES), I32),
                   jax.ShapeDtypeStruct((SUBLANES, LANES), I32), tile_tab, tile_tab, tile_tab],
        grid=(n_tiles,),
        in_specs=[whole(n_chunks), pl.BlockSpec((chunks * BLOCK, ROUTE_W), lambda i: (i, 0))],
        out_specs=[pl.BlockSpec((chunks, 1, LANES), lambda i: (i, 0, 0)), whole(max_blocks), whole(SUBLANES),
                   whole(n_tiles), whole(n_tiles), whole(n_tiles)],
        scratch_shapes=[pltpu.VMEM((n_chunks, LANES), F32), pltpu.VMEM((n_tiles, LANES), F32)],
        compiler_params=_params("arbitrary"),
        name="plan",
    )(counts, rec)


def _tile_rows(base, j, pitch):
    return pl.ds(base + j, BLOCK, stride=pitch)


def _load_tiles(ref, base, pitch, width):
    return jnp.concatenate([ref[_tile_rows(base, j, pitch), :] for j in range(width // LANES)], axis=1)


def _store_tiles(ref, base, pitch, val):
    for j in range(val.shape[1] // LANES):
        ref[_tile_rows(base, j, pitch), :] = val[:, j * LANES:(j + 1) * LANES]


def _start_runs(cnt_ref, lst_ref, run_ref, streams, sem, to_hbm):
    def issue(c):
        for buf, base, hbm, pitch in streams:
            rows = cnt_ref[0, c] * pitch
            aligned = (lambda row: pl.multiple_of(row, pitch)) if pitch > 1 else (lambda row: row)
            v = buf.at[pl.ds(aligned(base + lst_ref[0, c] * pitch), rows)]
            h = hbm.at[pl.ds(aligned(run_ref[0, c] * pitch), rows)]
            (pltpu.make_async_copy(v, h, sem) if to_hbm else pltpu.make_async_copy(h, v, sem)).start(priority=c % 2)

    for c in range(N_CLASSES):
        pl.when(cnt_ref[0, c] > 0)(functools.partial(issue, c))


def _wait_runs(streams, tokens, sem):
    for buf, base, hbm, pitch in streams:
        rows = tokens * pitch
        pltpu.make_async_copy(hbm.at[pl.ds(0, rows)], buf.at[pl.ds(base, rows)], sem).wait()


def _scatter_body(cls_blk_ref, run_ref, cnt_ref, lst_ref, lpos_ref, x1_ref, rec_ref, sh_ref, sc_ref, xs_hbm, meta_hbm,
                  abuf, bbuf, mbuf, zbuf, ssem, zsem):
    u = pl.program_id(0)
    slot = u % 2
    tokens = x1_ref.shape[0]
    chunks = tokens // BLOCK
    tile_rows = BLOCK * SUBLANES
    sort_rows = chunks * tile_rows

    def streams(s):
        return ((bbuf, s * sort_rows, xs_hbm, SUBLANES), (mbuf, s * tokens, meta_hbm, 1))

    @pl.when(u == 0)
    def _():
        zbuf[...] = jnp.zeros_like(zbuf)

        def zero_block(blk, start):
            for hbm, rows in ((xs_hbm, tile_rows), (meta_hbm, BLOCK)):
                cp = pltpu.make_async_copy(zbuf.at[pl.ds(0, rows)], hbm.at[pl.ds(pl.multiple_of(blk * rows, rows), rows)],
                                           zsem)
                cp.start() if start else cp.wait()

        for start in (True, False):
            for c in range(N_CLASSES):
                pl.when(cls_blk_ref[1, c] > cls_blk_ref[0, c])(functools.partial(zero_block, cls_blk_ref[1, c] - 1, start))
        gaps = [(cls_blk_ref[1, g * PAIRS - 1], cls_blk_ref[0, g * PAIRS]) for g in range(1, N_GROUPS)]
        gaps.append((cls_blk_ref[1, N_CLASSES - 1], xs_hbm.shape[0] // tile_rows))
        for start in (True, False):
            for lo, hi in gaps:
                lax.fori_loop(lo, hi, lambda blk, carry: (zero_block(blk, start), carry)[1], 0)

    @pl.when(u >= 2)
    def _():
        _wait_runs(streams(slot), tokens, ssem.at[slot])

    scale = 1.0 + sc_ref[...]
    shift = sh_ref[...]

    unroll = abuf.shape[0] // tile_rows

    def sort_chunks(i, carry):
        for k in range(unroll):
            c = i * unroll + k
            stage = k * tile_rows
            rows = pl.ds(pl.multiple_of(c * BLOCK, BLOCK), BLOCK)
            _store_tiles(abuf, stage, SUBLANES, _ln(x1_ref[rows, :]) * scale + shift)
            rec = rec_ref[rows, :]
            for r in range(BLOCK):
                p = lpos_ref[c, 0, r]
                dst = pl.multiple_of(slot * sort_rows + p * SUBLANES, SUBLANES)
                bbuf[pl.ds(dst, SUBLANES), :] = abuf[stage + r * SUBLANES:stage + (r + 1) * SUBLANES, :]
                mbuf[pl.ds(slot * tokens + p, 1), :] = rec[r:r + 1, :]
        return carry

    lax.fori_loop(0, chunks // unroll, sort_chunks, 0)
    _start_runs(cnt_ref, lst_ref, run_ref, streams(slot), ssem.at[slot], True)

    @pl.when(u == pl.num_programs(0) - 1)
    def _():
        _wait_runs(streams(slot), tokens, ssem.at[slot])

        @pl.when(u >= 1)
        def _():
            _wait_runs(streams(1 - slot), tokens, ssem.at[1 - slot])


def _stage_chunks(chunks):
    return next(p for p in (4, 2, 1) if chunks % p == 0)


def _tile_tab_spec(off, n_tiles):
    return pl.BlockSpec((None, 1, LANES), lambda u: (jnp.minimum(u + off, n_tiles - 1), 0, 0),
                        memory_space=pltpu.SMEM)


def _scatter(x1, rec, mod3, lpos, tile_tabs, cls_blk, max_blocks, seq, chunks):
    n, d = x1.shape
    n_tiles = n // (chunks * BLOCK)
    per_batch = seq // (chunks * BLOCK)
    tile_rows = BLOCK * SUBLANES
    hbm = pl.BlockSpec(memory_space=pl.ANY)
    return pl.pallas_call(
        _scatter_body,
        out_shape=[jax.ShapeDtypeStruct((max_blocks * tile_rows, LANES), F32),
                   jax.ShapeDtypeStruct((max_blocks * BLOCK, ROUTE_W), F32)],
        grid=(n_tiles,),
        in_specs=[
            pl.BlockSpec(memory_space=pltpu.SMEM),
            _tile_tab_spec(0, n_tiles), _tile_tab_spec(0, n_tiles), _tile_tab_spec(0, n_tiles),
            pl.BlockSpec((chunks, 1, LANES), lambda u: (u, 0, 0), memory_space=pltpu.SMEM),
            pl.BlockSpec((chunks * BLOCK, d), lambda u: (u, 0)),
            pl.BlockSpec((chunks * BLOCK, ROUTE_W), lambda u: (u, 0)),
            pl.BlockSpec((None, 1, d), lambda u: (u // per_batch, 0, 3)),
            pl.BlockSpec((None, 1, d), lambda u: (u // per_batch, 0, 4)),
        ],
        out_specs=[hbm, hbm],
        scratch_shapes=[
            pltpu.VMEM((_stage_chunks(chunks) * tile_rows, LANES), F32),
            pltpu.VMEM((2 * chunks * tile_rows, LANES), F32),
            pltpu.VMEM((2 * chunks * BLOCK, ROUTE_W), F32),
            pltpu.VMEM((tile_rows, LANES), F32),
            pltpu.SemaphoreType.DMA((2,)),
            pltpu.SemaphoreType.DMA,
        ],
        compiler_params=_params("arbitrary"),
        name="scatter",
    )(cls_blk, *tile_tabs, lpos, x1, rec, mod3, mod3)


def _expert(h, w1, w3, w2):
    hid = jax.nn.silu(_dot(h, w1)) * _dot(h, w3)
    return _dot(hid, w2)


def _moe_body(sgrp_ref, blo_ref, bhi_ref, nsteps_ref, xs_ref, meta_ref, w1_ref, w3_ref, w2_ref, o_ref):
    u = pl.program_id(0)
    d = w1_ref.shape[1]
    tile_rows = BLOCK * SUBLANES

    @pl.when(u < nsteps_ref[0])
    def _():
        jobs = [(k, e) for k in range(MOE_PACK) for e in (blo_ref[MOE_PACK * u + k], bhi_ref[MOE_PACK * u + k])]
        h = [_load_tiles(xs_ref, k * tile_rows, SUBLANES, d).astype(BF16) for k in range(MOE_PACK)]
        up = [(_dot(h[k], w1_ref[e]), _dot(h[k], w3_ref[e])) for k, e in jobs]
        hid = [(jax.nn.silu(a) * b).astype(BF16) for a, b in up]
        y = [_dot(g, w2_ref[e]) for (k, e), g in zip(jobs, hid)]
        for k in range(MOE_PACK):
            meta = meta_ref[k * BLOCK:(k + 1) * BLOCK, :]
            _store_tiles(o_ref, k * tile_rows, SUBLANES,
                         meta[:, REC_WLO:REC_WLO + 1] * y[2 * k] + meta[:, REC_WHI:REC_WHI + 1] * y[2 * k + 1])

    @pl.when(u >= nsteps_ref[0])
    def _():
        o_ref[...] = jnp.zeros_like(o_ref)


def _moe(xs, meta, tables, w1, w3, w2, d):
    sgrp, blo, bhi, n_steps = tables
    max_steps = sgrp.shape[0]
    de = w1.shape[1]
    in_rows = MOE_PACK * BLOCK * SUBLANES
    w1g = w1.reshape(N_GROUPS, EPG, d, de)
    w3g = w3.reshape(N_GROUPS, EPG, d, de)
    w2g = w2.reshape(N_GROUPS, EPG, de, d)

    def w_spec(a):
        return pl.BlockSpec((None,) + a.shape[1:], lambda u, g, lo, hi, n_steps: (g[u], 0, 0, 0))

    grid_spec = pltpu.PrefetchScalarGridSpec(
        num_scalar_prefetch=4,
        grid=(max_steps,),
        in_specs=[
            pl.BlockSpec((in_rows, LANES), lambda u, g, lo, hi, n_steps: (jnp.minimum(u, n_steps[0] - 1), 0)),
            pl.BlockSpec((MOE_PACK * BLOCK, ROUTE_W), lambda u, g, lo, hi, n_steps: (jnp.minimum(u, n_steps[0] - 1), 0)),
            w_spec(w1g), w_spec(w3g), w_spec(w2g),
        ],
        out_specs=pl.BlockSpec((in_rows, LANES), lambda u, *_: (u, 0)),
    )
    return pl.pallas_call(
        _moe_body,
        out_shape=jax.ShapeDtypeStruct((max_steps * in_rows, LANES), F32),
        grid_spec=grid_spec,
        compiler_params=pltpu.CompilerParams(dimension_semantics=("arbitrary",), vmem_limit_bytes=MOE_VMEM_LIMIT_BYTES),
        name="moe",
    )(sgrp, blo, bhi, n_steps, xs, meta, w1g, w3g, w2g)


def _final_body(run_ref, cnt_ref, lst_ref, runn_ref, cntn_ref, lstn_ref, lpos_ref, x1_ref, g2_ref,
                lng_ref, lnb_ref, ms_hbm, o_ref, gbuf, tbuf, gsem):
    u = pl.program_id(0)
    slot = u % 2
    d = x1_ref.shape[-1]
    tokens = x1_ref.shape[0]
    chunks = tokens // BLOCK
    tile_rows = BLOCK * SUBLANES
    sort_rows = chunks * tile_rows

    def streams(s):
        return ((gbuf, s * sort_rows, ms_hbm, SUBLANES),)

    @pl.when(u == 0)
    def _():
        _start_runs(cnt_ref, lst_ref, run_ref, streams(0), gsem.at[0], False)

    _wait_runs(streams(slot), tokens, gsem.at[slot])
    _start_runs(cntn_ref, lstn_ref, runn_ref, streams(1 - slot), gsem.at[1 - slot], False)
    gain = g2_ref[...]
    ln_g = lng_ref[...]
    ln_b = lnb_ref[...]
    unroll = tbuf.shape[0] // tile_rows

    def finish_chunks(i, carry):
        for k in range(unroll):
            c = i * unroll + k
            stage = k * tile_rows
            for r in range(BLOCK):
                src = pl.ds(pl.multiple_of(slot * sort_rows + lpos_ref[c, 0, r] * SUBLANES, SUBLANES), SUBLANES)
                tbuf[stage + r * SUBLANES:stage + (r + 1) * SUBLANES, :] = gbuf[src, :]
            rows = pl.ds(pl.multiple_of(c * BLOCK, BLOCK), BLOCK)
            moe = _load_tiles(tbuf, stage, SUBLANES, d)
            o_ref[rows, :] = _ln(ALPHA * x1_ref[rows, :] + gain * moe) * ln_g + ln_b
        return carry

    lax.fori_loop(0, chunks // unroll, finish_chunks, 0)

    @pl.when(u == pl.num_programs(0) - 1)
    def _():
        _wait_runs(streams(1 - slot), tokens, gsem.at[1 - slot])


def _final(x1, ms, lpos, tile_tabs, mod3, ln_g, ln_b, seq, chunks):
    n, d = x1.shape
    n_tiles = n // (chunks * BLOCK)
    per_batch = seq // (chunks * BLOCK)
    tile_rows = BLOCK * SUBLANES
    vec = pl.BlockSpec((1, d), lambda u: (0, 0))
    return pl.pallas_call(
        _final_body,
        out_shape=jax.ShapeDtypeStruct((n, d), F32),
        grid=(n_tiles,),
        in_specs=[
            _tile_tab_spec(0, n_tiles), _tile_tab_spec(0, n_tiles), _tile_tab_spec(0, n_tiles),
            _tile_tab_spec(1, n_tiles), _tile_tab_spec(1, n_tiles), _tile_tab_spec(1, n_tiles),
            pl.BlockSpec((chunks, 1, LANES), lambda u: (u, 0, 0), memory_space=pltpu.SMEM),
            pl.BlockSpec((chunks * BLOCK, d), lambda u: (u, 0)),
            pl.BlockSpec((None, 1, d), lambda u: (u // per_batch, 0, 5)),
            vec, vec, pl.BlockSpec(memory_space=pl.ANY),
        ],
        out_specs=pl.BlockSpec((chunks * BLOCK, d), lambda u: (u, 0)),
        scratch_shapes=[
            pltpu.VMEM((2 * chunks * tile_rows, LANES), F32),
            pltpu.VMEM((_stage_chunks(chunks) * tile_rows, LANES), F32),
            pltpu.SemaphoreType.DMA((2,)),
        ],
        compiler_params=_params("arbitrary"),
        name="final",
    )(*tile_tabs, *tile_tabs, lpos, x1, mod3, ln_g.reshape(1, d), ln_b.reshape(1, d), ms)


def _block_tables(tab):
    cls = tab[:, 0]
    pair = cls % PAIRS
    first = jnp.array([i for i in range(EPG) for _ in range(i + 1, EPG)], I32)
    second = jnp.array([j for i in range(EPG) for j in range(i + 1, EPG)], I32)
    return (cls // PAIRS)[::MOE_PACK], first[pair], second[pair], tab[0, 1].reshape(1)


def kernel(x, c, ctx, c_ctx, w_ada, b_ada, w_in, attn_sink, gm_ln_g, gm_ln_b, gm_ws, gm_bs, w_pa, w_pb, w_o,
           ln1_g, ln1_b, router_g_w, router_g_b, router_e_w, router_e_b, moe_w1, moe_w3, moe_w2, ln2_g, ln2_b):
    b, s, d = x.shape
    assert w_ada.shape[0] == DEPTH and s % BLOCK == 0 and s % GRID_W == 0
    mod_rows = 2 * SUBLANES
    assert b + 1 <= mod_rows
    c_rows = jnp.concatenate([c, c_ctx[None, :], jnp.zeros((mod_rows - b - 1, d), F32)], axis=0)
    mod = _adaln_mod(c_rows, w_ada[0], b_ada[0])
    mod3 = mod.reshape(mod_rows, 1, 6 * d)
    w_in_bf = w_in[0].astype(BF16)
    kv_ctx = _ctx_kv(ctx, mod3, b, w_in_bf)
    tile = next(t for t in (1024, 512, BLOCK) if s % t == 0)
    q, kv, u, vg, gates = _project(x, mod3, w_in_bf, _rope_tables(s), gm_ln_g[0], gm_ln_b[0], tile)
    nb = s // BLOCK
    experts, _, d_expert = moe_w1.shape[1:]
    moe_w = [moe_w1[0].reshape(experts * d, d_expert), moe_w3[0].reshape(experts * d, d_expert),
             moe_w2[0].reshape(experts * d_expert, d)]
    per = next(p for p in (8, 4, 2, 1) if nb % p == 0)
    steps = b * (nb // per)
    sliced = all(w.shape[0] % (steps * 2 * SUBLANES) == 0 for w in moe_w)
    y_attn, moe_w_bf = _attend(q, kv, kv_ctx, attn_sink[0], per, moe_w if sliced else [])
    if not sliced:
        moe_w_bf = [w.astype(BF16) for w in moe_w]
    bs_rows = jnp.repeat(gm_bs[0].T, GM_HEAD, axis=1)
    w_route = jnp.concatenate([router_g_w[0], jnp.transpose(router_e_w[0], (1, 0, 2)).reshape(d, N_EXPERTS)], axis=1)
    w_route = jnp.pad(w_route, ((0, 0), (0, ROUTE_W - w_route.shape[1])))
    b_route = jnp.pad(jnp.concatenate([router_g_b[0], router_e_b[0].reshape(-1)]), (0, ROUTE_W - N_GROUPS - N_EXPERTS))
    wr_hi = w_route.astype(BF16)
    wr_lo = (w_route - wr_hi.astype(F32)).astype(BF16)
    merge_tile = BLOCK * next(p for p in (8, 4, 2, 1) if nb % p == 0)
    x1, rec, counts = _merge(x, y_attn, u, vg, gates, gm_ws[0].astype(BF16), bs_rows, w_pa[0].astype(BF16),
                             w_pb[0].astype(BF16), w_o[0].astype(BF16), ln1_g[0], ln1_b[0], mod3, wr_hi, wr_lo,
                             b_route.reshape(1, ROUTE_W), merge_tile)
    n = b * s
    x1 = x1.reshape(n, d)
    rec = rec.reshape(n, ROUTE_W)
    max_blocks = n // BLOCK + N_CLASSES - 1 + (N_GROUPS - 1) * (MOE_PACK - 1)
    max_blocks = -(-max_blocks // MOE_PACK) * MOE_PACK
    sort_chunks = next(p for p in (SUBLANES, 4, 2, 1) if nb % p == 0)
    lpos, tab, cls_blk, *tile_tabs = _plan(counts.reshape(n // BLOCK, LANES), rec, max_blocks, sort_chunks)
    tile_tabs = [t.reshape(t.shape[0], 1, LANES) for t in tile_tabs]
    xs, meta = _scatter(x1, rec, mod3, lpos, tile_tabs, cls_blk, max_blocks, s, sort_chunks)
    ms = _moe(xs, meta, _block_tables(tab), *moe_w_bf, d)
    out = _final(x1, ms, lpos, tile_tabs, mod3, ln2_g[0], ln2_b[0], s, sort_chunks)
    return out.reshape(b, s, d)
```

```python
import functools

import jax
import jax.numpy as jnp
import numpy as np
from jax import lax
from jax.experimental import pallas as pl
from jax.experimental.pallas import tpu as pltpu

F32 = jnp.float32
BF16 = jnp.bfloat16
I32 = jnp.int32

GRID_W = 64
HEAD_DIM = 64
Q_HEADS = 8
KV_HEADS = 2
GROUP = Q_HEADS // KV_HEADS
BLOCK = 128
ROPE_BASE = 10000.0
ROPE_FREQS = HEAD_DIM // 4
GM_GROUPS = 8
GM_HEAD = 64
ATTN_W = Q_HEADS * HEAD_DIM
KV_W = KV_HEADS * HEAD_DIM
GM_W = GM_GROUPS * GM_HEAD
N_GROUPS = 4
EPG = 8
N_EXPERTS = N_GROUPS * EPG
PAIRS = EPG * (EPG - 1) // 2
N_CLASSES = N_GROUPS * PAIRS
LN_EPS = 1e-6
NEG_INF = -1e30
DEPTH = 1
ALPHA = (2.0 * DEPTH) ** 0.25
LOG2_E = 1.4426950408889634

LANES = 128
SUBLANES = 8
VMEM_LIMIT_BYTES = 56 * 1024 * 1024
MOE_VMEM_LIMIT_BYTES = 60 * 1024 * 1024
MOE_PACK = 4
ROUTE_W = LANES
REC_WLO, REC_WHI, REC_CLS, REC_RANK = 0, 1, 2, 3


def _params(*sem):
    return pltpu.CompilerParams(dimension_semantics=sem, vmem_limit_bytes=VMEM_LIMIT_BYTES)


def _ln(x):
    mu = jnp.mean(x, -1, keepdims=True)
    xc = x - mu
    var = jnp.mean(xc * xc, -1, keepdims=True)
    return xc * lax.rsqrt(var + LN_EPS)


def _dot(a, b):
    return jnp.dot(a, b, preferred_element_type=F32)


def _dot_nt(a, b):
    return lax.dot_general(a, b, (((1,), (1,)), ((), ())), preferred_element_type=F32)


def _split(a):
    hi = a.astype(BF16)
    lo = (a - hi.astype(F32)).astype(BF16)
    return hi, lo


def _dot_split(a, b_hi, b_lo):
    a_hi, a_lo = _split(a)
    return _dot(a_hi, b_hi) + _dot(a_hi, b_lo) + _dot(a_lo, b_hi)


def _iota(shape, axis):
    return lax.broadcasted_iota(I32, shape, axis)


def _mod_body(c_ref, w_ref, b_ref, o_ref):
    w_hi, w_lo = _split(w_ref[...])
    o_ref[...] = _dot_split(jax.nn.silu(c_ref[...]), w_hi, w_lo) + b_ref[...]


def _adaln_mod(c_rows, w_ada, b_ada):
    rows, d = c_rows.shape
    width = w_ada.shape[1]
    tn = 6 * LANES
    return pl.pallas_call(
        _mod_body,
        out_shape=jax.ShapeDtypeStruct((rows, width), F32),
        grid=(width // tn,),
        in_specs=[
            pl.BlockSpec((rows, d), lambda j: (0, 0)),
            pl.BlockSpec((d, tn), lambda j: (0, j)),
            pl.BlockSpec((1, tn), lambda j: (0, j)),
        ],
        out_specs=pl.BlockSpec((rows, tn), lambda j: (0, j)),
        compiler_params=_params("arbitrary"),
        name="mod",
    )(c_rows, w_ada, b_ada.reshape(1, width))


def _ctx_body(ctx_ref, sh_ref, sc_ref, w_ref, o_ref):
    h = _ln(ctx_ref[...]) * (1.0 + sc_ref[...]) + sh_ref[...]
    o_ref[...] = _dot(h.astype(BF16), w_ref[...]).astype(BF16)


def _ctx_kv(ctx, mod3, ctx_row, w_in_bf):
    b, c_len, d = ctx.shape
    kvw = 2 * KV_W
    return pl.pallas_call(
        _ctx_body,
        out_shape=jax.ShapeDtypeStruct((b, c_len, kvw), BF16),
        grid=(b,),
        in_specs=[
            pl.BlockSpec((None, c_len, d), lambda i: (i, 0, 0)),
            pl.BlockSpec((None, 1, d), lambda i: (ctx_row, 0, 0)),
            pl.BlockSpec((None, 1, d), lambda i: (ctx_row, 0, 1)),
            pl.BlockSpec((d, kvw), lambda i: (0, ATTN_W // kvw)),
        ],
        out_specs=pl.BlockSpec((None, c_len, kvw), lambda i: (i, 0, 0)),
        compiler_params=_params("arbitrary"),
        name="ctx_kv",
    )(ctx, mod3, mod3, w_in_bf)


def _rope(x, cos, s_up, s_dn):
    return x * cos + pltpu.roll(x, LANES - ROPE_FREQS, 1) * s_up + pltpu.roll(x, ROPE_FREQS, 1) * s_dn


def _project_body(x_ref, sh_ref, sc_ref, w_ref, cos_ref, sup_ref, sdn_ref, lng_ref, lnb_ref,
                  q_ref, kv_ref, u_ref, vg_ref, gate_ref):
    h = (_ln(x_ref[...]) * (1.0 + sc_ref[...]) + sh_ref[...]).astype(BF16)
    cos, s_up, s_dn = cos_ref[...], sup_ref[...], sdn_ref[...]
    q_scale = HEAD_DIM ** -0.5 * LOG2_E
    k0 = ATTN_W
    u0 = k0 + 2 * KV_W
    vg0 = u0 + GM_W
    gate0 = vg0 + GM_W
    q = _dot(h, w_ref[:, 0:k0])
    for j in range(ATTN_W // LANES):
        sl = slice(j * LANES, (j + 1) * LANES)
        q_ref[:, sl] = (_rope(q[:, sl], cos, s_up, s_dn) * q_scale).astype(BF16)
    kv = _dot(h, w_ref[:, k0:u0])
    kv_ref[:, 0:KV_W] = _rope(kv[:, 0:KV_W], cos, s_up, s_dn).astype(BF16)
    kv_ref[:, KV_W:] = kv[:, KV_W:].astype(BF16)
    u_ref[...] = jax.nn.gelu(_dot(h, w_ref[:, u0:vg0])).astype(BF16)
    vg = _ln(jax.nn.gelu(_dot(h, w_ref[:, vg0:gate0]))) * lng_ref[...] + lnb_ref[...]
    vg_ref[...] = vg.astype(BF16)
    gate_ref[...] = jax.nn.sigmoid(_dot(h, w_ref[:, gate0:])).astype(BF16)


def _project(x, mod3, w_in_bf, rope_tabs, gm_ln_g, gm_ln_b, tile):
    b, s, d = x.shape
    in_w = w_in_bf.shape[1]
    gate_w = in_w - (ATTN_W + 2 * KV_W + 2 * GM_W)
    row = lambda width: pl.BlockSpec((None, tile, width), lambda i, t: (i, t, 0))
    tab = pl.BlockSpec((tile, LANES), lambda i, t: (t, 0))
    vec = lambda width: pl.BlockSpec((1, width), lambda i, t: (0, 0))
    return pl.pallas_call(
        _project_body,
        out_shape=[
            jax.ShapeDtypeStruct((b, s, ATTN_W), BF16),
            jax.ShapeDtypeStruct((b, s, 2 * KV_W), BF16),
            jax.ShapeDtypeStruct((b, s, GM_W), BF16),
            jax.ShapeDtypeStruct((b, s, GM_W), BF16),
            jax.ShapeDtypeStruct((b, s, gate_w), BF16),
        ],
        grid=(b, s // tile),
        in_specs=[
            row(d),
            pl.BlockSpec((None, 1, d), lambda i, t: (i, 0, 0)),
            pl.BlockSpec((None, 1, d), lambda i, t: (i, 0, 1)),
            pl.BlockSpec((d, in_w), lambda i, t: (0, 0)),
            tab, tab, tab,
            vec(GM_W), vec(GM_W),
        ],
        out_specs=[row(ATTN_W), row(2 * KV_W), row(GM_W), row(GM_W), row(gate_w)],
        compiler_params=_params("arbitrary", "arbitrary"),
        name="project",
    )(x, mod3, mod3, w_in_bf, *rope_tabs, gm_ln_g.reshape(1, GM_W), gm_ln_b.reshape(1, GM_W))


def _rope_tables(s):
    pos = np.arange(s)
    inv = 1.0 / (ROPE_BASE ** (np.arange(ROPE_FREQS, dtype=np.float64) / ROPE_FREQS))
    ang_r = (pos // GRID_W)[:, None] * inv[None, :]
    ang_c = (pos % GRID_W)[:, None] * inv[None, :]
    cr, sr, cc, sn = np.cos(ang_r), np.sin(ang_r), np.cos(ang_c), np.sin(ang_c)
    z = np.zeros_like(sr)
    reps = LANES // HEAD_DIM
    tabs = ([cr, cr, cc, cc], [-sr, z, -sn, z], [z, sr, z, sn])
    return tuple(jnp.asarray(np.tile(np.concatenate(t, -1), (1, reps)), F32) for t in tabs)


def _attend_body(sink_ref, q_ref, kvp_ref, kvc_ref, kvn_ref, ctx_ref, *rest):
    n_side = (len(rest) - 1) // 2
    side_in, o_ref, side_out = rest[:n_side], rest[n_side], rest[n_side + 1:]
    for src, dst in zip(side_in, side_out):
        dst[...] = src[...].astype(BF16)
    i = pl.program_id(1)
    n_steps = pl.num_programs(1)
    per = q_ref.shape[0] // BLOCK
    rows = GROUP * BLOCK
    r = _iota((rows, 3 * BLOCK), 0) & (BLOCK - 1)
    m = _iota((rows, 3 * BLOCK), 1)
    band = (m >= r) & (m <= r + 2 * BLOCK)
    grp = _iota((rows, 1), 0) // BLOCK
    kv_all = jnp.concatenate([kvp_ref[...], kvc_ref[...], kvn_ref[...]], axis=0)
    kv_ctx = ctx_ref[...]
    for t in range(per):
        mask = band
        if t == 0:
            mask = mask & ((m >= BLOCK) | (i > 0))
        if t == per - 1:
            mask = mask & ((m < 2 * BLOCK) | (i < n_steps - 1))
        q = q_ref[t * BLOCK:(t + 1) * BLOCK, :]
        kv_loc = kv_all[t * BLOCK:(t + 3) * BLOCK]
        scores = []
        for h in range(KV_HEADS):
            ks = slice(h * HEAD_DIM, (h + 1) * HEAD_DIM)
            qg = jnp.concatenate(
                [q[:, (h * GROUP + g) * HEAD_DIM:(h * GROUP + g + 1) * HEAD_DIM] for g in range(GROUP)], axis=0)
            s_loc = _dot_nt(qg, kv_loc[:, ks])
            s_loc = jnp.concatenate([jnp.where(mask[:, :BLOCK], s_loc[:, :BLOCK], NEG_INF), s_loc[:, BLOCK:2 * BLOCK],
                                     jnp.where(mask[:, 2 * BLOCK:], s_loc[:, 2 * BLOCK:], NEG_INF)], axis=1)
            scores.append((_dot_nt(qg, kv_ctx[:, ks]), s_loc))
        probs = []
        for h, (s_ctx, s_loc) in enumerate(scores):
            sink = jnp.zeros((rows, 1), F32)
            for g in range(GROUP):
                sink = jnp.where(grp == g, sink_ref[h * GROUP + g] * LOG2_E, sink)
            top = jnp.maximum(jnp.maximum(jnp.max(s_ctx, -1, keepdims=True), jnp.max(s_loc, -1, keepdims=True)), sink)
            e_ctx = jnp.exp2(s_ctx - top)
            e_loc = jnp.exp2(s_loc - top)
            den = jnp.sum(e_ctx, -1, keepdims=True) + jnp.sum(e_loc, -1, keepdims=True) + jnp.exp2(sink - top)
            probs.append((e_ctx.astype(BF16), e_loc.astype(BF16), 1.0 / den))
        outs = []
        for h, (e_ctx, e_loc, inv) in enumerate(probs):
            vs = slice(KV_W + h * HEAD_DIM, KV_W + (h + 1) * HEAD_DIM)
            o = (_dot(e_ctx, kv_ctx[:, vs]) + _dot(e_loc, kv_loc[:, vs])) * inv
            outs += [o[g * BLOCK:(g + 1) * BLOCK] for g in range(GROUP)]
        o_ref[t * BLOCK:(t + 1) * BLOCK, :] = jnp.concatenate(outs, axis=1).astype(BF16)


def _attend(q, kv, kv_ctx, sink, per, side):
    b, s, _ = q.shape
    nb = s // BLOCK
    nt = nb // per
    c_len = kv_ctx.shape[1]
    kvw = 2 * KV_W
    side_specs = [pl.BlockSpec((a.shape[0] // (b * nt), a.shape[1]), lambda i, j: (i * nt + j, 0)) for a in side]
    out_spec = pl.BlockSpec((None, per * BLOCK, ATTN_W), lambda i, j: (i, j, 0))
    outs = pl.pallas_call(
        _attend_body,
        out_shape=[jax.ShapeDtypeStruct((b, s, ATTN_W), BF16)] + [jax.ShapeDtypeStruct(a.shape, BF16) for a in side],
        grid=(b, nt),
        in_specs=[
            pl.BlockSpec(memory_space=pltpu.SMEM),
            pl.BlockSpec((None, per * BLOCK, ATTN_W), lambda i, j: (i, j, 0)),
            pl.BlockSpec((None, BLOCK, kvw), lambda i, j: (i, jnp.maximum(j * per - 1, 0), 0)),
            pl.BlockSpec((None, per * BLOCK, kvw), lambda i, j: (i, j, 0)),
            pl.BlockSpec((None, BLOCK, kvw), lambda i, j: (i, jnp.minimum((j + 1) * per, nb - 1), 0)),
            pl.BlockSpec((None, c_len, kvw), lambda i, j: (i, 0, 0)),
        ] + side_specs,
        out_specs=[out_spec] + side_specs,
        compiler_params=_params("arbitrary", "arbitrary"),
        name="attend",
    )(sink, q, kv, kv, kv, kv_ctx, *side)
    return outs[0], outs[1:]


def _route(logits):
    lane = _iota(logits.shape, 1)
    far = jnp.int32(2 * ROUTE_W)

    def first_max(vals):
        top = jnp.max(vals, -1, keepdims=True)
        return top, jnp.min(jnp.where(vals == top, lane, far), -1, keepdims=True)

    is_g = lane < N_GROUPS
    g_top, g_idx = first_max(jnp.where(is_g, logits, NEG_INF))
    g_w = 1.0 / jnp.sum(jnp.where(is_g, jnp.exp(logits - g_top), 0.0), -1, keepdims=True)
    lo = N_GROUPS + EPG * g_idx
    e_vals = jnp.where((lane >= lo) & (lane < lo + EPG), logits, NEG_INF)
    v1, i1 = first_max(e_vals)
    v2, i2 = first_max(jnp.where(lane == i1, NEG_INF, e_vals))
    t = jnp.exp(v2 - v1)
    w1 = g_w * (1.0 / (1.0 + t))
    w2 = g_w * (t / (1.0 + t))
    swap = i2 < i1
    a = jnp.where(swap, i2, i1) - lo
    b = jnp.where(swap, i1, i2) - lo
    cls = g_idx * PAIRS + ((a * (2 * EPG - 1 - a)) >> 1) + (b - a - 1)
    return jnp.where(swap, w2, w1), jnp.where(swap, w1, w2), cls


def _merge_body(x_ref, ya_ref, u_ref, vg_ref, gate_ref, ws_ref, bs_ref, wpa_ref, wpb_ref, wo_ref,
                lng_ref, lnb_ref, g1_ref, sh2_ref, sc2_ref, wrh_ref, wrl_ref, br_ref, x1_ref, rec_ref, cnt_ref):
    tile, d = x_ref.shape
    lane = _iota((BLOCK, LANES), 1)
    below = (lane < _iota((BLOCK, LANES), 0)).astype(BF16)
    chunks = []
    for c in range(tile // BLOCK):
        rows = slice(c * BLOCK, (c + 1) * BLOCK)
        pieces = []
        for j in range(GM_W // LANES):
            v = vg_ref[rows, j * LANES:(j + 1) * LANES]
            per = LANES // GM_HEAD
            sp = _dot(ws_ref[per * j], v)
            for k in range(1, per):
                sp = jnp.where(lane < k * GM_HEAD, sp, _dot(ws_ref[per * j + k], v))
            pieces.append(sp)
        chunks.append(jnp.concatenate(pieces, axis=1) + bs_ref[...])
    sp = jnp.concatenate(chunks, axis=0)
    y_gm = (u_ref[...].astype(F32) * sp).astype(BF16)
    gate = gate_ref[...].astype(F32)
    y = gate[:, :d] * _dot(ya_ref[...], wpa_ref[...]) + gate[:, d:] * _dot(y_gm, wpb_ref[...])
    mix = _dot(y.astype(BF16), wo_ref[...])
    x1 = _ln(ALPHA * x_ref[...] + g1_ref[...] * mix) * lng_ref[...] + lnb_ref[...]
    x1_ref[...] = x1
    h2 = _ln(x1) * (1.0 + sc2_ref[...]) + sh2_ref[...]
    w_lo, w_hi, cls = _route(_dot_split(h2, wrh_ref[...], wrl_ref[...]) + br_ref[...])
    for c in range(tile // BLOCK):
        rows = slice(c * BLOCK, (c + 1) * BLOCK)
        onehot = lane == cls[rows]
        hot = onehot.astype(BF16)
        rank = jnp.sum(jnp.where(onehot, _dot(below, hot), 0.0), -1, keepdims=True)
        cnt_ref[c] = jnp.sum(hot.astype(F32), 0, keepdims=True)
        rec = jnp.where(lane == REC_WLO, w_lo[rows], 0.0)
        rec = jnp.where(lane == REC_WHI, w_hi[rows], rec)
        rec = jnp.where(lane == REC_CLS, cls[rows].astype(F32), rec)
        rec_ref[rows, :] = jnp.where(lane == REC_RANK, rank, rec)


def _merge(x, y_attn, u, vg, gates, ws_bf, bs_rows, wpa, wpb, wo, ln_g, ln_b, mod3, wr_hi, wr_lo, br, tile):
    b, s, d = x.shape
    nt = s // tile
    per = tile // BLOCK
    row = lambda width: pl.BlockSpec((None, tile, width), lambda i, j: (i, j, 0))
    full = lambda a: pl.BlockSpec(a.shape, lambda i, j: (0,) * a.ndim)
    modv = lambda k: pl.BlockSpec((None, 1, d), lambda i, j: (i, 0, k))
    ln_g = ln_g.reshape(1, d)
    ln_b = ln_b.reshape(1, d)
    return pl.pallas_call(
        _merge_body,
        out_shape=[jax.ShapeDtypeStruct((b, s, d), F32), jax.ShapeDtypeStruct((b, s, ROUTE_W), F32),
                   jax.ShapeDtypeStruct((b * s // BLOCK, 1, LANES), F32)],
        grid=(b, nt),
        in_specs=[
            row(d), row(ATTN_W), row(GM_W), row(GM_W), row(2 * d),
            full(ws_bf), full(bs_rows), full(wpa), full(wpb), full(wo), full(ln_g), full(ln_b),
            modv(2), modv(3), modv(4),
            full(wr_hi), full(wr_lo), full(br),
        ],
        out_specs=[row(d), row(ROUTE_W), pl.BlockSpec((per, 1, LANES), lambda i, j: (i * nt + j, 0, 0))],
        compiler_params=_params("arbitrary", "arbitrary"),
        name="merge",
    )(x, y_attn, u, vg, gates, ws_bf, bs_rows, wpa, wpb, wo, ln_g, ln_b, mod3, mod3, mod3, wr_hi, wr_lo, br)


def _plan_body(cnt_ref, rec_ref, lpos_ref, tab_ref, cblk_ref, run_ref, cntt_ref, lst_ref, base_ref, lstf_ref, *, chunks):
    step = pl.program_id(0)
    n_chunks = cnt_ref.shape[0]
    n_tiles = run_ref.shape[0]
    max_blocks = tab_ref.shape[0]

    @pl.when(step == 0)
    def _():
        cnt = cnt_ref[...]
        total = jnp.sum(cnt, 0, keepdims=True)
        n_blk = jnp.floor((total + (BLOCK - 1)) * (1.0 / BLOCK))
        before = (_iota((LANES, LANES), 0) < _iota((LANES, LANES), 1)).astype(BF16)
        nb_hi, nb_lo = _split(jnp.broadcast_to(n_blk, (SUBLANES, LANES)))
        blk_start = (_dot(nb_hi, before) + _dot(nb_lo, before))[0:1]
        lane1 = _iota((1, LANES), 1)
        packed = blk_start
        shift = jnp.zeros((1, 1), F32)
        for g in range(1, N_GROUPS):
            first = jnp.sum(jnp.where(lane1 == g * PAIRS, blk_start, 0.0), -1, keepdims=True) + shift
            over = first - MOE_PACK * jnp.floor(first * (1.0 / MOE_PACK))
            shift = shift + jnp.where(over > 0.0, MOE_PACK - over, 0.0)
            packed = jnp.where(lane1 >= g * PAIRS, blk_start + shift, packed)
        blk_start = packed
        earlier = (_iota((n_chunks, n_chunks), 1) < _iota((n_chunks, n_chunks), 0)).astype(BF16)
        base_ref[...] = blk_start * float(BLOCK) + _dot(earlier, cnt.astype(BF16))
        blk_end = blk_start + n_blk
        sub = _iota((SUBLANES, LANES), 0)
        cblk_ref[...] = jnp.where(sub == 0, blk_start, jnp.where(sub == 1, blk_end, 0.0)).astype(I32)
        lane = _iota((max_blocks, LANES), 1)
        blk = _iota((max_blocks, LANES), 0).astype(F32)
        is_cls = lane < N_CLASSES
        n_real = jnp.max(jnp.where(is_cls, blk_end, 0.0), -1, keepdims=True)
        u = jnp.minimum(blk, n_real - 1.0)
        cls_of = jnp.sum(jnp.where(is_cls & (blk_end <= u), 1.0, 0.0), -1, keepdims=True)
        n_steps = jnp.floor((n_real + (MOE_PACK - 1)) * (1.0 / MOE_PACK))
        tab = jnp.where(lane == 0, cls_of, jnp.where(lane == 1, n_steps, 0.0))
        tab_ref[...] = tab.astype(I32)
        tile = _iota((n_tiles, n_chunks), 0)
        chunk = _iota((n_tiles, n_chunks), 1)
        owner = ((chunk >= tile * chunks) & (chunk < (tile + 1) * chunks)).astype(BF16)
        cnt_tile = _dot(owner, cnt.astype(BF16))
        ct_hi, ct_lo = _split(cnt_tile)
        local_start = _dot(ct_hi, before) + _dot(ct_lo, before)
        run_ref[...] = base_ref[pl.ds(0, n_tiles, stride=chunks), :].astype(I32)
        cntt_ref[...] = cnt_tile.astype(I32)
        lst_ref[...] = local_start.astype(I32)
        lstf_ref[...] = local_start

    lane = _iota((BLOCK, LANES), 1)
    tile_off = lstf_ref[pl.ds(step, 1), :] - base_ref[pl.ds(step * chunks, 1), :]
    pick_rank = (_iota((2 * SUBLANES, LANES), 1) == REC_RANK).astype(BF16)
    for c in range(chunks):
        rec = rec_ref[c * BLOCK:(c + 1) * BLOCK, :]
        hot = (lane == rec[:, REC_CLS:REC_CLS + 1].astype(I32)).astype(BF16)
        base = base_ref[pl.ds(step * chunks + c, 1), :] + tile_off
        b_hi, b_lo = _split(jnp.broadcast_to(base, (2 * SUBLANES, LANES)))
        lpos = _dot_nt(b_hi, hot) + _dot_nt(b_lo, hot) + _dot_nt(pick_rank, rec.astype(BF16))
        lpos_ref[c] = lpos[0:1].astype(I32)


def _plan(counts, rec, max_blocks, chunks):
    n = rec.shape[0]
    n_chunks = n // BLOCK
    n_tiles = n_chunks // chunks
    tile_tab = jax.ShapeDtypeStruct((n_tiles, LANES), I32)
    whole = lambda rows: pl.BlockSpec((rows, LANES), lambda i: (0, 0))
    return pl.pallas_call(
        functools.partial(_plan_body, chunks=chunks),
        out_shape=[jax.ShapeDtypeStruct((n_chunks, 1, LANES), I32), jax.ShapeDtypeStruct((max_blocks, LANES), I32),
                   jax.ShapeDtypeStruct((SUBLANES, LANES), I32), tile_tab, tile_tab, tile_tab],
        grid=(n_tiles,),
        in_specs=[whole(n_chunks), pl.BlockSpec((chunks * BLOCK, ROUTE_W), lambda i: (i, 0))],
        out_specs=[pl.BlockSpec((chunks, 1, LANES), lambda i: (i, 0, 0)), whole(max_blocks), whole(SUBLANES),
                   whole(n_tiles), whole(n_tiles), whole(n_tiles)],
        scratch_shapes=[pltpu.VMEM((n_chunks, LANES), F32), pltpu.VMEM((n_tiles, LANES), F32)],
        compiler_params=_params("arbitrary"),
        name="plan",
    )(counts, rec)


def _tile_rows(base, j, pitch):
    return pl.ds(base + j, BLOCK, stride=pitch)


def _load_tiles(ref, base, pitch, width):
    return jnp.concatenate([ref[_tile_rows(base, j, pitch), :] for j in range(width // LANES)], axis=1)


def _store_tiles(ref, base, pitch, val):
    for j in range(val.shape[1] // LANES):
        ref[_tile_rows(base, j, pitch), :] = val[:, j * LANES:(j + 1) * LANES]


def _start_runs(cnt_ref, lst_ref, run_ref, streams, sem, to_hbm):
    def issue(c):
        for buf, base, hbm, pitch in streams:
            rows = cnt_ref[0, c] * pitch
            aligned = (lambda row: pl.multiple_of(row, pitch)) if pitch > 1 else (lambda row: row)
            v = buf.at[pl.ds(aligned(base + lst_ref[0, c] * pitch), rows)]
            h = hbm.at[pl.ds(aligned(run_ref[0, c] * pitch), rows)]
            (pltpu.make_async_copy(v, h, sem) if to_hbm else pltpu.make_async_copy(h, v, sem)).start(priority=c % 2)

    for c in range(N_CLASSES):
        pl.when(cnt_ref[0, c] > 0)(functools.partial(issue, c))


def _wait_runs(streams, tokens, sem):
    for buf, base, hbm, pitch in streams:
        rows = tokens * pitch
        pltpu.make_async_copy(hbm.at[pl.ds(0, rows)], buf.at[pl.ds(base, rows)], sem).wait()


def _scatter_body(cls_blk_ref, run_ref, cnt_ref, lst_ref, lpos_ref, x1_ref, rec_ref, sh_ref, sc_ref, xs_hbm, meta_hbm,
                  abuf, bbuf, mbuf, zbuf, ssem, zsem):
    u = pl.program_id(0)
    slot = u % 2
    tokens = x1_ref.shape[0]
    chunks = tokens // BLOCK
    tile_rows = BLOCK * SUBLANES
    sort_rows = chunks * tile_rows

    def streams(s):
        return ((bbuf, s * sort_rows, xs_hbm, SUBLANES), (mbuf, s * tokens, meta_hbm, 1))

    @pl.when(u == 0)
    def _():
        zbuf[...] = jnp.zeros_like(zbuf)

        def zero_block(blk, start):
            for hbm, rows in ((xs_hbm, tile_rows), (meta_hbm, BLOCK)):
                cp = pltpu.make_async_copy(zbuf.at[pl.ds(0, rows)], hbm.at[pl.ds(pl.multiple_of(blk * rows, rows), rows)],
                                           zsem)
                cp.start() if start else cp.wait()

        for start in (True, False):
            for c in range(N_CLASSES):
                pl.when(cls_blk_ref[1, c] > cls_blk_ref[0, c])(functools.partial(zero_block, cls_blk_ref[1, c] - 1, start))
        gaps = [(cls_blk_ref[1, g * PAIRS - 1], cls_blk_ref[0, g * PAIRS]) for g in range(1, N_GROUPS)]
        gaps.append((cls_blk_ref[1, N_CLASSES - 1], xs_hbm.shape[0] // tile_rows))
        for start in (True, False):
            for lo, hi in gaps:
                lax.fori_loop(lo, hi, lambda blk, carry: (zero_block(blk, start), carry)[1], 0)

    @pl.when(u >= 2)
    def _():
        _wait_runs(streams(slot), tokens, ssem.at[slot])

    scale = 1.0 + sc_ref[...]
    shift = sh_ref[...]

    unroll = abuf.shape[0] // tile_rows

    def sort_chunks(i, carry):
        for k in range(unroll):
            c = i * unroll + k
            stage = k * tile_rows
            rows = pl.ds(pl.multiple_of(c * BLOCK, BLOCK), BLOCK)
            _store_tiles(abuf, stage, SUBLANES, _ln(x1_ref[rows, :]) * scale + shift)
            rec = rec_ref[rows, :]
            for r in range(BLOCK):
                p = lpos_ref[c, 0, r]
                dst = pl.multiple_of(slot * sort_rows + p * SUBLANES, SUBLANES)
                bbuf[pl.ds(dst, SUBLANES), :] = abuf[stage + r * SUBLANES:stage + (r + 1) * SUBLANES, :]
                mbuf[pl.ds(slot * tokens + p, 1), :] = rec[r:r + 1, :]
        return carry

    lax.fori_loop(0, chunks // unroll, sort_chunks, 0)
    _start_runs(cnt_ref, lst_ref, run_ref, streams(slot), ssem.at[slot], True)

    @pl.when(u == pl.num_programs(0) - 1)
    def _():
        _wait_runs(streams(slot), tokens, ssem.at[slot])

        @pl.when(u >= 1)
        def _():
            _wait_runs(streams(1 - slot), tokens, ssem.at[1 - slot])


def _stage_chunks(chunks):
    return next(p for p in (4, 2, 1) if chunks % p == 0)


def _tile_tab_spec(off, n_tiles):
    return pl.BlockSpec((None, 1, LANES), lambda u: (jnp.minimum(u + off, n_tiles - 1), 0, 0),
                        memory_space=pltpu.SMEM)


def _scatter(x1, rec, mod3, lpos, tile_tabs, cls_blk, max_blocks, seq, chunks):
    n, d = x1.shape
    n_tiles = n // (chunks * BLOCK)
    per_batch = seq // (chunks * BLOCK)
    tile_rows = BLOCK * SUBLANES
    hbm = pl.BlockSpec(memory_space=pl.ANY)
    return pl.pallas_call(
        _scatter_body,
        out_shape=[jax.ShapeDtypeStruct((max_blocks * tile_rows, LANES), F32),
                   jax.ShapeDtypeStruct((max_blocks * BLOCK, ROUTE_W), F32)],
        grid=(n_tiles,),
        in_specs=[
            pl.BlockSpec(memory_space=pltpu.SMEM),
            _tile_tab_spec(0, n_tiles), _tile_tab_spec(0, n_tiles), _tile_tab_spec(0, n_tiles),
            pl.BlockSpec((chunks, 1, LANES), lambda u: (u, 0, 0), memory_space=pltpu.SMEM),
            pl.BlockSpec((chunks * BLOCK, d), lambda u: (u, 0)),
            pl.BlockSpec((chunks * BLOCK, ROUTE_W), lambda u: (u, 0)),
            pl.BlockSpec((None, 1, d), lambda u: (u // per_batch, 0, 3)),
            pl.BlockSpec((None, 1, d), lambda u: (u // per_batch, 0, 4)),
        ],
        out_specs=[hbm, hbm],
        scratch_shapes=[
            pltpu.VMEM((_stage_chunks(chunks) * tile_rows, LANES), F32),
            pltpu.VMEM((2 * chunks * tile_rows, LANES), F32),
            pltpu.VMEM((2 * chunks * BLOCK, ROUTE_W), F32),
            pltpu.VMEM((tile_rows, LANES), F32),
            pltpu.SemaphoreType.DMA((2,)),
            pltpu.SemaphoreType.DMA,
        ],
        compiler_params=_params("arbitrary"),
        name="scatter",
    )(cls_blk, *tile_tabs, lpos, x1, rec, mod3, mod3)


def _expert(h, w1, w3, w2):
    hid = jax.nn.silu(_dot(h, w1)) * _dot(h, w3)
    return _dot(hid, w2)


def _moe_body(sgrp_ref, blo_ref, bhi_ref, nsteps_ref, xs_ref, meta_ref, w1_ref, w3_ref, w2_ref, o_ref):
    u = pl.program_id(0)
    d = w1_ref.shape[1]
    tile_rows = BLOCK * SUBLANES

    @pl.when(u < nsteps_ref[0])
    def _():
        jobs = [(k, e) for k in range(MOE_PACK) for e in (blo_ref[MOE_PACK * u + k], bhi_ref[MOE_PACK * u + k])]
        h = [_load_tiles(xs_ref, k * tile_rows, SUBLANES, d).astype(BF16) for k in range(MOE_PACK)]
        up = [(_dot(h[k], w1_ref[e]), _dot(h[k], w3_ref[e])) for k, e in jobs]
        hid = [(jax.nn.silu(a) * b).astype(BF16) for a, b in up]
        y = [_dot(g, w2_ref[e]) for (k, e), g in zip(jobs, hid)]
        for k in range(MOE_PACK):
            meta = meta_ref[k * BLOCK:(k + 1) * BLOCK, :]
            _store_tiles(o_ref, k * tile_rows, SUBLANES,
                         meta[:, REC_WLO:REC_WLO + 1] * y[2 * k] + meta[:, REC_WHI:REC_WHI + 1] * y[2 * k + 1])

    @pl.when(u >= nsteps_ref[0])
    def _():
        o_ref[...] = jnp.zeros_like(o_ref)


def _moe(xs, meta, tables, w1, w3, w2, d):
    sgrp, blo, bhi, n_steps = tables
    max_steps = sgrp.shape[0]
    de = w1.shape[1]
    in_rows = MOE_PACK * BLOCK * SUBLANES
    w1g = w1.reshape(N_GROUPS, EPG, d, de)
    w3g = w3.reshape(N_GROUPS, EPG, d, de)
    w2g = w2.reshape(N_GROUPS, EPG, de, d)

    def w_spec(a):
        return pl.BlockSpec((None,) + a.shape[1:], lambda u, g, lo, hi, n_steps: (g[u], 0, 0, 0))

    grid_spec = pltpu.PrefetchScalarGridSpec(
        num_scalar_prefetch=4,
        grid=(max_steps,),
        in_specs=[
            pl.BlockSpec((in_rows, LANES), lambda u, g, lo, hi, n_steps: (jnp.minimum(u, n_steps[0] - 1), 0)),
            pl.BlockSpec((MOE_PACK * BLOCK, ROUTE_W), lambda u, g, lo, hi, n_steps: (jnp.minimum(u, n_steps[0] - 1), 0)),
            w_spec(w1g), w_spec(w3g), w_spec(w2g),
        ],
        out_specs=pl.BlockSpec((in_rows, LANES), lambda u, *_: (u, 0)),
    )
    return pl.pallas_call(
        _moe_body,
        out_shape=jax.ShapeDtypeStruct((max_steps * in_rows, LANES), F32),
        grid_spec=grid_spec,
        compiler_params=pltpu.CompilerParams(dimension_semantics=("arbitrary",), vmem_limit_bytes=MOE_VMEM_LIMIT_BYTES),
        name="moe",
    )(sgrp, blo, bhi, n_steps, xs, meta, w1g, w3g, w2g)


def _final_body(run_ref, cnt_ref, lst_ref, runn_ref, cntn_ref, lstn_ref, lpos_ref, x1_ref, g2_ref,
                lng_ref, lnb_ref, ms_hbm, o_ref, gbuf, tbuf, gsem):
    u = pl.program_id(0)
    slot = u % 2
    d = x1_ref.shape[-1]
    tokens = x1_ref.shape[0]
    chunks = tokens // BLOCK
    tile_rows = BLOCK * SUBLANES
    sort_rows = chunks * tile_rows

    def streams(s):
        return ((gbuf, s * sort_rows, ms_hbm, SUBLANES),)

    @pl.when(u == 0)
    def _():
        _start_runs(cnt_ref, lst_ref, run_ref, streams(0), gsem.at[0], False)

    _wait_runs(streams(slot), tokens, gsem.at[slot])
    _start_runs(cntn_ref, lstn_ref, runn_ref, streams(1 - slot), gsem.at[1 - slot], False)
    gain = g2_ref[...]
    ln_g = lng_ref[...]
    ln_b = lnb_ref[...]
    unroll = tbuf.shape[0] // tile_rows

    def finish_chunks(i, carry):
        for k in range(unroll):
            c = i * unroll + k
            stage = k * tile_rows
            for r in range(BLOCK):
                src = pl.ds(pl.multiple_of(slot * sort_rows + lpos_ref[c, 0, r] * SUBLANES, SUBLANES), SUBLANES)
                tbuf[stage + r * SUBLANES:stage + (r + 1) * SUBLANES, :] = gbuf[src, :]
            rows = pl.ds(pl.multiple_of(c * BLOCK, BLOCK), BLOCK)
            moe = _load_tiles(tbuf, stage, SUBLANES, d)
            o_ref[rows, :] = _ln(ALPHA * x1_ref[rows, :] + gain * moe) * ln_g + ln_b
        return carry

    lax.fori_loop(0, chunks // unroll, finish_chunks, 0)

    @pl.when(u == pl.num_programs(0) - 1)
    def _():
        _wait_runs(streams(1 - slot), tokens, gsem.at[1 - slot])


def _final(x1, ms, lpos, tile_tabs, mod3, ln_g, ln_b, seq, chunks):
    n, d = x1.shape
    n_tiles = n // (chunks * BLOCK)
    per_batch = seq // (chunks * BLOCK)
    tile_rows = BLOCK * SUBLANES
    vec = pl.BlockSpec((1, d), lambda u: (0, 0))
    return pl.pallas_call(
        _final_body,
        out_shape=jax.ShapeDtypeStruct((n, d), F32),
        grid=(n_tiles,),
        in_specs=[
            _tile_tab_spec(0, n_tiles), _tile_tab_spec(0, n_tiles), _tile_tab_spec(0, n_tiles),
            _tile_tab_spec(1, n_tiles), _tile_tab_spec(1, n_tiles), _tile_tab_spec(1, n_tiles),
            pl.BlockSpec((chunks, 1, LANES), lambda u: (u, 0, 0), memory_space=pltpu.SMEM),
            pl.BlockSpec((chunks * BLOCK, d), lambda u: (u, 0)),
            pl.BlockSpec((None, 1, d), lambda u: (u // per_batch, 0, 5)),
            vec, vec, pl.BlockSpec(memory_space=pl.ANY),
        ],
        out_specs=pl.BlockSpec((chunks * BLOCK, d), lambda u: (u, 0)),
        scratch_shapes=[
            pltpu.VMEM((2 * chunks * tile_rows, LANES), F32),
            pltpu.VMEM((_stage_chunks(chunks) * tile_rows, LANES), F32),
            pltpu.SemaphoreType.DMA((2,)),
        ],
        compiler_params=_params("arbitrary"),
        name="final",
    )(*tile_tabs, *tile_tabs, lpos, x1, mod3, ln_g.reshape(1, d), ln_b.reshape(1, d), ms)


def _block_tables(tab):
    cls = tab[:, 0]
    pair = cls % PAIRS
    first = jnp.array([i for i in range(EPG) for _ in range(i + 1, EPG)], I32)
    second = jnp.array([j for i in range(EPG) for j in range(i + 1, EPG)], I32)
    return (cls // PAIRS)[::MOE_PACK], first[pair], second[pair], tab[0, 1].reshape(1)


def kernel(x, c, ctx, c_ctx, w_ada, b_ada, w_in, attn_sink, gm_ln_g, gm_ln_b, gm_ws, gm_bs, w_pa, w_pb, w_o,
           ln1_g, ln1_b, router_g_w, router_g_b, router_e_w, router_e_b, moe_w1, moe_w3, moe_w2, ln2_g, ln2_b):
    b, s, d = x.shape
    assert w_ada.shape[0] == DEPTH and s % BLOCK == 0 and s % GRID_W == 0
    mod_rows = 2 * SUBLANES
    assert b + 1 <= mod_rows
    c_rows = jnp.concatenate([c, c_ctx[None, :], jnp.zeros((mod_rows - b - 1, d), F32)], axis=0)
    mod = _adaln_mod(c_rows, w_ada[0], b_ada[0])
    mod3 = mod.reshape(mod_rows, 1, 6 * d)
    w_in_bf = w_in[0].astype(BF16)
    kv_ctx = _ctx_kv(ctx, mod3, b, w_in_bf)
    tile = next(t for t in (1024, 512, BLOCK) if s % t == 0)
    q, kv, u, vg, gates = _project(x, mod3, w_in_bf, _rope_tables(s), gm_ln_g[0], gm_ln_b[0], tile)
    nb = s // BLOCK
    experts, _, d_expert = moe_w1.shape[1:]
    moe_w = [moe_w1[0].reshape(experts * d, d_expert), moe_w3[0].reshape(experts * d, d_expert),
             moe_w2[0].reshape(experts * d_expert, d)]
    per = next(p for p in (8, 4, 2, 1) if nb % p == 0)
    steps = b * (nb // per)
    sliced = all(w.shape[0] % (steps * 2 * SUBLANES) == 0 for w in moe_w)
    y_attn, moe_w_bf = _attend(q, kv, kv_ctx, attn_sink[0], per, moe_w if sliced else [])
    if not sliced:
        moe_w_bf = [w.astype(BF16) for w in moe_w]
    bs_rows = jnp.repeat(gm_bs[0].T, GM_HEAD, axis=1)
    w_route = jnp.concatenate([router_g_w[0], jnp.transpose(router_e_w[0], (1, 0, 2)).reshape(d, N_EXPERTS)], axis=1)
    w_route = jnp.pad(w_route, ((0, 0), (0, ROUTE_W - w_route.shape[1])))
    b_route = jnp.pad(jnp.concatenate([router_g_b[0], router_e_b[0].reshape(-1)]), (0, ROUTE_W - N_GROUPS - N_EXPERTS))
    wr_hi = w_route.astype(BF16)
    wr_lo = (w_route - wr_hi.astype(F32)).astype(BF16)
    merge_tile = BLOCK * next(p for p in (8, 4, 2, 1) if nb % p == 0)
    x1, rec, counts = _merge(x, y_attn, u, vg, gates, gm_ws[0].astype(BF16), bs_rows, w_pa[0].astype(BF16),
                             w_pb[0].astype(BF16), w_o[0].astype(BF16), ln1_g[0], ln1_b[0], mod3, wr_hi, wr_lo,
                             b_route.reshape(1, ROUTE_W), merge_tile)
    n = b * s
    x1 = x1.reshape(n, d)
    rec = rec.reshape(n, ROUTE_W)
    max_blocks = n // BLOCK + N_CLASSES - 1 + (N_GROUPS - 1) * (MOE_PACK - 1)
    max_blocks = -(-max_blocks // MOE_PACK) * MOE_PACK
    sort_chunks = next(p for p in (SUBLANES, 4, 2, 1) if nb % p == 0)
    lpos, tab, cls_blk, *tile_tabs = _plan(counts.reshape(n // BLOCK, LANES), rec, max_blocks, sort_chunks)
    tile_tabs = [t.reshape(t.shape[0], 1, LANES) for t in tile_tabs]
    xs, meta = _scatter(x1, rec, mod3, lpos, tile_tabs, cls_blk, max_blocks, s, sort_chunks)
    ms = _moe(xs, meta, _block_tables(tab), *moe_w_bf, d)
    out = _final(x1, ms, lpos, tile_tabs, mod3, ln2_g[0], ln2_b[0], s, sort_chunks)
    return out.reshape(b, s, d)
```

```python
import functools

import jax
import jax.numpy as jnp
import numpy as np
from jax import lax
from jax.experimental import pallas as pl
from jax.experimental.pallas import tpu as pltpu

F32 = jnp.float32
BF16 = jnp.bfloat16
I32 = jnp.int32

GRID_W = 64
HEAD_DIM = 64
Q_HEADS = 8
KV_HEADS = 2
GROUP = Q_HEADS // KV_HEADS
BLOCK = 128
ROPE_BASE = 10000.0
ROPE_FREQS = HEAD_DIM // 4
GM_GROUPS = 8
GM_HEAD = 64
ATTN_W = Q_HEADS * HEAD_DIM
KV_W = KV_HEADS * HEAD_DIM
GM_W = GM_GROUPS * GM_HEAD
N_GROUPS = 4
EPG = 8
N_EXPERTS = N_GROUPS * EPG
PAIRS = EPG * (EPG - 1) // 2
N_CLASSES = N_GROUPS * PAIRS
LN_EPS = 1e-6
NEG_INF = -1e30
DEPTH = 1
ALPHA = (2.0 * DEPTH) ** 0.25
LOG2_E = 1.4426950408889634

LANES = 128
SUBLANES = 8
VMEM_LIMIT_BYTES = 56 * 1024 * 1024
MOE_VMEM_LIMIT_BYTES = 60 * 1024 * 1024
MOE_PACK = 4
ROUTE_W = LANES
REC_WLO, REC_WHI, REC_CLS, REC_RANK = 0, 1, 2, 3


def _params(*sem):
    return pltpu.CompilerParams(dimension_semantics=sem, vmem_limit_bytes=VMEM_LIMIT_BYTES)


def _ln(x):
    mu = jnp.mean(x, -1, keepdims=True)
    xc = x - mu
    var = jnp.mean(xc * xc, -1, keepdims=True)
    return xc * lax.rsqrt(var + LN_EPS)


def _dot(a, b):
    return jnp.dot(a, b, preferred_element_type=F32)


def _dot_nt(a, b):
    return lax.dot_general(a, b, (((1,), (1,)), ((), ())), preferred_element_type=F32)


def _split(a):
    hi = a.astype(BF16)
    lo = (a - hi.astype(F32)).astype(BF16)
    return hi, lo


def _dot_split(a, b_hi, b_lo):
    a_hi, a_lo = _split(a)
    return _dot(a_hi, b_hi) + _dot(a_hi, b_lo) + _dot(a_lo, b_hi)


def _iota(shape, axis):
    return lax.broadcasted_iota(I32, shape, axis)


def _mod_body(c_ref, w_ref, b_ref, o_ref):
    w_hi, w_lo = _split(w_ref[...])
    o_ref[...] = _dot_split(jax.nn.silu(c_ref[...]), w_hi, w_lo) + b_ref[...]


def _adaln_mod(c_rows, w_ada, b_ada):
    rows, d = c_rows.shape
    width = w_ada.shape[1]
    tn = 6 * LANES
    return pl.pallas_call(
        _mod_body,
        out_shape=jax.ShapeDtypeStruct((rows, width), F32),
        grid=(width // tn,),
        in_specs=[
            pl.BlockSpec((rows, d), lambda j: (0, 0)),
            pl.BlockSpec((d, tn), lambda j: (0, j)),
            pl.BlockSpec((1, tn), lambda j: (0, j)),
        ],
        out_specs=pl.BlockSpec((rows, tn), lambda j: (0, j)),
        compiler_params=_params("arbitrary"),
        name="mod",
    )(c_rows, w_ada, b_ada.reshape(1, width))


def _ctx_body(ctx_ref, sh_ref, sc_ref, w_ref, o_ref):
    h = _ln(ctx_ref[...]) * (1.0 + sc_ref[...]) + sh_ref[...]
    o_ref[...] = _dot(h.astype(BF16), w_ref[...]).astype(BF16)


def _ctx_kv(ctx, mod3, ctx_row, w_in_bf):
    b, c_len, d = ctx.shape
    kvw = 2 * KV_W
    return pl.pallas_call(
        _ctx_body,
        out_shape=jax.ShapeDtypeStruct((b, c_len, kvw), BF16),
        grid=(b,),
        in_specs=[
            pl.BlockSpec((None, c_len, d), lambda i: (i, 0, 0)),
            pl.BlockSpec((None, 1, d), lambda i: (ctx_row, 0, 0)),
            pl.BlockSpec((None, 1, d), lambda i: (ctx_row, 0, 1)),
            pl.BlockSpec((d, kvw), lambda i: (0, ATTN_W // kvw)),
        ],
        out_specs=pl.BlockSpec((None, c_len, kvw), lambda i: (i, 0, 0)),
        compiler_params=_params("arbitrary"),
        name="ctx_kv",
    )(ctx, mod3, mod3, w_in_bf)


def _rope(x, cos, s_up, s_dn):
    return x * cos + pltpu.roll(x, LANES - ROPE_FREQS, 1) * s_up + pltpu.roll(x, ROPE_FREQS, 1) * s_dn


def _project_body(x_ref, sh_ref, sc_ref, w_ref, cos_ref, sup_ref, sdn_ref, lng_ref, lnb_ref,
                  q_ref, kv_ref, u_ref, vg_ref, gate_ref):
    h = (_ln(x_ref[...]) * (1.0 + sc_ref[...]) + sh_ref[...]).astype(BF16)
    cos, s_up, s_dn = cos_ref[...], sup_ref[...], sdn_ref[...]
    q_scale = HEAD_DIM ** -0.5 * LOG2_E
    k0 = ATTN_W
    u0 = k0 + 2 * KV_W
    vg0 = u0 + GM_W
    gate0 = vg0 + GM_W
    q = _dot(h, w_ref[:, 0:k0])
    for j in range(ATTN_W // LANES):
        sl = slice(j * LANES, (j + 1) * LANES)
        q_ref[:, sl] = (_rope(q[:, sl], cos, s_up, s_dn) * q_scale).astype(BF16)
    kv = _dot(h, w_ref[:, k0:u0])
    kv_ref[:, 0:KV_W] = _rope(kv[:, 0:KV_W], cos, s_up, s_dn).astype(BF16)
    kv_ref[:, KV_W:] = kv[:, KV_W:].astype(BF16)
    u_ref[...] = jax.nn.gelu(_dot(h, w_ref[:, u0:vg0])).astype(BF16)
    vg = _ln(jax.nn.gelu(_dot(h, w_ref[:, vg0:gate0]))) * lng_ref[...] + lnb_ref[...]
    vg_ref[...] = vg.astype(BF16)
    gate_ref[...] = jax.nn.sigmoid(_dot(h, w_ref[:, gate0:])).astype(BF16)


def _project(x, mod3, w_in_bf, rope_tabs, gm_ln_g, gm_ln_b, tile):
    b, s, d = x.shape
    in_w = w_in_bf.shape[1]
    gate_w = in_w - (ATTN_W + 2 * KV_W + 2 * GM_W)
    row = lambda width: pl.BlockSpec((None, tile, width), lambda i, t: (i, t, 0))
    tab = pl.BlockSpec((tile, LANES), lambda i, t: (t, 0))
    vec = lambda width: pl.BlockSpec((1, width), lambda i, t: (0, 0))
    return pl.pallas_call(
        _project_body,
        out_shape=[
            jax.ShapeDtypeStruct((b, s, ATTN_W), BF16),
            jax.ShapeDtypeStruct((b, s, 2 * KV_W), BF16),
            jax.ShapeDtypeStruct((b, s, GM_W), BF16),
            jax.ShapeDtypeStruct((b, s, GM_W), BF16),
            jax.ShapeDtypeStruct((b, s, gate_w), BF16),
        ],
        grid=(b, s // tile),
        in_specs=[
            row(d),
            pl.BlockSpec((None, 1, d), lambda i, t: (i, 0, 0)),
            pl.BlockSpec((None, 1, d), lambda i, t: (i, 0, 1)),
            pl.BlockSpec((d, in_w), lambda i, t: (0, 0)),
            tab, tab, tab,
            vec(GM_W), vec(GM_W),
        ],
        out_specs=[row(ATTN_W), row(2 * KV_W), row(GM_W), row(GM_W), row(gate_w)],
        compiler_params=_params("arbitrary", "arbitrary"),
        name="project",
    )(x, mod3, mod3, w_in_bf, *rope_tabs, gm_ln_g.reshape(1, GM_W), gm_ln_b.reshape(1, GM_W))


def _rope_tables(s):
    pos = np.arange(s)
    inv = 1.0 / (ROPE_BASE ** (np.arange(ROPE_FREQS, dtype=np.float64) / ROPE_FREQS))
    ang_r = (pos // GRID_W)[:, None] * inv[None, :]
    ang_c = (pos % GRID_W)[:, None] * inv[None, :]
    cr, sr, cc, sn = np.cos(ang_r), np.sin(ang_r), np.cos(ang_c), np.sin(ang_c)
    z = np.zeros_like(sr)
    reps = LANES // HEAD_DIM
    tabs = ([cr, cr, cc, cc], [-sr, z, -sn, z], [z, sr, z, sn])
    return tuple(jnp.asarray(np.tile(np.concatenate(t, -1), (1, reps)), F32) for t in tabs)


def _attend_body(sink_ref, q_ref, kvp_ref, kvc_ref, kvn_ref, ctx_ref, *rest):
    n_side = (len(rest) - 1) // 2
    side_in, o_ref, side_out = rest[:n_side], rest[n_side], rest[n_side + 1:]
    for src, dst in zip(side_in, side_out):
        dst[...] = src[...].astype(BF16)
    i = pl.program_id(1)
    n_steps = pl.num_programs(1)
    per = q_ref.shape[0] // BLOCK
    rows = GROUP * BLOCK
    r = _iota((rows, 3 * BLOCK), 0) & (BLOCK - 1)
    m = _iota((rows, 3 * BLOCK), 1)
    band = (m >= r) & (m <= r + 2 * BLOCK)
    grp = _iota((rows, 1), 0) // BLOCK
    kv_all = jnp.concatenate([kvp_ref[...], kvc_ref[...], kvn_ref[...]], axis=0)
    kv_ctx = ctx_ref[...]
    for t in range(per):
        mask = band
        if t == 0:
            mask = mask & ((m >= BLOCK) | (i > 0))
        if t == per - 1:
            mask = mask & ((m < 2 * BLOCK) | (i < n_steps - 1))
        q = q_ref[t * BLOCK:(t + 1) * BLOCK, :]
        kv_loc = kv_all[t * BLOCK:(t + 3) * BLOCK]
        scores = []
        for h in range(KV_HEADS):
            ks = slice(h * HEAD_DIM, (h + 1) * HEAD_DIM)
            qg = jnp.concatenate(
                [q[:, (h * GROUP + g) * HEAD_DIM:(h * GROUP + g + 1) * HEAD_DIM] for g in range(GROUP)], axis=0)
            s_loc = _dot_nt(qg, kv_loc[:, ks])
            s_loc = jnp.concatenate([jnp.where(mask[:, :BLOCK], s_loc[:, :BLOCK], NEG_INF), s_loc[:, BLOCK:2 * BLOCK],
                                     jnp.where(mask[:, 2 * BLOCK:], s_loc[:, 2 * BLOCK:], NEG_INF)], axis=1)
            scores.append((_dot_nt(qg, kv_ctx[:, ks]), s_loc))
        probs = []
        for h, (s_ctx, s_loc) in enumerate(scores):
            sink = jnp.zeros((rows, 1), F32)
            for g in range(GROUP):
                sink = jnp.where(grp == g, sink_ref[h * GROUP + g] * LOG2_E, sink)
            top = jnp.maximum(jnp.maximum(jnp.max(s_ctx, -1, keepdims=True), jnp.max(s_loc, -1, keepdims=True)), sink)
            e_ctx = jnp.exp2(s_ctx - top)
            e_loc = jnp.exp2(s_loc - top)
            den = jnp.sum(e_ctx, -1, keepdims=True) + jnp.sum(e_loc, -1, keepdims=True) + jnp.exp2(sink - top)
            probs.append((e_ctx.astype(BF16), e_loc.astype(BF16), 1.0 / den))
        outs = []
        for h, (e_ctx, e_loc, inv) in enumerate(probs):
            vs = slice(KV_W + h * HEAD_DIM, KV_W + (h + 1) * HEAD_DIM)
            o = (_dot(e_ctx, kv_ctx[:, vs]) + _dot(e_loc, kv_loc[:, vs])) * inv
            outs += [o[g * BLOCK:(g + 1) * BLOCK] for g in range(GROUP)]
        o_ref[t * BLOCK:(t + 1) * BLOCK, :] = jnp.concatenate(outs, axis=1).astype(BF16)


def _attend(q, kv, kv_ctx, sink, per, side):
    b, s, _ = q.shape
    nb = s // BLOCK
    nt = nb // per
    c_len = kv_ctx.shape[1]
    kvw = 2 * KV_W
    side_specs = [pl.BlockSpec((a.shape[0] // (b * nt), a.shape[1]), lambda i, j: (i * nt + j, 0)) for a in side]
    out_spec = pl.BlockSpec((None, per * BLOCK, ATTN_W), lambda i, j: (i, j, 0))
    outs = pl.pallas_call(
        _attend_body,
        out_shape=[jax.ShapeDtypeStruct((b, s, ATTN_W), BF16)] + [jax.ShapeDtypeStruct(a.shape, BF16) for a in side],
        grid=(b, nt),
        in_specs=[
            pl.BlockSpec(memory_space=pltpu.SMEM),
            pl.BlockSpec((None, per * BLOCK, ATTN_W), lambda i, j: (i, j, 0)),
            pl.BlockSpec((None, BLOCK, kvw), lambda i, j: (i, jnp.maximum(j * per - 1, 0), 0)),
            pl.BlockSpec((None, per * BLOCK, kvw), lambda i, j: (i, j, 0)),
            pl.BlockSpec((None, BLOCK, kvw), lambda i, j: (i, jnp.minimum((j + 1) * per, nb - 1), 0)),
            pl.BlockSpec((None, c_len, kvw), lambda i, j: (i, 0, 0)),
        ] + side_specs,
        out_specs=[out_spec] + side_specs,
        compiler_params=_params("arbitrary", "arbitrary"),
        name="attend",
    )(sink, q, kv, kv, kv, kv_ctx, *side)
    return outs[0], outs[1:]


def _route(logits):
    lane = _iota(logits.shape, 1)
    far = jnp.int32(2 * ROUTE_W)

    def first_max(vals):
        top = jnp.max(vals, -1, keepdims=True)
        return top, jnp.min(jnp.where(vals == top, lane, far), -1, keepdims=True)

    is_g = lane < N_GROUPS
    g_top, g_idx = first_max(jnp.where(is_g, logits, NEG_INF))
    g_w = 1.0 / jnp.sum(jnp.where(is_g, jnp.exp(logits - g_top), 0.0), -1, keepdims=True)
    lo = N_GROUPS + EPG * g_idx
    e_vals = jnp.where((lane >= lo) & (lane < lo + EPG), logits, NEG_INF)
    v1, i1 = first_max(e_vals)
    v2, i2 = first_max(jnp.where(lane == i1, NEG_INF, e_vals))
    t = jnp.exp(v2 - v1)
    w1 = g_w * (1.0 / (1.0 + t))
    w2 = g_w * (t / (1.0 + t))
    swap = i2 < i1
    a = jnp.where(swap, i2, i1) - lo
    b = jnp.where(swap, i1, i2) - lo
    cls = g_idx * PAIRS + ((a * (2 * EPG - 1 - a)) >> 1) + (b - a - 1)
    return jnp.where(swap, w2, w1), jnp.where(swap, w1, w2), cls


def _merge_body(x_ref, ya_ref, u_ref, vg_ref, gate_ref, ws_ref, bs_ref, wpa_ref, wpb_ref, wo_ref,
                lng_ref, lnb_ref, g1_ref, sh2_ref, sc2_ref, wrh_ref, wrl_ref, br_ref, x1_ref, rec_ref, cnt_ref):
    tile, d = x_ref.shape
    lane = _iota((BLOCK, LANES), 1)
    below = (lane < _iota((BLOCK, LANES), 0)).astype(BF16)
    chunks = []
    for c in range(tile // BLOCK):
        rows = slice(c * BLOCK, (c + 1) * BLOCK)
        pieces = []
        for j in range(GM_W // LANES):
            v = vg_ref[rows, j * LANES:(j + 1) * LANES]
            per = LANES // GM_HEAD
            sp = _dot(ws_ref[per * j], v)
            for k in range(1, per):
                sp = jnp.where(lane < k * GM_HEAD, sp, _dot(ws_ref[per * j + k], v))
            pieces.append(sp)
        chunks.append(jnp.concatenate(pieces, axis=1) + bs_ref[...])
    sp = jnp.concatenate(chunks, axis=0)
    y_gm = (u_ref[...].astype(F32) * sp).astype(BF16)
    gate = gate_ref[...].astype(F32)
    y = gate[:, :d] * _dot(ya_ref[...], wpa_ref[...]) + gate[:, d:] * _dot(y_gm, wpb_ref[...])
    mix = _dot(y.astype(BF16), wo_ref[...])
    x1 = _ln(ALPHA * x_ref[...] + g1_ref[...] * mix) * lng_ref[...] + lnb_ref[...]
    x1_ref[...] = x1
    h2 = _ln(x1) * (1.0 + sc2_ref[...]) + sh2_ref[...]
    w_lo, w_hi, cls = _route(_dot_split(h2, wrh_ref[...], wrl_ref[...]) + br_ref[...])
    for c in range(tile // BLOCK):
        rows = slice(c * BLOCK, (c + 1) * BLOCK)
        onehot = lane == cls[rows]
        hot = onehot.astype(BF16)
        rank = jnp.sum(jnp.where(onehot, _dot(below, hot), 0.0), -1, keepdims=True)
        cnt_ref[c] = jnp.sum(hot.astype(F32), 0, keepdims=True)
        rec = jnp.where(lane == REC_WLO, w_lo[rows], 0.0)
        rec = jnp.where(lane == REC_WHI, w_hi[rows], rec)
        rec = jnp.where(lane == REC_CLS, cls[rows].astype(F32), rec)
        rec_ref[rows, :] = jnp.where(lane == REC_RANK, rank, rec)


def _merge(x, y_attn, u, vg, gates, ws_bf, bs_rows, wpa, wpb, wo, ln_g, ln_b, mod3, wr_hi, wr_lo, br, tile):
    b, s, d = x.shape
    nt = s // tile
    per = tile // BLOCK
    row = lambda width: pl.BlockSpec((None, tile, width), lambda i, j: (i, j, 0))
    full = lambda a: pl.BlockSpec(a.shape, lambda i, j: (0,) * a.ndim)
    modv = lambda k: pl.BlockSpec((None, 1, d), lambda i, j: (i, 0, k))
    ln_g = ln_g.reshape(1, d)
    ln_b = ln_b.reshape(1, d)
    return pl.pallas_call(
        _merge_body,
        out_shape=[jax.ShapeDtypeStruct((b, s, d), F32), jax.ShapeDtypeStruct((b, s, ROUTE_W), F32),
                   jax.ShapeDtypeStruct((b * s // BLOCK, 1, LANES), F32)],
        grid=(b, nt),
        in_specs=[
            row(d), row(ATTN_W), row(GM_W), row(GM_W), row(2 * d),
            full(ws_bf), full(bs_rows), full(wpa), full(wpb), full(wo), full(ln_g), full(ln_b),
            modv(2), modv(3), modv(4),
            full(wr_hi), full(wr_lo), full(br),
        ],
        out_specs=[row(d), row(ROUTE_W), pl.BlockSpec((per, 1, LANES), lambda i, j: (i * nt + j, 0, 0))],
        compiler_params=_params("arbitrary", "arbitrary"),
        name="merge",
    )(x, y_attn, u, vg, gates, ws_bf, bs_rows, wpa, wpb, wo, ln_g, ln_b, mod3, mod3, mod3, wr_hi, wr_lo, br)


def _plan_body(cnt_ref, rec_ref, lpos_ref, tab_ref, cblk_ref, run_ref, cntt_ref, lst_ref, base_ref, lstf_ref, *, chunks):
    step = pl.program_id(0)
    n_chunks = cnt_ref.shape[0]
    n_tiles = run_ref.shape[0]
    max_blocks = tab_ref.shape[0]

    @pl.when(step == 0)
    def _():
        cnt = cnt_ref[...]
        total = jnp.sum(cnt, 0, keepdims=True)
        n_blk = jnp.floor((total + (BLOCK - 1)) * (1.0 / BLOCK))
        before = (_iota((LANES, LANES), 0) < _iota((LANES, LANES), 1)).astype(BF16)
        nb_hi, nb_lo = _split(jnp.broadcast_to(n_blk, (SUBLANES, LANES)))
        blk_start = (_dot(nb_hi, before) + _dot(nb_lo, before))[0:1]
        lane1 = _iota((1, LANES), 1)
        packed = blk_start
        shift = jnp.zeros((1, 1), F32)
        for g in range(1, N_GROUPS):
            first = jnp.sum(jnp.where(lane1 == g * PAIRS, blk_start, 0.0), -1, keepdims=True) + shift
            over = first - MOE_PACK * jnp.floor(first * (1.0 / MOE_PACK))
            shift = shift + jnp.where(over > 0.0, MOE_PACK - over, 0.0)
            packed = jnp.where(lane1 >= g * PAIRS, blk_start + shift, packed)
        blk_start = packed
        earlier = (_iota((n_chunks, n_chunks), 1) < _iota((n_chunks, n_chunks), 0)).astype(BF16)
        base_ref[...] = blk_start * float(BLOCK) + _dot(earlier, cnt.astype(BF16))
        blk_end = blk_start + n_blk
        sub = _iota((SUBLANES, LANES), 0)
        cblk_ref[...] = jnp.where(sub == 0, blk_start, jnp.where(sub == 1, blk_end, 0.0)).astype(I32)
        lane = _iota((max_blocks, LANES), 1)
        blk = _iota((max_blocks, LANES), 0).astype(F32)
        is_cls = lane < N_CLASSES
        n_real = jnp.max(jnp.where(is_cls, blk_end, 0.0), -1, keepdims=True)
        u = jnp.minimum(blk, n_real - 1.0)
        cls_of = jnp.sum(jnp.where(is_cls & (blk_end <= u), 1.0, 0.0), -1, keepdims=True)
        n_steps = jnp.floor((n_real + (MOE_PACK - 1)) * (1.0 / MOE_PACK))
        tab = jnp.where(lane == 0, cls_of, jnp.where(lane == 1, n_steps, 0.0))
        tab_ref[...] = tab.astype(I32)
        tile = _iota((n_tiles, n_chunks), 0)
        chunk = _iota((n_tiles, n_chunks), 1)
        owner = ((chunk >= tile * chunks) & (chunk < (tile + 1) * chunks)).astype(BF16)
        cnt_tile = _dot(owner, cnt.astype(BF16))
        ct_hi, ct_lo = _split(cnt_tile)
        local_start = _dot(ct_hi, before) + _dot(ct_lo, before)
        run_ref[...] = base_ref[pl.ds(0, n_tiles, stride=chunks), :].astype(I32)
        cntt_ref[...] = cnt_tile.astype(I32)
        lst_ref[...] = local_start.astype(I32)
        lstf_ref[...] = local_start

    lane = _iota((BLOCK, LANES), 1)
    tile_off = lstf_ref[pl.ds(step, 1), :] - base_ref[pl.ds(step * chunks, 1), :]
    pick_rank = (_iota((2 * SUBLANES, LANES), 1) == REC_RANK).astype(BF16)
    for c in range(chunks):
        rec = rec_ref[c * BLOCK:(c + 1) * BLOCK, :]
        hot = (lane == rec[:, REC_CLS:REC_CLS + 1].astype(I32)).astype(BF16)
        base = base_ref[pl.ds(step * chunks + c, 1), :] + tile_off
        b_hi, b_lo = _split(jnp.broadcast_to(base, (2 * SUBLANES, LANES)))
        lpos = _dot_nt(b_hi, hot) + _dot_nt(b_lo, hot) + _dot_nt(pick_rank, rec.astype(BF16))
        lpos_ref[c] = lpos[0:1].astype(I32)


def _plan(counts, rec, max_blocks, chunks):
    n = rec.shape[0]
    n_chunks = n // BLOCK
    n_tiles = n_chunks // chunks
    tile_tab = jax.ShapeDtypeStruct((n_tiles, LANES), I32)
    whole = lambda rows: pl.BlockSpec((rows, LANES), lambda i: (0, 0))
    return pl.pallas_call(
        functools.partial(_plan_body, chunks=chunks),
        out_shape=[jax.ShapeDtypeStruct((n_chunks, 1, LANES), I32), jax.ShapeDtypeStruct((max_blocks, LANES), I32),
                   jax.ShapeDtypeStruct((SUBLANES, LANES), I32), tile_tab, tile_tab, tile_tab],
        grid=(n_tiles,),
        in_specs=[whole(n_chunks), pl.BlockSpec((chunks * BLOCK, ROUTE_W), lambda i: (i, 0))],
        out_specs=[pl.BlockSpec((chunks, 1, LANES), lambda i: (i, 0, 0)), whole(max_blocks), whole(SUBLANES),
                   whole(n_tiles), whole(n_tiles), whole(n_tiles)],
        scratch_shapes=[pltpu.VMEM((n_chunks, LANES), F32), pltpu.VMEM((n_tiles, LANES), F32)],
        compiler_params=_params("arbitrary"),
        name="plan",
    )(counts, rec)


def _tile_rows(base, j, pitch):
    return pl.ds(base + j, BLOCK, stride=pitch)


def _load_tiles(ref, base, pitch, width):
    return jnp.concatenate([ref[_tile_rows(base, j, pitch), :] for j in range(width // LANES)], axis=1)


def _store_tiles(ref, base, pitch, val):
    for j in range(val.shape[1] // LANES):
        ref[_tile_rows(base, j, pitch), :] = val[:, j * LANES:(j + 1) * LANES]


def _start_runs(cnt_ref, lst_ref, run_ref, streams, sem, to_hbm):
    def issue(c):
        for buf, base, hbm, pitch in streams:
            rows = cnt_ref[0, c] * pitch
            aligned = (lambda row: pl.multiple_of(row, pitch)) if pitch > 1 else (lambda row: row)
            v = buf.at[pl.ds(aligned(base + lst_ref[0, c] * pitch), rows)]
            h = hbm.at[pl.ds(aligned(run_ref[0, c] * pitch), rows)]
            (pltpu.make_async_copy(v, h, sem) if to_hbm else pltpu.make_async_copy(h, v, sem)).start(priority=c % 2)

    for c in range(N_CLASSES):
        pl.when(cnt_ref[0, c] > 0)(functools.partial(issue, c))


def _wait_runs(streams, tokens, sem):
    for buf, base, hbm, pitch in streams:
        rows = tokens * pitch
        pltpu.make_async_copy(hbm.at[pl.ds(0, rows)], buf.at[pl.ds(base, rows)], sem).wait()


def _scatter_body(cls_blk_ref, run_ref, cnt_ref, lst_ref, lpos_ref, x1_ref, rec_ref, sh_ref, sc_ref, xs_hbm, meta_hbm,
                  abuf, bbuf, mbuf, zbuf, ssem, zsem):
    u = pl.program_id(0)
    slot = u % 2
    tokens = x1_ref.shape[0]
    chunks = tokens // BLOCK
    tile_rows = BLOCK * SUBLANES
    sort_rows = chunks * tile_rows

    def streams(s):
        return ((bbuf, s * sort_rows, xs_hbm, SUBLANES), (mbuf, s * tokens, meta_hbm, 1))

    @pl.when(u == 0)
    def _():
        zbuf[...] = jnp.zeros_like(zbuf)

        def zero_block(blk, start):
            for hbm, rows in ((xs_hbm, tile_rows), (meta_hbm, BLOCK)):
                cp = pltpu.make_async_copy(zbuf.at[pl.ds(0, rows)], hbm.at[pl.ds(pl.multiple_of(blk * rows, rows), rows)],
                                           zsem)
                cp.start() if start else cp.wait()

        for start in (True, False):
            for c in range(N_CLASSES):
                pl.when(cls_blk_ref[1, c] > cls_blk_ref[0, c])(functools.partial(zero_block, cls_blk_ref[1, c] - 1, start))
        gaps = [(cls_blk_ref[1, g * PAIRS - 1], cls_blk_ref[0, g * PAIRS]) for g in range(1, N_GROUPS)]
        gaps.append((cls_blk_ref[1, N_CLASSES - 1], xs_hbm.shape[0] // tile_rows))
        for start in (True, False):
            for lo, hi in gaps:
                lax.fori_loop(lo, hi, lambda blk, carry: (zero_block(blk, start), carry)[1], 0)

    @pl.when(u >= 2)
    def _():
        _wait_runs(streams(slot), tokens, ssem.at[slot])

    scale = 1.0 + sc_ref[...]
    shift = sh_ref[...]

    unroll = abuf.shape[0] // tile_rows

    def sort_chunks(i, carry):
        for k in range(unroll):
            c = i * unroll + k
            stage = k * tile_rows
            rows = pl.ds(pl.multiple_of(c * BLOCK, BLOCK), BLOCK)
            _store_tiles(abuf, stage, SUBLANES, _ln(x1_ref[rows, :]) * scale + shift)
            rec = rec_ref[rows, :]
            for r in range(BLOCK):
                p = lpos_ref[c, 0, r]
                dst = pl.multiple_of(slot * sort_rows + p * SUBLANES, SUBLANES)
                bbuf[pl.ds(dst, SUBLANES), :] = abuf[stage + r * SUBLANES:stage + (r + 1) * SUBLANES, :]
                mbuf[pl.ds(slot * tokens + p, 1), :] = rec[r:r + 1, :]
        return carry

    lax.fori_loop(0, chunks // unroll, sort_chunks, 0)
    _start_runs(cnt_ref, lst_ref, run_ref, streams(slot), ssem.at[slot], True)

    @pl.when(u == pl.num_programs(0) - 1)
    def _():
        _wait_runs(streams(slot), tokens, ssem.at[slot])

        @pl.when(u >= 1)
        def _():
            _wait_runs(streams(1 - slot), tokens, ssem.at[1 - slot])


def _stage_chunks(chunks):
    return next(p for p in (4, 2, 1) if chunks % p == 0)


def _tile_tab_spec(off, n_tiles):
    return pl.BlockSpec((None, 1, LANES), lambda u: (jnp.minimum(u + off, n_tiles - 1), 0, 0),
                        memory_space=pltpu.SMEM)


def _scatter(x1, rec, mod3, lpos, tile_tabs, cls_blk, max_blocks, seq, chunks):
    n, d = x1.shape
    n_tiles = n // (chunks * BLOCK)
    per_batch = seq // (chunks * BLOCK)
    tile_rows = BLOCK * SUBLANES
    hbm = pl.BlockSpec(memory_space=pl.ANY)
    return pl.pallas_call(
        _scatter_body,
        out_shape=[jax.ShapeDtypeStruct((max_blocks * tile_rows, LANES), F32),
                   jax.ShapeDtypeStruct((max_blocks * BLOCK, ROUTE_W), F32)],
        grid=(n_tiles,),
        in_specs=[
            pl.BlockSpec(memory_space=pltpu.SMEM),
            _tile_tab_spec(0, n_tiles), _tile_tab_spec(0, n_tiles), _tile_tab_spec(0, n_tiles),
            pl.BlockSpec((chunks, 1, LANES), lambda u: (u, 0, 0), memory_space=pltpu.SMEM),
            pl.BlockSpec((chunks * BLOCK, d), lambda u: (u, 0)),
            pl.BlockSpec((chunks * BLOCK, ROUTE_W), lambda u: (u, 0)),
            pl.BlockSpec((None, 1, d), lambda u: (u // per_batch, 0, 3)),
            pl.BlockSpec((None, 1, d), lambda u: (u // per_batch, 0, 4)),
        ],
        out_specs=[hbm, hbm],
        scratch_shapes=[
            pltpu.VMEM((_stage_chunks(chunks) * tile_rows, LANES), F32),
            pltpu.VMEM((2 * chunks * tile_rows, LANES), F32),
            pltpu.VMEM((2 * chunks * BLOCK, ROUTE_W), F32),
            pltpu.VMEM((tile_rows, LANES), F32),
            pltpu.SemaphoreType.DMA((2,)),
            pltpu.SemaphoreType.DMA,
        ],
        compiler_params=_params("arbitrary"),
        name="scatter",
    )(cls_blk, *tile_tabs, lpos, x1, rec, mod3, mod3)


def _expert(h, w1, w3, w2):
    hid = jax.nn.silu(_dot(h, w1)) * _dot(h, w3)
    return _dot(hid, w2)


def _moe_body(sgrp_ref, blo_ref, bhi_ref, nsteps_ref, xs_ref, meta_ref, w1_ref, w3_ref, w2_ref, o_ref):
    u = pl.program_id(0)
    d = w1_ref.shape[1]
    tile_rows = BLOCK * SUBLANES

    @pl.when(u < nsteps_ref[0])
    def _():
        jobs = [(k, e) for k in range(MOE_PACK) for e in (blo_ref[MOE_PACK * u + k], bhi_ref[MOE_PACK * u + k])]
        h = [_load_tiles(xs_ref, k * tile_rows, SUBLANES, d).astype(BF16) for k in range(MOE_PACK)]
        up = [(_dot(h[k], w1_ref[e]), _dot(h[k], w3_ref[e])) for k, e in jobs]
        hid = [(jax.nn.silu(a) * b).astype(BF16) for a, b in up]
        y = [_dot(g, w2_ref[e]) for (k, e), g in zip(jobs, hid)]
        for k in range(MOE_PACK):
            meta = meta_ref[k * BLOCK:(k + 1) * BLOCK, :]
            _store_tiles(o_ref, k * tile_rows, SUBLANES,
                         meta[:, REC_WLO:REC_WLO + 1] * y[2 * k] + meta[:, REC_WHI:REC_WHI + 1] * y[2 * k + 1])

    @pl.when(u >= nsteps_ref[0])
    def _():
        o_ref[...] = jnp.zeros_like(o_ref)


def _moe(xs, meta, tables, w1, w3, w2, d):
    sgrp, blo, bhi, n_steps = tables
    max_steps = sgrp.shape[0]
    de = w1.shape[1]
    in_rows = MOE_PACK * BLOCK * SUBLANES
    w1g = w1.reshape(N_GROUPS, EPG, d, de)
    w3g = w3.reshape(N_GROUPS, EPG, d, de)
    w2g = w2.reshape(N_GROUPS, EPG, de, d)

    def w_spec(a):
        return pl.BlockSpec((None,) + a.shape[1:], lambda u, g, lo, hi, n_steps: (g[u], 0, 0, 0))

    grid_spec = pltpu.PrefetchScalarGridSpec(
        num_scalar_prefetch=4,
        grid=(max_steps,),
        in_specs=[
            pl.BlockSpec((in_rows, LANES), lambda u, g, lo, hi, n_steps: (jnp.minimum(u, n_steps[0] - 1), 0)),
            pl.BlockSpec((MOE_PACK * BLOCK, ROUTE_W), lambda u, g, lo, hi, n_steps: (jnp.minimum(u, n_steps[0] - 1), 0)),
            w_spec(w1g), w_spec(w3g), w_spec(w2g),
        ],
        out_specs=pl.BlockSpec((in_rows, LANES), lambda u, *_: (u, 0)),
    )
    return pl.pallas_call(
        _moe_body,
        out_shape=jax.ShapeDtypeStruct((max_steps * in_rows, LANES), F32),
        grid_spec=grid_spec,
        compiler_params=pltpu.CompilerParams(dimension_semantics=("arbitrary",), vmem_limit_bytes=MOE_VMEM_LIMIT_BYTES),
        name="moe",
    )(sgrp, blo, bhi, n_steps, xs, meta, w1g, w3g, w2g)


def _final_body(run_ref, cnt_ref, lst_ref, runn_ref, cntn_ref, lstn_ref, lpos_ref, x1_ref, g2_ref,
                lng_ref, lnb_ref, ms_hbm, o_ref, gbuf, tbuf, gsem):
    u = pl.program_id(0)
    slot = u % 2
    d = x1_ref.shape[-1]
    tokens = x1_ref.shape[0]
    chunks = tokens // BLOCK
    tile_rows = BLOCK * SUBLANES
    sort_rows = chunks * tile_rows

    def streams(s):
        return ((gbuf, s * sort_rows, ms_hbm, SUBLANES),)

    @pl.when(u == 0)
    def _():
        _start_runs(cnt_ref, lst_ref, run_ref, streams(0), gsem.at[0], False)

    _wait_runs(streams(slot), tokens, gsem.at[slot])
    _start_runs(cntn_ref, lstn_ref, runn_ref, streams(1 - slot), gsem.at[1 - slot], False)
    gain = g2_ref[...]
    ln_g = lng_ref[...]
    ln_b = lnb_ref[...]
    unroll = tbuf.shape[0] // tile_rows

    def finish_chunks(i, carry):
        for k in range(unroll):
            c = i * unroll + k
            stage = k * tile_rows
            for r in range(BLOCK):
                src = pl.ds(pl.multiple_of(slot * sort_rows + lpos_ref[c, 0, r] * SUBLANES, SUBLANES), SUBLANES)
                tbuf[stage + r * SUBLANES:stage + (r + 1) * SUBLANES, :] = gbuf[src, :]
            rows = pl.ds(pl.multiple_of(c * BLOCK, BLOCK), BLOCK)
            moe = _load_tiles(tbuf, stage, SUBLANES, d)
            o_ref[rows, :] = _ln(ALPHA * x1_ref[rows, :] + gain * moe) * ln_g + ln_b
        return carry

    lax.fori_loop(0, chunks // unroll, finish_chunks, 0)

    @pl.when(u == pl.num_programs(0) - 1)
    def _():
        _wait_runs(streams(1 - slot), tokens, gsem.at[1 - slot])


def _final(x1, ms, lpos, tile_tabs, mod3, ln_g, ln_b, seq, chunks):
    n, d = x1.shape
    n_tiles = n // (chunks * BLOCK)
    per_batch = seq // (chunks * BLOCK)
    tile_rows = BLOCK * SUBLANES
    vec = pl.BlockSpec((1, d), lambda u: (0, 0))
    return pl.pallas_call(
        _final_body,
        out_shape=jax.ShapeDtypeStruct((n, d), F32),
        grid=(n_tiles,),
        in_specs=[
            _tile_tab_spec(0, n_tiles), _tile_tab_spec(0, n_tiles), _tile_tab_spec(0, n_tiles),
            _tile_tab_spec(1, n_tiles), _tile_tab_spec(1, n_tiles), _tile_tab_spec(1, n_tiles),
            pl.BlockSpec((chunks, 1, LANES), lambda u: (u, 0, 0), memory_space=pltpu.SMEM),
            pl.BlockSpec((chunks * BLOCK, d), lambda u: (u, 0)),
            pl.BlockSpec((None, 1, d), lambda u: (u // per_batch, 0, 5)),
            vec, vec, pl.BlockSpec(memory_space=pl.ANY),
        ],
        out_specs=pl.BlockSpec((chunks * BLOCK, d), lambda u: (u, 0)),
        scratch_shapes=[
            pltpu.VMEM((2 * chunks * tile_rows, LANES), F32),
            pltpu.VMEM((_stage_chunks(chunks) * tile_rows, LANES), F32),
            pltpu.SemaphoreType.DMA((2,)),
        ],
        compiler_params=_params("arbitrary"),
        name="final",
    )(*tile_tabs, *tile_tabs, lpos, x1, mod3, ln_g.reshape(1, d), ln_b.reshape(1, d), ms)


def _block_tables(tab):
    cls = tab[:, 0]
    pair = cls % PAIRS
    first = jnp.array([i for i in range(EPG) for _ in range(i + 1, EPG)], I32)
    second = jnp.array([j for i in range(EPG) for j in range(i + 1, EPG)], I32)
    return (cls // PAIRS)[::MOE_PACK], first[pair], second[pair], tab[0, 1].reshape(1)


def kernel(x, c, ctx, c_ctx, w_ada, b_ada, w_in, attn_sink, gm_ln_g, gm_ln_b, gm_ws, gm_bs, w_pa, w_pb, w_o,
           ln1_g, ln1_b, router_g_w, router_g_b, router_e_w, router_e_b, moe_w1, moe_w3, moe_w2, ln2_g, ln2_b):
    b, s, d = x.shape
    assert w_ada.shape[0] == DEPTH and s % BLOCK == 0 and s % GRID_W == 0
    mod_rows = 2 * SUBLANES
    assert b + 1 <= mod_rows
    c_rows = jnp.concatenate([c, c_ctx[None, :], jnp.zeros((mod_rows - b - 1, d), F32)], axis=0)
    mod = _adaln_mod(c_rows, w_ada[0], b_ada[0])
    mod3 = mod.reshape(mod_rows, 1, 6 * d)
    w_in_bf = w_in[0].astype(BF16)
    kv_ctx = _ctx_kv(ctx, mod3, b, w_in_bf)
    tile = next(t for t in (1024, 512, BLOCK) if s % t == 0)
    q, kv, u, vg, gates = _project(x, mod3, w_in_bf, _rope_tables(s), gm_ln_g[0], gm_ln_b[0], tile)
    nb = s // BLOCK
    experts, _, d_expert = moe_w1.shape[1:]
    moe_w = [moe_w1[0].reshape(experts * d, d_expert), moe_w3[0].reshape(experts * d, d_expert),
             moe_w2[0].reshape(experts * d_expert, d)]
    per = next(p for p in (8, 4, 2, 1) if nb % p == 0)
    steps = b * (nb // per)
    sliced = all(w.shape[0] % (steps * 2 * SUBLANES) == 0 for w in moe_w)
    y_attn, moe_w_bf = _attend(q, kv, kv_ctx, attn_sink[0], per, moe_w if sliced else [])
    if not sliced:
        moe_w_bf = [w.astype(BF16) for w in moe_w]
    bs_rows = jnp.repeat(gm_bs[0].T, GM_HEAD, axis=1)
    w_route = jnp.concatenate([router_g_w[0], jnp.transpose(router_e_w[0], (1, 0, 2)).reshape(d, N_EXPERTS)], axis=1)
    w_route = jnp.pad(w_route, ((0, 0), (0, ROUTE_W - w_route.shape[1])))
    b_route = jnp.pad(jnp.concatenate([router_g_b[0], router_e_b[0].reshape(-1)]), (0, ROUTE_W - N_GROUPS - N_EXPERTS))
    wr_hi = w_route.astype(BF16)
    wr_lo = (w_route - wr_hi.astype(F32)).astype(BF16)
    merge_tile = BLOCK * next(p for p in (8, 4, 2, 1) if nb % p == 0)
    x1, rec, counts = _merge(x, y_attn, u, vg, gates, gm_ws[0].astype(BF16), bs_rows, w_pa[0].astype(BF16),
                             w_pb[0].astype(BF16), w_o[0].astype(BF16), ln1_g[0], ln1_b[0], mod3, wr_hi, wr_lo,
                             b_route.reshape(1, ROUTE_W), merge_tile)
    n = b * s
    x1 = x1.reshape(n, d)
    rec = rec.reshape(n, ROUTE_W)
    max_blocks = n // BLOCK + N_CLASSES - 1 + (N_GROUPS - 1) * (MOE_PACK - 1)
    max_blocks = -(-max_blocks // MOE_PACK) * MOE_PACK
    sort_chunks = next(p for p in (16, 8, 4, 2, 1) if nb % p == 0)
    lpos, tab, cls_blk, *tile_tabs = _plan(counts.reshape(n // BLOCK, LANES), rec, max_blocks, sort_chunks)
    tile_tabs = [t.reshape(t.shape[0], 1, LANES) for t in tile_tabs]
    xs, meta = _scatter(x1, rec, mod3, lpos, tile_tabs, cls_blk, max_blocks, s, sort_chunks)
    ms = _moe(xs, meta, _block_tables(tab), *moe_w_bf, d)
    out = _final(x1, ms, lpos, tile_tabs, mod3, ln2_g[0], ln2_b[0], s, sort_chunks)
    return out.reshape(b, s, d)
```

```python
import functools

import jax
import jax.numpy as jnp
import numpy as np
from jax import lax
from jax.experimental import pallas as pl
from jax.experimental.pallas import tpu as pltpu

F32 = jnp.float32
BF16 = jnp.bfloat16
I32 = jnp.int32

GRID_W = 64
HEAD_DIM = 64
Q_HEADS = 8
KV_HEADS = 2
GROUP = Q_HEADS // KV_HEADS
BLOCK = 128
ROPE_BASE = 10000.0
ROPE_FREQS = HEAD_DIM // 4
GM_GROUPS = 8
GM_HEAD = 64
ATTN_W = Q_HEADS * HEAD_DIM
KV_W = KV_HEADS * HEAD_DIM
GM_W = GM_GROUPS * GM_HEAD
N_GROUPS = 4
EPG = 8
N_EXPERTS = N_GROUPS * EPG
PAIRS = EPG * (EPG - 1) // 2
N_CLASSES = N_GROUPS * PAIRS
LN_EPS = 1e-6
NEG_INF = -1e30
DEPTH = 1
ALPHA = (2.0 * DEPTH) ** 0.25
LOG2_E = 1.4426950408889634

LANES = 128
SUBLANES = 8
VMEM_LIMIT_BYTES = 56 * 1024 * 1024
MOE_VMEM_LIMIT_BYTES = 60 * 1024 * 1024
MOE_PACK = 4
ROUTE_W = LANES
REC_WLO, REC_WHI, REC_CLS, REC_RANK = 0, 1, 2, 3


def _params(*sem):
    return pltpu.CompilerParams(dimension_semantics=sem, vmem_limit_bytes=VMEM_LIMIT_BYTES)


def _ln(x):
    mu = jnp.mean(x, -1, keepdims=True)
    xc = x - mu
    var = jnp.mean(xc * xc, -1, keepdims=True)
    return xc * lax.rsqrt(var + LN_EPS)


def _dot(a, b):
    return jnp.dot(a, b, preferred_element_type=F32)


def _dot_nt(a, b):
    return lax.dot_general(a, b, (((1,), (1,)), ((), ())), preferred_element_type=F32)


def _split(a):
    hi = a.astype(BF16)
    lo = (a - hi.astype(F32)).astype(BF16)
    return hi, lo


def _dot_split(a, b_hi, b_lo):
    a_hi, a_lo = _split(a)
    return _dot(a_hi, b_hi) + _dot(a_hi, b_lo) + _dot(a_lo, b_hi)


def _iota(shape, axis):
    return lax.broadcasted_iota(I32, shape, axis)


def _mod_body(c_ref, w_ref, b_ref, o_ref):
    w_hi, w_lo = _split(w_ref[...])
    o_ref[...] = _dot_split(jax.nn.silu(c_ref[...]), w_hi, w_lo) + b_ref[...]


def _adaln_mod(c_rows, w_ada, b_ada):
    rows, d = c_rows.shape
    width = w_ada.shape[1]
    tn = 6 * LANES
    return pl.pallas_call(
        _mod_body,
        out_shape=jax.ShapeDtypeStruct((rows, width), F32),
        grid=(width // tn,),
        in_specs=[
            pl.BlockSpec((rows, d), lambda j: (0, 0)),
            pl.BlockSpec((d, tn), lambda j: (0, j)),
            pl.BlockSpec((1, tn), lambda j: (0, j)),
        ],
        out_specs=pl.BlockSpec((rows, tn), lambda j: (0, j)),
        compiler_params=_params("arbitrary"),
        name="mod",
    )(c_rows, w_ada, b_ada.reshape(1, width))


def _ctx_body(ctx_ref, sh_ref, sc_ref, w_ref, o_ref):
    h = _ln(ctx_ref[...]) * (1.0 + sc_ref[...]) + sh_ref[...]
    o_ref[...] = _dot(h.astype(BF16), w_ref[...]).astype(BF16)


def _ctx_kv(ctx, mod3, ctx_row, w_in_bf):
    b, c_len, d = ctx.shape
    kvw = 2 * KV_W
    return pl.pallas_call(
        _ctx_body,
        out_shape=jax.ShapeDtypeStruct((b, c_len, kvw), BF16),
        grid=(b,),
        in_specs=[
            pl.BlockSpec((None, c_len, d), lambda i: (i, 0, 0)),
            pl.BlockSpec((None, 1, d), lambda i: (ctx_row, 0, 0)),
            pl.BlockSpec((None, 1, d), lambda i: (ctx_row, 0, 1)),
            pl.BlockSpec((d, kvw), lambda i: (0, ATTN_W // kvw)),
        ],
        out_specs=pl.BlockSpec((None, c_len, kvw), lambda i: (i, 0, 0)),
        compiler_params=_params("arbitrary"),
        name="ctx_kv",
    )(ctx, mod3, mod3, w_in_bf)


def _rope(x, cos, s_up, s_dn):
    return x * cos + pltpu.roll(x, LANES - ROPE_FREQS, 1) * s_up + pltpu.roll(x, ROPE_FREQS, 1) * s_dn


def _project_body(x_ref, sh_ref, sc_ref, w_ref, cos_ref, sup_ref, sdn_ref, lng_ref, lnb_ref,
                  q_ref, kv_ref, u_ref, vg_ref, gate_ref):
    h = (_ln(x_ref[...]) * (1.0 + sc_ref[...]) + sh_ref[...]).astype(BF16)
    cos, s_up, s_dn = cos_ref[...], sup_ref[...], sdn_ref[...]
    q_scale = HEAD_DIM ** -0.5 * LOG2_E
    k0 = ATTN_W
    u0 = k0 + 2 * KV_W
    vg0 = u0 + GM_W
    gate0 = vg0 + GM_W
    q = _dot(h, w_ref[:, 0:k0])
    for j in range(ATTN_W // LANES):
        sl = slice(j * LANES, (j + 1) * LANES)
        q_ref[:, sl] = (_rope(q[:, sl], cos, s_up, s_dn) * q_scale).astype(BF16)
    kv = _dot(h, w_ref[:, k0:u0])
    kv_ref[:, 0:KV_W] = _rope(kv[:, 0:KV_W], cos, s_up, s_dn).astype(BF16)
    kv_ref[:, KV_W:] = kv[:, KV_W:].astype(BF16)
    u_ref[...] = jax.nn.gelu(_dot(h, w_ref[:, u0:vg0])).astype(BF16)
    vg = _ln(jax.nn.gelu(_dot(h, w_ref[:, vg0:gate0]))) * lng_ref[...] + lnb_ref[...]
    vg_ref[...] = vg.astype(BF16)
    gate_ref[...] = jax.nn.sigmoid(_dot(h, w_ref[:, gate0:])).astype(BF16)


def _project(x, mod3, w_in_bf, rope_tabs, gm_ln_g, gm_ln_b, tile):
    b, s, d = x.shape
    in_w = w_in_bf.shape[1]
    gate_w = in_w - (ATTN_W + 2 * KV_W + 2 * GM_W)
    row = lambda width: pl.BlockSpec((None, tile, width), lambda i, t: (i, t, 0))
    tab = pl.BlockSpec((tile, LANES), lambda i, t: (t, 0))
    vec = lambda width: pl.BlockSpec((1, width), lambda i, t: (0, 0))
    return pl.pallas_call(
        _project_body,
        out_shape=[
            jax.ShapeDtypeStruct((b, s, ATTN_W), BF16),
            jax.ShapeDtypeStruct((b, s, 2 * KV_W), BF16),
            jax.ShapeDtypeStruct((b, s, GM_W), BF16),
            jax.ShapeDtypeStruct((b, s, GM_W), BF16),
            jax.ShapeDtypeStruct((b, s, gate_w), BF16),
        ],
        grid=(b, s // tile),
        in_specs=[
            row(d),
            pl.BlockSpec((None, 1, d), lambda i, t: (i, 0, 0)),
            pl.BlockSpec((None, 1, d), lambda i, t: (i, 0, 1)),
            pl.BlockSpec((d, in_w), lambda i, t: (0, 0)),
            tab, tab, tab,
            vec(GM_W), vec(GM_W),
        ],
        out_specs=[row(ATTN_W), row(2 * KV_W), row(GM_W), row(GM_W), row(gate_w)],
        compiler_params=_params("arbitrary", "arbitrary"),
        name="project",
    )(x, mod3, mod3, w_in_bf, *rope_tabs, gm_ln_g.reshape(1, GM_W), gm_ln_b.reshape(1, GM_W))


def _rope_tables(s):
    pos = np.arange(s)
    inv = 1.0 / (ROPE_BASE ** (np.arange(ROPE_FREQS, dtype=np.float64) / ROPE_FREQS))
    ang_r = (pos // GRID_W)[:, None] * inv[None, :]
    ang_c = (pos % GRID_W)[:, None] * inv[None, :]
    cr, sr, cc, sn = np.cos(ang_r), np.sin(ang_r), np.cos(ang_c), np.sin(ang_c)
    z = np.zeros_like(sr)
    reps = LANES // HEAD_DIM
    tabs = ([cr, cr, cc, cc], [-sr, z, -sn, z], [z, sr, z, sn])
    return tuple(jnp.asarray(np.tile(np.concatenate(t, -1), (1, reps)), F32) for t in tabs)


def _attend_body(sink_ref, q_ref, kvp_ref, kvc_ref, kvn_ref, ctx_ref, *rest):
    n_side = (len(rest) - 1) // 2
    side_in, o_ref, side_out = rest[:n_side], rest[n_side], rest[n_side + 1:]
    for src, dst in zip(side_in, side_out):
        dst[...] = src[...].astype(BF16)
    i = pl.program_id(1)
    n_steps = pl.num_programs(1)
    per = q_ref.shape[0] // BLOCK
    rows = GROUP * BLOCK
    r = _iota((rows, 3 * BLOCK), 0) & (BLOCK - 1)
    m = _iota((rows, 3 * BLOCK), 1)
    band = (m >= r) & (m <= r + 2 * BLOCK)
    grp = _iota((rows, 1), 0) // BLOCK
    kv_all = jnp.concatenate([kvp_ref[...], kvc_ref[...], kvn_ref[...]], axis=0)
    kv_ctx = ctx_ref[...]
    for t in range(per):
        mask = band
        if t == 0:
            mask = mask & ((m >= BLOCK) | (i > 0))
        if t == per - 1:
            mask = mask & ((m < 2 * BLOCK) | (i < n_steps - 1))
        q = q_ref[t * BLOCK:(t + 1) * BLOCK, :]
        kv_loc = kv_all[t * BLOCK:(t + 3) * BLOCK]
        scores = []
        for h in range(KV_HEADS):
            ks = slice(h * HEAD_DIM, (h + 1) * HEAD_DIM)
            qg = jnp.concatenate(
                [q[:, (h * GROUP + g) * HEAD_DIM:(h * GROUP + g + 1) * HEAD_DIM] for g in range(GROUP)], axis=0)
            s_loc = _dot_nt(qg, kv_loc[:, ks])
            s_loc = jnp.concatenate([jnp.where(mask[:, :BLOCK], s_loc[:, :BLOCK], NEG_INF), s_loc[:, BLOCK:2 * BLOCK],
                                     jnp.where(mask[:, 2 * BLOCK:], s_loc[:, 2 * BLOCK:], NEG_INF)], axis=1)
            scores.append((_dot_nt(qg, kv_ctx[:, ks]), s_loc))
        probs = []
        for h, (s_ctx, s_loc) in enumerate(scores):
            sink = jnp.zeros((rows, 1), F32)
            for g in range(GROUP):
                sink = jnp.where(grp == g, sink_ref[h * GROUP + g] * LOG2_E, sink)
            top = jnp.maximum(jnp.maximum(jnp.max(s_ctx, -1, keepdims=True), jnp.max(s_loc, -1, keepdims=True)), sink)
            e_ctx = jnp.exp2(s_ctx - top)
            e_loc = jnp.exp2(s_loc - top)
            den = jnp.sum(e_ctx, -1, keepdims=True) + jnp.sum(e_loc, -1, keepdims=True) + jnp.exp2(sink - top)
            probs.append((e_ctx.astype(BF16), e_loc.astype(BF16), 1.0 / den))
        outs = []
        for h, (e_ctx, e_loc, inv) in enumerate(probs):
            vs = slice(KV_W + h * HEAD_DIM, KV_W + (h + 1) * HEAD_DIM)
            o = (_dot(e_ctx, kv_ctx[:, vs]) + _dot(e_loc, kv_loc[:, vs])) * inv
            outs += [o[g * BLOCK:(g + 1) * BLOCK] for g in range(GROUP)]
        o_ref[t * BLOCK:(t + 1) * BLOCK, :] = jnp.concatenate(outs, axis=1).astype(BF16)


def _attend(q, kv, kv_ctx, sink, per, side):
    b, s, _ = q.shape
    nb = s // BLOCK
    nt = nb // per
    c_len = kv_ctx.shape[1]
    kvw = 2 * KV_W
    side_specs = [pl.BlockSpec((a.shape[0] // (b * nt), a.shape[1]), lambda i, j: (i * nt + j, 0)) for a in side]
    out_spec = pl.BlockSpec((None, per * BLOCK, ATTN_W), lambda i, j: (i, j, 0))
    outs = pl.pallas_call(
        _attend_body,
        out_shape=[jax.ShapeDtypeStruct((b, s, ATTN_W), BF16)] + [jax.ShapeDtypeStruct(a.shape, BF16) for a in side],
        grid=(b, nt),
        in_specs=[
            pl.BlockSpec(memory_space=pltpu.SMEM),
            pl.BlockSpec((None, per * BLOCK, ATTN_W), lambda i, j: (i, j, 0)),
            pl.BlockSpec((None, BLOCK, kvw), lambda i, j: (i, jnp.maximum(j * per - 1, 0), 0)),
            pl.BlockSpec((None, per * BLOCK, kvw), lambda i, j: (i, j, 0)),
            pl.BlockSpec((None, BLOCK, kvw), lambda i, j: (i, jnp.minimum((j + 1) * per, nb - 1), 0)),
            pl.BlockSpec((None, c_len, kvw), lambda i, j: (i, 0, 0)),
        ] + side_specs,
        out_specs=[out_spec] + side_specs,
        compiler_params=_params("arbitrary", "arbitrary"),
        name="attend",
    )(sink, q, kv, kv, kv, kv_ctx, *side)
    return outs[0], outs[1:]


def _route(logits):
    lane = _iota(logits.shape, 1)
    far = jnp.int32(2 * ROUTE_W)

    def first_max(vals):
        top = jnp.max(vals, -1, keepdims=True)
        return top, jnp.min(jnp.where(vals == top, lane, far), -1, keepdims=True)

    is_g = lane < N_GROUPS
    g_top, g_idx = first_max(jnp.where(is_g, logits, NEG_INF))
    g_w = 1.0 / jnp.sum(jnp.where(is_g, jnp.exp(logits - g_top), 0.0), -1, keepdims=True)
    lo = N_GROUPS + EPG * g_idx
    e_vals = jnp.where((lane >= lo) & (lane < lo + EPG), logits, NEG_INF)
    v1, i1 = first_max(e_vals)
    v2, i2 = first_max(jnp.where(lane == i1, NEG_INF, e_vals))
    t = jnp.exp(v2 - v1)
    w1 = g_w * (1.0 / (1.0 + t))
    w2 = g_w * (t / (1.0 + t))
    swap = i2 < i1
    a = jnp.where(swap, i2, i1) - lo
    b = jnp.where(swap, i1, i2) - lo
    cls = g_idx * PAIRS + ((a * (2 * EPG - 1 - a)) >> 1) + (b - a - 1)
    return jnp.where(swap, w2, w1), jnp.where(swap, w1, w2), cls


def _merge_body(x_ref, ya_ref, u_ref, vg_ref, gate_ref, ws_ref, bs_ref, wpa_ref, wpb_ref, wo_ref,
                lng_ref, lnb_ref, g1_ref, sh2_ref, sc2_ref, wrh_ref, wrl_ref, br_ref, x1_ref, rec_ref, cnt_ref, h2t_ref):
    tile, d = x_ref.shape
    lane = _iota((BLOCK, LANES), 1)
    below = (lane < _iota((BLOCK, LANES), 0)).astype(BF16)
    chunks = []
    for c in range(tile // BLOCK):
        rows = slice(c * BLOCK, (c + 1) * BLOCK)
        pieces = []
        for j in range(GM_W // LANES):
            v = vg_ref[rows, j * LANES:(j + 1) * LANES]
            per = LANES // GM_HEAD
            sp = _dot(ws_ref[per * j], v)
            for k in range(1, per):
                sp = jnp.where(lane < k * GM_HEAD, sp, _dot(ws_ref[per * j + k], v))
            pieces.append(sp)
        chunks.append(jnp.concatenate(pieces, axis=1) + bs_ref[...])
    sp = jnp.concatenate(chunks, axis=0)
    y_gm = (u_ref[...].astype(F32) * sp).astype(BF16)
    gate = gate_ref[...].astype(F32)
    y = gate[:, :d] * _dot(ya_ref[...], wpa_ref[...]) + gate[:, d:] * _dot(y_gm, wpb_ref[...])
    mix = _dot(y.astype(BF16), wo_ref[...])
    x1 = _ln(ALPHA * x_ref[...] + g1_ref[...] * mix) * lng_ref[...] + lnb_ref[...]
    x1_ref[...] = x1
    h2 = _ln(x1) * (1.0 + sc2_ref[...]) + sh2_ref[...]
    w_lo, w_hi, cls = _route(_dot_split(h2, wrh_ref[...], wrl_ref[...]) + br_ref[...])
    for c in range(tile // BLOCK):
        rows = slice(c * BLOCK, (c + 1) * BLOCK)
        _store_tiles(h2t_ref, c * BLOCK * SUBLANES, SUBLANES, h2[rows])
        onehot = lane == cls[rows]
        hot = onehot.astype(BF16)
        rank = jnp.sum(jnp.where(onehot, _dot(below, hot), 0.0), -1, keepdims=True)
        cnt_ref[c] = jnp.sum(hot.astype(F32), 0, keepdims=True)
        rec = jnp.where(lane == REC_WLO, w_lo[rows], 0.0)
        rec = jnp.where(lane == REC_WHI, w_hi[rows], rec)
        rec = jnp.where(lane == REC_CLS, cls[rows].astype(F32), rec)
        rec_ref[rows, :] = jnp.where(lane == REC_RANK, rank, rec)


def _merge(x, y_attn, u, vg, gates, ws_bf, bs_rows, wpa, wpb, wo, ln_g, ln_b, mod3, wr_hi, wr_lo, br, tile):
    b, s, d = x.shape
    nt = s // tile
    per = tile // BLOCK
    row = lambda width: pl.BlockSpec((None, tile, width), lambda i, j: (i, j, 0))
    full = lambda a: pl.BlockSpec(a.shape, lambda i, j: (0,) * a.ndim)
    modv = lambda k: pl.BlockSpec((None, 1, d), lambda i, j: (i, 0, k))
    ln_g = ln_g.reshape(1, d)
    ln_b = ln_b.reshape(1, d)
    return pl.pallas_call(
        _merge_body,
        out_shape=[jax.ShapeDtypeStruct((b, s, d), F32), jax.ShapeDtypeStruct((b, s, ROUTE_W), F32),
                   jax.ShapeDtypeStruct((b * s // BLOCK, 1, LANES), F32),
                   jax.ShapeDtypeStruct((b * s * d // LANES, LANES), F32)],
        grid=(b, nt),
        in_specs=[
            row(d), row(ATTN_W), row(GM_W), row(GM_W), row(2 * d),
            full(ws_bf), full(bs_rows), full(wpa), full(wpb), full(wo), full(ln_g), full(ln_b),
            modv(2), modv(3), modv(4),
            full(wr_hi), full(wr_lo), full(br),
        ],
        out_specs=[row(d), row(ROUTE_W), pl.BlockSpec((per, 1, LANES), lambda i, j: (i * nt + j, 0, 0)),
                   pl.BlockSpec((tile * d // LANES, LANES), lambda i, j: (i * nt + j, 0))],
        compiler_params=_params("arbitrary", "arbitrary"),
        name="merge",
    )(x, y_attn, u, vg, gates, ws_bf, bs_rows, wpa, wpb, wo, ln_g, ln_b, mod3, mod3, mod3, wr_hi, wr_lo, br)


def _plan_body(cnt_ref, rec_ref, lpos_ref, tab_ref, cblk_ref, run_ref, cntt_ref, lst_ref, base_ref, lstf_ref, *, chunks):
    step = pl.program_id(0)
    n_chunks = cnt_ref.shape[0]
    n_tiles = run_ref.shape[0]
    max_blocks = tab_ref.shape[0]

    @pl.when(step == 0)
    def _():
        cnt = cnt_ref[...]
        total = jnp.sum(cnt, 0, keepdims=True)
        n_blk = jnp.floor((total + (BLOCK - 1)) * (1.0 / BLOCK))
        before = (_iota((LANES, LANES), 0) < _iota((LANES, LANES), 1)).astype(BF16)
        nb_hi, nb_lo = _split(jnp.broadcast_to(n_blk, (SUBLANES, LANES)))
        blk_start = (_dot(nb_hi, before) + _dot(nb_lo, before))[0:1]
        lane1 = _iota((1, LANES), 1)
        packed = blk_start
        shift = jnp.zeros((1, 1), F32)
        for g in range(1, N_GROUPS):
            first = jnp.sum(jnp.where(lane1 == g * PAIRS, blk_start, 0.0), -1, keepdims=True) + shift
            over = first - MOE_PACK * jnp.floor(first * (1.0 / MOE_PACK))
            shift = shift + jnp.where(over > 0.0, MOE_PACK - over, 0.0)
            packed = jnp.where(lane1 >= g * PAIRS, blk_start + shift, packed)
        blk_start = packed
        earlier = (_iota((n_chunks, n_chunks), 1) < _iota((n_chunks, n_chunks), 0)).astype(BF16)
        base_ref[...] = blk_start * float(BLOCK) + _dot(earlier, cnt.astype(BF16))
        blk_end = blk_start + n_blk
        sub = _iota((SUBLANES, LANES), 0)
        cblk_ref[...] = jnp.where(sub == 0, blk_start, jnp.where(sub == 1, blk_end, 0.0)).astype(I32)
        lane = _iota((max_blocks, LANES), 1)
        blk = _iota((max_blocks, LANES), 0).astype(F32)
        is_cls = lane < N_CLASSES
        n_real = jnp.max(jnp.where(is_cls, blk_end, 0.0), -1, keepdims=True)
        u = jnp.minimum(blk, n_real - 1.0)
        cls_of = jnp.sum(jnp.where(is_cls & (blk_end <= u), 1.0, 0.0), -1, keepdims=True)
        n_steps = jnp.floor((n_real + (MOE_PACK - 1)) * (1.0 / MOE_PACK))
        tab = jnp.where(lane == 0, cls_of, jnp.where(lane == 1, n_steps, 0.0))
        tab_ref[...] = tab.astype(I32)
        tile = _iota((n_tiles, n_chunks), 0)
        chunk = _iota((n_tiles, n_chunks), 1)
        owner = ((chunk >= tile * chunks) & (chunk < (tile + 1) * chunks)).astype(BF16)
        cnt_tile = _dot(owner, cnt.astype(BF16))
        ct_hi, ct_lo = _split(cnt_tile)
        local_start = _dot(ct_hi, before) + _dot(ct_lo, before)
        run_ref[...] = base_ref[pl.ds(0, n_tiles, stride=chunks), :].astype(I32)
        cntt_ref[...] = cnt_tile.astype(I32)
        lst_ref[...] = local_start.astype(I32)
        lstf_ref[...] = local_start

    lane = _iota((BLOCK, LANES), 1)
    tile_off = lstf_ref[pl.ds(step, 1), :] - base_ref[pl.ds(step * chunks, 1), :]
    pick_rank = (_iota((2 * SUBLANES, LANES), 1) == REC_RANK).astype(BF16)
    for c in range(chunks):
        rec = rec_ref[c * BLOCK:(c + 1) * BLOCK, :]
        hot = (lane == rec[:, REC_CLS:REC_CLS + 1].astype(I32)).astype(BF16)
        base = base_ref[pl.ds(step * chunks + c, 1), :] + tile_off
        b_hi, b_lo = _split(jnp.broadcast_to(base, (2 * SUBLANES, LANES)))
        lpos = _dot_nt(b_hi, hot) + _dot_nt(b_lo, hot) + _dot_nt(pick_rank, rec.astype(BF16))
        lpos_ref[c] = lpos[0:1].astype(I32)


def _plan(counts, rec, max_blocks, chunks):
    n = rec.shape[0]
    n_chunks = n // BLOCK
    n_tiles = n_chunks // chunks
    tile_tab = jax.ShapeDtypeStruct((n_tiles, LANES), I32)
    whole = lambda rows: pl.BlockSpec((rows, LANES), lambda i: (0, 0))
    return pl.pallas_call(
        functools.partial(_plan_body, chunks=chunks),
        out_shape=[jax.ShapeDtypeStruct((n_chunks, 1, LANES), I32), jax.ShapeDtypeStruct((max_blocks, LANES), I32),
                   jax.ShapeDtypeStruct((SUBLANES, LANES), I32), tile_tab, tile_tab, tile_tab],
        grid=(n_tiles,),
        in_specs=[whole(n_chunks), pl.BlockSpec((chunks * BLOCK, ROUTE_W), lambda i: (i, 0))],
        out_specs=[pl.BlockSpec((chunks, 1, LANES), lambda i: (i, 0, 0)), whole(max_blocks), whole(SUBLANES),
                   whole(n_tiles), whole(n_tiles), whole(n_tiles)],
        scratch_shapes=[pltpu.VMEM((n_chunks, LANES), F32), pltpu.VMEM((n_tiles, LANES), F32)],
        compiler_params=_params("arbitrary"),
        name="plan",
    )(counts, rec)


def _tile_rows(base, j, pitch):
    return pl.ds(base + j, BLOCK, stride=pitch)


def _load_tiles(ref, base, pitch, width):
    return jnp.concatenate([ref[_tile_rows(base, j, pitch), :] for j in range(width // LANES)], axis=1)


def _store_tiles(ref, base, pitch, val):
    for j in range(val.shape[1] // LANES):
        ref[_tile_rows(base, j, pitch), :] = val[:, j * LANES:(j + 1) * LANES]


def _start_runs(cnt_ref, lst_ref, run_ref, streams, sem, to_hbm):
    def issue(c):
        for buf, base, hbm, pitch in streams:
            rows = cnt_ref[0, c] * pitch
            aligned = (lambda row: pl.multiple_of(row, pitch)) if pitch > 1 else (lambda row: row)
            v = buf.at[pl.ds(aligned(base + lst_ref[0, c] * pitch), rows)]
            h = hbm.at[pl.ds(aligned(run_ref[0, c] * pitch), rows)]
            (pltpu.make_async_copy(v, h, sem) if to_hbm else pltpu.make_async_copy(h, v, sem)).start(priority=c % 2)

    for c in range(N_CLASSES):
        pl.when(cnt_ref[0, c] > 0)(functools.partial(issue, c))


def _wait_runs(streams, tokens, sem):
    for buf, base, hbm, pitch in streams:
        rows = tokens * pitch
        pltpu.make_async_copy(hbm.at[pl.ds(0, rows)], buf.at[pl.ds(base, rows)], sem).wait()


def _scatter_body(cls_blk_ref, run_ref, cnt_ref, lst_ref, lpos_ref, h2t_ref, rec_ref, xs_hbm, meta_hbm,
                  bbuf, mbuf, zbuf, ssem, zsem):
    u = pl.program_id(0)
    slot = u % 2
    tokens = rec_ref.shape[0]
    chunks = tokens // BLOCK
    tile_rows = BLOCK * SUBLANES
    sort_rows = chunks * tile_rows

    def streams(s):
        return ((bbuf, s * sort_rows, xs_hbm, SUBLANES), (mbuf, s * tokens, meta_hbm, 1))

    @pl.when(u == 0)
    def _():
        zbuf[...] = jnp.zeros_like(zbuf)

        def zero_block(blk, start):
            for hbm, rows in ((xs_hbm, tile_rows), (meta_hbm, BLOCK)):
                cp = pltpu.make_async_copy(zbuf.at[pl.ds(0, rows)], hbm.at[pl.ds(pl.multiple_of(blk * rows, rows), rows)],
                                           zsem)
                cp.start() if start else cp.wait()

        for start in (True, False):
            for c in range(N_CLASSES):
                pl.when(cls_blk_ref[1, c] > cls_blk_ref[0, c])(functools.partial(zero_block, cls_blk_ref[1, c] - 1, start))
        gaps = [(cls_blk_ref[1, g * PAIRS - 1], cls_blk_ref[0, g * PAIRS]) for g in range(1, N_GROUPS)]
        gaps.append((cls_blk_ref[1, N_CLASSES - 1], xs_hbm.shape[0] // tile_rows))
        for start in (True, False):
            for lo, hi in gaps:
                lax.fori_loop(lo, hi, lambda blk, carry: (zero_block(blk, start), carry)[1], 0)

    @pl.when(u >= 2)
    def _():
        _wait_runs(streams(slot), tokens, ssem.at[slot])

    def sort_chunk(c, carry):
        rec = rec_ref[pl.ds(pl.multiple_of(c * BLOCK, BLOCK), BLOCK), :]
        first = pl.multiple_of(c * tile_rows, tile_rows)
        for r in range(BLOCK):
            p = lpos_ref[c, 0, r]
            dst = pl.multiple_of(slot * sort_rows + p * SUBLANES, SUBLANES)
            bbuf[pl.ds(dst, SUBLANES), :] = h2t_ref[pl.ds(first + r * SUBLANES, SUBLANES), :]
            mbuf[pl.ds(slot * tokens + p, 1), :] = rec[r:r + 1, :]
        return carry

    lax.fori_loop(0, chunks, sort_chunk, 0)
    _start_runs(cnt_ref, lst_ref, run_ref, streams(slot), ssem.at[slot], True)

    @pl.when(u == pl.num_programs(0) - 1)
    def _():
        _wait_runs(streams(slot), tokens, ssem.at[slot])

        @pl.when(u >= 1)
        def _():
            _wait_runs(streams(1 - slot), tokens, ssem.at[1 - slot])


def _stage_chunks(chunks):
    return next(p for p in (4, 2, 1) if chunks % p == 0)


def _tile_tab_spec(off, n_tiles):
    return pl.BlockSpec((None, 1, LANES), lambda u: (jnp.minimum(u + off, n_tiles - 1), 0, 0),
                        memory_space=pltpu.SMEM)


def _scatter(h2t, rec, lpos, tile_tabs, cls_blk, max_blocks, chunks):
    n = rec.shape[0]
    n_tiles = n // (chunks * BLOCK)
    tile_rows = BLOCK * SUBLANES
    hbm = pl.BlockSpec(memory_space=pl.ANY)
    return pl.pallas_call(
        _scatter_body,
        out_shape=[jax.ShapeDtypeStruct((max_blocks * tile_rows, LANES), F32),
                   jax.ShapeDtypeStruct((max_blocks * BLOCK, ROUTE_W), F32)],
        grid=(n_tiles,),
        in_specs=[
            pl.BlockSpec(memory_space=pltpu.SMEM),
            _tile_tab_spec(0, n_tiles), _tile_tab_spec(0, n_tiles), _tile_tab_spec(0, n_tiles),
            pl.BlockSpec((chunks, 1, LANES), lambda u: (u, 0, 0), memory_space=pltpu.SMEM),
            pl.BlockSpec((chunks * tile_rows, LANES), lambda u: (u, 0)),
            pl.BlockSpec((chunks * BLOCK, ROUTE_W), lambda u: (u, 0)),
        ],
        out_specs=[hbm, hbm],
        scratch_shapes=[
            pltpu.VMEM((2 * chunks * tile_rows, LANES), F32),
            pltpu.VMEM((2 * chunks * BLOCK, ROUTE_W), F32),
            pltpu.VMEM((tile_rows, LANES), F32),
            pltpu.SemaphoreType.DMA((2,)),
            pltpu.SemaphoreType.DMA,
        ],
        compiler_params=_params("arbitrary"),
        name="scatter",
    )(cls_blk, *tile_tabs, lpos, h2t, rec)


def _expert(h, w1, w3, w2):
    hid = jax.nn.silu(_dot(h, w1)) * _dot(h, w3)
    return _dot(hid, w2)


def _moe_body(sgrp_ref, blo_ref, bhi_ref, nsteps_ref, xs_ref, meta_ref, w1_ref, w3_ref, w2_ref, o_ref):
    u = pl.program_id(0)
    d = w1_ref.shape[1]
    tile_rows = BLOCK * SUBLANES

    @pl.when(u < nsteps_ref[0])
    def _():
        jobs = [(k, e) for k in range(MOE_PACK) for e in (blo_ref[MOE_PACK * u + k], bhi_ref[MOE_PACK * u + k])]
        h = [_load_tiles(xs_ref, k * tile_rows, SUBLANES, d).astype(BF16) for k in range(MOE_PACK)]
        up = [(_dot(h[k], w1_ref[e]), _dot(h[k], w3_ref[e])) for k, e in jobs]
        hid = [(jax.nn.silu(a) * b).astype(BF16) for a, b in up]
        y = [_dot(g, w2_ref[e]) for (k, e), g in zip(jobs, hid)]
        for k in range(MOE_PACK):
            meta = meta_ref[k * BLOCK:(k + 1) * BLOCK, :]
            _store_tiles(o_ref, k * tile_rows, SUBLANES,
                         meta[:, REC_WLO:REC_WLO + 1] * y[2 * k] + meta[:, REC_WHI:REC_WHI + 1] * y[2 * k + 1])

    @pl.when(u >= nsteps_ref[0])
    def _():
        o_ref[...] = jnp.zeros_like(o_ref)


def _moe(xs, meta, tables, w1, w3, w2, d):
    sgrp, blo, bhi, n_steps = tables
    max_steps = sgrp.shape[0]
    de = w1.shape[1]
    in_rows = MOE_PACK * BLOCK * SUBLANES
    w1g = w1.reshape(N_GROUPS, EPG, d, de)
    w3g = w3.reshape(N_GROUPS, EPG, d, de)
    w2g = w2.reshape(N_GROUPS, EPG, de, d)

    def w_spec(a):
        return pl.BlockSpec((None,) + a.shape[1:], lambda u, g, lo, hi, n_steps: (g[u], 0, 0, 0))

    grid_spec = pltpu.PrefetchScalarGridSpec(
        num_scalar_prefetch=4,
        grid=(max_steps,),
        in_specs=[
            pl.BlockSpec((in_rows, LANES), lambda u, g, lo, hi, n_steps: (jnp.minimum(u, n_steps[0] - 1), 0)),
            pl.BlockSpec((MOE_PACK * BLOCK, ROUTE_W), lambda u, g, lo, hi, n_steps: (jnp.minimum(u, n_steps[0] - 1), 0)),
            w_spec(w1g), w_spec(w3g), w_spec(w2g),
        ],
        out_specs=pl.BlockSpec((in_rows, LANES), lambda u, *_: (u, 0)),
    )
    return pl.pallas_call(
        _moe_body,
        out_shape=jax.ShapeDtypeStruct((max_steps * in_rows, LANES), F32),
        grid_spec=grid_spec,
        compiler_params=pltpu.CompilerParams(dimension_semantics=("arbitrary",), vmem_limit_bytes=MOE_VMEM_LIMIT_BYTES),
        name="moe",
    )(sgrp, blo, bhi, n_steps, xs, meta, w1g, w3g, w2g)


def _final_body(run_ref, cnt_ref, lst_ref, runn_ref, cntn_ref, lstn_ref, lpos_ref, x1_ref, g2_ref,
                lng_ref, lnb_ref, ms_hbm, o_ref, gbuf, tbuf, gsem):
    u = pl.program_id(0)
    slot = u % 2
    d = x1_ref.shape[-1]
    tokens = x1_ref.shape[0]
    chunks = tokens // BLOCK
    tile_rows = BLOCK * SUBLANES
    sort_rows = chunks * tile_rows

    def streams(s):
        return ((gbuf, s * sort_rows, ms_hbm, SUBLANES),)

    @pl.when(u == 0)
    def _():
        _start_runs(cnt_ref, lst_ref, run_ref, streams(0), gsem.at[0], False)

    _wait_runs(streams(slot), tokens, gsem.at[slot])
    _start_runs(cntn_ref, lstn_ref, runn_ref, streams(1 - slot), gsem.at[1 - slot], False)
    gain = g2_ref[...]
    ln_g = lng_ref[...]
    ln_b = lnb_ref[...]
    unroll = tbuf.shape[0] // tile_rows

    def finish_chunks(i, carry):
        for k in range(unroll):
            c = i * unroll + k
            stage = k * tile_rows
            for r in range(BLOCK):
                src = pl.ds(pl.multiple_of(slot * sort_rows + lpos_ref[c, 0, r] * SUBLANES, SUBLANES), SUBLANES)
                tbuf[stage + r * SUBLANES:stage + (r + 1) * SUBLANES, :] = gbuf[src, :]
            rows = pl.ds(pl.multiple_of(c * BLOCK, BLOCK), BLOCK)
            moe = _load_tiles(tbuf, stage, SUBLANES, d)
            o_ref[rows, :] = _ln(ALPHA * x1_ref[rows, :] + gain * moe) * ln_g + ln_b
        return carry

    lax.fori_loop(0, chunks // unroll, finish_chunks, 0)

    @pl.when(u == pl.num_programs(0) - 1)
    def _():
        _wait_runs(streams(1 - slot), tokens, gsem.at[1 - slot])


def _final(x1, ms, lpos, tile_tabs, mod3, ln_g, ln_b, seq, chunks):
    n, d = x1.shape
    n_tiles = n // (chunks * BLOCK)
    per_batch = seq // (chunks * BLOCK)
    tile_rows = BLOCK * SUBLANES
    vec = pl.BlockSpec((1, d), lambda u: (0, 0))
    return pl.pallas_call(
        _final_body,
        out_shape=jax.ShapeDtypeStruct((n, d), F32),
        grid=(n_tiles,),
        in_specs=[
            _tile_tab_spec(0, n_tiles), _tile_tab_spec(0, n_tiles), _tile_tab_spec(0, n_tiles),
            _tile_tab_spec(1, n_tiles), _tile_tab_spec(1, n_tiles), _tile_tab_spec(1, n_tiles),
            pl.BlockSpec((chunks, 1, LANES), lambda u: (u, 0, 0), memory_space=pltpu.SMEM),
            pl.BlockSpec((chunks * BLOCK, d), lambda u: (u, 0)),
            pl.BlockSpec((None, 1, d), lambda u: (u // per_batch, 0, 5)),
            vec, vec, pl.BlockSpec(memory_space=pl.ANY),
        ],
        out_specs=pl.BlockSpec((chunks * BLOCK, d), lambda u: (u, 0)),
        scratch_shapes=[
            pltpu.VMEM((2 * chunks * tile_rows, LANES), F32),
            pltpu.VMEM((_stage_chunks(chunks) * tile_rows, LANES), F32),
            pltpu.SemaphoreType.DMA((2,)),
        ],
        compiler_params=_params("arbitrary"),
        name="final",
    )(*tile_tabs, *tile_tabs, lpos, x1, mod3, ln_g.reshape(1, d), ln_b.reshape(1, d), ms)


def _block_tables(tab):
    cls = tab[:, 0]
    pair = cls % PAIRS
    first = jnp.array([i for i in range(EPG) for _ in range(i + 1, EPG)], I32)
    second = jnp.array([j for i in range(EPG) for j in range(i + 1, EPG)], I32)
    return (cls // PAIRS)[::MOE_PACK], first[pair], second[pair], tab[0, 1].reshape(1)


def kernel(x, c, ctx, c_ctx, w_ada, b_ada, w_in, attn_sink, gm_ln_g, gm_ln_b, gm_ws, gm_bs, w_pa, w_pb, w_o,
           ln1_g, ln1_b, router_g_w, router_g_b, router_e_w, router_e_b, moe_w1, moe_w3, moe_w2, ln2_g, ln2_b):
    b, s, d = x.shape
    assert w_ada.shape[0] == DEPTH and s % BLOCK == 0 and s % GRID_W == 0
    mod_rows = 2 * SUBLANES
    assert b + 1 <= mod_rows
    c_rows = jnp.concatenate([c, c_ctx[None, :], jnp.zeros((mod_rows - b - 1, d), F32)], axis=0)
    mod = _adaln_mod(c_rows, w_ada[0], b_ada[0])
    mod3 = mod.reshape(mod_rows, 1, 6 * d)
    w_in_bf = w_in[0].astype(BF16)
    kv_ctx = _ctx_kv(ctx, mod3, b, w_in_bf)
    tile = next(t for t in (1024, 512, BLOCK) if s % t == 0)
    q, kv, u, vg, gates = _project(x, mod3, w_in_bf, _rope_tables(s), gm_ln_g[0], gm_ln_b[0], tile)
    nb = s // BLOCK
    experts, _, d_expert = moe_w1.shape[1:]
    moe_w = [moe_w1[0].reshape(experts * d, d_expert), moe_w3[0].reshape(experts * d, d_expert),
             moe_w2[0].reshape(experts * d_expert, d)]
    per = next(p for p in (8, 4, 2, 1) if nb % p == 0)
    steps = b * (nb // per)
    sliced = all(w.shape[0] % (steps * 2 * SUBLANES) == 0 for w in moe_w)
    y_attn, moe_w_bf = _attend(q, kv, kv_ctx, attn_sink[0], per, moe_w if sliced else [])
    if not sliced:
        moe_w_bf = [w.astype(BF16) for w in moe_w]
    bs_rows = jnp.repeat(gm_bs[0].T, GM_HEAD, axis=1)
    w_route = jnp.concatenate([router_g_w[0], jnp.transpose(router_e_w[0], (1, 0, 2)).reshape(d, N_EXPERTS)], axis=1)
    w_route = jnp.pad(w_route, ((0, 0), (0, ROUTE_W - w_route.shape[1])))
    b_route = jnp.pad(jnp.concatenate([router_g_b[0], router_e_b[0].reshape(-1)]), (0, ROUTE_W - N_GROUPS - N_EXPERTS))
    wr_hi = w_route.astype(BF16)
    wr_lo = (w_route - wr_hi.astype(F32)).astype(BF16)
    merge_tile = BLOCK * next(p for p in (8, 4, 2, 1) if nb % p == 0)
    x1, rec, counts, h2t = _merge(x, y_attn, u, vg, gates, gm_ws[0].astype(BF16), bs_rows, w_pa[0].astype(BF16),
                             w_pb[0].astype(BF16), w_o[0].astype(BF16), ln1_g[0], ln1_b[0], mod3, wr_hi, wr_lo,
                             b_route.reshape(1, ROUTE_W), merge_tile)
    n = b * s
    x1 = x1.reshape(n, d)
    rec = rec.reshape(n, ROUTE_W)
    max_blocks = n // BLOCK + N_CLASSES - 1 + (N_GROUPS - 1) * (MOE_PACK - 1)
    max_blocks = -(-max_blocks // MOE_PACK) * MOE_PACK
    sort_chunks = next(p for p in (16, 8, 4, 2, 1) if nb % p == 0)
    lpos, tab, cls_blk, *tile_tabs = _plan(counts.reshape(n // BLOCK, LANES), rec, max_blocks, sort_chunks)
    tile_tabs = [t.reshape(t.shape[0], 1, LANES) for t in tile_tabs]
    xs, meta = _scatter(h2t, rec, lpos, tile_tabs, cls_blk, max_blocks, sort_chunks)
    ms = _moe(xs, meta, _block_tables(tab), *moe_w_bf, d)
    out = _final(x1, ms, lpos, tile_tabs, mod3, ln2_g[0], ln2_b[0], s, sort_chunks)
    return out.reshape(b, s, d)
```
